```python
import math
import jax, jax.numpy as jnp
from jax import lax
import numpy as np

D_MODEL = 1024
BATCH = 4
SEQ = 8192
DEPTH = 1

RET_HEADS = 4
RET_DK = 128
RET_DV = 128
RET_CHUNK = 128
ROPE_BASE = 10000.0
ATT_Q_HEADS = 8
ATT_KV_HEADS = 2
ATT_GROUP = ATT_Q_HEADS // ATT_KV_HEADS
ATT_HEAD_DIM = 64
WINDOW = 128
ATT_BLOCK = 128
NUM_BUCKETS = 32
MAX_DISTANCE = 128
N_EXPERTS = 32
TOP_K = 4
D_FF = D_MODEL
SWIGLU_ALPHA = 1.702
SWIGLU_LIMIT = 7.0
MOE_BLOCK = 128
PLE_DIM = 256
EPS = 1e-5

RET_WIDTH = RET_HEADS * RET_DV
ATT_WIDTH = ATT_Q_HEADS * ATT_HEAD_DIM
MIX_WIDTH = RET_WIDTH + ATT_WIDTH
IN_SIZES = (RET_HEADS * RET_DK, RET_HEADS * RET_DK, RET_HEADS * RET_DV, RET_WIDTH,
            ATT_Q_HEADS * ATT_HEAD_DIM, ATT_KV_HEADS * ATT_HEAD_DIM, ATT_KV_HEADS * ATT_HEAD_DIM)
IN_WIDTH = sum(IN_SIZES)

kernel_name = "hymba_retention_swa_sinks_moe_ple"


def rmsnorm(x, g):
    x32 = x.astype(jnp.float32)
    y = x32 * lax.rsqrt(jnp.mean(x32 * x32, axis=-1, keepdims=True) + EPS)
    return (y * g.astype(jnp.float32)).astype(x.dtype)


def split_cols(u, sizes):
    outs, start = [], 0
    for s in sizes:
        outs.append(u[..., start:start + s])
        start += s
    return outs


def rope(t, positions):
    half = t.shape[-1] // 2
    inv = ROPE_BASE ** (-jnp.arange(half, dtype=jnp.float32) / half)
    ang = positions.astype(jnp.float32)[..., None] * inv
    cos, sin = jnp.cos(ang)[:, :, None, :], jnp.sin(ang)[:, :, None, :]
    t32 = t.astype(jnp.float32)
    t1, t2 = t32[..., :half], t32[..., half:]
    return jnp.concatenate([t1 * cos - t2 * sin, t1 * sin + t2 * cos], axis=-1).astype(t.dtype)


def retention(q, k, v, positions):
    B, S, H, dk = q.shape
    dv = v.shape[-1]
    dt = q.dtype
    q = rope(q, positions)
    k = rope(k, positions) * (dk ** -0.5)
    C = RET_CHUNK
    nc = S // C
    qc = q.reshape(B, nc, C, H, dk)
    kc = k.reshape(B, nc, C, H, dk)
    vc = v.reshape(B, nc, C, H, dv)
    log_g = jnp.log(1.0 - 2.0 ** (-5.0 - jnp.arange(H, dtype=jnp.float32)))
    c = jnp.arange(C)
    diff = (c[:, None] - c[None, :]).astype(jnp.float32)
    dmask = jnp.where(diff >= 0, jnp.exp(jnp.maximum(diff, 0.0)[None] * log_g[:, None, None]), 0.0).astype(dt)
    scores = jnp.einsum('bnchd,bnmhd->bnhcm', qc, kc) * dmask
    inner = jnp.einsum('bnhcm,bnmhe->bnche', scores, vc)
    cf = c.astype(jnp.float32)
    k_decay = jnp.exp((C - 1 - cf)[None, :] * log_g[:, None]).astype(dt)
    q_decay = jnp.exp((cf + 1)[None, :] * log_g[:, None]).astype(dt)
    chunk_decay = jnp.exp(C * log_g).astype(dt)
    kv = jnp.einsum('bnmhd,bnmhe,hm->bnhde', kc, vc, k_decay)

    def step(state, kv_n):
        new = state * chunk_decay[None, :, None, None] + kv_n
        return new, state

    init = jnp.zeros((B, H, dk, dv), kv.dtype)
    _, r_prev = lax.scan(step, init, jnp.moveaxis(kv, 1, 0))
    r_prev = jnp.moveaxis(r_prev, 0, 1)
    cross = jnp.einsum('bnchd,bnhde,hc->bnche', qc, r_prev, q_decay)
    return (inner + cross).reshape(B, S, H, dv)


def t5_bucket(n):
    max_exact = NUM_BUCKETS // 2
    nf = jnp.maximum(n, 1).astype(jnp.float32)
    large = max_exact + (jnp.log(nf / max_exact) / math.log(MAX_DISTANCE / max_exact)
                         * (NUM_BUCKETS - max_exact)).astype(jnp.int32)
    large = jnp.minimum(large, NUM_BUCKETS - 1)
    return jnp.where(n < max_exact, n, large)


def sliding_window_attention(q, k, v, sinks, bias_table):
    B, S, Hq, dh = q.shape
    L = ATT_BLOCK
    nb = S // L
    qb = q.reshape(B, nb, L, ATT_KV_HEADS, ATT_GROUP, dh)

    def band(t):
        tp = jnp.concatenate([jnp.zeros_like(t[:, :L]), t], axis=1)
        tp = tp.reshape(B, nb + 1, L, ATT_KV_HEADS, dh)
        return jnp.concatenate([tp[:, :-1], tp[:, 1:]], axis=2)

    kb, vb = band(k), band(v)
    qi = jnp.arange(L)[:, None]
    ki = jnp.arange(2 * L)[None, :]
    dist = qi + L - ki
    band_ok = (dist >= 0) & (dist < WINDOW)
    key_ok = (jnp.arange(nb)[:, None] * L - L + ki) >= 0
    mask = band_ok[None] & key_ok[:, None, :]
    bucket = t5_bucket(jnp.maximum(dist, 0))
    bias = jnp.transpose(bias_table[bucket], (2, 0, 1)).astype(jnp.float32)
    bias = bias.reshape(ATT_KV_HEADS, ATT_GROUP, L, 2 * L)
    s = jnp.einsum('bnqhgd,bnkhd->bnhgqk', qb, kb).astype(jnp.float32) * (dh ** -0.5) + bias
    s = jnp.where(mask[None, :, None, None], s, jnp.finfo(jnp.float32).min)
    sink = jnp.broadcast_to(sinks.astype(jnp.float32).reshape(1, 1, ATT_KV_HEADS, ATT_GROUP, 1, 1),
                            s.shape[:-1] + (1,))
    probs = jax.nn.softmax(jnp.concatenate([s, sink], axis=-1), axis=-1)[..., :-1].astype(v.dtype)
    o = jnp.einsum('bnhgqk,bnkhd->bnqhgd', probs, vb)
    return o.reshape(B, S, Hq, dh)


def clamped_swiglu(h):
    glu = jnp.minimum(h[..., ::2], SWIGLU_LIMIT)
    lin = jnp.clip(h[..., 1::2], -SWIGLU_LIMIT, SWIGLU_LIMIT)
    return glu * jax.nn.sigmoid(SWIGLU_ALPHA * glu) * (lin + 1.0)


def moe(x, w_router, b_router, w_up, b_up, w_down, b_down):
    B, S, D = x.shape
    N = B * S
    xf = x.reshape(N, D)
    logits = (xf @ w_router + b_router).astype(jnp.float32)
    top_v, top_i = lax.top_k(logits, TOP_K)
    gate = jax.nn.softmax(top_v, axis=-1).astype(x.dtype)
    n_slots = N * TOP_K
    flat_e = top_i.reshape(-1)
    order = jnp.argsort(flat_e, stable=True)
    sorted_e = flat_e[order]
    counts = jnp.bincount(flat_e, length=N_EXPERTS)
    padded = (counts + MOE_BLOCK - 1) // MOE_BLOCK * MOE_BLOCK
    off = jnp.cumsum(counts) - counts
    pend = jnp.cumsum(padded)
    poff = pend - padded
    dest_sorted = poff[sorted_e] + (jnp.arange(n_slots) - off[sorted_e])
    cap = n_slots + N_EXPERTS * MOE_BLOCK
    n_blocks = cap // MOE_BLOCK
    row_token = jnp.full((cap,), N, jnp.int32).at[dest_sorted].set((order // TOP_K).astype(jnp.int32))
    block_e = jnp.minimum(jnp.searchsorted(pend, jnp.arange(n_blocks) * MOE_BLOCK, side='right'),
                          N_EXPERTS - 1)
    x_pad = jnp.concatenate([xf, jnp.zeros((1, D), xf.dtype)], axis=0)
    xb = x_pad[row_token].reshape(n_blocks, MOE_BLOCK, D)

    def expert_block(args):
        xblk, e = args
        h = xblk @ w_up[e] + b_up[e]
        return clamped_swiglu(h) @ w_down[e] + b_down[e]

    yb = lax.map(expert_block, (xb, block_e)).reshape(cap, D)
    slot_dest = jnp.zeros((n_slots,), dest_sorted.dtype).at[order].set(dest_sorted)
    y_slots = yb[slot_dest].reshape(N, TOP_K, D)
    return jnp.einsum('nk,nkd->nd', gate, y_slots).reshape(B, S, D)


def setup_inputs(seed: int = 0) -> dict:
    key = jax.random.key(seed)
    ks = jax.random.split(key, 24)

    def nrm(k, shape, scale):
        return jax.random.normal(k, shape, jnp.float32) * scale

    return {
        "x": nrm(ks[0], (BATCH, SEQ, D_MODEL), 1.0),
        "p": nrm(ks[1], (DEPTH, BATCH, SEQ, PLE_DIM), 1.0),
        "positions": (jnp.arange(SEQ, dtype=jnp.int32)[None, :]
                      + jax.random.randint(ks[2], (BATCH, 1), 0, 1024, jnp.int32)),
        "rel_bias_table": nrm(ks[3], (NUM_BUCKETS, ATT_Q_HEADS), 0.5),
        "g_mix_norm": 1.0 + nrm(ks[4], (DEPTH, D_MODEL), 0.01),
        "w_in": nrm(ks[5], (DEPTH, D_MODEL, IN_WIDTH), D_MODEL ** -0.5),
        "b_in": nrm(ks[6], (DEPTH, IN_WIDTH), 0.01),
        "ret_norm_g": 1.0 + nrm(ks[7], (DEPTH, RET_HEADS, RET_DV), 0.01),
        "att_sinks": nrm(ks[8], (DEPTH, ATT_Q_HEADS), 0.5),
        "w_out": nrm(ks[9], (DEPTH, MIX_WIDTH, D_MODEL), MIX_WIDTH ** -0.5),
        "g_moe_norm": 1.0 + nrm(ks[10], (DEPTH, D_MODEL), 0.01),
        "w_router": nrm(ks[11], (DEPTH, D_MODEL, N_EXPERTS), D_MODEL ** -0.5),
        "b_router": nrm(ks[12], (DEPTH, N_EXPERTS), 0.01),
        "w_up": nrm(ks[13], (DEPTH, N_EXPERTS, D_MODEL, 2 * D_FF), D_MODEL ** -0.5),
        "b_up": nrm(ks[14], (DEPTH, N_EXPERTS, 2 * D_FF), 0.01),
        "w_down": nrm(ks[15], (DEPTH, N_EXPERTS, D_FF, D_MODEL), D_FF ** -0.5),
        "b_down": nrm(ks[16], (DEPTH, N_EXPERTS, D_MODEL), 0.01),
        "w_ple_gate": nrm(ks[17], (DEPTH, D_MODEL, D_MODEL), D_MODEL ** -0.5),
        "w_ple_proj": nrm(ks[18], (DEPTH, PLE_DIM, D_MODEL), PLE_DIM ** -0.5),
        "g_ple_norm": 1.0 + nrm(ks[19], (DEPTH, D_MODEL), 0.01),
        "g_final": 1.0 + nrm(ks[20], (D_MODEL,), 0.01),
    }


def reference(x, p, positions, rel_bias_table, g_mix_norm, w_in, b_in, ret_norm_g, att_sinks,
              w_out, g_moe_norm, w_router, b_router, w_up, b_up, w_down, b_down,
              w_ple_gate, w_ple_proj, g_ple_norm, g_final):
    B, S, _ = x.shape
    h = x
    for i in range(DEPTH):
        hn = rmsnorm(h, g_mix_norm[i])
        u = hn @ w_in[i] + b_in[i]
        rq, rk, rv, rg, aq, ak, av = split_cols(u, IN_SIZES)
        ret = retention(rq.reshape(B, S, RET_HEADS, RET_DK), rk.reshape(B, S, RET_HEADS, RET_DK),
                        rv.reshape(B, S, RET_HEADS, RET_DV), positions)
        ret = rmsnorm(ret, ret_norm_g[i]).reshape(B, S, RET_WIDTH)
        ret = jax.nn.silu(rg) * ret
        att = sliding_window_attention(aq.reshape(B, S, ATT_Q_HEADS, ATT_HEAD_DIM),
                                       ak.reshape(B, S, ATT_KV_HEADS, ATT_HEAD_DIM),
                                       av.reshape(B, S, ATT_KV_HEADS, ATT_HEAD_DIM),
                                       att_sinks[i], rel_bias_table).reshape(B, S, ATT_WIDTH)
        h = h + jnp.concatenate([ret, att], axis=-1) @ w_out[i]
        h = h + moe(rmsnorm(h, g_moe_norm[i]), w_router[i], b_router[i],
                    w_up[i], b_up[i], w_down[i], b_down[i])
        gate = jax.nn.sigmoid(h @ w_ple_gate[i])
        h = h + rmsnorm(gate * (p[i] @ w_ple_proj[i]), g_ple_norm[i])
    return rmsnorm(h, g_final)
```

```python
import functools
import math

import jax
import jax.numpy as jnp
from jax import lax
from jax.experimental import pallas as pl
from jax.experimental.pallas import tpu as pltpu

RET_HEADS = 4
RET_DK = 128
RET_DV = 128
RET_CHUNK = 128
ROPE_BASE = 10000.0
ATT_Q_HEADS = 8
ATT_KV_HEADS = 2
ATT_GROUP = ATT_Q_HEADS // ATT_KV_HEADS
ATT_HEAD_DIM = 64
WINDOW = 128
ATT_BLOCK = 128
NUM_BUCKETS = 32
MAX_DISTANCE = 128
N_EXPERTS = 32
TOP_K = 4
SWIGLU_ALPHA = 1.702
SWIGLU_LIMIT = 7.0
EPS = 1e-5

RET_WIDTH = RET_HEADS * RET_DV
ATT_WIDTH = ATT_Q_HEADS * ATT_HEAD_DIM
KV_WIDTH = ATT_KV_HEADS * ATT_HEAD_DIM

VMEM_LIMIT_BYTES = 48 * 1024 * 1024

TOKEN_TILE = 512
COMBINE_TILE = 256
MOE_BLOCK = 256

F32 = jnp.float32
BF16 = jnp.bfloat16
NEG = float(jnp.finfo(jnp.float32).min)


def _rms(x, g):
    return x * lax.rsqrt(jnp.mean(x * x, axis=-1, keepdims=True) + EPS) * g


def _dot(a, b):
    return jnp.dot(a, b, preferred_element_type=F32)


def _dot_nt(a, b):
    return lax.dot_general(a, b, (((1,), (1,)), ((), ())), preferred_element_type=F32)


def _dot_tn(a, b):
    return lax.dot_general(a, b, (((0,), (0,)), ((), ())), preferred_element_type=F32)


def _params(semantics):
    return pltpu.CompilerParams(dimension_semantics=semantics, vmem_limit_bytes=VMEM_LIMIT_BYTES)


def _in_proj_kernel(x_ref, g_ref, w_ref, b_ref, cos_ref, sin_ref,
                    qr_ref, kr_ref, vr_ref, gr_ref, qa_ref, kva_ref):
    hn = _rms(x_ref[...], g_ref[...]).astype(BF16)
    cos = cos_ref[...]
    sin = sin_ref[...]

    def proj(lo, hi):
        return _dot(hn, w_ref[:, lo:hi]) + b_ref[:, lo:hi]

    def rope(t):
        heads = []
        for h in range(RET_HEADS):
            th = t[:, h * RET_DK:(h + 1) * RET_DK]
            heads.append(th * cos + pltpu.roll(th, RET_DK // 2, 1) * sin)
        return jnp.concatenate(heads, axis=1)

    o = 0
    qr_ref[...] = rope(proj(o, o + RET_WIDTH)).astype(BF16)
    o += RET_WIDTH
    kr_ref[...] = (rope(proj(o, o + RET_WIDTH)) * (RET_DK ** -0.5)).astype(BF16)
    o += RET_WIDTH
    vr_ref[...] = proj(o, o + RET_WIDTH).astype(BF16)
    o += RET_WIDTH
    gr_ref[...] = proj(o, o + RET_WIDTH).astype(BF16)
    o += RET_WIDTH
    qa_ref[...] = (proj(o, o + ATT_WIDTH) * (ATT_HEAD_DIM ** -0.5)).astype(BF16)
    o += ATT_WIDTH
    kva_ref[...] = proj(o, o + 2 * KV_WIDTH).astype(BF16)


def _in_proj(x2, g, w_in, b_in, cosf, sinf):
    n, d = x2.shape
    width = w_in.shape[1]
    tm = TOKEN_TILE
    row = lambda i: (i, 0)
    fixed = lambda i: (0, 0)
    outs = [RET_WIDTH, RET_WIDTH, RET_WIDTH, RET_WIDTH, ATT_WIDTH, 2 * KV_WIDTH]
    return pl.pallas_call(
        _in_proj_kernel,
        grid=(n // tm,),
        in_specs=[
            pl.BlockSpec((tm, d), row),
            pl.BlockSpec((1, d), fixed),
            pl.BlockSpec((d, width), fixed),
            pl.BlockSpec((1, width), fixed),
            pl.BlockSpec((tm, RET_DK), row),
            pl.BlockSpec((tm, RET_DK), row),
        ],
        out_specs=[pl.BlockSpec((tm, w), row) for w in outs],
        out_shape=[jax.ShapeDtypeStruct((n, w), BF16) for w in outs],
        compiler_params=_params(("parallel",)),
        name="in_proj",
    )(x2, g, w_in, b_in, cosf, sinf)


def _retention_kernel(q_ref, k_ref, v_ref, g_ref, dmask_ref, qdec_ref, kdec_ref, cdec_ref,
                      gn_ref, o_ref, state_ref):
    batch = q_ref.shape[0]

    @pl.when(pl.program_id(0) == 0)
    def _():
        state_ref[...] = jnp.zeros_like(state_ref)

    def per_batch(b, carry):
        for h in range(RET_HEADS):
            sl = slice(h * RET_DK, (h + 1) * RET_DK)
            q = q_ref[b, :, sl]
            k = k_ref[b, :, sl]
            v = v_ref[b, :, sl]
            scores = _dot_nt(q, k) * dmask_ref[h]
            inner = _dot(scores.astype(BF16), v)
            state = state_ref[b * RET_HEADS + h]
            cross = _dot(q, state.astype(BF16)) * qdec_ref[h]
            kd = (k.astype(F32) * kdec_ref[h]).astype(BF16)
            state_ref[b * RET_HEADS + h] = state * cdec_ref[h] + _dot_tn(kd, v)
            o = _rms(inner + cross, gn_ref[h])
            gate = g_ref[b, :, sl].astype(F32)
            o_ref[b, :, sl] = (gate * jax.nn.sigmoid(gate) * o).astype(BF16)
        return carry

    lax.fori_loop(0, batch, per_batch, 0)


def _retention(q, k, v, g, gn):
    b, s, w = q.shape
    c = RET_CHUNK
    log_g = jnp.log(1.0 - 2.0 ** (-5.0 - jnp.arange(RET_HEADS, dtype=F32)))
    ci = jnp.arange(c)
    diff = (ci[:, None] - ci[None, :]).astype(F32)
    dmask = jnp.where(diff >= 0, jnp.exp(jnp.maximum(diff, 0.0)[None] * log_g[:, None, None]), 0.0)
    cf = ci.astype(F32)
    kdec = jnp.exp((c - 1 - cf)[None, :] * log_g[:, None])
    qdec = jnp.exp((cf + 1)[None, :] * log_g[:, None])
    cdec = jnp.exp(c * log_g)
    kdec = jnp.broadcast_to(kdec[:, :, None], (RET_HEADS, c, RET_DK))
    qdec = jnp.broadcast_to(qdec[:, :, None], (RET_HEADS, c, RET_DV))
    cdec = jnp.broadcast_to(cdec[:, None, None], (RET_HEADS, RET_DK, RET_DV))
    gn3 = gn.reshape(RET_HEADS, 1, RET_DV)

    seq = lambda i: (0, i, 0)
    fixed = lambda i: (0, 0, 0)
    table = lambda shape: pl.BlockSpec(shape, fixed)
    return pl.pallas_call(
        _retention_kernel,
        grid=(s // c,),
        in_specs=[pl.BlockSpec((b, c, w), seq)] * 4 + [
            table((RET_HEADS, c, c)),
            table((RET_HEADS, c, RET_DV)),
            table((RET_HEADS, c, RET_DK)),
            table((RET_HEADS, RET_DK, RET_DV)),
            table((RET_HEADS, 1, RET_DV)),
        ],
        out_specs=pl.BlockSpec((b, c, w), seq),
        out_shape=jax.ShapeDtypeStruct((b, s, w), BF16),
        scratch_shapes=[pltpu.VMEM((b * RET_HEADS, RET_DK, RET_DV), F32)],
        compiler_params=_params(("arbitrary",)),
        name="retention",
    )(q, k, v, g, dmask, qdec, kdec, cdec, gn3)


def _attention_kernel(sinks_ref, q_ref, kvp_ref, kvc_ref, bias_ref, band_ref, o_ref):
    blk = ATT_BLOCK
    dh = ATT_HEAD_DIM
    j = pl.program_id(1)
    col = lax.broadcasted_iota(jnp.int32, (1, 2 * blk), 1)
    key_ok = jnp.logical_or(col >= blk, j > 0)
    ok = jnp.logical_and(band_ref[...] > 0.5, key_ok)
    q = q_ref[0]
    kvp = kvp_ref[0]
    kvc = kvc_ref[0]
    outs = []
    for g in range(ATT_KV_HEADS):
        ks = slice(g * dh, (g + 1) * dh)
        vs = slice(KV_WIDTH + g * dh, KV_WIDTH + (g + 1) * dh)
        k = jnp.concatenate([kvp[:, ks], kvc[:, ks]], axis=0)
        v = jnp.concatenate([kvp[:, vs], kvc[:, vs]], axis=0)
        heads = range(g * ATT_GROUP, (g + 1) * ATT_GROUP)
        qs = jnp.concatenate([q[:, h * dh:(h + 1) * dh] for h in heads], axis=0)
        s = jnp.where(ok, _dot_nt(qs, k) + bias_ref[g], NEG)
        sink = jnp.concatenate([jnp.full((blk, 1), sinks_ref[h], F32) for h in heads], axis=0)
        m = jnp.maximum(jnp.max(s, axis=1, keepdims=True), sink)
        p = jnp.exp(s - m)
        denom = jnp.sum(p, axis=1, keepdims=True) + jnp.exp(sink - m)
        o = _dot(p.astype(BF16), v) / denom
        outs.extend(o[i * blk:(i + 1) * blk] for i in range(ATT_GROUP))
    o_ref[0] = jnp.concatenate(outs, axis=1).astype(BF16)


def _t5_bucket(n):
    max_exact = NUM_BUCKETS // 2
    nf = jnp.maximum(n, 1).astype(F32)
    large = max_exact + (jnp.log(nf / max_exact) / math.log(MAX_DISTANCE / max_exact)
                         * (NUM_BUCKETS - max_exact)).astype(jnp.int32)
    large = jnp.minimum(large, NUM_BUCKETS - 1)
    return jnp.where(n < max_exact, n, large)


def _attention(q, kv, sinks, bias_table):
    b, s, _ = q.shape
    blk = ATT_BLOCK
    qi = jnp.arange(blk)[:, None]
    ki = jnp.arange(2 * blk)[None, :]
    dist = qi + blk - ki
    band = ((dist >= 0) & (dist < WINDOW)).astype(F32)
    band = jnp.tile(band, (ATT_GROUP, 1))
    bias = jnp.transpose(bias_table[_t5_bucket(jnp.maximum(dist, 0))], (2, 0, 1)).astype(F32)
    bias = bias.reshape(ATT_KV_HEADS, ATT_GROUP * blk, 2 * blk)

    return pl.pallas_call(
        _attention_kernel,
        grid=(b, s // blk),
        in_specs=[
            pl.BlockSpec(memory_space=pltpu.SMEM),
            pl.BlockSpec((1, blk, ATT_WIDTH), lambda i, j: (i, j, 0)),
            pl.BlockSpec((1, blk, 2 * KV_WIDTH), lambda i, j: (i, jnp.maximum(j - 1, 0), 0)),
            pl.BlockSpec((1, blk, 2 * KV_WIDTH), lambda i, j: (i, j, 0)),
            pl.BlockSpec((ATT_KV_HEADS, ATT_GROUP * blk, 2 * blk), lambda i, j: (0, 0, 0)),
            pl.BlockSpec((ATT_GROUP * blk, 2 * blk), lambda i, j: (0, 0)),
        ],
        out_specs=pl.BlockSpec((1, blk, ATT_WIDTH), lambda i, j: (i, j, 0)),
        out_shape=jax.ShapeDtypeStruct((b, s, ATT_WIDTH), BF16),
        compiler_params=_params(("parallel", "arbitrary")),
        name="attention",
    )(sinks, q, kv, kv, bias, band)


def _pack_bf16_pairs(lo, hi):
    return pltpu.bitcast(hi, jnp.uint32) | (pltpu.bitcast(lo, jnp.uint32) >> 16)


def _unpack_bf16_pairs(packed):
    lo = pltpu.bitcast(packed << 16, F32).astype(BF16)
    hi = pltpu.bitcast(packed & jnp.uint32(0xFFFF0000), F32).astype(BF16)
    return lo, hi


def _columns(cols, dtype):
    t = cols[0].shape[0]
    lane = lax.broadcasted_iota(jnp.int32, (t, len(cols)), 1)
    out = jnp.zeros((t, len(cols)), dtype)
    for i, c in enumerate(cols):
        out = jnp.where(lane == i, c.astype(dtype), out)
    return out


def _out_router_kernel(ret_ref, att_ref, x_ref, wo_ref, g_ref, wr_ref, br_ref,
                       h1_ref, xp_ref, topi_ref, rank_ref, gate_ref, cnt_ref, carry_ref):
    tm = x_ref.shape[0]
    half = x_ref.shape[1] // 2

    @pl.when(pl.program_id(0) == 0)
    def _():
        carry_ref[...] = jnp.zeros_like(carry_ref)

    h1 = (x_ref[...] + _dot(ret_ref[...], wo_ref[:RET_WIDTH, :])
          + _dot(att_ref[...], wo_ref[RET_WIDTH:, :]))
    h1_ref[...] = h1
    xb = _rms(h1, g_ref[...]).astype(BF16)
    xp_ref[...] = _pack_bf16_pairs(xb[:, :half].astype(F32), xb[:, half:].astype(F32))

    logits = _dot(xb, wr_ref[...]) + br_ref[...]
    lane = lax.broadcasted_iota(jnp.int32, logits.shape, 1)
    vals = logits
    top_v, top_i, sels = [], [], []
    for _ in range(TOP_K):
        m = jnp.max(vals, axis=1, keepdims=True)
        idx = jnp.min(jnp.where(vals == m, lane, N_EXPERTS), axis=1, keepdims=True)
        sel = lane == idx
        vals = jnp.where(sel, -jnp.inf, vals)
        top_v.append(m)
        top_i.append(idx)
        sels.append(sel)
    exps = [jnp.exp(v - top_v[0]) for v in top_v]
    denom = exps[0] + exps[1] + exps[2] + exps[3]
    gates = [e / denom for e in exps]

    chosen = jnp.zeros(logits.shape, F32)
    for sel in sels:
        chosen = chosen + sel.astype(F32)
    r = lax.broadcasted_iota(jnp.int32, (tm, tm), 0)
    c = lax.broadcasted_iota(jnp.int32, (tm, tm), 1)
    lower = jnp.where(r > c, 1.0, 0.0).astype(BF16)
    before = _dot(lower, chosen.astype(BF16)) + carry_ref[...]
    ranks = [jnp.sum(jnp.where(sel, before, 0.0), axis=1, keepdims=True) for sel in sels]
    carry_ref[...] = carry_ref[...] + jnp.sum(chosen, axis=0, keepdims=True)

    topi_ref[...] = _columns(top_i, jnp.int32)
    rank_ref[...] = _columns(ranks, jnp.int32)
    gate_ref[...] = _columns(gates, F32)
    cnt_ref[...] = carry_ref[...]


def _out_router(ret, att, x2, w_out, g, w_router, b_router):
    n, d = x2.shape
    tm = TOKEN_TILE
    row = lambda i: (i, 0)
    fixed = lambda i: (0, 0)
    return pl.pallas_call(
        _out_router_kernel,
        grid=(n // tm,),
        in_specs=[
            pl.BlockSpec((tm, RET_WIDTH), row),
            pl.BlockSpec((tm, ATT_WIDTH), row),
            pl.BlockSpec((tm, d), row),
            pl.BlockSpec((RET_WIDTH + ATT_WIDTH, d), fixed),
            pl.BlockSpec((1, d), fixed),
            pl.BlockSpec((d, N_EXPERTS), fixed),
            pl.BlockSpec((1, N_EXPERTS), fixed),
        ],
        out_specs=[
            pl.BlockSpec((tm, d), row),
            pl.BlockSpec((tm, d // 2), row),
            pl.BlockSpec((tm, TOP_K), row),
            pl.BlockSpec((tm, TOP_K), row),
            pl.BlockSpec((tm, TOP_K), row),
            pl.BlockSpec((1, N_EXPERTS), fixed),
        ],
        out_shape=[
            jax.ShapeDtypeStruct((n, d), F32),
            jax.ShapeDtypeStruct((n, d // 2), jnp.uint32),
            jax.ShapeDtypeStruct((n, TOP_K), jnp.int32),
            jax.ShapeDtypeStruct((n, TOP_K), jnp.int32),
            jax.ShapeDtypeStruct((n, TOP_K), F32),
            jax.ShapeDtypeStruct((1, N_EXPERTS), F32),
        ],
        scratch_shapes=[pltpu.VMEM((1, N_EXPERTS), F32)],
        compiler_params=_params(("arbitrary",)),
        name="out_router",
    )(ret, att, x2, w_out, g, w_router, b_router)


def _index_copy(dest_hbm, idx_smem, isem, step, slot):
    per = idx_smem.shape[1]
    return pltpu.make_async_copy(dest_hbm.at[pl.ds(step * per, per)], idx_smem.at[slot], isem.at[slot])


def _prefetch_indices(dest_hbm, idx_smem, isem):
    i = pl.program_id(0)
    slot = lax.rem(i, 2)

    @pl.when(i == 0)
    def _():
        _index_copy(dest_hbm, idx_smem, isem, 0, 0).start()

    @pl.when(i + 1 < pl.num_programs(0))
    def _():
        _index_copy(dest_hbm, idx_smem, isem, i + 1, 1 - slot).start()

    _index_copy(dest_hbm, idx_smem, isem, i, slot).wait()
    return slot


def _dispatch_kernel(dest_hbm, xp_ref, xs_in, xs_out, idx_smem, isem, sem):
    del xs_in
    tm = xp_ref.shape[0]
    slot = _prefetch_indices(dest_hbm, idx_smem, isem)

    def row_copy(t, d):
        return pltpu.make_async_copy(xp_ref.at[pl.ds(t, 1), :], xs_out.at[pl.ds(d, 1), :], sem)

    def body(t, carry):
        for k in range(TOP_K):
            row_copy(t, idx_smem[slot, t * TOP_K + k]).start()
        return carry

    lax.fori_loop(0, tm, body, 0, unroll=8)
    for _ in range(TOP_K):
        pltpu.make_async_copy(xp_ref, xs_out.at[pl.ds(0, tm), :], sem).wait()


def _dispatch(dest_flat, xp, cap):
    n, half = xp.shape
    tm = TOKEN_TILE
    xs0 = jnp.zeros((cap, half), jnp.uint32)
    return pl.pallas_call(
        _dispatch_kernel,
        grid=(n // tm,),
        in_specs=[
            pl.BlockSpec(memory_space=pl.ANY),
            pl.BlockSpec((tm, half), lambda i: (i, 0)),
            pl.BlockSpec(memory_space=pl.ANY),
        ],
        out_specs=pl.BlockSpec(memory_space=pl.ANY),
        out_shape=jax.ShapeDtypeStruct((cap, half), jnp.uint32),
        scratch_shapes=[
            pltpu.SMEM((2, tm * TOP_K), jnp.int32),
            pltpu.SemaphoreType.DMA((2,)),
            pltpu.SemaphoreType.DMA,
        ],
        input_output_aliases={2: 0},
        compiler_params=_params(("arbitrary",)),
        name="dispatch",
    )(dest_flat, xp, xs0)


def _moe_kernel(be_ref, nused_ref, xs_ref, wg_ref, wl_ref, bg_ref, bl_ref, wd_ref, bd_ref, ys_ref):
    del be_ref
    half = xs_ref.shape[1]
    used = pl.program_id(0) < nused_ref[0]

    @pl.when(jnp.logical_not(used))
    def _():
        ys_ref[...] = jnp.zeros_like(ys_ref)

    @pl.when(used)
    def _():
        lo, hi = _unpack_bf16_pairs(xs_ref[...])

        def up(w_ref, b_ref):
            return _dot(lo, w_ref[:half, :]) + _dot(hi, w_ref[half:, :]) + b_ref[...]

        glu = jnp.minimum(up(wg_ref, bg_ref), SWIGLU_LIMIT)
        lin = jnp.clip(up(wl_ref, bl_ref), -SWIGLU_LIMIT, SWIGLU_LIMIT)
        act = glu * jax.nn.sigmoid(SWIGLU_ALPHA * glu) * (lin + 1.0)
        ys_ref[...] = _dot(act.astype(BF16), wd_ref[...]) + bd_ref[...]


def _moe(block_e, n_used, xs, wg, wl, bg, bl, wd, bd):
    cap, half = xs.shape
    e, d, f = wg.shape
    bm = MOE_BLOCK
    rows = lambda i, be, nu: (i, 0)
    expert3 = lambda i, be, nu: (be[i], 0, 0)
    grid_spec = pltpu.PrefetchScalarGridSpec(
        num_scalar_prefetch=2,
        grid=(cap // bm,),
        in_specs=[
            pl.BlockSpec((bm, half), rows),
            pl.BlockSpec((None, d, f), expert3),
            pl.BlockSpec((None, d, f), expert3),
            pl.BlockSpec((None, 1, f), expert3),
            pl.BlockSpec((None, 1, f), expert3),
            pl.BlockSpec((None, f, d), expert3),
            pl.BlockSpec((None, 1, d), expert3),
        ],
        out_specs=pl.BlockSpec((bm, d), rows),
    )
    return pl.pallas_call(
        _moe_kernel,
        grid_spec=grid_spec,
        out_shape=jax.ShapeDtypeStruct((cap, d), F32),
        compiler_params=_params(("arbitrary",)),
        name="moe",
    )(block_e, n_used, xs, wg, wl, bg, bl, wd, bd)


def _combine_ple_kernel(dest_hbm, ys_hbm, h1_ref, gate_ref, p_ref, wpg_ref, wpp_ref, gple_ref,
                        gfin_ref, o_ref, idx_smem, isem, ybuf, sem):
    tm = h1_ref.shape[0]
    slot = _prefetch_indices(dest_hbm, idx_smem, isem)

    def body(t, carry):
        for k in range(TOP_K):
            d = idx_smem[slot, t * TOP_K + k]
            pltpu.make_async_copy(ys_hbm.at[pl.ds(d, 1), :], ybuf.at[k, pl.ds(t, 1), :], sem).start()
        return carry

    lax.fori_loop(0, tm, body, 0, unroll=8)
    for k in range(TOP_K):
        pltpu.make_async_copy(ys_hbm.at[pl.ds(0, tm), :], ybuf.at[k], sem).wait()

    gates = gate_ref[...]
    h2 = h1_ref[...]
    for k in range(TOP_K):
        h2 = h2 + gates[:, k:k + 1] * ybuf[k]
    gate = jax.nn.sigmoid(_dot(h2.astype(BF16), wpg_ref[...]))
    proj = _dot(p_ref[...].astype(BF16), wpp_ref[...])
    h3 = h2 + _rms(gate * proj, gple_ref[...])
    o_ref[...] = _rms(h3, gfin_ref[...])


def _combine_ple(dest_flat, ys, h1, gates, p2, wpg, wpp, gple, gfin):
    n, d = h1.shape
    pd = p2.shape[1]
    tm = COMBINE_TILE
    row = lambda i: (i, 0)
    fixed = lambda i: (0, 0)
    return pl.pallas_call(
        _combine_ple_kernel,
        grid=(n // tm,),
        in_specs=[
            pl.BlockSpec(memory_space=pl.ANY),
            pl.BlockSpec(memory_space=pl.ANY),
            pl.BlockSpec((tm, d), row),
            pl.BlockSpec((tm, TOP_K), row),
            pl.BlockSpec((tm, pd), row),
            pl.BlockSpec((d, d), fixed),
            pl.BlockSpec((pd, d), fixed),
            pl.BlockSpec((1, d), fixed),
            pl.BlockSpec((1, d), fixed),
        ],
        out_specs=pl.BlockSpec((tm, d), row),
        out_shape=jax.ShapeDtypeStruct((n, d), F32),
        scratch_shapes=[
            pltpu.SMEM((2, tm * TOP_K), jnp.int32),
            pltpu.SemaphoreType.DMA((2,)),
            pltpu.VMEM((TOP_K, tm, d), F32),
            pltpu.SemaphoreType.DMA,
        ],
        compiler_params=_params(("arbitrary",)),
        name="combine_ple",
    )(dest_flat, ys, h1, gates, p2, wpg, wpp, gple, gfin)


def kernel(x, p, positions, rel_bias_table, g_mix_norm, w_in, b_in, ret_norm_g, att_sinks, w_out,
           g_moe_norm, w_router, b_router, w_up, b_up, w_down, b_down, w_ple_gate, w_ple_proj,
           g_ple_norm, g_final):
    b, s, d = x.shape
    depth = w_in.shape[0]
    assert depth == 1, "single-layer stack only"
    n = b * s
    x2 = x.reshape(n, d)

    half = RET_DK // 2
    inv = ROPE_BASE ** (-jnp.arange(half, dtype=F32) / half)
    ang = positions.astype(F32)[..., None] * inv
    cos, sin = jnp.cos(ang), jnp.sin(ang)
    cosf = jnp.concatenate([cos, cos], axis=-1).reshape(n, RET_DK)
    sinf = jnp.concatenate([-sin, sin], axis=-1).reshape(n, RET_DK)

    qr, kr, vr, gr, qa, kva = _in_proj(
        x2, g_mix_norm[0][None], w_in[0].astype(BF16), b_in[0][None], cosf, sinf)

    shape3 = lambda t: t.reshape(b, s, t.shape[-1])
    ret = _retention(shape3(qr), shape3(kr), shape3(vr), shape3(gr), ret_norm_g[0])
    att = _attention(shape3(qa), shape3(kva), att_sinks[0], rel_bias_table)

    h1, xp, top_i, rank, gates, counts = _out_router(
        ret.reshape(n, RET_WIDTH), att.reshape(n, ATT_WIDTH), x2, w_out[0].astype(BF16),
        g_moe_norm[0][None], w_router[0].astype(BF16), b_router[0][None])

    bm = MOE_BLOCK
    cap = n * TOP_K + N_EXPERTS * bm
    counts = counts[0].astype(jnp.int32)
    padded = (counts + bm - 1) // bm * bm
    pend = jnp.cumsum(padded)
    poff = pend - padded
    experts = jnp.arange(N_EXPERTS, dtype=jnp.int32)
    dest = rank + jnp.sum(jnp.where(top_i[..., None] == experts, poff, 0), axis=-1)
    dest_flat = dest.reshape(-1).astype(jnp.int32)
    block_start = jnp.arange(cap // bm, dtype=jnp.int32) * bm
    block_e = jnp.minimum(jnp.sum(block_start[:, None] >= pend[None, :], axis=1), N_EXPERTS - 1)
    n_used = (pend[-1:] // bm).astype(jnp.int32)

    xs = _dispatch(dest_flat, xp, cap)

    wu = w_up[0].astype(BF16)
    ys = _moe(block_e.astype(jnp.int32), n_used, xs,
              wu[:, :, 0::2], wu[:, :, 1::2],
              b_up[0][:, None, 0::2], b_up[0][:, None, 1::2],
              w_down[0].astype(BF16), b_down[0].reshape(N_EXPERTS, 1, d))

    out = _combine_ple(dest_flat, ys, h1, gates, p[0].reshape(n, -1),
                       w_ple_gate[0].astype(BF16), w_ple_proj[0].astype(BF16),
                       g_ple_norm[0][None], g_final[None])
    return out.reshape(b, s, d)
```

```python
import functools
import math

import jax
import jax.numpy as jnp
from jax import lax
from jax.experimental import pallas as pl
from jax.experimental.pallas import tpu as pltpu

RET_HEADS = 4
RET_DK = 128
RET_DV = 128
RET_CHUNK = 128
ROPE_BASE = 10000.0
ATT_Q_HEADS = 8
ATT_KV_HEADS = 2
ATT_GROUP = ATT_Q_HEADS // ATT_KV_HEADS
ATT_HEAD_DIM = 64
WINDOW = 128
ATT_BLOCK = 128
NUM_BUCKETS = 32
MAX_DISTANCE = 128
N_EXPERTS = 32
TOP_K = 4
SWIGLU_ALPHA = 1.702
SWIGLU_LIMIT = 7.0
EPS = 1e-5

RET_WIDTH = RET_HEADS * RET_DV
ATT_WIDTH = ATT_Q_HEADS * ATT_HEAD_DIM
KV_WIDTH = ATT_KV_HEADS * ATT_HEAD_DIM

VMEM_LIMIT_BYTES = 48 * 1024 * 1024

TOKEN_TILE = 512
COMBINE_TILE = 256
MOE_BLOCK = 256
ATT_STEP_BLOCKS = 4
PERM_CHUNK = 256

F32 = jnp.float32
BF16 = jnp.bfloat16
NEG = float(jnp.finfo(jnp.float32).min)


def _rms(x, g):
    return x * lax.rsqrt(jnp.mean(x * x, axis=-1, keepdims=True) + EPS) * g


def _dot(a, b):
    return jnp.dot(a, b, preferred_element_type=F32)


def _dot_nt(a, b):
    return lax.dot_general(a, b, (((1,), (1,)), ((), ())), preferred_element_type=F32)


def _dot_tn(a, b):
    return lax.dot_general(a, b, (((0,), (0,)), ((), ())), preferred_element_type=F32)


def _params(semantics):
    return pltpu.CompilerParams(dimension_semantics=semantics, vmem_limit_bytes=VMEM_LIMIT_BYTES)


def _in_proj_kernel(x_ref, g_ref, w_ref, b_ref, cos_ref, sin_ref,
                    qr_ref, kr_ref, vr_ref, gr_ref, qa_ref, kva_ref):
    hn = _rms(x_ref[...], g_ref[...]).astype(BF16)
    cos = cos_ref[...]
    sin = sin_ref[...]

    def proj(lo, hi):
        return _dot(hn, w_ref[:, lo:hi]) + b_ref[:, lo:hi]

    def rope(t):
        heads = []
        for h in range(RET_HEADS):
            th = t[:, h * RET_DK:(h + 1) * RET_DK]
            heads.append(th * cos + pltpu.roll(th, RET_DK // 2, 1) * sin)
        return jnp.concatenate(heads, axis=1)

    o = 0
    qr_ref[...] = rope(proj(o, o + RET_WIDTH)).astype(BF16)
    o += RET_WIDTH
    kr_ref[...] = (rope(proj(o, o + RET_WIDTH)) * (RET_DK ** -0.5)).astype(BF16)
    o += RET_WIDTH
    vr_ref[...] = proj(o, o + RET_WIDTH).astype(BF16)
    o += RET_WIDTH
    gr_ref[...] = proj(o, o + RET_WIDTH).astype(BF16)
    o += RET_WIDTH
    qa_ref[...] = (proj(o, o + ATT_WIDTH) * (ATT_HEAD_DIM ** -0.5)).astype(BF16)
    o += ATT_WIDTH
    low = lax.broadcasted_iota(jnp.int32, (1, KV_WIDTH), 1) < ATT_HEAD_DIM
    slabs = []
    for part in range(2):
        t = proj(o + part * KV_WIDTH, o + (part + 1) * KV_WIDTH)
        swapped = pltpu.roll(t, ATT_HEAD_DIM, 1)
        slabs += [jnp.where(low, t, swapped), jnp.where(low, swapped, t)]
    kva_ref[...] = jnp.concatenate(slabs, axis=1).astype(BF16)


def _in_proj(x2, g, w_in, b_in, cosf, sinf):
    n, d = x2.shape
    width = w_in.shape[1]
    tm = TOKEN_TILE
    row = lambda i: (i, 0)
    fixed = lambda i: (0, 0)
    outs = [RET_WIDTH, RET_WIDTH, RET_WIDTH, RET_WIDTH, ATT_WIDTH, 4 * KV_WIDTH]
    return pl.pallas_call(
        _in_proj_kernel,
        grid=(n // tm,),
        in_specs=[
            pl.BlockSpec((tm, d), row),
            pl.BlockSpec((1, d), fixed),
            pl.BlockSpec((d, width), fixed),
            pl.BlockSpec((1, width), fixed),
            pl.BlockSpec((tm, RET_DK), row),
            pl.BlockSpec((tm, RET_DK), row),
        ],
        out_specs=[pl.BlockSpec((tm, w), row) for w in outs],
        out_shape=[jax.ShapeDtypeStruct((n, w), BF16) for w in outs],
        compiler_params=_params(("parallel",)),
        name="in_proj",
    )(x2, g, w_in, b_in, cosf, sinf)


def _retention_kernel(q_ref, k_ref, v_ref, g_ref, dmask_ref, qdec_ref, kdec_ref, cdec_ref,
                      gn_ref, o_ref, state_ref):
    batch = q_ref.shape[0]

    @pl.when(pl.program_id(0) == 0)
    def _():
        state_ref[...] = jnp.zeros_like(state_ref)

    def per_batch(b, carry):
        for h in range(RET_HEADS):
            sl = slice(h * RET_DK, (h + 1) * RET_DK)
            q = q_ref[b, :, sl]
            k = k_ref[b, :, sl]
            v = v_ref[b, :, sl]
            scores = _dot_nt(q, k) * dmask_ref[h]
            inner = _dot(scores.astype(BF16), v)
            state = state_ref[b * RET_HEADS + h]
            cross = _dot(q, state.astype(BF16)) * qdec_ref[h]
            kd = (k.astype(F32) * kdec_ref[h]).astype(BF16)
            state_ref[b * RET_HEADS + h] = state * cdec_ref[h] + _dot_tn(kd, v)
            o = _rms(inner + cross, gn_ref[h])
            gate = g_ref[b, :, sl].astype(F32)
            o_ref[b, :, sl] = (gate * jax.nn.sigmoid(gate) * o).astype(BF16)
        return carry

    lax.fori_loop(0, batch, per_batch, 0)


def _retention(q, k, v, g, gn):
    b, s, w = q.shape
    c = RET_CHUNK
    log_g = jnp.log(1.0 - 2.0 ** (-5.0 - jnp.arange(RET_HEADS, dtype=F32)))
    ci = jnp.arange(c)
    diff = (ci[:, None] - ci[None, :]).astype(F32)
    dmask = jnp.where(diff >= 0, jnp.exp(jnp.maximum(diff, 0.0)[None] * log_g[:, None, None]), 0.0)
    cf = ci.astype(F32)
    kdec = jnp.exp((c - 1 - cf)[None, :] * log_g[:, None])
    qdec = jnp.exp((cf + 1)[None, :] * log_g[:, None])
    cdec = jnp.exp(c * log_g)
    kdec = jnp.broadcast_to(kdec[:, :, None], (RET_HEADS, c, RET_DK))
    qdec = jnp.broadcast_to(qdec[:, :, None], (RET_HEADS, c, RET_DV))
    cdec = jnp.broadcast_to(cdec[:, None, None], (RET_HEADS, RET_DK, RET_DV))
    gn3 = gn.reshape(RET_HEADS, 1, RET_DV)

    seq = lambda i: (0, i, 0)
    fixed = lambda i: (0, 0, 0)
    table = lambda shape: pl.BlockSpec(shape, fixed)
    return pl.pallas_call(
        _retention_kernel,
        grid=(s // c,),
        in_specs=[pl.BlockSpec((b, c, w), seq)] * 4 + [
            table((RET_HEADS, c, c)),
            table((RET_HEADS, c, RET_DV)),
            table((RET_HEADS, c, RET_DK)),
            table((RET_HEADS, RET_DK, RET_DV)),
            table((RET_HEADS, 1, RET_DV)),
        ],
        out_specs=pl.BlockSpec((b, c, w), seq),
        out_shape=jax.ShapeDtypeStruct((b, s, w), BF16),
        scratch_shapes=[pltpu.VMEM((b * RET_HEADS, RET_DK, RET_DV), F32)],
        compiler_params=_params(("arbitrary",)),
        name="retention",
    )(q, k, v, g, dmask, qdec, kdec, cdec, gn3)


def _attention_kernel(sinks_ref, q_ref, kvp_ref, kvc_ref, bias_ref, band_ref, o_ref):
    blk = ATT_BLOCK
    slab = 2 * ATT_HEAD_DIM
    band = band_ref[...] > 0.5
    col = lax.broadcasted_iota(jnp.int32, (1, 2 * blk), 1)
    first_ok = jnp.logical_and(band, jnp.logical_or(col >= blk, pl.program_id(1) > 0))
    low = lax.broadcasted_iota(jnp.int32, (1, slab), 1) < ATT_HEAD_DIM
    zero = jnp.zeros((), BF16)
    for i in range(q_ref.shape[1] // blk):
        rows = slice(i * blk, (i + 1) * blk)
        prev = (lambda cs: kvp_ref[0, :, cs]) if i == 0 else (
            lambda cs, r=slice((i - 1) * blk, i * blk): kvc_ref[0, r, cs])
        ok = first_ok if i == 0 else band
        slabs = []
        for a in range(ATT_Q_HEADS // 2):
            g = (2 * a) // ATT_GROUP
            ks = slice(g * slab, (g + 1) * slab)
            vs = slice((ATT_KV_HEADS + g) * slab, (ATT_KV_HEADS + g + 1) * slab)
            k = jnp.concatenate([prev(ks), kvc_ref[0, rows, ks]], axis=0)
            v = jnp.concatenate([prev(vs), kvc_ref[0, rows, vs]], axis=0)
            q = q_ref[0, rows, a * slab:(a + 1) * slab]
            halves = []
            for part in range(2):
                h = 2 * a + part
                qz = jnp.where(low if part == 0 else jnp.logical_not(low), q, zero)
                s = jnp.where(ok, _dot_nt(qz, k) + bias_ref[h], NEG)
                sink = sinks_ref[h]
                m = jnp.maximum(jnp.max(s, axis=1, keepdims=True), sink)
                p = jnp.exp(s - m)
                denom = jnp.sum(p, axis=1, keepdims=True) + jnp.exp(sink - m)
                halves.append(_dot(p.astype(BF16), v) / denom)
            slabs.append(jnp.where(low, halves[0], halves[1]))
        o_ref[0, rows, :] = jnp.concatenate(slabs, axis=1).astype(BF16)


def _t5_bucket(n):
    max_exact = NUM_BUCKETS // 2
    nf = jnp.maximum(n, 1).astype(F32)
    large = max_exact + (jnp.log(nf / max_exact) / math.log(MAX_DISTANCE / max_exact)
                         * (NUM_BUCKETS - max_exact)).astype(jnp.int32)
    large = jnp.minimum(large, NUM_BUCKETS - 1)
    return jnp.where(n < max_exact, n, large)


def _attention(q, kv, sinks, bias_table):
    b, s, _ = q.shape
    blk = ATT_BLOCK
    qi = jnp.arange(blk)[:, None]
    ki = jnp.arange(2 * blk)[None, :]
    dist = qi + blk - ki
    band = ((dist >= 0) & (dist < WINDOW)).astype(F32)
    onehot = jax.nn.one_hot(_t5_bucket(jnp.maximum(dist, 0)), NUM_BUCKETS, dtype=F32)
    bias = jnp.einsum("qkn,nh->hqk", onehot, bias_table.astype(F32), precision=lax.Precision.HIGHEST)

    sub = ATT_STEP_BLOCKS
    kvw = kv.shape[-1]
    return pl.pallas_call(
        _attention_kernel,
        grid=(b, s // (sub * blk)),
        in_specs=[
            pl.BlockSpec(memory_space=pltpu.SMEM),
            pl.BlockSpec((1, sub * blk, ATT_WIDTH), lambda i, j: (i, j, 0)),
            pl.BlockSpec((1, blk, kvw), lambda i, j: (i, jnp.maximum(sub * j - 1, 0), 0)),
            pl.BlockSpec((1, sub * blk, kvw), lambda i, j: (i, j, 0)),
            pl.BlockSpec((ATT_Q_HEADS, blk, 2 * blk), lambda i, j: (0, 0, 0)),
            pl.BlockSpec((blk, 2 * blk), lambda i, j: (0, 0)),
        ],
        out_specs=pl.BlockSpec((1, sub * blk, ATT_WIDTH), lambda i, j: (i, j, 0)),
        out_shape=jax.ShapeDtypeStruct((b, s, ATT_WIDTH), BF16),
        compiler_params=_params(("parallel", "arbitrary")),
        name="attention",
    )(sinks, q, kv, kv, bias, band)


def _pack_bf16_pairs(lo, hi):
    return pltpu.bitcast(hi, jnp.uint32) | (pltpu.bitcast(lo, jnp.uint32) >> 16)


def _unpack_bf16_pairs(packed):
    lo = pltpu.bitcast(packed << 16, F32).astype(BF16)
    hi = pltpu.bitcast(packed & jnp.uint32(0xFFFF0000), F32).astype(BF16)
    return lo, hi


def _columns(cols, dtype):
    t = cols[0].shape[0]
    lane = lax.broadcasted_iota(jnp.int32, (t, len(cols)), 1)
    out = jnp.zeros((t, len(cols)), dtype)
    for i, c in enumerate(cols):
        out = jnp.where(lane == i, c.astype(dtype), out)
    return out


def _out_router_kernel(ret_ref, att_ref, x_ref, wo_ref, g_ref, wr_ref, br_ref,
                       h1_ref, xp_ref, topi_ref, rank_ref, gate_ref, cnt_ref, carry_ref):
    tm = x_ref.shape[0]
    half = x_ref.shape[1] // 2

    @pl.when(pl.program_id(0) == 0)
    def _():
        carry_ref[...] = jnp.zeros_like(carry_ref)

    h1 = (x_ref[...] + _dot(ret_ref[...], wo_ref[:RET_WIDTH, :])
          + _dot(att_ref[...], wo_ref[RET_WIDTH:, :]))
    h1_ref[...] = h1
    xb = _rms(h1, g_ref[...]).astype(BF16)
    xp_ref[...] = _pack_bf16_pairs(xb[:, :half].astype(F32), xb[:, half:].astype(F32))

    logits = _dot(xb, wr_ref[...]) + br_ref[...]
    lane = lax.broadcasted_iota(jnp.int32, logits.shape, 1)
    vals = logits
    top_v, top_i, sels = [], [], []
    for _ in range(TOP_K):
        m = jnp.max(vals, axis=1, keepdims=True)
        idx = jnp.min(jnp.where(vals == m, lane, N_EXPERTS), axis=1, keepdims=True)
        sel = lane == idx
        vals = jnp.where(sel, -jnp.inf, vals)
        top_v.append(m)
        top_i.append(idx)
        sels.append(sel)
    exps = [jnp.exp(v - top_v[0]) for v in top_v]
    denom = exps[0] + exps[1] + exps[2] + exps[3]
    gates = [e / denom for e in exps]

    chosen = jnp.zeros(logits.shape, F32)
    for sel in sels:
        chosen = chosen + sel.astype(F32)
    r = lax.broadcasted_iota(jnp.int32, (tm, tm), 0)
    c = lax.broadcasted_iota(jnp.int32, (tm, tm), 1)
    lower = jnp.where(r > c, 1.0, 0.0).astype(BF16)
    before = _dot(lower, chosen.astype(BF16)) + carry_ref[...]
    ranks = [jnp.sum(jnp.where(sel, before, 0.0), axis=1, keepdims=True) for sel in sels]
    carry_ref[...] = carry_ref[...] + jnp.sum(chosen, axis=0, keepdims=True)

    topi_ref[...] = _columns(top_i, jnp.int32)
    rank_ref[...] = _columns(ranks, jnp.int32)
    gate_ref[...] = _columns(gates, F32)
    cnt_ref[...] = carry_ref[...]


def _out_router(ret, att, x2, w_out, g, w_router, b_router):
    n, d = x2.shape
    tm = TOKEN_TILE
    row = lambda i: (i, 0)
    fixed = lambda i: (0, 0)
    return pl.pallas_call(
        _out_router_kernel,
        grid=(n // tm,),
        in_specs=[
            pl.BlockSpec((tm, RET_WIDTH), row),
            pl.BlockSpec((tm, ATT_WIDTH), row),
            pl.BlockSpec((tm, d), row),
            pl.BlockSpec((RET_WIDTH + ATT_WIDTH, d), fixed),
            pl.BlockSpec((1, d), fixed),
            pl.BlockSpec((d, N_EXPERTS), fixed),
            pl.BlockSpec((1, N_EXPERTS), fixed),
        ],
        out_specs=[
            pl.BlockSpec((tm, d), row),
            pl.BlockSpec((tm, d // 2), row),
            pl.BlockSpec((tm, TOP_K), row),
            pl.BlockSpec((tm, TOP_K), row),
            pl.BlockSpec((tm, TOP_K), row),
            pl.BlockSpec((1, N_EXPERTS), fixed),
        ],
        out_shape=[
            jax.ShapeDtypeStruct((n, d), F32),
            jax.ShapeDtypeStruct((n, d // 2), jnp.uint32),
            jax.ShapeDtypeStruct((n, TOP_K), jnp.int32),
            jax.ShapeDtypeStruct((n, TOP_K), jnp.int32),
            jax.ShapeDtypeStruct((n, TOP_K), F32),
            jax.ShapeDtypeStruct((1, N_EXPERTS), F32),
        ],
        scratch_shapes=[pltpu.VMEM((1, N_EXPERTS), F32)],
        compiler_params=_params(("arbitrary",)),
        name="out_router",
    )(ret, att, x2, w_out, g, w_router, b_router)


def _index_copy(dest_hbm, idx_smem, isem, step, slot):
    per = idx_smem.shape[1]
    return pltpu.make_async_copy(dest_hbm.at[pl.ds(step * per, per)], idx_smem.at[slot], isem.at[slot])


def _prefetch_indices(dest_hbm, idx_smem, isem):
    i = pl.program_id(0)
    slot = lax.rem(i, 2)

    @pl.when(i == 0)
    def _():
        _index_copy(dest_hbm, idx_smem, isem, 0, 0).start()

    @pl.when(i + 1 < pl.num_programs(0))
    def _():
        _index_copy(dest_hbm, idx_smem, isem, i + 1, 1 - slot).start()

    _index_copy(dest_hbm, idx_smem, isem, i, slot).wait()
    return slot


def _dispatch_kernel(dest_hbm, xp_ref, xs_in, xs_out, idx_smem, isem, sem):
    del xs_in
    tm = xp_ref.shape[0]
    slot = _prefetch_indices(dest_hbm, idx_smem, isem)

    def row_copy(t, d):
        return pltpu.make_async_copy(xp_ref.at[pl.ds(t, 1), :], xs_out.at[pl.ds(d, 1), :], sem)

    def body(t, carry):
        for k in range(TOP_K):
            row_copy(t, idx_smem[slot, t * TOP_K + k]).start()
        return carry

    lax.fori_loop(0, tm, body, 0, unroll=8)
    for _ in range(TOP_K):
        pltpu.make_async_copy(xp_ref, xs_out.at[pl.ds(0, tm), :], sem).wait()


def _dispatch(dest_flat, xp, cap):
    n, half = xp.shape
    tm = TOKEN_TILE
    xs0 = jnp.zeros((cap, half), jnp.uint32)
    return pl.pallas_call(
        _dispatch_kernel,
        grid=(n // tm,),
        in_specs=[
            pl.BlockSpec(memory_space=pl.ANY),
            pl.BlockSpec((tm, half), lambda i: (i, 0)),
            pl.BlockSpec(memory_space=pl.ANY),
        ],
        out_specs=pl.BlockSpec(memory_space=pl.ANY),
        out_shape=jax.ShapeDtypeStruct((cap, half), jnp.uint32),
        scratch_shapes=[
            pltpu.SMEM((2, tm * TOP_K), jnp.int32),
            pltpu.SemaphoreType.DMA((2,)),
            pltpu.SemaphoreType.DMA,
        ],
        input_output_aliases={2: 0},
        compiler_params=_params(("arbitrary",)),
        name="dispatch",
    )(dest_flat, xp, xs0)


def _expert_prep_kernel(wu_ref, wd_ref, wg_ref, wl_ref, wdo_ref):
    ch = PERM_CHUNK
    hc = ch // 2
    r = lax.broadcasted_iota(jnp.int32, (ch, ch), 0)
    c = lax.broadcasted_iota(jnp.int32, (ch, ch), 1)
    perm = jnp.where(r == jnp.where(c < hc, 2 * c, 2 * (c - hc) + 1), 1.0, 0.0).astype(BF16)
    for j in range(wu_ref.shape[1] // ch):
        both = _dot(wu_ref[:, j * ch:(j + 1) * ch].astype(BF16), perm)
        wg_ref[:, j * hc:(j + 1) * hc] = both[:, :hc].astype(BF16)
        wl_ref[:, j * hc:(j + 1) * hc] = both[:, hc:].astype(BF16)
    wdo_ref[...] = wd_ref[...].astype(BF16)


def _expert_prep(w_up, w_down):
    e, d, f2 = w_up.shape
    f = f2 // 2
    expert = lambda i: (i, 0, 0)
    return pl.pallas_call(
        _expert_prep_kernel,
        grid=(e,),
        in_specs=[pl.BlockSpec((None, d, f2), expert), pl.BlockSpec((None, f, d), expert)],
        out_specs=[pl.BlockSpec((None, d, f), expert), pl.BlockSpec((None, d, f), expert),
                   pl.BlockSpec((None, f, d), expert)],
        out_shape=[jax.ShapeDtypeStruct((e, d, f), BF16), jax.ShapeDtypeStruct((e, d, f), BF16),
                   jax.ShapeDtypeStruct((e, f, d), BF16)],
        compiler_params=_params(("parallel",)),
        name="expert_prep",
    )(w_up, w_down)


def _moe_kernel(be_ref, nused_ref, xs_ref, wg_ref, wl_ref, bg_ref, bl_ref, wd_ref, bd_ref, ys_ref):
    del be_ref
    half = xs_ref.shape[1]
    used = pl.program_id(0) < nused_ref[0]

    @pl.when(jnp.logical_not(used))
    def _():
        ys_ref[...] = jnp.zeros_like(ys_ref)

    @pl.when(used)
    def _():
        lo, hi = _unpack_bf16_pairs(xs_ref[...])

        def up(w_ref, b_ref):
            return _dot(lo, w_ref[:half, :]) + _dot(hi, w_ref[half:, :]) + b_ref[...]

        glu = jnp.minimum(up(wg_ref, bg_ref), SWIGLU_LIMIT)
        lin = jnp.clip(up(wl_ref, bl_ref), -SWIGLU_LIMIT, SWIGLU_LIMIT)
        act = glu * jax.nn.sigmoid(SWIGLU_ALPHA * glu) * (lin + 1.0)
        ys_ref[...] = _dot(act.astype(BF16), wd_ref[...]) + bd_ref[...]


def _moe(block_e, n_used, xs, wg, wl, bg, bl, wd, bd):
    cap, half = xs.shape
    e, d, f = wg.shape
    bm = MOE_BLOCK
    rows = lambda i, be, nu: (i, 0)
    expert3 = lambda i, be, nu: (be[i], 0, 0)
    grid_spec = pltpu.PrefetchScalarGridSpec(
        num_scalar_prefetch=2,
        grid=(cap // bm,),
        in_specs=[
            pl.BlockSpec((bm, half), rows),
            pl.BlockSpec((None, d, f), expert3),
            pl.BlockSpec((None, d, f), expert3),
            pl.BlockSpec((None, 1, f), expert3),
            pl.BlockSpec((None, 1, f), expert3),
            pl.BlockSpec((None, f, d), expert3),
            pl.BlockSpec((None, 1, d), expert3),
        ],
        out_specs=pl.BlockSpec((bm, d), rows),
    )
    return pl.pallas_call(
        _moe_kernel,
        grid_spec=grid_spec,
        out_shape=jax.ShapeDtypeStruct((cap, d), F32),
        compiler_params=_params(("arbitrary",)),
        name="moe",
    )(block_e, n_used, xs, wg, wl, bg, bl, wd, bd)


def _combine_ple_kernel(dest_hbm, ys_hbm, h1_ref, gate_ref, p_ref, wpg_ref, wpp_ref, gple_ref,
                        gfin_ref, o_ref, idx_smem, isem, ybuf, sem):
    tm = h1_ref.shape[0]
    slot = _prefetch_indices(dest_hbm, idx_smem, isem)

    def body(t, carry):
        for k in range(TOP_K):
            d = idx_smem[slot, t * TOP_K + k]
            pltpu.make_async_copy(ys_hbm.at[pl.ds(d, 1), :], ybuf.at[k, pl.ds(t, 1), :], sem).start()
        return carry

    lax.fori_loop(0, tm, body, 0, unroll=8)
    for k in range(TOP_K):
        pltpu.make_async_copy(ys_hbm.at[pl.ds(0, tm), :], ybuf.at[k], sem).wait()

    gates = gate_ref[...]
    h2 = h1_ref[...]
    for k in range(TOP_K):
        h2 = h2 + gates[:, k:k + 1] * ybuf[k]
    gate = jax.nn.sigmoid(_dot(h2.astype(BF16), wpg_ref[...]))
    proj = _dot(p_ref[...].astype(BF16), wpp_ref[...])
    h3 = h2 + _rms(gate * proj, gple_ref[...])
    o_ref[...] = _rms(h3, gfin_ref[...])


def _combine_ple(dest_flat, ys, h1, gates, p2, wpg, wpp, gple, gfin):
    n, d = h1.shape
    pd = p2.shape[1]
    tm = COMBINE_TILE
    row = lambda i: (i, 0)
    fixed = lambda i: (0, 0)
    return pl.pallas_call(
        _combine_ple_kernel,
        grid=(n // tm,),
        in_specs=[
            pl.BlockSpec(memory_space=pl.ANY),
            pl.BlockSpec(memory_space=pl.ANY),
            pl.BlockSpec((tm, d), row),
            pl.BlockSpec((tm, TOP_K), row),
            pl.BlockSpec((tm, pd), row),
            pl.BlockSpec((d, d), fixed),
            pl.BlockSpec((pd, d), fixed),
            pl.BlockSpec((1, d), fixed),
            pl.BlockSpec((1, d), fixed),
        ],
        out_specs=pl.BlockSpec((tm, d), row),
        out_shape=jax.ShapeDtypeStruct((n, d), F32),
        scratch_shapes=[
            pltpu.SMEM((2, tm * TOP_K), jnp.int32),
            pltpu.SemaphoreType.DMA((2,)),
            pltpu.VMEM((TOP_K, tm, d), F32),
            pltpu.SemaphoreType.DMA,
        ],
        compiler_params=_params(("arbitrary",)),
        name="combine_ple",
    )(dest_flat, ys, h1, gates, p2, wpg, wpp, gple, gfin)


def kernel(x, p, positions, rel_bias_table, g_mix_norm, w_in, b_in, ret_norm_g, att_sinks, w_out,
           g_moe_norm, w_router, b_router, w_up, b_up, w_down, b_down, w_ple_gate, w_ple_proj,
           g_ple_norm, g_final):
    b, s, d = x.shape
    depth = w_in.shape[0]
    assert depth == 1, "single-layer stack only"
    n = b * s
    x2 = x.reshape(n, d)

    half = RET_DK // 2
    inv = ROPE_BASE ** (-jnp.arange(half, dtype=F32) / half)
    ang = positions.astype(F32)[..., None] * inv
    cos, sin = jnp.cos(ang), jnp.sin(ang)
    cosf = jnp.concatenate([cos, cos], axis=-1).reshape(n, RET_DK)
    sinf = jnp.concatenate([-sin, sin], axis=-1).reshape(n, RET_DK)

    qr, kr, vr, gr, qa, kva = _in_proj(
        x2, g_mix_norm[0][None], w_in[0].astype(BF16), b_in[0][None], cosf, sinf)

    shape3 = lambda t: t.reshape(b, s, t.shape[-1])
    ret = _retention(shape3(qr), shape3(kr), shape3(vr), shape3(gr), ret_norm_g[0])
    att = _attention(shape3(qa), shape3(kva), att_sinks[0], rel_bias_table)

    h1, xp, top_i, rank, gates, counts = _out_router(
        ret.reshape(n, RET_WIDTH), att.reshape(n, ATT_WIDTH), x2, w_out[0].astype(BF16),
        g_moe_norm[0][None], w_router[0].astype(BF16), b_router[0][None])

    bm = MOE_BLOCK
    cap = n * TOP_K + N_EXPERTS * bm
    counts = counts[0].astype(jnp.int32)
    padded = (counts + bm - 1) // bm * bm
    pend = jnp.cumsum(padded)
    poff = pend - padded
    experts = jnp.arange(N_EXPERTS, dtype=jnp.int32)
    dest = rank + jnp.sum(jnp.where(top_i[..., None] == experts, poff, 0), axis=-1)
    dest_flat = dest.reshape(-1).astype(jnp.int32)
    block_start = jnp.arange(cap // bm, dtype=jnp.int32) * bm
    block_e = jnp.minimum(jnp.sum(block_start[:, None] >= pend[None, :], axis=1), N_EXPERTS - 1)
    n_used = (pend[-1:] // bm).astype(jnp.int32)

    xs = _dispatch(dest_flat, xp, cap)

    wg, wl, wd = _expert_prep(w_up[0], w_down[0])
    ys = _moe(block_e.astype(jnp.int32), n_used, xs, wg, wl,
              b_up[0][:, None, 0::2], b_up[0][:, None, 1::2],
              wd, b_down[0].reshape(N_EXPERTS, 1, d))

    out = _combine_ple(dest_flat, ys, h1, gates, p[0].reshape(n, -1),
                       w_ple_gate[0].astype(BF16), w_ple_proj[0].astype(BF16),
                       g_ple_norm[0][None], g_final[None])
    return out.reshape(b, s, d)
```

```python
import math

import jax
import jax.numpy as jnp
from jax import lax
from jax.experimental import pallas as pl
from jax.experimental.pallas import tpu as pltpu

RET_HEADS = 4
RET_DK = 128
RET_DV = 128
RET_CHUNK = 128
ROPE_BASE = 10000.0
ATT_Q_HEADS = 8
ATT_KV_HEADS = 2
ATT_GROUP = ATT_Q_HEADS // ATT_KV_HEADS
ATT_HEAD_DIM = 64
WINDOW = 128
ATT_BLOCK = 128
NUM_BUCKETS = 32
MAX_DISTANCE = 128
N_EXPERTS = 32
TOP_K = 4
SWIGLU_ALPHA = 1.702
SWIGLU_LIMIT = 7.0
EPS = 1e-5

RET_WIDTH = RET_HEADS * RET_DV
ATT_WIDTH = ATT_Q_HEADS * ATT_HEAD_DIM
KV_WIDTH = ATT_KV_HEADS * ATT_HEAD_DIM

VMEM_LIMIT_BYTES = 48 * 1024 * 1024

TOKEN_TILE = 512
ROUTE_TILE = 256
CHUNK_ROWS = 8
LOCAL_ROWS = ROUTE_TILE * TOP_K + N_EXPERTS * CHUNK_ROWS
SPARE_CHUNKS = (LOCAL_ROWS - ROUTE_TILE * TOP_K - N_EXPERTS * (CHUNK_ROWS - 1)) // CHUNK_ROWS
MOE_BLOCK = 256
BLOCK_CHUNKS = MOE_BLOCK // CHUNK_ROWS
ATT_STEP_BLOCKS = 4
PERM_CHUNK = 256

F32 = jnp.float32
BF16 = jnp.bfloat16
NEG = float(jnp.finfo(jnp.float32).min)


def _rms(x, g):
    return x * lax.rsqrt(jnp.mean(x * x, axis=-1, keepdims=True) + EPS) * g


def _dot(a, b):
    return jnp.dot(a, b, preferred_element_type=F32)


def _dot_nt(a, b):
    return lax.dot_general(a, b, (((1,), (1,)), ((), ())), preferred_element_type=F32)


def _dot_tn(a, b):
    return lax.dot_general(a, b, (((0,), (0,)), ((), ())), preferred_element_type=F32)


def _params(semantics):
    return pltpu.CompilerParams(dimension_semantics=semantics, vmem_limit_bytes=VMEM_LIMIT_BYTES)


def _in_proj_kernel(x_ref, g_ref, w_ref, b_ref, cos_ref, sin_ref,
                    qr_ref, kr_ref, vr_ref, gr_ref, qa_ref, kva_ref):
    hn = _rms(x_ref[...], g_ref[...]).astype(BF16)
    cos = cos_ref[...]
    sin = sin_ref[...]

    def proj(lo, hi):
        return _dot(hn, w_ref[:, lo:hi]) + b_ref[:, lo:hi]

    def rope(t):
        heads = []
        for h in range(RET_HEADS):
            th = t[:, h * RET_DK:(h + 1) * RET_DK]
            heads.append(th * cos + pltpu.roll(th, RET_DK // 2, 1) * sin)
        return jnp.concatenate(heads, axis=1)

    o = 0
    qr_ref[...] = rope(proj(o, o + RET_WIDTH)).astype(BF16)
    o += RET_WIDTH
    kr_ref[...] = (rope(proj(o, o + RET_WIDTH)) * (RET_DK ** -0.5)).astype(BF16)
    o += RET_WIDTH
    vr_ref[...] = proj(o, o + RET_WIDTH).astype(BF16)
    o += RET_WIDTH
    gr_ref[...] = proj(o, o + RET_WIDTH).astype(BF16)
    o += RET_WIDTH
    qa_ref[...] = (proj(o, o + ATT_WIDTH) * (ATT_HEAD_DIM ** -0.5)).astype(BF16)
    o += ATT_WIDTH
    low = lax.broadcasted_iota(jnp.int32, (1, KV_WIDTH), 1) < ATT_HEAD_DIM
    slabs = []
    for part in range(2):
        t = proj(o + part * KV_WIDTH, o + (part + 1) * KV_WIDTH)
        swapped = pltpu.roll(t, ATT_HEAD_DIM, 1)
        slabs += [jnp.where(low, t, swapped), jnp.where(low, swapped, t)]
    kva_ref[...] = jnp.concatenate(slabs, axis=1).astype(BF16)


def _in_proj(x2, g, w_in, b_in, cosf, sinf):
    n, d = x2.shape
    width = w_in.shape[1]
    tm = TOKEN_TILE
    row = lambda i: (i, 0)
    fixed = lambda i: (0, 0)
    outs = [RET_WIDTH, RET_WIDTH, RET_WIDTH, RET_WIDTH, ATT_WIDTH, 4 * KV_WIDTH]
    return pl.pallas_call(
        _in_proj_kernel,
        grid=(n // tm,),
        in_specs=[
            pl.BlockSpec((tm, d), row),
            pl.BlockSpec((1, d), fixed),
            pl.BlockSpec((d, width), fixed),
            pl.BlockSpec((1, width), fixed),
            pl.BlockSpec((tm, RET_DK), row),
            pl.BlockSpec((tm, RET_DK), row),
        ],
        out_specs=[pl.BlockSpec((tm, w), row) for w in outs],
        out_shape=[jax.ShapeDtypeStruct((n, w), BF16) for w in outs],
        compiler_params=_params(("parallel",)),
        name="in_proj",
    )(x2, g, w_in, b_in, cosf, sinf)


def _retention_kernel(q_ref, k_ref, v_ref, g_ref, dmask_ref, qdec_ref, kdec_ref, cdec_ref,
                      gn_ref, o_ref, state_ref):
    batch = q_ref.shape[0]

    @pl.when(pl.program_id(0) == 0)
    def _():
        state_ref[...] = jnp.zeros_like(state_ref)

    def per_batch(b, carry):
        for h in range(RET_HEADS):
            sl = slice(h * RET_DK, (h + 1) * RET_DK)
            q = q_ref[b, :, sl]
            k = k_ref[b, :, sl]
            v = v_ref[b, :, sl]
            scores = _dot_nt(q, k) * dmask_ref[h]
            inner = _dot(scores.astype(BF16), v)
            state = state_ref[b * RET_HEADS + h]
            cross = _dot(q, state.astype(BF16)) * qdec_ref[h]
            kd = (k.astype(F32) * kdec_ref[h]).astype(BF16)
            state_ref[b * RET_HEADS + h] = state * cdec_ref[h] + _dot_tn(kd, v)
            o = _rms(inner + cross, gn_ref[h])
            gate = g_ref[b, :, sl].astype(F32)
            o_ref[b, :, sl] = (gate * jax.nn.sigmoid(gate) * o).astype(BF16)
        return carry

    lax.fori_loop(0, batch, per_batch, 0)


def _retention(q, k, v, g, gn):
    b, s, w = q.shape
    c = RET_CHUNK
    log_g = jnp.log(1.0 - 2.0 ** (-5.0 - jnp.arange(RET_HEADS, dtype=F32)))
    ci = jnp.arange(c)
    diff = (ci[:, None] - ci[None, :]).astype(F32)
    dmask = jnp.where(diff >= 0, jnp.exp(jnp.maximum(diff, 0.0)[None] * log_g[:, None, None]), 0.0)
    cf = ci.astype(F32)
    kdec = jnp.exp((c - 1 - cf)[None, :] * log_g[:, None])
    qdec = jnp.exp((cf + 1)[None, :] * log_g[:, None])
    cdec = jnp.exp(c * log_g)
    kdec = jnp.broadcast_to(kdec[:, :, None], (RET_HEADS, c, RET_DK))
    qdec = jnp.broadcast_to(qdec[:, :, None], (RET_HEADS, c, RET_DV))
    cdec = jnp.broadcast_to(cdec[:, None, None], (RET_HEADS, RET_DK, RET_DV))
    gn3 = gn.reshape(RET_HEADS, 1, RET_DV)

    seq = lambda i: (0, i, 0)
    fixed = lambda i: (0, 0, 0)
    table = lambda shape: pl.BlockSpec(shape, fixed)
    return pl.pallas_call(
        _retention_kernel,
        grid=(s // c,),
        in_specs=[pl.BlockSpec((b, c, w), seq)] * 4 + [
            table((RET_HEADS, c, c)),
            table((RET_HEADS, c, RET_DV)),
            table((RET_HEADS, c, RET_DK)),
            table((RET_HEADS, RET_DK, RET_DV)),
            table((RET_HEADS, 1, RET_DV)),
        ],
        out_specs=pl.BlockSpec((b, c, w), seq),
        out_shape=jax.ShapeDtypeStruct((b, s, w), BF16),
        scratch_shapes=[pltpu.VMEM((b * RET_HEADS, RET_DK, RET_DV), F32)],
        compiler_params=_params(("arbitrary",)),
        name="retention",
    )(q, k, v, g, dmask, qdec, kdec, cdec, gn3)


def _attention_kernel(sinks_ref, q_ref, kvp_ref, kvc_ref, bias_ref, band_ref, o_ref):
    blk = ATT_BLOCK
    slab = 2 * ATT_HEAD_DIM
    band = band_ref[...] > 0.5
    col = lax.broadcasted_iota(jnp.int32, (1, 2 * blk), 1)
    first_ok = jnp.logical_and(band, jnp.logical_or(col >= blk, pl.program_id(1) > 0))
    low = lax.broadcasted_iota(jnp.int32, (1, slab), 1) < ATT_HEAD_DIM
    zero = jnp.zeros((), BF16)
    for i in range(q_ref.shape[1] // blk):
        rows = slice(i * blk, (i + 1) * blk)
        prev = (lambda cs: kvp_ref[0, :, cs]) if i == 0 else (
            lambda cs, r=slice((i - 1) * blk, i * blk): kvc_ref[0, r, cs])
        ok = first_ok if i == 0 else band
        slabs = []
        for a in range(ATT_Q_HEADS // 2):
            g = (2 * a) // ATT_GROUP
            ks = slice(g * slab, (g + 1) * slab)
            vs = slice((ATT_KV_HEADS + g) * slab, (ATT_KV_HEADS + g + 1) * slab)
            k = jnp.concatenate([prev(ks), kvc_ref[0, rows, ks]], axis=0)
            v = jnp.concatenate([prev(vs), kvc_ref[0, rows, vs]], axis=0)
            q = q_ref[0, rows, a * slab:(a + 1) * slab]
            halves = []
            for part in range(2):
                h = 2 * a + part
                qz = jnp.where(low if part == 0 else jnp.logical_not(low), q, zero)
                s = jnp.where(ok, _dot_nt(qz, k) + bias_ref[h], NEG)
                sink = sinks_ref[h]
                m = jnp.maximum(jnp.max(s, axis=1, keepdims=True), sink)
                p = jnp.exp(s - m)
                denom = jnp.sum(p, axis=1, keepdims=True) + jnp.exp(sink - m)
                halves.append(_dot(p.astype(BF16), v) / denom)
            slabs.append(jnp.where(low, halves[0], halves[1]))
        o_ref[0, rows, :] = jnp.concatenate(slabs, axis=1).astype(BF16)


def _t5_bucket(n):
    max_exact = NUM_BUCKETS // 2
    nf = jnp.maximum(n, 1).astype(F32)
    large = max_exact + (jnp.log(nf / max_exact) / math.log(MAX_DISTANCE / max_exact)
                         * (NUM_BUCKETS - max_exact)).astype(jnp.int32)
    large = jnp.minimum(large, NUM_BUCKETS - 1)
    return jnp.where(n < max_exact, n, large)


def _attention(q, kv, sinks, bias_table):
    b, s, _ = q.shape
    blk = ATT_BLOCK
    qi = jnp.arange(blk)[:, None]
    ki = jnp.arange(2 * blk)[None, :]
    dist = qi + blk - ki
    band = ((dist >= 0) & (dist < WINDOW)).astype(F32)
    onehot = jax.nn.one_hot(_t5_bucket(jnp.maximum(dist, 0)), NUM_BUCKETS, dtype=F32)
    bias = jnp.einsum("qkn,nh->hqk", onehot, bias_table.astype(F32), precision=lax.Precision.HIGHEST)

    sub = ATT_STEP_BLOCKS
    kvw = kv.shape[-1]
    return pl.pallas_call(
        _attention_kernel,
        grid=(b, s // (sub * blk)),
        in_specs=[
            pl.BlockSpec(memory_space=pltpu.SMEM),
            pl.BlockSpec((1, sub * blk, ATT_WIDTH), lambda i, j: (i, j, 0)),
            pl.BlockSpec((1, blk, kvw), lambda i, j: (i, jnp.maximum(sub * j - 1, 0), 0)),
            pl.BlockSpec((1, sub * blk, kvw), lambda i, j: (i, j, 0)),
            pl.BlockSpec((ATT_Q_HEADS, blk, 2 * blk), lambda i, j: (0, 0, 0)),
            pl.BlockSpec((blk, 2 * blk), lambda i, j: (0, 0)),
        ],
        out_specs=pl.BlockSpec((1, sub * blk, ATT_WIDTH), lambda i, j: (i, j, 0)),
        out_shape=jax.ShapeDtypeStruct((b, s, ATT_WIDTH), BF16),
        compiler_params=_params(("parallel", "arbitrary")),
        name="attention",
    )(sinks, q, kv, kv, bias, band)


def _pack_bf16_pairs(lo, hi):
    return pltpu.bitcast(hi, jnp.uint32) | (pltpu.bitcast(lo, jnp.uint32) >> 16)


def _unpack_bf16_pairs(packed):
    lo = pltpu.bitcast(packed << 16, F32).astype(BF16)
    hi = pltpu.bitcast(packed & jnp.uint32(0xFFFF0000), F32).astype(BF16)
    return lo, hi


def _columns(cols, dtype):
    t = cols[0].shape[0]
    lane = lax.broadcasted_iota(jnp.int32, (t, len(cols)), 1)
    out = jnp.zeros((t, len(cols)), dtype)
    for i, c in enumerate(cols):
        out = jnp.where(lane == i, c.astype(dtype), out)
    return out


def _out_router_kernel(ret_ref, att_ref, x_ref, wo_ref, g_ref, wr_ref, br_ref,
                       h1_ref, xloc_ref, yzero_ref, slot_ref, gate_ref, cnt_ref):
    tm = x_ref.shape[0]
    half = x_ref.shape[1] // 2
    rows_local = xloc_ref.shape[0]

    h1 = (x_ref[...] + _dot(ret_ref[...], wo_ref[:RET_WIDTH, :])
          + _dot(att_ref[...], wo_ref[RET_WIDTH:, :]))
    h1_ref[...] = h1
    xb = _rms(h1, g_ref[...]).astype(BF16)

    logits = _dot(xb, wr_ref[...]) + br_ref[...]
    lane = lax.broadcasted_iota(jnp.int32, logits.shape, 1)
    vals = logits
    top_v, top_i, sels = [], [], []
    for _ in range(TOP_K):
        m = jnp.max(vals, axis=1, keepdims=True)
        idx = jnp.min(jnp.where(vals == m, lane, N_EXPERTS), axis=1, keepdims=True)
        sel = lane == idx
        vals = jnp.where(sel, -jnp.inf, vals)
        top_v.append(m)
        top_i.append(idx)
        sels.append(sel)
    exps = [jnp.exp(v - top_v[0]) for v in top_v]
    denom = exps[0] + exps[1] + exps[2] + exps[3]
    gates = [e / denom for e in exps]

    chosen = jnp.zeros(logits.shape, F32)
    for sel in sels:
        chosen = chosen + sel.astype(F32)
    r = lax.broadcasted_iota(jnp.int32, (tm, tm), 0)
    c = lax.broadcasted_iota(jnp.int32, (tm, tm), 1)
    lower = jnp.where(r > c, 1.0, 0.0).astype(BF16)
    before = _dot(lower, chosen.astype(BF16))
    counts = jnp.sum(chosen, axis=0, keepdims=True)
    chunks = jnp.floor((counts + (CHUNK_ROWS - 1)) * (1.0 / CHUNK_ROWS))
    er = lax.broadcasted_iota(jnp.int32, (N_EXPERTS, N_EXPERTS), 0)
    ec = lax.broadcasted_iota(jnp.int32, (N_EXPERTS, N_EXPERTS), 1)
    upper = jnp.where(er < ec, 1.0, 0.0).astype(BF16)
    start = _dot(chunks.astype(BF16), upper) * float(CHUNK_ROWS)
    rows = [jnp.sum(jnp.where(sel, before + start, 0.0), axis=1, keepdims=True).astype(jnp.int32)
            for sel in sels]

    lane = lax.broadcasted_iota(jnp.int32, (1, rows_local), 1)
    select = jnp.zeros((tm, rows_local), F32)
    for row in rows:
        select = jnp.where(lane == row, 1.0, select)
    xloc = _dot_tn(select.astype(BF16), xb)
    xloc_ref[...] = _pack_bf16_pairs(xloc[:, :half], xloc[:, half:])
    yzero_ref[...] = jnp.zeros_like(yzero_ref)

    slot_ref[...] = _columns(rows, jnp.int32)
    gate_ref[...] = _columns(gates, F32)
    cnt_ref[...] = counts


def _out_router(ret, att, x2, w_out, g, w_router, b_router):
    n, d = x2.shape
    tm = ROUTE_TILE
    tiles = n // tm
    row = lambda i: (i, 0)
    fixed = lambda i: (0, 0)
    return pl.pallas_call(
        _out_router_kernel,
        grid=(tiles,),
        in_specs=[
            pl.BlockSpec((tm, RET_WIDTH), row),
            pl.BlockSpec((tm, ATT_WIDTH), row),
            pl.BlockSpec((tm, d), row),
            pl.BlockSpec((RET_WIDTH + ATT_WIDTH, d), fixed),
            pl.BlockSpec((1, d), fixed),
            pl.BlockSpec((d, N_EXPERTS), fixed),
            pl.BlockSpec((1, N_EXPERTS), fixed),
        ],
        out_specs=[
            pl.BlockSpec((tm, d), row),
            pl.BlockSpec((LOCAL_ROWS, d // 2), row),
            pl.BlockSpec((LOCAL_ROWS, d // 2), row),
            pl.BlockSpec((tm, TOP_K), row),
            pl.BlockSpec((tm, TOP_K), row),
            pl.BlockSpec((None, 1, N_EXPERTS), lambda i: (i, 0, 0)),
        ],
        out_shape=[
            jax.ShapeDtypeStruct((n, d), F32),
            jax.ShapeDtypeStruct((tiles * LOCAL_ROWS, d // 2), jnp.uint32),
            jax.ShapeDtypeStruct((tiles * LOCAL_ROWS, d // 2), jnp.uint32),
            jax.ShapeDtypeStruct((n, TOP_K), jnp.int32),
            jax.ShapeDtypeStruct((n, TOP_K), F32),
            jax.ShapeDtypeStruct((tiles, 1, N_EXPERTS), F32),
        ],
        compiler_params=_params(("parallel",)),
        name="out_router",
    )(ret, att, x2, w_out, g, w_router, b_router)


def _expert_prep_kernel(wu_ref, wd_ref, wg_ref, wl_ref, wdo_ref):
    ch = PERM_CHUNK
    hc = ch // 2
    r = lax.broadcasted_iota(jnp.int32, (ch, ch), 0)
    c = lax.broadcasted_iota(jnp.int32, (ch, ch), 1)
    perm = jnp.where(r == jnp.where(c < hc, 2 * c, 2 * (c - hc) + 1), 1.0, 0.0).astype(BF16)
    for j in range(wu_ref.shape[1] // ch):
        both = _dot(wu_ref[:, j * ch:(j + 1) * ch].astype(BF16), perm)
        wg_ref[:, j * hc:(j + 1) * hc] = both[:, :hc].astype(BF16)
        wl_ref[:, j * hc:(j + 1) * hc] = both[:, hc:].astype(BF16)
    wdo_ref[...] = wd_ref[...].astype(BF16)


def _expert_prep(w_up, w_down):
    e, d, f2 = w_up.shape
    f = f2 // 2
    expert = lambda i: (i, 0, 0)
    return pl.pallas_call(
        _expert_prep_kernel,
        grid=(e,),
        in_specs=[pl.BlockSpec((None, d, f2), expert), pl.BlockSpec((None, f, d), expert)],
        out_specs=[pl.BlockSpec((None, d, f), expert), pl.BlockSpec((None, d, f), expert),
                   pl.BlockSpec((None, f, d), expert)],
        out_shape=[jax.ShapeDtypeStruct((e, d, f), BF16), jax.ShapeDtypeStruct((e, d, f), BF16),
                   jax.ShapeDtypeStruct((e, f, d), BF16)],
        compiler_params=_params(("parallel",)),
        name="expert_prep",
    )(w_up, w_down)


def _moe_kernel(be_ref, nused_ref, src_ref, dst_ref,
                xloc_hbm, wg_ref, wl_ref, bg_ref, bl_ref, wd_ref, bd_ref, yinit_hbm,
                yloc_hbm, xbuf, ybuf, gsem, ssem):
    del be_ref, yinit_hbm
    i = pl.program_id(0)
    n_used = nused_ref[0]
    slot = lax.rem(i, 2)
    bm, half = xbuf.shape[1], xbuf.shape[2]

    def chunk(ref, c):
        first = c * CHUNK_ROWS if isinstance(c, int) else pl.multiple_of(c * CHUNK_ROWS, CHUNK_ROWS)
        return ref.at[pl.ds(first, CHUNK_ROWS), :]

    def start_gather(step, sl):
        for j in range(BLOCK_CHUNKS):
            pltpu.make_async_copy(chunk(xloc_hbm, src_ref[step * BLOCK_CHUNKS + j]),
                                  chunk(xbuf.at[sl], j), gsem.at[sl]).start()

    def start_writeback(step, sl):
        for j in range(BLOCK_CHUNKS):
            pltpu.make_async_copy(chunk(ybuf.at[sl], j),
                                  chunk(yloc_hbm, dst_ref[step * BLOCK_CHUNKS + j]), ssem.at[sl]).start()

    def wait_gather(sl):
        pltpu.make_async_copy(xloc_hbm.at[pl.ds(0, bm), :], xbuf.at[sl], gsem.at[sl]).wait()

    def wait_writeback(sl):
        pltpu.make_async_copy(ybuf.at[sl], yloc_hbm.at[pl.ds(0, bm), :], ssem.at[sl]).wait()

    @pl.when(i == 0)
    def _():
        start_gather(0, 0)

    @pl.when(i < n_used)
    def _():
        @pl.when(i + 1 < n_used)
        def _():
            start_gather(i + 1, 1 - slot)

        wait_gather(slot)
        lo, hi = _unpack_bf16_pairs(xbuf[slot])

        def up(w_ref, b_ref):
            return _dot(lo, w_ref[:half, :]) + _dot(hi, w_ref[half:, :]) + b_ref[...]

        glu = jnp.minimum(up(wg_ref, bg_ref), SWIGLU_LIMIT)
        lin = jnp.clip(up(wl_ref, bl_ref), -SWIGLU_LIMIT, SWIGLU_LIMIT)
        act = glu * jax.nn.sigmoid(SWIGLU_ALPHA * glu) * (lin + 1.0)
        y = (_dot(act.astype(BF16), wd_ref[...]) + bd_ref[...]).astype(BF16).astype(F32)

        @pl.when(i >= 2)
        def _():
            wait_writeback(slot)

        ybuf[slot] = _pack_bf16_pairs(y[:, :half], y[:, half:])
        start_writeback(i, slot)

        @pl.when(i == n_used - 1)
        def _():
            @pl.when(i >= 1)
            def _():
                wait_writeback(1 - slot)

            wait_writeback(slot)


def _moe(block_e, n_used, chunk_src, chunk_dst, xloc, wg, wl, bg, bl, wd, bd, yinit):
    half = xloc.shape[1]
    e, d, f = wg.shape
    bm = MOE_BLOCK
    n_blocks = chunk_src.shape[0] // BLOCK_CHUNKS
    expert3 = lambda i, be, nu, cs, cd: (be[i], 0, 0)
    grid_spec = pltpu.PrefetchScalarGridSpec(
        num_scalar_prefetch=4,
        grid=(n_blocks,),
        in_specs=[
            pl.BlockSpec(memory_space=pl.ANY),
            pl.BlockSpec((None, d, f), expert3),
            pl.BlockSpec((None, d, f), expert3),
            pl.BlockSpec((None, 1, f), expert3),
            pl.BlockSpec((None, 1, f), expert3),
            pl.BlockSpec((None, f, d), expert3),
            pl.BlockSpec((None, 1, d), expert3),
            pl.BlockSpec(memory_space=pl.ANY),
        ],
        out_specs=pl.BlockSpec(memory_space=pl.ANY),
        scratch_shapes=[
            pltpu.VMEM((2, bm, half), jnp.uint32),
            pltpu.VMEM((2, bm, half), jnp.uint32),
            pltpu.SemaphoreType.DMA((2,)),
            pltpu.SemaphoreType.DMA((2,)),
        ],
    )
    return pl.pallas_call(
        _moe_kernel,
        grid_spec=grid_spec,
        out_shape=jax.ShapeDtypeStruct(yinit.shape, jnp.uint32),
        input_output_aliases={11: 0},
        compiler_params=_params(("arbitrary",)),
        name="moe",
    )(block_e, n_used, chunk_src, chunk_dst, xloc, wg, wl, bg, bl, wd, bd, yinit)


def _combine_ple_kernel(yloc_ref, slot_ref, gate_ref, h1_ref, p_ref, wpg_ref, wpp_ref, gple_ref,
                        gfin_ref, o_ref):
    tm = h1_ref.shape[0]
    rows_local = yloc_ref.shape[0]
    lane = lax.broadcasted_iota(jnp.int32, (1, rows_local), 1)
    rows = slot_ref[...]
    gates = gate_ref[...]
    weights = jnp.zeros((tm, rows_local), F32)
    for k in range(TOP_K):
        weights = jnp.where(lane == rows[:, k:k + 1], gates[:, k:k + 1], weights)
    w_hi = weights.astype(BF16)
    w_lo = (weights - w_hi.astype(F32)).astype(BF16)
    y_lo, y_hi = _unpack_bf16_pairs(yloc_ref[...])
    moe = jnp.concatenate([_dot(w_hi, y_lo) + _dot(w_lo, y_lo),
                           _dot(w_hi, y_hi) + _dot(w_lo, y_hi)], axis=1)
    h2 = h1_ref[...] + moe
    gate = jax.nn.sigmoid(_dot(h2.astype(BF16), wpg_ref[...]))
    proj = _dot(p_ref[...].astype(BF16), wpp_ref[...])
    h3 = h2 + _rms(gate * proj, gple_ref[...])
    o_ref[...] = _rms(h3, gfin_ref[...])


def _combine_ple(yloc, slot_rows, gates, h1, p2, wpg, wpp, gple, gfin):
    n, d = h1.shape
    pd = p2.shape[1]
    tm = ROUTE_TILE
    row = lambda i: (i, 0)
    fixed = lambda i: (0, 0)
    return pl.pallas_call(
        _combine_ple_kernel,
        grid=(n // tm,),
        in_specs=[
            pl.BlockSpec((LOCAL_ROWS, d // 2), row),
            pl.BlockSpec((tm, TOP_K), row),
            pl.BlockSpec((tm, TOP_K), row),
            pl.BlockSpec((tm, d), row),
            pl.BlockSpec((tm, pd), row),
            pl.BlockSpec((d, d), fixed),
            pl.BlockSpec((pd, d), fixed),
            pl.BlockSpec((1, d), fixed),
            pl.BlockSpec((1, d), fixed),
        ],
        out_specs=pl.BlockSpec((tm, d), row),
        out_shape=jax.ShapeDtypeStruct((n, d), F32),
        compiler_params=_params(("parallel",)),
        name="combine_ple",
    )(yloc, slot_rows, gates, h1, p2, wpg, wpp, gple, gfin)


def _routing_tables(counts):
    tiles = counts.shape[0]
    local_chunks = LOCAL_ROWS // CHUNK_ROWS
    assert 2 * BLOCK_CHUNKS <= tiles * SPARE_CHUNKS
    run = (counts + CHUNK_ROWS - 1) // CHUNK_ROWS
    local_start = jnp.cumsum(run, axis=1) - run
    run_end = jnp.cumsum(run, axis=0)
    run_start = run_end - run
    total = run_end[-1]
    padded = (total + BLOCK_CHUNKS - 1) // BLOCK_CHUNKS * BLOCK_CHUNKS
    pend = jnp.cumsum(padded)
    poff = pend - padded

    slots = tiles * ROUTE_TILE * TOP_K
    max_chunks = (slots + tiles * N_EXPERTS * (CHUNK_ROWS - 1)) // CHUNK_ROWS + N_EXPERTS * (BLOCK_CHUNKS - 1)
    n_chunks = -(-max_chunks // BLOCK_CHUNKS) * BLOCK_CHUNKS
    q = jnp.arange(n_chunks, dtype=jnp.int32)
    e = jnp.minimum(jnp.sum(q[:, None] >= pend[None, :], axis=1), N_EXPERTS - 1).astype(jnp.int32)
    j = q - poff[e]
    valid = j < total[e]
    tile = jnp.sum(jnp.take(run_end.T, e, axis=0) <= j[:, None], axis=1)
    flat = jnp.minimum(tile, tiles - 1) * N_EXPERTS + e
    local = jnp.take(local_start.reshape(-1), flat) + j - jnp.take(run_start.reshape(-1), flat)
    src = jnp.where(valid, jnp.minimum(tile, tiles - 1) * local_chunks + local, 0)
    spare = (q // BLOCK_CHUNKS % 2) * BLOCK_CHUNKS + q % BLOCK_CHUNKS
    spare_chunk = (spare // SPARE_CHUNKS) * local_chunks + (local_chunks - SPARE_CHUNKS) + spare % SPARE_CHUNKS
    dst = jnp.where(valid, src, spare_chunk)
    n_used = (pend[-1:] // BLOCK_CHUNKS).astype(jnp.int32)
    return e[::BLOCK_CHUNKS], n_used, src.astype(jnp.int32), dst.astype(jnp.int32)


def kernel(x, p, positions, rel_bias_table, g_mix_norm, w_in, b_in, ret_norm_g, att_sinks, w_out,
           g_moe_norm, w_router, b_router, w_up, b_up, w_down, b_down, w_ple_gate, w_ple_proj,
           g_ple_norm, g_final):
    b, s, d = x.shape
    depth = w_in.shape[0]
    assert depth == 1, "single-layer stack only"
    n = b * s
    x2 = x.reshape(n, d)

    half = RET_DK // 2
    inv = ROPE_BASE ** (-jnp.arange(half, dtype=F32) / half)
    ang = positions.astype(F32)[..., None] * inv
    cos, sin = jnp.cos(ang), jnp.sin(ang)
    cosf = jnp.concatenate([cos, cos], axis=-1).reshape(n, RET_DK)
    sinf = jnp.concatenate([-sin, sin], axis=-1).reshape(n, RET_DK)

    qr, kr, vr, gr, qa, kva = _in_proj(
        x2, g_mix_norm[0][None], w_in[0].astype(BF16), b_in[0][None], cosf, sinf)

    shape3 = lambda t: t.reshape(b, s, t.shape[-1])
    ret = _retention(shape3(qr), shape3(kr), shape3(vr), shape3(gr), ret_norm_g[0])
    att = _attention(shape3(qa), shape3(kva), att_sinks[0], rel_bias_table)

    h1, xloc, yinit, slot_rows, gates, counts = _out_router(
        ret.reshape(n, RET_WIDTH), att.reshape(n, ATT_WIDTH), x2, w_out[0].astype(BF16),
        g_moe_norm[0][None], w_router[0].astype(BF16), b_router[0][None])

    block_e, n_used, chunk_src, chunk_dst = _routing_tables(counts[:, 0, :].astype(jnp.int32))

    wg, wl, wd = _expert_prep(w_up[0], w_down[0])
    yloc = _moe(block_e, n_used, chunk_src, chunk_dst, xloc, wg, wl,
                b_up[0][:, None, 0::2], b_up[0][:, None, 1::2],
                wd, b_down[0].reshape(N_EXPERTS, 1, d), yinit)

    out = _combine_ple(yloc, slot_rows, gates, h1, p[0].reshape(n, -1),
                       w_ple_gate[0].astype(BF16), w_ple_proj[0].astype(BF16),
                       g_ple_norm[0][None], g_final[None])
    return out.reshape(b, s, d)
```

```python
import math

import jax
import jax.numpy as jnp
from jax import lax
from jax.experimental import pallas as pl
from jax.experimental.pallas import tpu as pltpu

RET_HEADS = 4
RET_DK = 128
RET_DV = 128
RET_CHUNK = 128
ROPE_BASE = 10000.0
ATT_Q_HEADS = 8
ATT_KV_HEADS = 2
ATT_GROUP = ATT_Q_HEADS // ATT_KV_HEADS
ATT_HEAD_DIM = 64
WINDOW = 128
ATT_BLOCK = 128
NUM_BUCKETS = 32
MAX_DISTANCE = 128
N_EXPERTS = 32
TOP_K = 4
SWIGLU_ALPHA = 1.702
SWIGLU_LIMIT = 7.0
EPS = 1e-5

RET_WIDTH = RET_HEADS * RET_DV
ATT_WIDTH = ATT_Q_HEADS * ATT_HEAD_DIM
KV_WIDTH = ATT_KV_HEADS * ATT_HEAD_DIM

VMEM_LIMIT_BYTES = 48 * 1024 * 1024

TOKEN_TILE = 512
SUBLANES = 8
ROUTE_TILE = 256
ROUTE_STEP_TILES = 2
CHUNK_ROWS = SUBLANES
LOCAL_ROWS = ROUTE_TILE * TOP_K + N_EXPERTS * CHUNK_ROWS
SPARE_CHUNKS = (LOCAL_ROWS - ROUTE_TILE * TOP_K - N_EXPERTS * (CHUNK_ROWS - 1)) // CHUNK_ROWS
MOE_BLOCK = 256
BLOCK_CHUNKS = MOE_BLOCK // CHUNK_ROWS
ATT_STEP_BLOCKS = 4
PERM_CHUNK = 256

F32 = jnp.float32
BF16 = jnp.bfloat16
NEG = float(jnp.finfo(jnp.float32).min)


def _rms(x, g):
    return x * lax.rsqrt(jnp.mean(x * x, axis=-1, keepdims=True) + EPS) * g


def _dot(a, b):
    return jnp.dot(a, b, preferred_element_type=F32)


def _dot_nt(a, b):
    return lax.dot_general(a, b, (((1,), (1,)), ((), ())), preferred_element_type=F32)


def _dot_tn(a, b):
    return lax.dot_general(a, b, (((0,), (0,)), ((), ())), preferred_element_type=F32)


def _params(semantics):
    return pltpu.CompilerParams(dimension_semantics=semantics, vmem_limit_bytes=VMEM_LIMIT_BYTES)


def _in_proj_kernel(x_ref, g_ref, w_ref, b_ref, cos_ref, sin_ref,
                    qr_ref, kr_ref, vr_ref, gr_ref, qa_ref, kva_ref):
    hn = _rms(x_ref[...], g_ref[...]).astype(BF16)
    cos = jnp.concatenate([cos_ref[...], cos_ref[...]], axis=1)
    sin = jnp.concatenate([-sin_ref[...], sin_ref[...]], axis=1)

    def proj(lo, hi):
        return _dot(hn, w_ref[:, lo:hi]) + b_ref[:, lo:hi]

    def rope(t):
        heads = []
        for h in range(RET_HEADS):
            th = t[:, h * RET_DK:(h + 1) * RET_DK]
            heads.append(th * cos + pltpu.roll(th, RET_DK // 2, 1) * sin)
        return jnp.concatenate(heads, axis=1)

    o = 0
    qr_ref[...] = rope(proj(o, o + RET_WIDTH)).astype(BF16)
    o += RET_WIDTH
    kr_ref[...] = (rope(proj(o, o + RET_WIDTH)) * (RET_DK ** -0.5)).astype(BF16)
    o += RET_WIDTH
    vr_ref[...] = proj(o, o + RET_WIDTH).astype(BF16)
    o += RET_WIDTH
    gr_ref[...] = proj(o, o + RET_WIDTH).astype(BF16)
    o += RET_WIDTH
    qa_ref[...] = (proj(o, o + ATT_WIDTH) * (ATT_HEAD_DIM ** -0.5)).astype(BF16)
    o += ATT_WIDTH
    low = lax.broadcasted_iota(jnp.int32, (1, KV_WIDTH), 1) < ATT_HEAD_DIM
    slabs = []
    for part in range(2):
        t = proj(o + part * KV_WIDTH, o + (part + 1) * KV_WIDTH)
        swapped = pltpu.roll(t, ATT_HEAD_DIM, 1)
        slabs += [jnp.where(low, t, swapped), jnp.where(low, swapped, t)]
    kva_ref[...] = jnp.concatenate(slabs, axis=1).astype(BF16)


def _in_proj(x2, g, w_in, b_in, cosf, sinf):
    n, d = x2.shape
    width = w_in.shape[1]
    tm = TOKEN_TILE
    row = lambda i: (i, 0)
    fixed = lambda i: (0, 0)
    outs = [RET_WIDTH, RET_WIDTH, RET_WIDTH, RET_WIDTH, ATT_WIDTH, 4 * KV_WIDTH]
    return pl.pallas_call(
        _in_proj_kernel,
        grid=(n // tm,),
        in_specs=[
            pl.BlockSpec((tm, d), row),
            pl.BlockSpec((1, d), fixed),
            pl.BlockSpec((d, width), fixed),
            pl.BlockSpec((1, width), fixed),
            pl.BlockSpec((tm, RET_DK // 2), row),
            pl.BlockSpec((tm, RET_DK // 2), row),
        ],
        out_specs=[pl.BlockSpec((tm, w), row) for w in outs],
        out_shape=[jax.ShapeDtypeStruct((n, w), BF16) for w in outs],
        compiler_params=_params(("parallel",)),
        name="in_proj",
    )(x2, g, w_in, b_in, cosf, sinf)


def _retention_kernel(q_ref, k_ref, v_ref, g_ref, dmask_ref, qdec_ref, kdec_ref, cdec_ref,
                      gn_ref, o_ref, state_ref):
    batch = q_ref.shape[0]

    @pl.when(pl.program_id(0) == 0)
    def _():
        state_ref[...] = jnp.zeros_like(state_ref)

    def per_batch(b, carry):
        for h in range(RET_HEADS):
            sl = slice(h * RET_DK, (h + 1) * RET_DK)
            q = q_ref[b, :, sl]
            k = k_ref[b, :, sl]
            v = v_ref[b, :, sl]
            scores = _dot_nt(q, k) * dmask_ref[h]
            inner = _dot(scores.astype(BF16), v)
            state = state_ref[b * RET_HEADS + h]
            cross = _dot(q, state.astype(BF16)) * qdec_ref[h]
            kd = (k.astype(F32) * kdec_ref[h]).astype(BF16)
            state_ref[b * RET_HEADS + h] = state * cdec_ref[h] + _dot_tn(kd, v)
            o = _rms(inner + cross, gn_ref[h])
            gate = g_ref[b, :, sl].astype(F32)
            o_ref[b, :, sl] = (gate * jax.nn.sigmoid(gate) * o).astype(BF16)
        return carry

    lax.fori_loop(0, batch, per_batch, 0)


def _retention(q, k, v, g, gn):
    b, s, w = q.shape
    c = RET_CHUNK
    log_g = jnp.log(1.0 - 2.0 ** (-5.0 - jnp.arange(RET_HEADS, dtype=F32)))
    ci = jnp.arange(c)
    diff = (ci[:, None] - ci[None, :]).astype(F32)
    dmask = jnp.where(diff >= 0, jnp.exp(jnp.maximum(diff, 0.0)[None] * log_g[:, None, None]), 0.0)
    cf = ci.astype(F32)
    kdec = jnp.exp((c - 1 - cf)[None, :] * log_g[:, None])
    qdec = jnp.exp((cf + 1)[None, :] * log_g[:, None])
    cdec = jnp.exp(c * log_g)
    kdec = jnp.broadcast_to(kdec[:, :, None], (RET_HEADS, c, RET_DK))
    qdec = jnp.broadcast_to(qdec[:, :, None], (RET_HEADS, c, RET_DV))
    cdec = jnp.broadcast_to(cdec[:, None, None], (RET_HEADS, RET_DK, RET_DV))
    gn3 = gn.reshape(RET_HEADS, 1, RET_DV)

    seq = lambda i: (0, i, 0)
    fixed = lambda i: (0, 0, 0)
    table = lambda shape: pl.BlockSpec(shape, fixed)
    return pl.pallas_call(
        _retention_kernel,
        grid=(s // c,),
        in_specs=[pl.BlockSpec((b, c, w), seq)] * 4 + [
            table((RET_HEADS, c, c)),
            table((RET_HEADS, c, RET_DV)),
            table((RET_HEADS, c, RET_DK)),
            table((RET_HEADS, RET_DK, RET_DV)),
            table((RET_HEADS, 1, RET_DV)),
        ],
        out_specs=pl.BlockSpec((b, c, w), seq),
        out_shape=jax.ShapeDtypeStruct((b, s, w), BF16),
        scratch_shapes=[pltpu.VMEM((b * RET_HEADS, RET_DK, RET_DV), F32)],
        compiler_params=_params(("arbitrary",)),
        name="retention",
    )(q, k, v, g, dmask, qdec, kdec, cdec, gn3)


def _attention_kernel(sinks_ref, q_ref, kvp_ref, kvc_ref, bias_ref, band_ref, o_ref):
    blk = ATT_BLOCK
    slab = 2 * ATT_HEAD_DIM
    band = band_ref[...] > 0.5
    col = lax.broadcasted_iota(jnp.int32, (1, 2 * blk), 1)
    first_ok = jnp.logical_and(band, jnp.logical_or(col >= blk, pl.program_id(1) > 0))
    low = lax.broadcasted_iota(jnp.int32, (1, slab), 1) < ATT_HEAD_DIM
    zero = jnp.zeros((), BF16)
    for i in range(q_ref.shape[1] // blk):
        rows = slice(i * blk, (i + 1) * blk)
        prev = (lambda cs: kvp_ref[0, :, cs]) if i == 0 else (
            lambda cs, r=slice((i - 1) * blk, i * blk): kvc_ref[0, r, cs])
        ok = first_ok if i == 0 else band
        slabs = []
        for a in range(ATT_Q_HEADS // 2):
            g = (2 * a) // ATT_GROUP
            ks = slice(g * slab, (g + 1) * slab)
            vs = slice((ATT_KV_HEADS + g) * slab, (ATT_KV_HEADS + g + 1) * slab)
            k = jnp.concatenate([prev(ks), kvc_ref[0, rows, ks]], axis=0)
            v = jnp.concatenate([prev(vs), kvc_ref[0, rows, vs]], axis=0)
            q = q_ref[0, rows, a * slab:(a + 1) * slab]
            halves = []
            for part in range(2):
                h = 2 * a + part
                qz = jnp.where(low if part == 0 else jnp.logical_not(low), q, zero)
                s = jnp.where(ok, _dot_nt(qz, k) + bias_ref[h], NEG)
                sink = sinks_ref[h]
                m = jnp.maximum(jnp.max(s, axis=1, keepdims=True), sink)
                p = jnp.exp(s - m)
                denom = jnp.sum(p, axis=1, keepdims=True) + jnp.exp(sink - m)
                halves.append(_dot(p.astype(BF16), v) / denom)
            slabs.append(jnp.where(low, halves[0], halves[1]))
        o_ref[0, rows, :] = jnp.concatenate(slabs, axis=1).astype(BF16)


def _t5_bucket(n):
    max_exact = NUM_BUCKETS // 2
    nf = jnp.maximum(n, 1).astype(F32)
    large = max_exact + (jnp.log(nf / max_exact) / math.log(MAX_DISTANCE / max_exact)
                         * (NUM_BUCKETS - max_exact)).astype(jnp.int32)
    large = jnp.minimum(large, NUM_BUCKETS - 1)
    return jnp.where(n < max_exact, n, large)


def _attention(q, kv, sinks, bias_table):
    b, s, _ = q.shape
    blk = ATT_BLOCK
    qi = jnp.arange(blk)[:, None]
    ki = jnp.arange(2 * blk)[None, :]
    dist = qi + blk - ki
    band = ((dist >= 0) & (dist < WINDOW)).astype(F32)
    onehot = jax.nn.one_hot(_t5_bucket(jnp.maximum(dist, 0)), NUM_BUCKETS, dtype=F32)
    bias = jnp.einsum("qkn,nh->hqk", onehot, bias_table.astype(F32), precision=lax.Precision.HIGHEST)

    sub = ATT_STEP_BLOCKS
    kvw = kv.shape[-1]
    return pl.pallas_call(
        _attention_kernel,
        grid=(b, s // (sub * blk)),
        in_specs=[
            pl.BlockSpec(memory_space=pltpu.SMEM),
            pl.BlockSpec((1, sub * blk, ATT_WIDTH), lambda i, j: (i, j, 0)),
            pl.BlockSpec((1, blk, kvw), lambda i, j: (i, jnp.maximum(sub * j - 1, 0), 0)),
            pl.BlockSpec((1, sub * blk, kvw), lambda i, j: (i, j, 0)),
            pl.BlockSpec((ATT_Q_HEADS, blk, 2 * blk), lambda i, j: (0, 0, 0)),
            pl.BlockSpec((blk, 2 * blk), lambda i, j: (0, 0)),
        ],
        out_specs=pl.BlockSpec((1, sub * blk, ATT_WIDTH), lambda i, j: (i, j, 0)),
        out_shape=jax.ShapeDtypeStruct((b, s, ATT_WIDTH), BF16),
        compiler_params=_params(("parallel", "arbitrary")),
        name="attention",
    )(sinks, q, kv, kv, bias, band)


def _pack_bf16_pairs(lo, hi):
    return pltpu.bitcast(hi, jnp.uint32) | (pltpu.bitcast(lo, jnp.uint32) >> 16)


def _unpack_bf16_pairs(packed):
    lo = pltpu.bitcast(packed << 16, F32).astype(BF16)
    hi = pltpu.bitcast(packed & jnp.uint32(0xFFFF0000), F32).astype(BF16)
    return lo, hi


def _stack_rows(rows, dtype):
    t = rows[0].shape[1]
    sub = lax.broadcasted_iota(jnp.int32, (SUBLANES, t), 0)
    out = jnp.zeros((SUBLANES, t), dtype)
    for i, r in enumerate(rows):
        out = jnp.where(sub == i, r.astype(dtype), out)
    return out


def _route_tile(xb, wrt_ref, brt_ref):
    t = xb.shape[0]
    logits = _dot_nt(wrt_ref[...], xb) + brt_ref[...]
    expert = lax.broadcasted_iota(jnp.int32, logits.shape, 0)
    vals = logits
    top_v, sels = [], []
    for _ in range(TOP_K):
        m = jnp.max(vals, axis=0, keepdims=True)
        idx = jnp.min(jnp.where(vals == m, expert, N_EXPERTS), axis=0, keepdims=True)
        sel = expert == idx
        vals = jnp.where(sel, -jnp.inf, vals)
        top_v.append(m)
        sels.append(sel)
    exps = [jnp.exp(v - top_v[0]) for v in top_v]
    denom = exps[0] + exps[1] + exps[2] + exps[3]
    gates = [e / denom for e in exps]

    chosen = jnp.zeros(logits.shape, F32)
    for sel in sels:
        chosen = chosen + sel.astype(F32)
    chosen_b = chosen.astype(BF16)
    r = lax.broadcasted_iota(jnp.int32, (t, t), 0)
    c = lax.broadcasted_iota(jnp.int32, (t, t), 1)
    earlier = jnp.where(r < c, 1.0, 0.0).astype(BF16)
    before = _dot(chosen_b, earlier)
    counts = jnp.sum(chosen, axis=1, keepdims=True)
    chunks = jnp.floor((counts + (CHUNK_ROWS - 1)) * (1.0 / CHUNK_ROWS))
    er = lax.broadcasted_iota(jnp.int32, (N_EXPERTS, N_EXPERTS), 0)
    ec = lax.broadcasted_iota(jnp.int32, (N_EXPERTS, N_EXPERTS), 1)
    lower = jnp.where(ec < er, 1.0, 0.0).astype(BF16)
    start = _dot(lower, jnp.broadcast_to(chunks, logits.shape).astype(BF16)) * float(CHUNK_ROWS)
    rows = [jnp.sum(jnp.where(sel, before + start, 0.0), axis=0, keepdims=True).astype(jnp.int32)
            for sel in sels]
    counts_rows = _dot_nt(jnp.ones((SUBLANES, t), BF16), chosen_b)

    local = lax.broadcasted_iota(jnp.int32, (LOCAL_ROWS, 1), 0)
    select = jnp.zeros((LOCAL_ROWS, t), F32)
    for row in rows:
        select = jnp.where(local == row, 1.0, select)
    return rows, gates, counts_rows, select.astype(BF16)


def _out_router_kernel(ret_ref, att_ref, x_ref, wo_ref, g_ref, wrt_ref, brt_ref,
                       h1_ref, xloc_ref, yzero_ref, slot_ref, gate_ref, cnt_ref):
    half = x_ref.shape[1] // 2
    for s in range(x_ref.shape[0] // ROUTE_TILE):
        tok = slice(s * ROUTE_TILE, (s + 1) * ROUTE_TILE)
        loc = slice(s * LOCAL_ROWS, (s + 1) * LOCAL_ROWS)
        h1 = (x_ref[tok, :] + _dot(ret_ref[tok, :], wo_ref[:RET_WIDTH, :])
              + _dot(att_ref[tok, :], wo_ref[RET_WIDTH:, :]))
        h1_ref[tok, :] = h1
        xb = _rms(h1, g_ref[...]).astype(BF16)
        rows, gates, counts_rows, select = _route_tile(xb, wrt_ref, brt_ref)
        xloc = _dot(select, xb)
        xloc_ref[loc, :] = _pack_bf16_pairs(xloc[:, :half], xloc[:, half:])
        slot_ref[:, tok] = _stack_rows(rows, jnp.int32)
        gate_ref[:, tok] = _stack_rows(gates, F32)
        cnt_ref[s] = counts_rows
    yzero_ref[...] = jnp.zeros_like(yzero_ref)


def _out_router(ret, att, x2, w_out, g, w_router_t, b_router_t):
    n, d = x2.shape
    per = ROUTE_STEP_TILES
    tm = per * ROUTE_TILE
    tiles = n // ROUTE_TILE
    row = lambda i: (i, 0)
    col = lambda i: (0, i)
    fixed = lambda i: (0, 0)
    return pl.pallas_call(
        _out_router_kernel,
        grid=(n // tm,),
        in_specs=[
            pl.BlockSpec((tm, RET_WIDTH), row),
            pl.BlockSpec((tm, ATT_WIDTH), row),
            pl.BlockSpec((tm, d), row),
            pl.BlockSpec((RET_WIDTH + ATT_WIDTH, d), fixed),
            pl.BlockSpec((1, d), fixed),
            pl.BlockSpec((N_EXPERTS, d), fixed),
            pl.BlockSpec((N_EXPERTS, 1), fixed),
        ],
        out_specs=[
            pl.BlockSpec((tm, d), row),
            pl.BlockSpec((per * LOCAL_ROWS, d // 2), row),
            pl.BlockSpec((per * LOCAL_ROWS, d // 2), row),
            pl.BlockSpec((SUBLANES, tm), col),
            pl.BlockSpec((SUBLANES, tm), col),
            pl.BlockSpec((per, SUBLANES, N_EXPERTS), lambda i: (i, 0, 0)),
        ],
        out_shape=[
            jax.ShapeDtypeStruct((n, d), F32),
            jax.ShapeDtypeStruct((tiles * LOCAL_ROWS, d // 2), jnp.uint32),
            jax.ShapeDtypeStruct((tiles * LOCAL_ROWS, d // 2), jnp.uint32),
            jax.ShapeDtypeStruct((SUBLANES, n), jnp.int32),
            jax.ShapeDtypeStruct((SUBLANES, n), F32),
            jax.ShapeDtypeStruct((tiles, SUBLANES, N_EXPERTS), F32),
        ],
        compiler_params=_params(("parallel",)),
        name="out_router",
    )(ret, att, x2, w_out, g, w_router_t, b_router_t)


def _expert_prep_kernel(wu_ref, wd_ref, wg_ref, wl_ref, wdo_ref):
    ch = PERM_CHUNK
    hc = ch // 2
    r = lax.broadcasted_iota(jnp.int32, (ch, ch), 0)
    c = lax.broadcasted_iota(jnp.int32, (ch, ch), 1)
    perm = jnp.where(r == jnp.where(c < hc, 2 * c, 2 * (c - hc) + 1), 1.0, 0.0).astype(BF16)
    for j in range(wu_ref.shape[1] // ch):
        both = _dot(wu_ref[:, j * ch:(j + 1) * ch].astype(BF16), perm)
        wg_ref[:, j * hc:(j + 1) * hc] = both[:, :hc].astype(BF16)
        wl_ref[:, j * hc:(j + 1) * hc] = both[:, hc:].astype(BF16)
    wdo_ref[...] = wd_ref[...].astype(BF16)


def _expert_prep(w_up, w_down):
    e, d, f2 = w_up.shape
    f = f2 // 2
    expert = lambda i: (i, 0, 0)
    return pl.pallas_call(
        _expert_prep_kernel,
        grid=(e,),
        in_specs=[pl.BlockSpec((None, d, f2), expert), pl.BlockSpec((None, f, d), expert)],
        out_specs=[pl.BlockSpec((None, d, f), expert), pl.BlockSpec((None, d, f), expert),
                   pl.BlockSpec((None, f, d), expert)],
        out_shape=[jax.ShapeDtypeStruct((e, d, f), BF16), jax.ShapeDtypeStruct((e, d, f), BF16),
                   jax.ShapeDtypeStruct((e, f, d), BF16)],
        compiler_params=_params(("parallel",)),
        name="expert_prep",
    )(w_up, w_down)


def _moe_kernel(be_ref, nused_ref, src_ref, dst_ref,
                xloc_hbm, wg_ref, wl_ref, bg_ref, bl_ref, wd_ref, bd_ref, yinit_hbm,
                yloc_hbm, xbuf, ybuf, gsem, ssem):
    del be_ref, yinit_hbm
    i = pl.program_id(0)
    n_used = nused_ref[0]
    slot = lax.rem(i, 2)
    bm, half = xbuf.shape[1], xbuf.shape[2]

    def chunk(ref, c):
        first = c * CHUNK_ROWS if isinstance(c, int) else pl.multiple_of(c * CHUNK_ROWS, CHUNK_ROWS)
        return ref.at[pl.ds(first, CHUNK_ROWS), :]

    def start_gather(step, sl):
        for j in range(BLOCK_CHUNKS):
            pltpu.make_async_copy(chunk(xloc_hbm, src_ref[step * BLOCK_CHUNKS + j]),
                                  chunk(xbuf.at[sl], j), gsem.at[sl]).start()

    def start_writeback(step, sl):
        for j in range(BLOCK_CHUNKS):
            pltpu.make_async_copy(chunk(ybuf.at[sl], j),
                                  chunk(yloc_hbm, dst_ref[step * BLOCK_CHUNKS + j]), ssem.at[sl]).start()

    def wait_gather(sl):
        pltpu.make_async_copy(xloc_hbm.at[pl.ds(0, bm), :], xbuf.at[sl], gsem.at[sl]).wait()

    def wait_writeback(sl):
        pltpu.make_async_copy(ybuf.at[sl], yloc_hbm.at[pl.ds(0, bm), :], ssem.at[sl]).wait()

    @pl.when(i == 0)
    def _():
        start_gather(0, 0)

    @pl.when(i < n_used)
    def _():
        @pl.when(i + 1 < n_used)
        def _():
            start_gather(i + 1, 1 - slot)

        wait_gather(slot)
        lo, hi = _unpack_bf16_pairs(xbuf[slot])

        def up(w_ref, b_ref):
            return _dot(lo, w_ref[:half, :]) + _dot(hi, w_ref[half:, :]) + b_ref[...]

        glu = jnp.minimum(up(wg_ref, bg_ref), SWIGLU_LIMIT)
        lin = jnp.clip(up(wl_ref, bl_ref), -SWIGLU_LIMIT, SWIGLU_LIMIT)
        act = glu * jax.nn.sigmoid(SWIGLU_ALPHA * glu) * (lin + 1.0)
        y = (_dot(act.astype(BF16), wd_ref[...]) + bd_ref[...]).astype(BF16).astype(F32)

        @pl.when(i >= 2)
        def _():
            wait_writeback(slot)

        ybuf[slot] = _pack_bf16_pairs(y[:, :half], y[:, half:])
        start_writeback(i, slot)

        @pl.when(i == n_used - 1)
        def _():
            @pl.when(i >= 1)
            def _():
                wait_writeback(1 - slot)

            wait_writeback(slot)


def _moe(block_e, n_used, chunk_src, chunk_dst, xloc, wg, wl, bg, bl, wd, bd, yinit):
    half = xloc.shape[1]
    e, d, f = wg.shape
    bm = MOE_BLOCK
    n_blocks = chunk_src.shape[0] // BLOCK_CHUNKS
    expert3 = lambda i, be, nu, cs, cd: (be[i], 0, 0)
    grid_spec = pltpu.PrefetchScalarGridSpec(
        num_scalar_prefetch=4,
        grid=(n_blocks,),
        in_specs=[
            pl.BlockSpec(memory_space=pl.ANY),
            pl.BlockSpec((None, d, f), expert3),
            pl.BlockSpec((None, d, f), expert3),
            pl.BlockSpec((None, 1, f), expert3),
            pl.BlockSpec((None, 1, f), expert3),
            pl.BlockSpec((None, f, d), expert3),
            pl.BlockSpec((None, 1, d), expert3),
            pl.BlockSpec(memory_space=pl.ANY),
        ],
        out_specs=pl.BlockSpec(memory_space=pl.ANY),
        scratch_shapes=[
            pltpu.VMEM((2, bm, half), jnp.uint32),
            pltpu.VMEM((2, bm, half), jnp.uint32),
            pltpu.SemaphoreType.DMA((2,)),
            pltpu.SemaphoreType.DMA((2,)),
        ],
    )
    return pl.pallas_call(
        _moe_kernel,
        grid_spec=grid_spec,
        out_shape=jax.ShapeDtypeStruct(yinit.shape, jnp.uint32),
        input_output_aliases={11: 0},
        compiler_params=_params(("arbitrary",)),
        name="moe",
    )(block_e, n_used, chunk_src, chunk_dst, xloc, wg, wl, bg, bl, wd, bd, yinit)


def _combine_ple_kernel(yloc_ref, slot_ref, gate_ref, h1_ref, p_ref, wpg_ref, wpp_ref, gple_ref,
                        gfin_ref, o_ref):
    tm = h1_ref.shape[0]
    rows_local = yloc_ref.shape[0]
    local = lax.broadcasted_iota(jnp.int32, (rows_local, 1), 0)
    rows = slot_ref[...]
    gates = gate_ref[...]
    weights = jnp.zeros((rows_local, tm), F32)
    for k in range(TOP_K):
        weights = jnp.where(local == rows[k:k + 1, :], gates[k:k + 1, :], weights)
    w_hi = weights.astype(BF16)
    w_lo = (weights - w_hi.astype(F32)).astype(BF16)
    y_lo, y_hi = _unpack_bf16_pairs(yloc_ref[...])
    moe = jnp.concatenate([_dot_tn(w_hi, y_lo) + _dot_tn(w_lo, y_lo),
                           _dot_tn(w_hi, y_hi) + _dot_tn(w_lo, y_hi)], axis=1)
    h2 = h1_ref[...] + moe
    gate = jax.nn.sigmoid(_dot(h2.astype(BF16), wpg_ref[...]))
    proj = _dot(p_ref[...].astype(BF16), wpp_ref[...])
    h3 = h2 + _rms(gate * proj, gple_ref[...])
    o_ref[...] = _rms(h3, gfin_ref[...])


def _combine_ple(yloc, slot_rows, gates, h1, p2, wpg, wpp, gple, gfin):
    n, d = h1.shape
    pd = p2.shape[1]
    tm = ROUTE_TILE
    row = lambda i: (i, 0)
    col = lambda i: (0, i)
    fixed = lambda i: (0, 0)
    return pl.pallas_call(
        _combine_ple_kernel,
        grid=(n // tm,),
        in_specs=[
            pl.BlockSpec((LOCAL_ROWS, d // 2), row),
            pl.BlockSpec((SUBLANES, tm), col),
            pl.BlockSpec((SUBLANES, tm), col),
            pl.BlockSpec((tm, d), row),
            pl.BlockSpec((tm, pd), row),
            pl.BlockSpec((d, d), fixed),
            pl.BlockSpec((pd, d), fixed),
            pl.BlockSpec((1, d), fixed),
            pl.BlockSpec((1, d), fixed),
        ],
        out_specs=pl.BlockSpec((tm, d), row),
        out_shape=jax.ShapeDtypeStruct((n, d), F32),
        compiler_params=_params(("parallel",)),
        name="combine_ple",
    )(yloc, slot_rows, gates, h1, p2, wpg, wpp, gple, gfin)


def _routing_tables(counts):
    tiles = counts.shape[0]
    local_chunks = LOCAL_ROWS // CHUNK_ROWS
    assert 2 * BLOCK_CHUNKS <= tiles * SPARE_CHUNKS
    run = (counts + CHUNK_ROWS - 1) // CHUNK_ROWS
    local_start = jnp.cumsum(run, axis=1) - run
    run_end = jnp.cumsum(run, axis=0)
    run_start = run_end - run
    total = run_end[-1]
    padded = (total + BLOCK_CHUNKS - 1) // BLOCK_CHUNKS * BLOCK_CHUNKS
    pend = jnp.cumsum(padded)
    poff = pend - padded

    slots = tiles * ROUTE_TILE * TOP_K
    max_chunks = (slots + tiles * N_EXPERTS * (CHUNK_ROWS - 1)) // CHUNK_ROWS + N_EXPERTS * (BLOCK_CHUNKS - 1)
    n_chunks = -(-max_chunks // BLOCK_CHUNKS) * BLOCK_CHUNKS
    lookup = lambda onehot, table: jnp.dot(onehot, table.astype(F32), precision=lax.Precision.HIGHEST)
    q = jnp.arange(n_chunks, dtype=jnp.int32)
    e = jnp.minimum(jnp.sum(q[:, None] >= pend[None, :], axis=1), N_EXPERTS - 1).astype(jnp.int32)
    e_hot = (e[:, None] == jnp.arange(N_EXPERTS)[None, :]).astype(F32)
    per_expert = lookup(e_hot, jnp.stack([poff, total], axis=1)).astype(jnp.int32)
    j = q - per_expert[:, 0]
    valid = j < per_expert[:, 1]
    ends = lookup(e_hot, run_end.T).astype(jnp.int32)
    tile = jnp.minimum(jnp.sum(ends <= j[:, None], axis=1), tiles - 1).astype(jnp.int32)
    tile_hot = (tile[:, None] == jnp.arange(tiles)[None, :]).astype(F32)
    shift = jnp.sum(lookup(tile_hot, local_start - run_start) * e_hot, axis=1).astype(jnp.int32)
    src = jnp.where(valid, tile * local_chunks + shift + j, 0)
    spare = (q // BLOCK_CHUNKS % 2) * BLOCK_CHUNKS + q % BLOCK_CHUNKS
    spare_chunk = (spare // SPARE_CHUNKS) * local_chunks + (local_chunks - SPARE_CHUNKS) + spare % SPARE_CHUNKS
    dst = jnp.where(valid, src, spare_chunk)
    n_used = (pend[-1:] // BLOCK_CHUNKS).astype(jnp.int32)
    return e[::BLOCK_CHUNKS], n_used, src.astype(jnp.int32), dst.astype(jnp.int32)


def kernel(x, p, positions, rel_bias_table, g_mix_norm, w_in, b_in, ret_norm_g, att_sinks, w_out,
           g_moe_norm, w_router, b_router, w_up, b_up, w_down, b_down, w_ple_gate, w_ple_proj,
           g_ple_norm, g_final):
    b, s, d = x.shape
    depth = w_in.shape[0]
    assert depth == 1, "single-layer stack only"
    n = b * s
    x2 = x.reshape(n, d)

    half = RET_DK // 2
    inv = ROPE_BASE ** (-jnp.arange(half, dtype=F32) / half)
    ang = (positions.astype(F32)[..., None] * inv).reshape(n, half)

    qr, kr, vr, gr, qa, kva = _in_proj(
        x2, g_mix_norm[0][None], w_in[0].astype(BF16), b_in[0][None], jnp.cos(ang), jnp.sin(ang))

    shape3 = lambda t: t.reshape(b, s, t.shape[-1])
    ret = _retention(shape3(qr), shape3(kr), shape3(vr), shape3(gr), ret_norm_g[0])
    att = _attention(shape3(qa), shape3(kva), att_sinks[0], rel_bias_table)

    h1, xloc, yinit, slot_rows, gates, counts = _out_router(
        ret.reshape(n, RET_WIDTH), att.reshape(n, ATT_WIDTH), x2, w_out[0].astype(BF16),
        g_moe_norm[0][None], w_router[0].T.astype(BF16), b_router[0][:, None])

    block_e, n_used, chunk_src, chunk_dst = _routing_tables(counts[:, 0, :].astype(jnp.int32))

    wg, wl, wd = _expert_prep(w_up[0], w_down[0])
    yloc = _moe(block_e, n_used, chunk_src, chunk_dst, xloc, wg, wl,
                b_up[0][:, None, 0::2], b_up[0][:, None, 1::2],
                wd, b_down[0].reshape(N_EXPERTS, 1, d), yinit)

    out = _combine_ple(yloc, slot_rows, gates, h1, p[0].reshape(n, -1),
                       w_ple_gate[0].astype(BF16), w_ple_proj[0].astype(BF16),
                       g_ple_norm[0][None], g_final[None])
    return out.reshape(b, s, d)
```

```python
import math

import jax
import jax.numpy as jnp
from jax import lax
from jax.experimental import pallas as pl
from jax.experimental.pallas import tpu as pltpu

RET_HEADS = 4
RET_DK = 128
RET_DV = 128
RET_CHUNK = 128
ROPE_BASE = 10000.0
ATT_Q_HEADS = 8
ATT_KV_HEADS = 2
ATT_GROUP = ATT_Q_HEADS // ATT_KV_HEADS
ATT_HEAD_DIM = 64
WINDOW = 128
ATT_BLOCK = 128
NUM_BUCKETS = 32
MAX_DISTANCE = 128
N_EXPERTS = 32
TOP_K = 4
SWIGLU_ALPHA = 1.702
SWIGLU_LIMIT = 7.0
EPS = 1e-5

RET_WIDTH = RET_HEADS * RET_DV
ATT_WIDTH = ATT_Q_HEADS * ATT_HEAD_DIM
KV_WIDTH = ATT_KV_HEADS * ATT_HEAD_DIM

VMEM_LIMIT_BYTES = 48 * 1024 * 1024

TOKEN_TILE = 512
SUBLANES = 8
ROUTE_TILE = 256
ROUTE_STEP_TILES = 2
CHUNK_ROWS = SUBLANES
LOCAL_ROWS = ROUTE_TILE * TOP_K + N_EXPERTS * CHUNK_ROWS
SPARE_CHUNKS = (LOCAL_ROWS - ROUTE_TILE * TOP_K - N_EXPERTS * (CHUNK_ROWS - 1)) // CHUNK_ROWS
MOE_BLOCK = 256
BLOCK_CHUNKS = MOE_BLOCK // CHUNK_ROWS
PRIME_BLOCKS = 2
ATT_STEP_BLOCKS = 4
PERM_CHUNK = 256

F32 = jnp.float32
BF16 = jnp.bfloat16
NEG = float(jnp.finfo(jnp.float32).min)


def _rms(x, g):
    return x * lax.rsqrt(jnp.mean(x * x, axis=-1, keepdims=True) + EPS) * g


def _dot(a, b):
    return jnp.dot(a, b, preferred_element_type=F32)


def _dot_nt(a, b):
    return lax.dot_general(a, b, (((1,), (1,)), ((), ())), preferred_element_type=F32)


def _dot_tn(a, b):
    return lax.dot_general(a, b, (((0,), (0,)), ((), ())), preferred_element_type=F32)


def _params(semantics):
    return pltpu.CompilerParams(dimension_semantics=semantics, vmem_limit_bytes=VMEM_LIMIT_BYTES)


def _in_proj_kernel(x_ref, g_ref, w_ref, b_ref, cos_ref, sin_ref,
                    qr_ref, kr_ref, vr_ref, gr_ref, qa_ref, kva_ref):
    hn = _rms(x_ref[...], g_ref[...]).astype(BF16)
    cos = jnp.concatenate([cos_ref[...], cos_ref[...]], axis=1)
    sin = jnp.concatenate([-sin_ref[...], sin_ref[...]], axis=1)

    def proj(lo, hi):
        return _dot(hn, w_ref[:, lo:hi]) + b_ref[:, lo:hi]

    def rope(t):
        heads = []
        for h in range(RET_HEADS):
            th = t[:, h * RET_DK:(h + 1) * RET_DK]
            heads.append(th * cos + pltpu.roll(th, RET_DK // 2, 1) * sin)
        return jnp.concatenate(heads, axis=1)

    o = 0
    qr_ref[...] = rope(proj(o, o + RET_WIDTH)).astype(BF16)
    o += RET_WIDTH
    kr_ref[...] = (rope(proj(o, o + RET_WIDTH)) * (RET_DK ** -0.5)).astype(BF16)
    o += RET_WIDTH
    vr_ref[...] = proj(o, o + RET_WIDTH).astype(BF16)
    o += RET_WIDTH
    gr_ref[...] = proj(o, o + RET_WIDTH).astype(BF16)
    o += RET_WIDTH
    qa_ref[...] = (proj(o, o + ATT_WIDTH) * (ATT_HEAD_DIM ** -0.5)).astype(BF16)
    o += ATT_WIDTH
    low = lax.broadcasted_iota(jnp.int32, (1, KV_WIDTH), 1) < ATT_HEAD_DIM
    slabs = []
    for part in range(2):
        t = proj(o + part * KV_WIDTH, o + (part + 1) * KV_WIDTH)
        swapped = pltpu.roll(t, ATT_HEAD_DIM, 1)
        slabs += [jnp.where(low, t, swapped), jnp.where(low, swapped, t)]
    kva_ref[...] = jnp.concatenate(slabs, axis=1).astype(BF16)


def _in_proj(x2, g, w_in, b_in, cosf, sinf):
    n, d = x2.shape
    width = w_in.shape[1]
    tm = TOKEN_TILE
    row = lambda i: (i, 0)
    fixed = lambda i: (0, 0)
    outs = [RET_WIDTH, RET_WIDTH, RET_WIDTH, RET_WIDTH, ATT_WIDTH, 4 * KV_WIDTH]
    return pl.pallas_call(
        _in_proj_kernel,
        grid=(n // tm,),
        in_specs=[
            pl.BlockSpec((tm, d), row),
            pl.BlockSpec((1, d), fixed),
            pl.BlockSpec((d, width), fixed),
            pl.BlockSpec((1, width), fixed),
            pl.BlockSpec((tm, RET_DK // 2), row),
            pl.BlockSpec((tm, RET_DK // 2), row),
        ],
        out_specs=[pl.BlockSpec((tm, w), row) for w in outs],
        out_shape=[jax.ShapeDtypeStruct((n, w), BF16) for w in outs],
        compiler_params=_params(("parallel",)),
        name="in_proj",
    )(x2, g, w_in, b_in, cosf, sinf)


def _retention_kernel(q_ref, k_ref, v_ref, g_ref, dmask_ref, qdec_ref, kdec_ref, cdec_ref,
                      gn_ref, o_ref, state_ref):
    batch = q_ref.shape[0]

    @pl.when(pl.program_id(0) == 0)
    def _():
        state_ref[...] = jnp.zeros_like(state_ref)

    def per_batch(b, carry):
        for h in range(RET_HEADS):
            sl = slice(h * RET_DK, (h + 1) * RET_DK)
            q = q_ref[b, :, sl]
            k = k_ref[b, :, sl]
            v = v_ref[b, :, sl]
            scores = _dot_nt(q, k) * dmask_ref[h]
            inner = _dot(scores.astype(BF16), v)
            state = state_ref[b * RET_HEADS + h]
            cross = _dot(q, state.astype(BF16)) * qdec_ref[h]
            kd = (k.astype(F32) * kdec_ref[h]).astype(BF16)
            state_ref[b * RET_HEADS + h] = state * cdec_ref[h] + _dot_tn(kd, v)
            o = _rms(inner + cross, gn_ref[h])
            gate = g_ref[b, :, sl].astype(F32)
            o_ref[b, :, sl] = (gate * jax.nn.sigmoid(gate) * o).astype(BF16)
        return carry

    lax.fori_loop(0, batch, per_batch, 0)


def _retention(q, k, v, g, gn):
    b, s, w = q.shape
    c = RET_CHUNK
    log_g = jnp.log(1.0 - 2.0 ** (-5.0 - jnp.arange(RET_HEADS, dtype=F32)))
    ci = jnp.arange(c)
    diff = (ci[:, None] - ci[None, :]).astype(F32)
    dmask = jnp.where(diff >= 0, jnp.exp(jnp.maximum(diff, 0.0)[None] * log_g[:, None, None]), 0.0)
    cf = ci.astype(F32)
    kdec = jnp.exp((c - 1 - cf)[None, :] * log_g[:, None])
    qdec = jnp.exp((cf + 1)[None, :] * log_g[:, None])
    cdec = jnp.exp(c * log_g)
    kdec = jnp.broadcast_to(kdec[:, :, None], (RET_HEADS, c, RET_DK))
    qdec = jnp.broadcast_to(qdec[:, :, None], (RET_HEADS, c, RET_DV))
    cdec = jnp.broadcast_to(cdec[:, None, None], (RET_HEADS, RET_DK, RET_DV))
    gn3 = gn.reshape(RET_HEADS, 1, RET_DV)

    seq = lambda i: (0, i, 0)
    fixed = lambda i: (0, 0, 0)
    table = lambda shape: pl.BlockSpec(shape, fixed)
    return pl.pallas_call(
        _retention_kernel,
        grid=(s // c,),
        in_specs=[pl.BlockSpec((b, c, w), seq)] * 4 + [
            table((RET_HEADS, c, c)),
            table((RET_HEADS, c, RET_DV)),
            table((RET_HEADS, c, RET_DK)),
            table((RET_HEADS, RET_DK, RET_DV)),
            table((RET_HEADS, 1, RET_DV)),
        ],
        out_specs=pl.BlockSpec((b, c, w), seq),
        out_shape=jax.ShapeDtypeStruct((b, s, w), BF16),
        scratch_shapes=[pltpu.VMEM((b * RET_HEADS, RET_DK, RET_DV), F32)],
        compiler_params=_params(("arbitrary",)),
        name="retention",
    )(q, k, v, g, dmask, qdec, kdec, cdec, gn3)


def _attention_kernel(sinks_ref, q_ref, kvp_ref, kvc_ref, bias_ref, band_ref, o_ref):
    blk = ATT_BLOCK
    slab = 2 * ATT_HEAD_DIM
    band = band_ref[...] > 0.5
    col = lax.broadcasted_iota(jnp.int32, (1, 2 * blk), 1)
    first_ok = jnp.logical_and(band, jnp.logical_or(col >= blk, pl.program_id(1) > 0))
    low = lax.broadcasted_iota(jnp.int32, (1, slab), 1) < ATT_HEAD_DIM
    zero = jnp.zeros((), BF16)
    for i in range(q_ref.shape[1] // blk):
        rows = slice(i * blk, (i + 1) * blk)
        prev = (lambda cs: kvp_ref[0, :, cs]) if i == 0 else (
            lambda cs, r=slice((i - 1) * blk, i * blk): kvc_ref[0, r, cs])
        ok = first_ok if i == 0 else band
        slabs = []
        for a in range(ATT_Q_HEADS // 2):
            g = (2 * a) // ATT_GROUP
            ks = slice(g * slab, (g + 1) * slab)
            vs = slice((ATT_KV_HEADS + g) * slab, (ATT_KV_HEADS + g + 1) * slab)
            k = jnp.concatenate([prev(ks), kvc_ref[0, rows, ks]], axis=0)
            v = jnp.concatenate([prev(vs), kvc_ref[0, rows, vs]], axis=0)
            q = q_ref[0, rows, a * slab:(a + 1) * slab]
            halves = []
            for part in range(2):
                h = 2 * a + part
                qz = jnp.where(low if part == 0 else jnp.logical_not(low), q, zero)
                s = jnp.where(ok, _dot_nt(qz, k) + bias_ref[h], NEG)
                sink = sinks_ref[h]
                m = jnp.maximum(jnp.max(s, axis=1, keepdims=True), sink)
                p = jnp.exp(s - m)
                denom = jnp.sum(p, axis=1, keepdims=True) + jnp.exp(sink - m)
                halves.append(_dot(p.astype(BF16), v) / denom)
            slabs.append(jnp.where(low, halves[0], halves[1]))
        o_ref[0, rows, :] = jnp.concatenate(slabs, axis=1).astype(BF16)


def _t5_bucket(n):
    max_exact = NUM_BUCKETS // 2
    nf = jnp.maximum(n, 1).astype(F32)
    large = max_exact + (jnp.log(nf / max_exact) / math.log(MAX_DISTANCE / max_exact)
                         * (NUM_BUCKETS - max_exact)).astype(jnp.int32)
    large = jnp.minimum(large, NUM_BUCKETS - 1)
    return jnp.where(n < max_exact, n, large)


def _attention(q, kv, sinks, bias_table):
    b, s, _ = q.shape
    blk = ATT_BLOCK
    qi = jnp.arange(blk)[:, None]
    ki = jnp.arange(2 * blk)[None, :]
    dist = qi + blk - ki
    band = ((dist >= 0) & (dist < WINDOW)).astype(F32)
    onehot = jax.nn.one_hot(_t5_bucket(jnp.maximum(dist, 0)), NUM_BUCKETS, dtype=F32)
    bias = jnp.einsum("qkn,nh->hqk", onehot, bias_table.astype(F32), precision=lax.Precision.HIGHEST)

    sub = ATT_STEP_BLOCKS
    kvw = kv.shape[-1]
    return pl.pallas_call(
        _attention_kernel,
        grid=(b, s // (sub * blk)),
        in_specs=[
            pl.BlockSpec(memory_space=pltpu.SMEM),
            pl.BlockSpec((1, sub * blk, ATT_WIDTH), lambda i, j: (i, j, 0)),
            pl.BlockSpec((1, blk, kvw), lambda i, j: (i, jnp.maximum(sub * j - 1, 0), 0)),
            pl.BlockSpec((1, sub * blk, kvw), lambda i, j: (i, j, 0)),
            pl.BlockSpec((ATT_Q_HEADS, blk, 2 * blk), lambda i, j: (0, 0, 0)),
            pl.BlockSpec((blk, 2 * blk), lambda i, j: (0, 0)),
        ],
        out_specs=pl.BlockSpec((1, sub * blk, ATT_WIDTH), lambda i, j: (i, j, 0)),
        out_shape=jax.ShapeDtypeStruct((b, s, ATT_WIDTH), BF16),
        compiler_params=_params(("parallel", "arbitrary")),
        name="attention",
    )(sinks, q, kv, kv, bias, band)


def _pack_bf16_pairs(lo, hi):
    return pltpu.bitcast(hi, jnp.uint32) | (pltpu.bitcast(lo, jnp.uint32) >> 16)


def _unpack_bf16_pairs(packed):
    lo = pltpu.bitcast(packed << 16, F32).astype(BF16)
    hi = pltpu.bitcast(packed & jnp.uint32(0xFFFF0000), F32).astype(BF16)
    return lo, hi


def _stack_rows(rows, dtype):
    t = rows[0].shape[1]
    sub = lax.broadcasted_iota(jnp.int32, (SUBLANES, t), 0)
    out = jnp.zeros((SUBLANES, t), dtype)
    for i, r in enumerate(rows):
        out = jnp.where(sub == i, r.astype(dtype), out)
    return out


def _route_tile(xb, wrt_ref, brt_ref):
    t = xb.shape[0]
    logits = _dot_nt(wrt_ref[...], xb) + brt_ref[...]
    expert = lax.broadcasted_iota(jnp.int32, logits.shape, 0)
    vals = logits
    top_v, sels = [], []
    for _ in range(TOP_K):
        m = jnp.max(vals, axis=0, keepdims=True)
        idx = jnp.min(jnp.where(vals == m, expert, N_EXPERTS), axis=0, keepdims=True)
        sel = expert == idx
        vals = jnp.where(sel, -jnp.inf, vals)
        top_v.append(m)
        sels.append(sel)
    exps = [jnp.exp(v - top_v[0]) for v in top_v]
    denom = exps[0] + exps[1] + exps[2] + exps[3]
    gates = [e / denom for e in exps]

    chosen = jnp.zeros(logits.shape, F32)
    for sel in sels:
        chosen = chosen + sel.astype(F32)
    chosen_b = chosen.astype(BF16)
    r = lax.broadcasted_iota(jnp.int32, (t, t), 0)
    c = lax.broadcasted_iota(jnp.int32, (t, t), 1)
    earlier = jnp.where(r < c, 1.0, 0.0).astype(BF16)
    before = _dot(chosen_b, earlier)
    counts = jnp.sum(chosen, axis=1, keepdims=True)
    chunks = jnp.floor((counts + (CHUNK_ROWS - 1)) * (1.0 / CHUNK_ROWS))
    er = lax.broadcasted_iota(jnp.int32, (N_EXPERTS, N_EXPERTS), 0)
    ec = lax.broadcasted_iota(jnp.int32, (N_EXPERTS, N_EXPERTS), 1)
    lower = jnp.where(ec < er, 1.0, 0.0).astype(BF16)
    start = _dot(lower, jnp.broadcast_to(chunks, logits.shape).astype(BF16)) * float(CHUNK_ROWS)
    rows = [jnp.sum(jnp.where(sel, before + start, 0.0), axis=0, keepdims=True).astype(jnp.int32)
            for sel in sels]
    counts_rows = _dot_nt(jnp.ones((SUBLANES, t), BF16), chosen_b)

    local = lax.broadcasted_iota(jnp.int32, (LOCAL_ROWS, 1), 0)
    select = jnp.zeros((LOCAL_ROWS, t), F32)
    for row in rows:
        select = jnp.where(local == row, 1.0, select)
    return rows, gates, counts_rows, select.astype(BF16)


def _out_router_kernel(ret_ref, att_ref, x_ref, wo_ref, g_ref, wrt_ref, brt_ref,
                       h1_ref, xloc_ref, yzero_ref, slot_ref, gate_ref, cnt_ref):
    half = x_ref.shape[1] // 2
    for s in range(x_ref.shape[0] // ROUTE_TILE):
        tok = slice(s * ROUTE_TILE, (s + 1) * ROUTE_TILE)
        loc = slice(s * LOCAL_ROWS, (s + 1) * LOCAL_ROWS)
        h1 = (x_ref[tok, :] + _dot(ret_ref[tok, :], wo_ref[:RET_WIDTH, :])
              + _dot(att_ref[tok, :], wo_ref[RET_WIDTH:, :]))
        h1_ref[tok, :] = h1
        xb = _rms(h1, g_ref[...]).astype(BF16)
        rows, gates, counts_rows, select = _route_tile(xb, wrt_ref, brt_ref)
        xloc = _dot(select, xb)
        xloc_ref[loc, :] = _pack_bf16_pairs(xloc[:, :half], xloc[:, half:])
        slot_ref[:, tok] = _stack_rows(rows, jnp.int32)
        gate_ref[:, tok] = _stack_rows(gates, F32)
        cnt_ref[s] = counts_rows
    yzero_ref[...] = jnp.zeros_like(yzero_ref)


def _out_router(ret, att, x2, w_out, g, w_router_t, b_router_t):
    n, d = x2.shape
    per = ROUTE_STEP_TILES
    tm = per * ROUTE_TILE
    tiles = n // ROUTE_TILE
    row = lambda i: (i, 0)
    col = lambda i: (0, i)
    fixed = lambda i: (0, 0)
    return pl.pallas_call(
        _out_router_kernel,
        grid=(n // tm,),
        in_specs=[
            pl.BlockSpec((tm, RET_WIDTH), row),
            pl.BlockSpec((tm, ATT_WIDTH), row),
            pl.BlockSpec((tm, d), row),
            pl.BlockSpec((RET_WIDTH + ATT_WIDTH, d), fixed),
            pl.BlockSpec((1, d), fixed),
            pl.BlockSpec((N_EXPERTS, d), fixed),
            pl.BlockSpec((N_EXPERTS, 1), fixed),
        ],
        out_specs=[
            pl.BlockSpec((tm, d), row),
            pl.BlockSpec((per * LOCAL_ROWS, d // 2), row),
            pl.BlockSpec((per * LOCAL_ROWS, d // 2), row),
            pl.BlockSpec((SUBLANES, tm), col),
            pl.BlockSpec((SUBLANES, tm), col),
            pl.BlockSpec((per, SUBLANES, N_EXPERTS), lambda i: (i, 0, 0)),
        ],
        out_shape=[
            jax.ShapeDtypeStruct((n, d), F32),
            jax.ShapeDtypeStruct((tiles * LOCAL_ROWS, d // 2), jnp.uint32),
            jax.ShapeDtypeStruct((tiles * LOCAL_ROWS, d // 2), jnp.uint32),
            jax.ShapeDtypeStruct((SUBLANES, n), jnp.int32),
            jax.ShapeDtypeStruct((SUBLANES, n), F32),
            jax.ShapeDtypeStruct((tiles, SUBLANES, N_EXPERTS), F32),
        ],
        compiler_params=_params(("parallel",)),
        name="out_router",
    )(ret, att, x2, w_out, g, w_router_t, b_router_t)


def _expert_prep_kernel(wu_ref, wd_ref, wg_ref, wl_ref, wdo_ref):
    ch = PERM_CHUNK
    hc = ch // 2
    r = lax.broadcasted_iota(jnp.int32, (ch, ch), 0)
    c = lax.broadcasted_iota(jnp.int32, (ch, ch), 1)
    perm = jnp.where(r == jnp.where(c < hc, 2 * c, 2 * (c - hc) + 1), 1.0, 0.0).astype(BF16)
    for j in range(wu_ref.shape[1] // ch):
        both = _dot(wu_ref[:, j * ch:(j + 1) * ch].astype(BF16), perm)
        wg_ref[:, j * hc:(j + 1) * hc] = both[:, :hc].astype(BF16)
        wl_ref[:, j * hc:(j + 1) * hc] = both[:, hc:].astype(BF16)
    wdo_ref[...] = wd_ref[...].astype(BF16)


def _expert_prep(w_up, w_down):
    e, d, f2 = w_up.shape
    f = f2 // 2
    expert = lambda i: (i, 0, 0)
    return pl.pallas_call(
        _expert_prep_kernel,
        grid=(e,),
        in_specs=[pl.BlockSpec((None, d, f2), expert), pl.BlockSpec((None, f, d), expert)],
        out_specs=[pl.BlockSpec((None, d, f), expert), pl.BlockSpec((None, d, f), expert),
                   pl.BlockSpec((None, f, d), expert)],
        out_shape=[jax.ShapeDtypeStruct((e, d, f), BF16), jax.ShapeDtypeStruct((e, d, f), BF16),
                   jax.ShapeDtypeStruct((e, f, d), BF16)],
        compiler_params=_params(("parallel",)),
        name="expert_prep",
    )(w_up, w_down)


def _moe_kernel(be_ref, nused_ref, src_ref, dst_ref,
                xloc_hbm, wg_ref, wl_ref, bg_ref, bl_ref, wd_ref, bd_ref, yinit_hbm,
                yloc_hbm, xbuf, ybuf, gsem, ssem):
    del be_ref, yinit_hbm
    i = pl.program_id(0)
    n_used = nused_ref[0]
    last = pl.num_programs(0) - 1
    slot = lax.rem(i, 2)
    bm, half = xbuf.shape[1], xbuf.shape[2]

    def chunk(ref, c):
        first = c * CHUNK_ROWS if isinstance(c, int) else pl.multiple_of(c * CHUNK_ROWS, CHUNK_ROWS)
        return ref.at[pl.ds(first, CHUNK_ROWS), :]

    def start_gather(step, sl):
        for j in range(BLOCK_CHUNKS):
            pltpu.make_async_copy(chunk(xloc_hbm, src_ref[step * BLOCK_CHUNKS + j]),
                                  chunk(xbuf.at[sl], j), gsem.at[sl]).start()

    def start_writeback(step, sl):
        for j in range(BLOCK_CHUNKS):
            dst = dst_ref[(step + PRIME_BLOCKS) * BLOCK_CHUNKS + j]
            pltpu.make_async_copy(chunk(ybuf.at[sl], j), chunk(yloc_hbm, dst), ssem.at[sl]).start()

    def wait_gather(sl):
        pltpu.make_async_copy(xloc_hbm.at[pl.ds(0, bm), :], xbuf.at[sl], gsem.at[sl]).wait()

    def wait_writeback(sl):
        pltpu.make_async_copy(ybuf.at[sl], yloc_hbm.at[pl.ds(0, bm), :], ssem.at[sl]).wait()

    @pl.when(i == 0)
    def _():
        ybuf[...] = jnp.zeros_like(ybuf)
        start_gather(0, 0)
        start_writeback(-2, 0)

    @pl.when(i < n_used)
    def _():
        wait_gather(slot)
        start_gather(jnp.minimum(i + 1, last), 1 - slot)
        lo, hi = _unpack_bf16_pairs(xbuf[slot])
        start_writeback(i - 1, 1 - slot)

        def up(w_ref, b_ref):
            return _dot(lo, w_ref[:half, :]) + _dot(hi, w_ref[half:, :]) + b_ref[...]

        glu = jnp.minimum(up(wg_ref, bg_ref), SWIGLU_LIMIT)
        lin = jnp.clip(up(wl_ref, bl_ref), -SWIGLU_LIMIT, SWIGLU_LIMIT)
        act = glu * jax.nn.sigmoid(SWIGLU_ALPHA * glu) * (lin + 1.0)
        y = (_dot(act.astype(BF16), wd_ref[...]) + bd_ref[...]).astype(BF16).astype(F32)

        wait_writeback(slot)
        ybuf[slot] = _pack_bf16_pairs(y[:, :half], y[:, half:])

        @pl.when(i == n_used - 1)
        def _():
            start_writeback(i, slot)
            wait_writeback(1 - slot)
            wait_writeback(slot)
            wait_gather(1 - slot)


def _moe(block_e, n_used, chunk_src, chunk_dst, xloc, wg, wl, bg, bl, wd, bd, yinit):
    half = xloc.shape[1]
    e, d, f = wg.shape
    bm = MOE_BLOCK
    n_blocks = chunk_src.shape[0] // BLOCK_CHUNKS
    assert chunk_dst.shape[0] == (n_blocks + PRIME_BLOCKS) * BLOCK_CHUNKS
    expert3 = lambda i, be, nu, cs, cd: (be[i], 0, 0)
    grid_spec = pltpu.PrefetchScalarGridSpec(
        num_scalar_prefetch=4,
        grid=(n_blocks,),
        in_specs=[
            pl.BlockSpec(memory_space=pl.ANY),
            pl.BlockSpec((None, d, f), expert3),
            pl.BlockSpec((None, d, f), expert3),
            pl.BlockSpec((None, 1, f), expert3),
            pl.BlockSpec((None, 1, f), expert3),
            pl.BlockSpec((None, f, d), expert3),
            pl.BlockSpec((None, 1, d), expert3),
            pl.BlockSpec(memory_space=pl.ANY),
        ],
        out_specs=pl.BlockSpec(memory_space=pl.ANY),
        scratch_shapes=[
            pltpu.VMEM((2, bm, half), jnp.uint32),
            pltpu.VMEM((2, bm, half), jnp.uint32),
            pltpu.SemaphoreType.DMA((2,)),
            pltpu.SemaphoreType.DMA((2,)),
        ],
    )
    return pl.pallas_call(
        _moe_kernel,
        grid_spec=grid_spec,
        out_shape=jax.ShapeDtypeStruct(yinit.shape, jnp.uint32),
        input_output_aliases={11: 0},
        compiler_params=_params(("arbitrary",)),
        name="moe",
    )(block_e, n_used, chunk_src, chunk_dst, xloc, wg, wl, bg, bl, wd, bd, yinit)


def _combine_ple_kernel(yloc_ref, slot_ref, gate_ref, h1_ref, p_ref, wpg_ref, wpp_ref, gple_ref,
                        gfin_ref, o_ref):
    tm = h1_ref.shape[0]
    rows_local = yloc_ref.shape[0]
    local = lax.broadcasted_iota(jnp.int32, (rows_local, 1), 0)
    rows = slot_ref[...]
    gates = gate_ref[...]
    weights = jnp.zeros((rows_local, tm), F32)
    for k in range(TOP_K):
        weights = jnp.where(local == rows[k:k + 1, :], gates[k:k + 1, :], weights)
    w_hi = weights.astype(BF16)
    w_lo = (weights - w_hi.astype(F32)).astype(BF16)
    y_lo, y_hi = _unpack_bf16_pairs(yloc_ref[...])
    moe = jnp.concatenate([_dot_tn(w_hi, y_lo) + _dot_tn(w_lo, y_lo),
                           _dot_tn(w_hi, y_hi) + _dot_tn(w_lo, y_hi)], axis=1)
    h2 = h1_ref[...] + moe
    gate = jax.nn.sigmoid(_dot(h2.astype(BF16), wpg_ref[...]))
    proj = _dot(p_ref[...].astype(BF16), wpp_ref[...])
    h3 = h2 + _rms(gate * proj, gple_ref[...])
    o_ref[...] = _rms(h3, gfin_ref[...])


def _combine_ple(yloc, slot_rows, gates, h1, p2, wpg, wpp, gple, gfin):
    n, d = h1.shape
    pd = p2.shape[1]
    tm = ROUTE_TILE
    row = lambda i: (i, 0)
    col = lambda i: (0, i)
    fixed = lambda i: (0, 0)
    return pl.pallas_call(
        _combine_ple_kernel,
        grid=(n // tm,),
        in_specs=[
            pl.BlockSpec((LOCAL_ROWS, d // 2), row),
            pl.BlockSpec((SUBLANES, tm), col),
            pl.BlockSpec((SUBLANES, tm), col),
            pl.BlockSpec((tm, d), row),
            pl.BlockSpec((tm, pd), row),
            pl.BlockSpec((d, d), fixed),
            pl.BlockSpec((pd, d), fixed),
            pl.BlockSpec((1, d), fixed),
            pl.BlockSpec((1, d), fixed),
        ],
        out_specs=pl.BlockSpec((tm, d), row),
        out_shape=jax.ShapeDtypeStruct((n, d), F32),
        compiler_params=_params(("parallel",)),
        name="combine_ple",
    )(yloc, slot_rows, gates, h1, p2, wpg, wpp, gple, gfin)


def _routing_tables(counts):
    tiles = counts.shape[0]
    local_chunks = LOCAL_ROWS // CHUNK_ROWS
    assert 2 * BLOCK_CHUNKS <= tiles * SPARE_CHUNKS
    run = (counts + CHUNK_ROWS - 1) // CHUNK_ROWS
    local_start = jnp.cumsum(run, axis=1) - run
    run_end = jnp.cumsum(run, axis=0)
    run_start = run_end - run
    total = run_end[-1]
    padded = (total + BLOCK_CHUNKS - 1) // BLOCK_CHUNKS * BLOCK_CHUNKS
    pend = jnp.cumsum(padded)
    poff = pend - padded

    slots = tiles * ROUTE_TILE * TOP_K
    max_chunks = (slots + tiles * N_EXPERTS * (CHUNK_ROWS - 1)) // CHUNK_ROWS + N_EXPERTS * (BLOCK_CHUNKS - 1)
    n_chunks = -(-max_chunks // BLOCK_CHUNKS) * BLOCK_CHUNKS
    lookup = lambda onehot, table: jnp.dot(onehot, table.astype(F32), precision=lax.Precision.HIGHEST)
    q = jnp.arange(n_chunks, dtype=jnp.int32)
    e = jnp.minimum(jnp.sum(q[:, None] >= pend[None, :], axis=1), N_EXPERTS - 1).astype(jnp.int32)
    e_hot = (e[:, None] == jnp.arange(N_EXPERTS)[None, :]).astype(F32)
    per_expert = lookup(e_hot, jnp.stack([poff, total], axis=1)).astype(jnp.int32)
    j = q - per_expert[:, 0]
    valid = j < per_expert[:, 1]
    ends = lookup(e_hot, run_end.T).astype(jnp.int32)
    tile = jnp.minimum(jnp.sum(ends <= j[:, None], axis=1), tiles - 1).astype(jnp.int32)
    tile_hot = (tile[:, None] == jnp.arange(tiles)[None, :]).astype(F32)
    shift = jnp.sum(lookup(tile_hot, local_start - run_start) * e_hot, axis=1).astype(jnp.int32)
    src = jnp.where(valid, tile * local_chunks + shift + j, 0)
    spare = (q // BLOCK_CHUNKS % 2) * BLOCK_CHUNKS + q % BLOCK_CHUNKS
    spare_chunk = (spare // SPARE_CHUNKS) * local_chunks + (local_chunks - SPARE_CHUNKS) + spare % SPARE_CHUNKS
    dst = jnp.where(valid, src, spare_chunk)
    dst = jnp.concatenate([spare_chunk[:PRIME_BLOCKS * BLOCK_CHUNKS], dst])
    n_used = (pend[-1:] // BLOCK_CHUNKS).astype(jnp.int32)
    return e[::BLOCK_CHUNKS], n_used, src.astype(jnp.int32), dst.astype(jnp.int32)


def kernel(x, p, positions, rel_bias_table, g_mix_norm, w_in, b_in, ret_norm_g, att_sinks, w_out,
           g_moe_norm, w_router, b_router, w_up, b_up, w_down, b_down, w_ple_gate, w_ple_proj,
           g_ple_norm, g_final):
    b, s, d = x.shape
    depth = w_in.shape[0]
    assert depth == 1, "single-layer stack only"
    n = b * s
    x2 = x.reshape(n, d)

    half = RET_DK // 2
    inv = ROPE_BASE ** (-jnp.arange(half, dtype=F32) / half)
    ang = (positions.astype(F32)[..., None] * inv).reshape(n, half)

    qr, kr, vr, gr, qa, kva = _in_proj(
        x2, g_mix_norm[0][None], w_in[0].astype(BF16), b_in[0][None], jnp.cos(ang), jnp.sin(ang))

    shape3 = lambda t: t.reshape(b, s, t.shape[-1])
    ret = _retention(shape3(qr), shape3(kr), shape3(vr), shape3(gr), ret_norm_g[0])
    att = _attention(shape3(qa), shape3(kva), att_sinks[0], rel_bias_table)

    h1, xloc, yinit, slot_rows, gates, counts = _out_router(
        ret.reshape(n, RET_WIDTH), att.reshape(n, ATT_WIDTH), x2, w_out[0].astype(BF16),
        g_moe_norm[0][None], w_router[0].T.astype(BF16), b_router[0][:, None])

    block_e, n_used, chunk_src, chunk_dst = _routing_tables(counts[:, 0, :].astype(jnp.int32))

    wg, wl, wd = _expert_prep(w_up[0], w_down[0])
    yloc = _moe(block_e, n_used, chunk_src, chunk_dst, xloc, wg, wl,
                b_up[0][:, None, 0::2], b_up[0][:, None, 1::2],
                wd, b_down[0].reshape(N_EXPERTS, 1, d), yinit)

    out = _combine_ple(yloc, slot_rows, gates, h1, p[0].reshape(n, -1),
                       w_ple_gate[0].astype(BF16), w_ple_proj[0].astype(BF16),
                       g_ple_norm[0][None], g_final[None])
    return out.reshape(b, s, d)
```

```python
import math

import jax
import jax.numpy as jnp
from jax import lax
from jax.experimental import pallas as pl
from jax.experimental.pallas import tpu as pltpu

RET_HEADS = 4
RET_DK = 128
RET_DV = 128
RET_CHUNK = 128
ROPE_BASE = 10000.0
ATT_Q_HEADS = 8
ATT_KV_HEADS = 2
ATT_GROUP = ATT_Q_HEADS // ATT_KV_HEADS
ATT_HEAD_DIM = 64
WINDOW = 128
ATT_BLOCK = 128
NUM_BUCKETS = 32
MAX_DISTANCE = 128
N_EXPERTS = 32
TOP_K = 4
SWIGLU_ALPHA = 1.702
SWIGLU_LIMIT = 7.0
EPS = 1e-5

RET_WIDTH = RET_HEADS * RET_DV
ATT_WIDTH = ATT_Q_HEADS * ATT_HEAD_DIM
KV_WIDTH = ATT_KV_HEADS * ATT_HEAD_DIM

VMEM_LIMIT_BYTES = 48 * 1024 * 1024

TOKEN_TILE = 512
SUBLANES = 8
ROUTE_TILE = 256
ROUTE_STEP_TILES = 2
CHUNK_ROWS = SUBLANES
LOCAL_ROWS = ROUTE_TILE * TOP_K + N_EXPERTS * CHUNK_ROWS
SPARE_CHUNKS = (LOCAL_ROWS - ROUTE_TILE * TOP_K - N_EXPERTS * (CHUNK_ROWS - 1)) // CHUNK_ROWS
MOE_BLOCK = 256
BLOCK_CHUNKS = MOE_BLOCK // CHUNK_ROWS
PRIME_BLOCKS = 2
ATT_STEP_BLOCKS = 4
PERM_CHUNK = 256

F32 = jnp.float32
BF16 = jnp.bfloat16
NEG = float(jnp.finfo(jnp.float32).min)


def _rms(x, g):
    return x * lax.rsqrt(jnp.mean(x * x, axis=-1, keepdims=True) + EPS) * g


def _dot(a, b):
    return jnp.dot(a, b, preferred_element_type=F32)


def _dot_nt(a, b):
    return lax.dot_general(a, b, (((1,), (1,)), ((), ())), preferred_element_type=F32)


def _dot_tn(a, b):
    return lax.dot_general(a, b, (((0,), (0,)), ((), ())), preferred_element_type=F32)


def _params(semantics):
    return pltpu.CompilerParams(dimension_semantics=semantics, vmem_limit_bytes=VMEM_LIMIT_BYTES)


def _in_proj_kernel(x_ref, g_ref, w_ref, b_ref, cos_ref, sin_ref,
                    qr_ref, kr_ref, vr_ref, gr_ref, qa_ref, kva_ref):
    hn = _rms(x_ref[...], g_ref[...]).astype(BF16)
    cos = jnp.concatenate([cos_ref[...], cos_ref[...]], axis=1)
    sin = jnp.concatenate([-sin_ref[...], sin_ref[...]], axis=1)

    def proj(lo, hi):
        return _dot(hn, w_ref[:, lo:hi]) + b_ref[:, lo:hi]

    def rope(t):
        heads = []
        for h in range(RET_HEADS):
            th = t[:, h * RET_DK:(h + 1) * RET_DK]
            heads.append(th * cos + pltpu.roll(th, RET_DK // 2, 1) * sin)
        return jnp.concatenate(heads, axis=1)

    o = 0
    qr_ref[...] = rope(proj(o, o + RET_WIDTH)).astype(BF16)
    o += RET_WIDTH
    kr_ref[...] = (rope(proj(o, o + RET_WIDTH)) * (RET_DK ** -0.5)).astype(BF16)
    o += RET_WIDTH
    vr_ref[...] = proj(o, o + RET_WIDTH).astype(BF16)
    o += RET_WIDTH
    gr_ref[...] = proj(o, o + RET_WIDTH).astype(BF16)
    o += RET_WIDTH
    qa_ref[...] = (proj(o, o + ATT_WIDTH) * (ATT_HEAD_DIM ** -0.5)).astype(BF16)
    o += ATT_WIDTH
    low = lax.broadcasted_iota(jnp.int32, (1, KV_WIDTH), 1) < ATT_HEAD_DIM
    slabs = []
    for part in range(2):
        t = proj(o + part * KV_WIDTH, o + (part + 1) * KV_WIDTH)
        swapped = pltpu.roll(t, ATT_HEAD_DIM, 1)
        slabs += [jnp.where(low, t, swapped), jnp.where(low, swapped, t)]
    kva_ref[...] = jnp.concatenate(slabs, axis=1).astype(BF16)


def _in_proj(x2, g, w_in, b_in, cosf, sinf):
    n, d = x2.shape
    width = w_in.shape[1]
    tm = TOKEN_TILE
    row = lambda i: (i, 0)
    fixed = lambda i: (0, 0)
    outs = [RET_WIDTH, RET_WIDTH, RET_WIDTH, RET_WIDTH, ATT_WIDTH, 4 * KV_WIDTH]
    return pl.pallas_call(
        _in_proj_kernel,
        grid=(n // tm,),
        in_specs=[
            pl.BlockSpec((tm, d), row),
            pl.BlockSpec((1, d), fixed),
            pl.BlockSpec((d, width), fixed),
            pl.BlockSpec((1, width), fixed),
            pl.BlockSpec((tm, RET_DK // 2), row),
            pl.BlockSpec((tm, RET_DK // 2), row),
        ],
        out_specs=[pl.BlockSpec((tm, w), row) for w in outs],
        out_shape=[jax.ShapeDtypeStruct((n, w), BF16) for w in outs],
        compiler_params=_params(("parallel",)),
        name="in_proj",
    )(x2, g, w_in, b_in, cosf, sinf)


def _retention_kernel(q_ref, k_ref, v_ref, g_ref, dmask_ref, qdec_ref, kdec_ref, cdec_ref,
                      gn_ref, o_ref, state_ref):
    batch = q_ref.shape[0]

    @pl.when(pl.program_id(0) == 0)
    def _():
        state_ref[...] = jnp.zeros_like(state_ref)

    def per_batch(b, carry):
        for h in range(RET_HEADS):
            sl = slice(h * RET_DK, (h + 1) * RET_DK)
            q = q_ref[b, :, sl]
            k = k_ref[b, :, sl]
            v = v_ref[b, :, sl]
            scores = _dot_nt(q, k) * dmask_ref[h]
            inner = _dot(scores.astype(BF16), v)
            state = state_ref[b * RET_HEADS + h]
            cross = _dot(q, state.astype(BF16)) * qdec_ref[h]
            kd = (k.astype(F32) * kdec_ref[h]).astype(BF16)
            state_ref[b * RET_HEADS + h] = state * cdec_ref[h] + _dot_tn(kd, v)
            o = _rms(inner + cross, gn_ref[h])
            gate = g_ref[b, :, sl].astype(F32)
            o_ref[b, :, sl] = (gate * jax.nn.sigmoid(gate) * o).astype(BF16)
        return carry

    lax.fori_loop(0, batch, per_batch, 0)


def _retention(q, k, v, g, gn):
    b, s, w = q.shape
    c = RET_CHUNK
    log_g = jnp.log(1.0 - 2.0 ** (-5.0 - jnp.arange(RET_HEADS, dtype=F32)))
    ci = jnp.arange(c)
    diff = (ci[:, None] - ci[None, :]).astype(F32)
    dmask = jnp.where(diff >= 0, jnp.exp(jnp.maximum(diff, 0.0)[None] * log_g[:, None, None]), 0.0)
    cf = ci.astype(F32)
    kdec = jnp.exp((c - 1 - cf)[None, :] * log_g[:, None])
    qdec = jnp.exp((cf + 1)[None, :] * log_g[:, None])
    cdec = jnp.exp(c * log_g)
    kdec = jnp.broadcast_to(kdec[:, :, None], (RET_HEADS, c, RET_DK))
    qdec = jnp.broadcast_to(qdec[:, :, None], (RET_HEADS, c, RET_DV))
    cdec = jnp.broadcast_to(cdec[:, None, None], (RET_HEADS, RET_DK, RET_DV))
    gn3 = gn.reshape(RET_HEADS, 1, RET_DV)

    seq = lambda i: (0, i, 0)
    fixed = lambda i: (0, 0, 0)
    table = lambda shape: pl.BlockSpec(shape, fixed)
    return pl.pallas_call(
        _retention_kernel,
        grid=(s // c,),
        in_specs=[pl.BlockSpec((b, c, w), seq)] * 4 + [
            table((RET_HEADS, c, c)),
            table((RET_HEADS, c, RET_DV)),
            table((RET_HEADS, c, RET_DK)),
            table((RET_HEADS, RET_DK, RET_DV)),
            table((RET_HEADS, 1, RET_DV)),
        ],
        out_specs=pl.BlockSpec((b, c, w), seq),
        out_shape=jax.ShapeDtypeStruct((b, s, w), BF16),
        scratch_shapes=[pltpu.VMEM((b * RET_HEADS, RET_DK, RET_DV), F32)],
        compiler_params=_params(("arbitrary",)),
        name="retention",
    )(q, k, v, g, dmask, qdec, kdec, cdec, gn3)


def _attention_kernel(sinks_ref, q_ref, kvp_ref, kvc_ref, bias_ref, o_ref):
    blk = ATT_BLOCK
    slab = 2 * ATT_HEAD_DIM
    r = lax.broadcasted_iota(jnp.int32, (blk, blk), 0)
    c = lax.broadcasted_iota(jnp.int32, (blk, blk), 1)
    own = c <= r
    first_ok = jnp.logical_or(own, pl.program_id(1) > 0)
    low = lax.broadcasted_iota(jnp.int32, (1, slab), 1) < ATT_HEAD_DIM
    zero = jnp.zeros((), BF16)
    for i in range(q_ref.shape[1] // blk):
        rows = slice(i * blk, (i + 1) * blk)
        prev = (lambda cs: kvp_ref[0, :, cs]) if i == 0 else (
            lambda cs, rs=slice((i - 1) * blk, i * blk): kvc_ref[0, rs, cs])
        slabs = []
        for a in range(ATT_Q_HEADS // 2):
            g = (2 * a) // ATT_GROUP
            ks = slice(g * slab, (g + 1) * slab)
            vs = slice((ATT_KV_HEADS + g) * slab, (ATT_KV_HEADS + g + 1) * slab)
            k = jnp.concatenate([prev(ks), kvc_ref[0, rows, ks]], axis=0)
            v_prev, v_own = prev(vs), kvc_ref[0, rows, vs]
            q = q_ref[0, rows, a * slab:(a + 1) * slab]
            halves = []
            for part in range(2):
                h = 2 * a + part
                qz = jnp.where(low if part == 0 else jnp.logical_not(low), q, zero)
                both = _dot_nt(qz, k)
                s = jnp.where(own, both[:, blk:], both[:, :blk]) + bias_ref[h]
                if i == 0:
                    s = jnp.where(first_ok, s, NEG)
                sink = sinks_ref[h]
                m = jnp.maximum(jnp.max(s, axis=1, keepdims=True), sink)
                p = jnp.exp(s - m)
                denom = jnp.sum(p, axis=1, keepdims=True) + jnp.exp(sink - m)
                p_own = jnp.where(own, p, 0.0).astype(BF16)
                p_prev = jnp.where(own, 0.0, p).astype(BF16)
                halves.append((_dot(p_own, v_own) + _dot(p_prev, v_prev)) / denom)
            slabs.append(jnp.where(low, halves[0], halves[1]))
        o_ref[0, rows, :] = jnp.concatenate(slabs, axis=1).astype(BF16)


def _t5_bucket(n):
    max_exact = NUM_BUCKETS // 2
    nf = jnp.maximum(n, 1).astype(F32)
    large = max_exact + (jnp.log(nf / max_exact) / math.log(MAX_DISTANCE / max_exact)
                         * (NUM_BUCKETS - max_exact)).astype(jnp.int32)
    large = jnp.minimum(large, NUM_BUCKETS - 1)
    return jnp.where(n < max_exact, n, large)


def _attention(q, kv, sinks, bias_table):
    b, s, _ = q.shape
    blk = ATT_BLOCK
    assert WINDOW == blk
    qi = jnp.arange(blk)[:, None]
    ki = jnp.arange(blk)[None, :]
    dist = jnp.where(ki <= qi, qi - ki, qi + blk - ki)
    onehot = jax.nn.one_hot(_t5_bucket(dist), NUM_BUCKETS, dtype=F32)
    bias = jnp.einsum("qkn,nh->hqk", onehot, bias_table.astype(F32), precision=lax.Precision.HIGHEST)

    sub = ATT_STEP_BLOCKS
    kvw = kv.shape[-1]
    return pl.pallas_call(
        _attention_kernel,
        grid=(b, s // (sub * blk)),
        in_specs=[
            pl.BlockSpec(memory_space=pltpu.SMEM),
            pl.BlockSpec((1, sub * blk, ATT_WIDTH), lambda i, j: (i, j, 0)),
            pl.BlockSpec((1, blk, kvw), lambda i, j: (i, jnp.maximum(sub * j - 1, 0), 0)),
            pl.BlockSpec((1, sub * blk, kvw), lambda i, j: (i, j, 0)),
            pl.BlockSpec((ATT_Q_HEADS, blk, blk), lambda i, j: (0, 0, 0)),
        ],
        out_specs=pl.BlockSpec((1, sub * blk, ATT_WIDTH), lambda i, j: (i, j, 0)),
        out_shape=jax.ShapeDtypeStruct((b, s, ATT_WIDTH), BF16),
        compiler_params=_params(("parallel", "arbitrary")),
        name="attention",
    )(sinks, q, kv, kv, bias)


def _pack_bf16_pairs(lo, hi):
    return pltpu.bitcast(hi, jnp.uint32) | (pltpu.bitcast(lo, jnp.uint32) >> 16)


def _unpack_bf16_pairs(packed):
    lo = pltpu.bitcast(packed << 16, F32).astype(BF16)
    hi = pltpu.bitcast(packed & jnp.uint32(0xFFFF0000), F32).astype(BF16)
    return lo, hi


def _stack_rows(rows, dtype):
    t = rows[0].shape[1]
    sub = lax.broadcasted_iota(jnp.int32, (SUBLANES, t), 0)
    out = jnp.zeros((SUBLANES, t), dtype)
    for i, r in enumerate(rows):
        out = jnp.where(sub == i, r.astype(dtype), out)
    return out


def _route_tile(xb, wrt_ref, brt_ref):
    t = xb.shape[0]
    logits = _dot_nt(wrt_ref[...], xb) + brt_ref[...]
    expert = lax.broadcasted_iota(jnp.int32, logits.shape, 0)
    vals = logits
    top_v, sels = [], []
    for _ in range(TOP_K):
        m = jnp.max(vals, axis=0, keepdims=True)
        idx = jnp.min(jnp.where(vals == m, expert, N_EXPERTS), axis=0, keepdims=True)
        sel = expert == idx
        vals = jnp.where(sel, -jnp.inf, vals)
        top_v.append(m)
        sels.append(sel)
    exps = [jnp.exp(v - top_v[0]) for v in top_v]
    denom = exps[0] + exps[1] + exps[2] + exps[3]
    gates = [e / denom for e in exps]

    chosen = jnp.zeros(logits.shape, F32)
    for sel in sels:
        chosen = chosen + sel.astype(F32)
    chosen_b = chosen.astype(BF16)
    r = lax.broadcasted_iota(jnp.int32, (t, t), 0)
    c = lax.broadcasted_iota(jnp.int32, (t, t), 1)
    earlier = jnp.where(r < c, 1.0, 0.0).astype(BF16)
    before = _dot(chosen_b, earlier)
    counts = jnp.sum(chosen, axis=1, keepdims=True)
    chunks = jnp.floor((counts + (CHUNK_ROWS - 1)) * (1.0 / CHUNK_ROWS))
    er = lax.broadcasted_iota(jnp.int32, (N_EXPERTS, N_EXPERTS), 0)
    ec = lax.broadcasted_iota(jnp.int32, (N_EXPERTS, N_EXPERTS), 1)
    lower = jnp.where(ec < er, 1.0, 0.0).astype(BF16)
    start = _dot(lower, jnp.broadcast_to(chunks, logits.shape).astype(BF16)) * float(CHUNK_ROWS)
    rows = [jnp.sum(jnp.where(sel, before + start, 0.0), axis=0, keepdims=True).astype(jnp.int32)
            for sel in sels]
    counts_rows = _dot_nt(jnp.ones((SUBLANES, t), BF16), chosen_b)

    local = lax.broadcasted_iota(jnp.int32, (LOCAL_ROWS, 1), 0)
    select = jnp.zeros((LOCAL_ROWS, t), F32)
    for row in rows:
        select = jnp.where(local == row, 1.0, select)
    return rows, gates, counts_rows, select.astype(BF16)


def _out_router_kernel(ret_ref, att_ref, x_ref, wo_ref, g_ref, wrt_ref, brt_ref,
                       h1_ref, xloc_ref, yzero_ref, slot_ref, gate_ref, cnt_ref):
    half = x_ref.shape[1] // 2
    for s in range(x_ref.shape[0] // ROUTE_TILE):
        tok = slice(s * ROUTE_TILE, (s + 1) * ROUTE_TILE)
        loc = slice(s * LOCAL_ROWS, (s + 1) * LOCAL_ROWS)
        h1 = (x_ref[tok, :] + _dot(ret_ref[tok, :], wo_ref[:RET_WIDTH, :])
              + _dot(att_ref[tok, :], wo_ref[RET_WIDTH:, :]))
        h1_ref[tok, :] = h1
        xb = _rms(h1, g_ref[...]).astype(BF16)
        rows, gates, counts_rows, select = _route_tile(xb, wrt_ref, brt_ref)
        xloc = _dot(select, xb)
        xloc_ref[loc, :] = _pack_bf16_pairs(xloc[:, :half], xloc[:, half:])
        slot_ref[:, tok] = _stack_rows(rows, jnp.int32)
        gate_ref[:, tok] = _stack_rows(gates, F32)
        cnt_ref[s] = counts_rows
    yzero_ref[...] = jnp.zeros_like(yzero_ref)


def _out_router(ret, att, x2, w_out, g, w_router_t, b_router_t):
    n, d = x2.shape
    per = ROUTE_STEP_TILES
    tm = per * ROUTE_TILE
    tiles = n // ROUTE_TILE
    row = lambda i: (i, 0)
    col = lambda i: (0, i)
    fixed = lambda i: (0, 0)
    return pl.pallas_call(
        _out_router_kernel,
        grid=(n // tm,),
        in_specs=[
            pl.BlockSpec((tm, RET_WIDTH), row),
            pl.BlockSpec((tm, ATT_WIDTH), row),
            pl.BlockSpec((tm, d), row),
            pl.BlockSpec((RET_WIDTH + ATT_WIDTH, d), fixed),
            pl.BlockSpec((1, d), fixed),
            pl.BlockSpec((N_EXPERTS, d), fixed),
            pl.BlockSpec((N_EXPERTS, 1), fixed),
        ],
        out_specs=[
            pl.BlockSpec((tm, d), row),
            pl.BlockSpec((per * LOCAL_ROWS, d // 2), row),
            pl.BlockSpec((per * LOCAL_ROWS, d // 2), row),
            pl.BlockSpec((SUBLANES, tm), col),
            pl.BlockSpec((SUBLANES, tm), col),
            pl.BlockSpec((per, SUBLANES, N_EXPERTS), lambda i: (i, 0, 0)),
        ],
        out_shape=[
            jax.ShapeDtypeStruct((n, d), F32),
            jax.ShapeDtypeStruct((tiles * LOCAL_ROWS, d // 2), jnp.uint32),
            jax.ShapeDtypeStruct((tiles * LOCAL_ROWS, d // 2), jnp.uint32),
            jax.ShapeDtypeStruct((SUBLANES, n), jnp.int32),
            jax.ShapeDtypeStruct((SUBLANES, n), F32),
            jax.ShapeDtypeStruct((tiles, SUBLANES, N_EXPERTS), F32),
        ],
        compiler_params=_params(("parallel",)),
        name="out_router",
    )(ret, att, x2, w_out, g, w_router_t, b_router_t)


def _convert_expert_weights(wu_ref, wdn_ref, wg_ref, wl_ref, wd_ref):
    ch = PERM_CHUNK
    hc = ch // 2
    r = lax.broadcasted_iota(jnp.int32, (ch, ch), 0)
    c = lax.broadcasted_iota(jnp.int32, (ch, ch), 1)
    perm = jnp.where(r == jnp.where(c < hc, 2 * c, 2 * (c - hc) + 1), 1.0, 0.0).astype(BF16)
    for j in range(wu_ref.shape[1] // ch):
        both = _dot(wu_ref[:, j * ch:(j + 1) * ch].astype(BF16), perm)
        wg_ref[:, j * hc:(j + 1) * hc] = both[:, :hc].astype(BF16)
        wl_ref[:, j * hc:(j + 1) * hc] = both[:, hc:].astype(BF16)
    wd_ref[...] = wdn_ref[...].astype(BF16)


def _moe_kernel(be_ref, nused_ref, src_ref, dst_ref,
                xloc_hbm, wu_ref, bg_ref, bl_ref, wdn_ref, bd_ref, yinit_hbm,
                yloc_hbm, xbuf, ybuf, gsem, ssem, wg_ref, wl_ref, wd_ref):
    del yinit_hbm
    i = pl.program_id(0)
    n_used = nused_ref[0]
    last = pl.num_programs(0) - 1
    slot = lax.rem(i, 2)
    bm, half = xbuf.shape[1], xbuf.shape[2]

    def chunk(ref, c):
        first = c * CHUNK_ROWS if isinstance(c, int) else pl.multiple_of(c * CHUNK_ROWS, CHUNK_ROWS)
        return ref.at[pl.ds(first, CHUNK_ROWS), :]

    def start_gather(step, sl):
        for j in range(BLOCK_CHUNKS):
            pltpu.make_async_copy(chunk(xloc_hbm, src_ref[step * BLOCK_CHUNKS + j]),
                                  chunk(xbuf.at[sl], j), gsem.at[sl]).start()

    def start_writeback(step, sl):
        for j in range(BLOCK_CHUNKS):
            dst = dst_ref[(step + PRIME_BLOCKS) * BLOCK_CHUNKS + j]
            pltpu.make_async_copy(chunk(ybuf.at[sl], j), chunk(yloc_hbm, dst), ssem.at[sl]).start()

    def wait_gather(sl):
        pltpu.make_async_copy(xloc_hbm.at[pl.ds(0, bm), :], xbuf.at[sl], gsem.at[sl]).wait()

    def wait_writeback(sl):
        pltpu.make_async_copy(ybuf.at[sl], yloc_hbm.at[pl.ds(0, bm), :], ssem.at[sl]).wait()

    @pl.when(i == 0)
    def _():
        ybuf[...] = jnp.zeros_like(ybuf)
        start_gather(0, 0)
        start_writeback(-2, 0)

    new_expert = jnp.logical_or(i == 0, be_ref[i] != be_ref[jnp.maximum(i - 1, 0)])

    @pl.when(jnp.logical_and(i < n_used, new_expert))
    def _():
        _convert_expert_weights(wu_ref, wdn_ref, wg_ref, wl_ref, wd_ref)

    @pl.when(i < n_used)
    def _():
        wait_gather(slot)
        start_gather(jnp.minimum(i + 1, last), 1 - slot)
        lo, hi = _unpack_bf16_pairs(xbuf[slot])
        start_writeback(i - 1, 1 - slot)

        def up(w_ref, b_ref):
            return _dot(lo, w_ref[:half, :]) + _dot(hi, w_ref[half:, :]) + b_ref[...]

        glu = jnp.minimum(up(wg_ref, bg_ref), SWIGLU_LIMIT)
        lin = jnp.clip(up(wl_ref, bl_ref), -SWIGLU_LIMIT, SWIGLU_LIMIT)
        act = glu * jax.nn.sigmoid(SWIGLU_ALPHA * glu) * (lin + 1.0)
        y = (_dot(act.astype(BF16), wd_ref[...]) + bd_ref[...]).astype(BF16).astype(F32)

        wait_writeback(slot)
        ybuf[slot] = _pack_bf16_pairs(y[:, :half], y[:, half:])

        @pl.when(i == n_used - 1)
        def _():
            start_writeback(i, slot)
            wait_writeback(1 - slot)
            wait_writeback(slot)
            wait_gather(1 - slot)


def _moe(block_e, n_used, chunk_src, chunk_dst, xloc, w_up, bg, bl, w_down, bd, yinit):
    half = xloc.shape[1]
    e, d, f2 = w_up.shape
    f = f2 // 2
    bm = MOE_BLOCK
    n_blocks = chunk_src.shape[0] // BLOCK_CHUNKS
    assert chunk_dst.shape[0] == (n_blocks + PRIME_BLOCKS) * BLOCK_CHUNKS
    expert3 = lambda i, be, nu, cs, cd: (be[i], 0, 0)
    operands = (block_e, n_used, chunk_src, chunk_dst, xloc, w_up, bg, bl, w_down, bd, yinit)
    grid_spec = pltpu.PrefetchScalarGridSpec(
        num_scalar_prefetch=4,
        grid=(n_blocks,),
        in_specs=[
            pl.BlockSpec(memory_space=pl.ANY),
            pl.BlockSpec((None, d, f2), expert3),
            pl.BlockSpec((None, 1, f), expert3),
            pl.BlockSpec((None, 1, f), expert3),
            pl.BlockSpec((None, f, d), expert3),
            pl.BlockSpec((None, 1, d), expert3),
            pl.BlockSpec(memory_space=pl.ANY),
        ],
        out_specs=pl.BlockSpec(memory_space=pl.ANY),
        scratch_shapes=[
            pltpu.VMEM((2, bm, half), jnp.uint32),
            pltpu.VMEM((2, bm, half), jnp.uint32),
            pltpu.SemaphoreType.DMA((2,)),
            pltpu.SemaphoreType.DMA((2,)),
            pltpu.VMEM((d, f), BF16),
            pltpu.VMEM((d, f), BF16),
            pltpu.VMEM((f, d), BF16),
        ],
    )
    return pl.pallas_call(
        _moe_kernel,
        grid_spec=grid_spec,
        out_shape=jax.ShapeDtypeStruct(yinit.shape, jnp.uint32),
        input_output_aliases={len(operands) - 1: 0},
        compiler_params=_params(("arbitrary",)),
        name="moe",
    )(*operands)


def _combine_ple_kernel(yloc_ref, slot_ref, gate_ref, h1_ref, p_ref, wpg_ref, wpp_ref, gple_ref,
                        gfin_ref, o_ref):
    tm = h1_ref.shape[0]
    rows_local = yloc_ref.shape[0]
    local = lax.broadcasted_iota(jnp.int32, (rows_local, 1), 0)
    rows = slot_ref[...]
    gates = gate_ref[...]
    weights = jnp.zeros((rows_local, tm), F32)
    for k in range(TOP_K):
        weights = jnp.where(local == rows[k:k + 1, :], gates[k:k + 1, :], weights)
    weights = weights.astype(BF16)
    y_lo, y_hi = _unpack_bf16_pairs(yloc_ref[...])
    moe = jnp.concatenate([_dot_tn(weights, y_lo), _dot_tn(weights, y_hi)], axis=1)
    h2 = h1_ref[...] + moe
    gate = jax.nn.sigmoid(_dot(h2.astype(BF16), wpg_ref[...]))
    proj = _dot(p_ref[...].astype(BF16), wpp_ref[...])
    h3 = h2 + _rms(gate * proj, gple_ref[...])
    o_ref[...] = _rms(h3, gfin_ref[...])


def _combine_ple(yloc, slot_rows, gates, h1, p2, wpg, wpp, gple, gfin):
    n, d = h1.shape
    pd = p2.shape[1]
    tm = ROUTE_TILE
    row = lambda i: (i, 0)
    col = lambda i: (0, i)
    fixed = lambda i: (0, 0)
    return pl.pallas_call(
        _combine_ple_kernel,
        grid=(n // tm,),
        in_specs=[
            pl.BlockSpec((LOCAL_ROWS, d // 2), row),
            pl.BlockSpec((SUBLANES, tm), col),
            pl.BlockSpec((SUBLANES, tm), col),
            pl.BlockSpec((tm, d), row),
            pl.BlockSpec((tm, pd), row),
            pl.BlockSpec((d, d), fixed),
            pl.BlockSpec((pd, d), fixed),
            pl.BlockSpec((1, d), fixed),
            pl.BlockSpec((1, d), fixed),
        ],
        out_specs=pl.BlockSpec((tm, d), row),
        out_shape=jax.ShapeDtypeStruct((n, d), F32),
        compiler_params=_params(("parallel",)),
        name="combine_ple",
    )(yloc, slot_rows, gates, h1, p2, wpg, wpp, gple, gfin)


def _routing_tables(counts):
    tiles = counts.shape[0]
    local_chunks = LOCAL_ROWS // CHUNK_ROWS
    assert 2 * BLOCK_CHUNKS <= tiles * SPARE_CHUNKS
    run = (counts + CHUNK_ROWS - 1) // CHUNK_ROWS
    local_start = jnp.cumsum(run, axis=1) - run
    run_end = jnp.cumsum(run, axis=0)
    run_start = run_end - run
    total = run_end[-1]
    padded = (total + BLOCK_CHUNKS - 1) // BLOCK_CHUNKS * BLOCK_CHUNKS
    pend = jnp.cumsum(padded)
    poff = pend - padded

    slots = tiles * ROUTE_TILE * TOP_K
    max_chunks = (slots + tiles * N_EXPERTS * (CHUNK_ROWS - 1)) // CHUNK_ROWS + N_EXPERTS * (BLOCK_CHUNKS - 1)
    n_chunks = -(-max_chunks // BLOCK_CHUNKS) * BLOCK_CHUNKS
    lookup = lambda onehot, table: jnp.dot(onehot, table.astype(F32), precision=lax.Precision.HIGHEST)
    q = jnp.arange(n_chunks, dtype=jnp.int32)
    e = jnp.minimum(jnp.sum(q[:, None] >= pend[None, :], axis=1), N_EXPERTS - 1).astype(jnp.int32)
    e_hot = (e[:, None] == jnp.arange(N_EXPERTS)[None, :]).astype(F32)
    per_expert = lookup(e_hot, jnp.stack([poff, total], axis=1)).astype(jnp.int32)
    j = q - per_expert[:, 0]
    valid = j < per_expert[:, 1]
    ends = lookup(e_hot, run_end.T).astype(jnp.int32)
    tile = jnp.minimum(jnp.sum(ends <= j[:, None], axis=1), tiles - 1).astype(jnp.int32)
    tile_hot = (tile[:, None] == jnp.arange(tiles)[None, :]).astype(F32)
    shift = jnp.sum(lookup(tile_hot, local_start - run_start) * e_hot, axis=1).astype(jnp.int32)
    src = jnp.where(valid, tile * local_chunks + shift + j, 0)
    spare = (q // BLOCK_CHUNKS % 2) * BLOCK_CHUNKS + q % BLOCK_CHUNKS
    spare_chunk = (spare // SPARE_CHUNKS) * local_chunks + (local_chunks - SPARE_CHUNKS) + spare % SPARE_CHUNKS
    dst = jnp.where(valid, src, spare_chunk)
    dst = jnp.concatenate([spare_chunk[:PRIME_BLOCKS * BLOCK_CHUNKS], dst])
    n_used = (pend[-1:] // BLOCK_CHUNKS).astype(jnp.int32)
    return e[::BLOCK_CHUNKS], n_used, src.astype(jnp.int32), dst.astype(jnp.int32)


def kernel(x, p, positions, rel_bias_table, g_mix_norm, w_in, b_in, ret_norm_g, att_sinks, w_out,
           g_moe_norm, w_router, b_router, w_up, b_up, w_down, b_down, w_ple_gate, w_ple_proj,
           g_ple_norm, g_final):
    b, s, d = x.shape
    depth = w_in.shape[0]
    assert depth == 1, "single-layer stack only"
    n = b * s
    x2 = x.reshape(n, d)

    half = RET_DK // 2
    inv = ROPE_BASE ** (-jnp.arange(half, dtype=F32) / half)
    ang = (positions.astype(F32)[..., None] * inv).reshape(n, half)

    qr, kr, vr, gr, qa, kva = _in_proj(
        x2, g_mix_norm[0][None], w_in[0].astype(BF16), b_in[0][None], jnp.cos(ang), jnp.sin(ang))

    shape3 = lambda t: t.reshape(b, s, t.shape[-1])
    ret = _retention(shape3(qr), shape3(kr), shape3(vr), shape3(gr), ret_norm_g[0])
    att = _attention(shape3(qa), shape3(kva), att_sinks[0], rel_bias_table)

    h1, xloc, yinit, slot_rows, gates, counts = _out_router(
        ret.reshape(n, RET_WIDTH), att.reshape(n, ATT_WIDTH), x2, w_out[0].astype(BF16),
        g_moe_norm[0][None], w_router[0].T.astype(BF16), b_router[0][:, None])

    block_e, n_used, chunk_src, chunk_dst = _routing_tables(counts[:, 0, :].astype(jnp.int32))

    yloc = _moe(block_e, n_used, chunk_src, chunk_dst, xloc, w_up[0],
                b_up[0][:, None, 0::2], b_up[0][:, None, 1::2],
                w_down[0], b_down[0].reshape(N_EXPERTS, 1, d), yinit)

    out = _combine_ple(yloc, slot_rows, gates, h1, p[0].reshape(n, -1),
                       w_ple_gate[0].astype(BF16), w_ple_proj[0].astype(BF16),
                       g_ple_norm[0][None], g_final[None])
    return out.reshape(b, s, d)
```

```python
import math

import jax
import jax.numpy as jnp
from jax import lax
from jax.experimental import pallas as pl
from jax.experimental.pallas import tpu as pltpu

RET_HEADS = 4
RET_DK = 128
RET_DV = 128
RET_CHUNK = 128
ROPE_BASE = 10000.0
ATT_Q_HEADS = 8
ATT_KV_HEADS = 2
ATT_GROUP = ATT_Q_HEADS // ATT_KV_HEADS
ATT_HEAD_DIM = 64
WINDOW = 128
ATT_BLOCK = 128
NUM_BUCKETS = 32
MAX_DISTANCE = 128
N_EXPERTS = 32
TOP_K = 4
SWIGLU_ALPHA = 1.702
SWIGLU_LIMIT = 7.0
EPS = 1e-5

RET_WIDTH = RET_HEADS * RET_DV
ATT_WIDTH = ATT_Q_HEADS * ATT_HEAD_DIM
KV_WIDTH = ATT_KV_HEADS * ATT_HEAD_DIM

VMEM_LIMIT_BYTES = 48 * 1024 * 1024

TOKEN_TILE = 1024
SUBLANES = 8
ROUTE_TILE = 256
ROUTE_STEP_TILES = 2
CHUNK_ROWS = SUBLANES
LOCAL_ROWS = ROUTE_TILE * TOP_K + N_EXPERTS * CHUNK_ROWS
SPARE_CHUNKS = (LOCAL_ROWS - ROUTE_TILE * TOP_K - N_EXPERTS * (CHUNK_ROWS - 1)) // CHUNK_ROWS
MOE_BLOCK = 256
BLOCK_CHUNKS = MOE_BLOCK // CHUNK_ROWS
PRIME_BLOCKS = 2
ATT_STEP_BLOCKS = 8
PERM_CHUNK = 256

F32 = jnp.float32
BF16 = jnp.bfloat16
NEG = float(jnp.finfo(jnp.float32).min)


def _rms(x, g):
    return x * lax.rsqrt(jnp.mean(x * x, axis=-1, keepdims=True) + EPS) * g


def _dot(a, b):
    return jnp.dot(a, b, preferred_element_type=F32)


def _dot_nt(a, b):
    return lax.dot_general(a, b, (((1,), (1,)), ((), ())), preferred_element_type=F32)


def _dot_tn(a, b):
    return lax.dot_general(a, b, (((0,), (0,)), ((), ())), preferred_element_type=F32)


def _params(semantics):
    return pltpu.CompilerParams(dimension_semantics=semantics, vmem_limit_bytes=VMEM_LIMIT_BYTES)


def _in_proj_kernel(x_ref, g_ref, w_ref, b_ref, cos_ref, sin_ref,
                    qr_ref, kr_ref, vr_ref, gr_ref, qa_ref, kva_ref):
    hn = _rms(x_ref[...], g_ref[...]).astype(BF16)
    cos = jnp.concatenate([cos_ref[...], cos_ref[...]], axis=1)
    sin = jnp.concatenate([-sin_ref[...], sin_ref[...]], axis=1)

    def proj(lo, hi):
        return _dot(hn, w_ref[:, lo:hi]) + b_ref[:, lo:hi]

    def rope(t):
        heads = []
        for h in range(RET_HEADS):
            th = t[:, h * RET_DK:(h + 1) * RET_DK]
            heads.append(th * cos + pltpu.roll(th, RET_DK // 2, 1) * sin)
        return jnp.concatenate(heads, axis=1)

    o = 0
    qr_ref[...] = rope(proj(o, o + RET_WIDTH)).astype(BF16)
    o += RET_WIDTH
    kr_ref[...] = (rope(proj(o, o + RET_WIDTH)) * (RET_DK ** -0.5)).astype(BF16)
    o += RET_WIDTH
    vr_ref[...] = proj(o, o + RET_WIDTH).astype(BF16)
    o += RET_WIDTH
    gr_ref[...] = proj(o, o + RET_WIDTH).astype(BF16)
    o += RET_WIDTH
    qa_ref[...] = (proj(o, o + ATT_WIDTH) * (ATT_HEAD_DIM ** -0.5)).astype(BF16)
    o += ATT_WIDTH
    low = lax.broadcasted_iota(jnp.int32, (1, KV_WIDTH), 1) < ATT_HEAD_DIM
    slabs = []
    for part in range(2):
        t = proj(o + part * KV_WIDTH, o + (part + 1) * KV_WIDTH)
        swapped = pltpu.roll(t, ATT_HEAD_DIM, 1)
        slabs += [jnp.where(low, t, swapped), jnp.where(low, swapped, t)]
    kva_ref[...] = jnp.concatenate(slabs, axis=1).astype(BF16)


def _in_proj(x2, g, w_in, b_in, cosf, sinf):
    n, d = x2.shape
    width = w_in.shape[1]
    tm = TOKEN_TILE
    row = lambda i: (i, 0)
    fixed = lambda i: (0, 0)
    outs = [RET_WIDTH, RET_WIDTH, RET_WIDTH, RET_WIDTH, ATT_WIDTH, 4 * KV_WIDTH]
    return pl.pallas_call(
        _in_proj_kernel,
        grid=(n // tm,),
        in_specs=[
            pl.BlockSpec((tm, d), row),
            pl.BlockSpec((1, d), fixed),
            pl.BlockSpec((d, width), fixed),
            pl.BlockSpec((1, width), fixed),
            pl.BlockSpec((tm, RET_DK // 2), row),
            pl.BlockSpec((tm, RET_DK // 2), row),
        ],
        out_specs=[pl.BlockSpec((tm, w), row) for w in outs],
        out_shape=[jax.ShapeDtypeStruct((n, w), BF16) for w in outs],
        compiler_params=_params(("parallel",)),
        name="in_proj",
    )(x2, g, w_in, b_in, cosf, sinf)


def _retention_kernel(q_ref, k_ref, v_ref, g_ref, dmask_ref, qdec_ref, kdec_ref, cdec_ref,
                      gn_ref, o_ref, state_ref):
    batch = q_ref.shape[0]

    @pl.when(pl.program_id(0) == 0)
    def _():
        state_ref[...] = jnp.zeros_like(state_ref)

    def per_batch(b, carry):
        for h in range(RET_HEADS):
            sl = slice(h * RET_DK, (h + 1) * RET_DK)
            q = q_ref[b, :, sl]
            k = k_ref[b, :, sl]
            v = v_ref[b, :, sl]
            scores = _dot_nt(q, k) * dmask_ref[h]
            inner = _dot(scores.astype(BF16), v)
            state = state_ref[b * RET_HEADS + h]
            cross = _dot(q, state.astype(BF16)) * qdec_ref[h]
            kd = (k.astype(F32) * kdec_ref[h]).astype(BF16)
            state_ref[b * RET_HEADS + h] = state * cdec_ref[h] + _dot_tn(kd, v)
            o = _rms(inner + cross, gn_ref[h])
            gate = g_ref[b, :, sl].astype(F32)
            o_ref[b, :, sl] = (gate * jax.nn.sigmoid(gate) * o).astype(BF16)
        return carry

    lax.fori_loop(0, batch, per_batch, 0, unroll=True)


def _retention(q, k, v, g, gn):
    b, s, w = q.shape
    c = RET_CHUNK
    log_g = jnp.log(1.0 - 2.0 ** (-5.0 - jnp.arange(RET_HEADS, dtype=F32)))
    ci = jnp.arange(c)
    diff = (ci[:, None] - ci[None, :]).astype(F32)
    dmask = jnp.where(diff >= 0, jnp.exp(jnp.maximum(diff, 0.0)[None] * log_g[:, None, None]), 0.0)
    cf = ci.astype(F32)
    kdec = jnp.exp((c - 1 - cf)[None, :] * log_g[:, None])
    qdec = jnp.exp((cf + 1)[None, :] * log_g[:, None])
    cdec = jnp.exp(c * log_g)
    kdec = jnp.broadcast_to(kdec[:, :, None], (RET_HEADS, c, RET_DK))
    qdec = jnp.broadcast_to(qdec[:, :, None], (RET_HEADS, c, RET_DV))
    cdec = jnp.broadcast_to(cdec[:, None, None], (RET_HEADS, RET_DK, RET_DV))
    gn3 = gn.reshape(RET_HEADS, 1, RET_DV)

    seq = lambda i: (0, i, 0)
    fixed = lambda i: (0, 0, 0)
    table = lambda shape: pl.BlockSpec(shape, fixed)
    return pl.pallas_call(
        _retention_kernel,
        grid=(s // c,),
        in_specs=[pl.BlockSpec((b, c, w), seq)] * 4 + [
            table((RET_HEADS, c, c)),
            table((RET_HEADS, c, RET_DV)),
            table((RET_HEADS, c, RET_DK)),
            table((RET_HEADS, RET_DK, RET_DV)),
            table((RET_HEADS, 1, RET_DV)),
        ],
        out_specs=pl.BlockSpec((b, c, w), seq),
        out_shape=jax.ShapeDtypeStruct((b, s, w), BF16),
        scratch_shapes=[pltpu.VMEM((b * RET_HEADS, RET_DK, RET_DV), F32)],
        compiler_params=_params(("arbitrary",)),
        name="retention",
    )(q, k, v, g, dmask, qdec, kdec, cdec, gn3)


def _attention_kernel(sinks_ref, q_ref, kvp_ref, kvc_ref, bias_ref, o_ref):
    blk = ATT_BLOCK
    slab = 2 * ATT_HEAD_DIM
    r = lax.broadcasted_iota(jnp.int32, (blk, blk), 0)
    c = lax.broadcasted_iota(jnp.int32, (blk, blk), 1)
    own = c <= r
    first_ok = jnp.logical_or(own, pl.program_id(1) > 0)
    low = lax.broadcasted_iota(jnp.int32, (1, slab), 1) < ATT_HEAD_DIM
    zero = jnp.zeros((), BF16)
    for i in range(q_ref.shape[1] // blk):
        rows = slice(i * blk, (i + 1) * blk)
        prev = (lambda cs: kvp_ref[0, :, cs]) if i == 0 else (
            lambda cs, rs=slice((i - 1) * blk, i * blk): kvc_ref[0, rs, cs])
        slabs = []
        for a in range(ATT_Q_HEADS // 2):
            g = (2 * a) // ATT_GROUP
            ks = slice(g * slab, (g + 1) * slab)
            vs = slice((ATT_KV_HEADS + g) * slab, (ATT_KV_HEADS + g + 1) * slab)
            k = jnp.concatenate([prev(ks), kvc_ref[0, rows, ks]], axis=0)
            v_prev, v_own = prev(vs), kvc_ref[0, rows, vs]
            q = q_ref[0, rows, a * slab:(a + 1) * slab]
            halves = []
            for part in range(2):
                h = 2 * a + part
                qz = jnp.where(low if part == 0 else jnp.logical_not(low), q, zero)
                both = _dot_nt(qz, k)
                s = jnp.where(own, both[:, blk:], both[:, :blk]) + bias_ref[h]
                if i == 0:
                    s = jnp.where(first_ok, s, NEG)
                sink = sinks_ref[h]
                m = jnp.maximum(jnp.max(s, axis=1, keepdims=True), sink)
                p = jnp.exp(s - m)
                denom = jnp.sum(p, axis=1, keepdims=True) + jnp.exp(sink - m)
                p_own = jnp.where(own, p, 0.0).astype(BF16)
                p_prev = jnp.where(own, 0.0, p).astype(BF16)
                halves.append((_dot(p_own, v_own) + _dot(p_prev, v_prev)) / denom)
            slabs.append(jnp.where(low, halves[0], halves[1]))
        o_ref[0, rows, :] = jnp.concatenate(slabs, axis=1).astype(BF16)


def _t5_bucket(n):
    max_exact = NUM_BUCKETS // 2
    nf = jnp.maximum(n, 1).astype(F32)
    large = max_exact + (jnp.log(nf / max_exact) / math.log(MAX_DISTANCE / max_exact)
                         * (NUM_BUCKETS - max_exact)).astype(jnp.int32)
    large = jnp.minimum(large, NUM_BUCKETS - 1)
    return jnp.where(n < max_exact, n, large)


def _attention(q, kv, sinks, bias_table):
    b, s, _ = q.shape
    blk = ATT_BLOCK
    assert WINDOW == blk
    qi = jnp.arange(blk)[:, None]
    ki = jnp.arange(blk)[None, :]
    dist = jnp.where(ki <= qi, qi - ki, qi + blk - ki)
    onehot = jax.nn.one_hot(_t5_bucket(dist), NUM_BUCKETS, dtype=F32)
    bias = jnp.einsum("qkn,nh->hqk", onehot, bias_table.astype(F32), precision=lax.Precision.HIGHEST)

    sub = ATT_STEP_BLOCKS
    kvw = kv.shape[-1]
    return pl.pallas_call(
        _attention_kernel,
        grid=(b, s // (sub * blk)),
        in_specs=[
            pl.BlockSpec(memory_space=pltpu.SMEM),
            pl.BlockSpec((1, sub * blk, ATT_WIDTH), lambda i, j: (i, j, 0)),
            pl.BlockSpec((1, blk, kvw), lambda i, j: (i, jnp.maximum(sub * j - 1, 0), 0)),
            pl.BlockSpec((1, sub * blk, kvw), lambda i, j: (i, j, 0)),
            pl.BlockSpec((ATT_Q_HEADS, blk, blk), lambda i, j: (0, 0, 0)),
        ],
        out_specs=pl.BlockSpec((1, sub * blk, ATT_WIDTH), lambda i, j: (i, j, 0)),
        out_shape=jax.ShapeDtypeStruct((b, s, ATT_WIDTH), BF16),
        compiler_params=_params(("parallel", "arbitrary")),
        name="attention",
    )(sinks, q, kv, kv, bias)


def _pack_bf16_pairs(lo, hi):
    return pltpu.bitcast(hi, jnp.uint32) | (pltpu.bitcast(lo, jnp.uint32) >> 16)


def _unpack_bf16_pairs(packed):
    lo = pltpu.bitcast(packed << 16, F32).astype(BF16)
    hi = pltpu.bitcast(packed & jnp.uint32(0xFFFF0000), F32).astype(BF16)
    return lo, hi


def _stack_rows(rows, dtype):
    t = rows[0].shape[1]
    sub = lax.broadcasted_iota(jnp.int32, (SUBLANES, t), 0)
    out = jnp.zeros((SUBLANES, t), dtype)
    for i, r in enumerate(rows):
        out = jnp.where(sub == i, r.astype(dtype), out)
    return out


def _route_tile(xb, wrt_ref, brt_ref):
    t = xb.shape[0]
    logits = _dot_nt(wrt_ref[...], xb) + brt_ref[...]
    expert = lax.broadcasted_iota(jnp.int32, logits.shape, 0)
    vals = logits
    top_v, sels = [], []
    for _ in range(TOP_K):
        m = jnp.max(vals, axis=0, keepdims=True)
        idx = jnp.min(jnp.where(vals == m, expert, N_EXPERTS), axis=0, keepdims=True)
        sel = expert == idx
        vals = jnp.where(sel, -jnp.inf, vals)
        top_v.append(m)
        sels.append(sel)
    exps = [jnp.exp(v - top_v[0]) for v in top_v]
    denom = exps[0] + exps[1] + exps[2] + exps[3]
    gates = [e / denom for e in exps]

    chosen = jnp.zeros(logits.shape, F32)
    for sel in sels:
        chosen = chosen + sel.astype(F32)
    chosen_b = chosen.astype(BF16)
    r = lax.broadcasted_iota(jnp.int32, (t, t), 0)
    c = lax.broadcasted_iota(jnp.int32, (t, t), 1)
    earlier = jnp.where(r < c, 1.0, 0.0).astype(BF16)
    before = _dot(chosen_b, earlier)
    counts = jnp.sum(chosen, axis=1, keepdims=True)
    chunks = jnp.floor((counts + (CHUNK_ROWS - 1)) * (1.0 / CHUNK_ROWS))
    er = lax.broadcasted_iota(jnp.int32, (N_EXPERTS, N_EXPERTS), 0)
    ec = lax.broadcasted_iota(jnp.int32, (N_EXPERTS, N_EXPERTS), 1)
    lower = jnp.where(ec < er, 1.0, 0.0).astype(BF16)
    start = _dot(lower, jnp.broadcast_to(chunks, logits.shape).astype(BF16)) * float(CHUNK_ROWS)
    rows = [jnp.sum(jnp.where(sel, before + start, 0.0), axis=0, keepdims=True).astype(jnp.int32)
            for sel in sels]
    counts_rows = _dot_nt(jnp.ones((SUBLANES, t), BF16), chosen_b)

    local = lax.broadcasted_iota(jnp.int32, (LOCAL_ROWS, 1), 0)
    select = jnp.zeros((LOCAL_ROWS, t), F32)
    for row in rows:
        select = jnp.where(local == row, 1.0, select)
    return rows, gates, counts_rows, select.astype(BF16)


def _out_router_kernel(ret_ref, att_ref, x_ref, wo_ref, g_ref, wrt_ref, brt_ref,
                       h1_ref, xloc_ref, yzero_ref, slot_ref, gate_ref, cnt_ref):
    half = x_ref.shape[1] // 2
    for s in range(x_ref.shape[0] // ROUTE_TILE):
        tok = slice(s * ROUTE_TILE, (s + 1) * ROUTE_TILE)
        loc = slice(s * LOCAL_ROWS, (s + 1) * LOCAL_ROWS)
        h1 = (x_ref[tok, :] + _dot(ret_ref[tok, :], wo_ref[:RET_WIDTH, :])
              + _dot(att_ref[tok, :], wo_ref[RET_WIDTH:, :]))
        h1_ref[tok, :] = h1
        xb = _rms(h1, g_ref[...]).astype(BF16)
        rows, gates, counts_rows, select = _route_tile(xb, wrt_ref, brt_ref)
        xloc = _dot(select, xb)
        xloc_ref[loc, :] = _pack_bf16_pairs(xloc[:, :half], xloc[:, half:])
        slot_ref[:, tok] = _stack_rows(rows, jnp.int32)
        gate_ref[:, tok] = _stack_rows(gates, F32)
        cnt_ref[s] = counts_rows
    yzero_ref[...] = jnp.zeros_like(yzero_ref)


def _out_router(ret, att, x2, w_out, g, w_router_t, b_router_t):
    n, d = x2.shape
    per = ROUTE_STEP_TILES
    tm = per * ROUTE_TILE
    tiles = n // ROUTE_TILE
    row = lambda i: (i, 0)
    col = lambda i: (0, i)
    fixed = lambda i: (0, 0)
    return pl.pallas_call(
        _out_router_kernel,
        grid=(n // tm,),
        in_specs=[
            pl.BlockSpec((tm, RET_WIDTH), row),
            pl.BlockSpec((tm, ATT_WIDTH), row),
            pl.BlockSpec((tm, d), row),
            pl.BlockSpec((RET_WIDTH + ATT_WIDTH, d), fixed),
            pl.BlockSpec((1, d), fixed),
            pl.BlockSpec((N_EXPERTS, d), fixed),
            pl.BlockSpec((N_EXPERTS, 1), fixed),
        ],
        out_specs=[
            pl.BlockSpec((tm, d), row),
            pl.BlockSpec((per * LOCAL_ROWS, d // 2), row),
            pl.BlockSpec((per * LOCAL_ROWS, d // 2), row),
            pl.BlockSpec((SUBLANES, tm), col),
            pl.BlockSpec((SUBLANES, tm), col),
            pl.BlockSpec((per, SUBLANES, N_EXPERTS), lambda i: (i, 0, 0)),
        ],
        out_shape=[
            jax.ShapeDtypeStruct((n, d), F32),
            jax.ShapeDtypeStruct((tiles * LOCAL_ROWS, d // 2), jnp.uint32),
            jax.ShapeDtypeStruct((tiles * LOCAL_ROWS, d // 2), jnp.uint32),
            jax.ShapeDtypeStruct((SUBLANES, n), jnp.int32),
            jax.ShapeDtypeStruct((SUBLANES, n), F32),
            jax.ShapeDtypeStruct((tiles, SUBLANES, N_EXPERTS), F32),
        ],
        compiler_params=_params(("parallel",)),
        name="out_router",
    )(ret, att, x2, w_out, g, w_router_t, b_router_t)


def _convert_expert_weights(wu_ref, wdn_ref, wg_ref, wl_ref, wd_ref):
    ch = PERM_CHUNK
    hc = ch // 2
    r = lax.broadcasted_iota(jnp.int32, (ch, ch), 0)
    c = lax.broadcasted_iota(jnp.int32, (ch, ch), 1)
    perm = jnp.where(r == jnp.where(c < hc, 2 * c, 2 * (c - hc) + 1), 1.0, 0.0).astype(BF16)
    for j in range(wu_ref.shape[1] // ch):
        both = _dot(wu_ref[:, j * ch:(j + 1) * ch].astype(BF16), perm)
        wg_ref[:, j * hc:(j + 1) * hc] = both[:, :hc].astype(BF16)
        wl_ref[:, j * hc:(j + 1) * hc] = both[:, hc:].astype(BF16)
    wd_ref[...] = wdn_ref[...].astype(BF16)


def _moe_kernel(be_ref, nused_ref, src_ref, dst_ref,
                xloc_hbm, wu_ref, bg_ref, bl_ref, wdn_ref, bd_ref, yinit_hbm,
                yloc_hbm, xbuf, ybuf, gsem, ssem, wg_ref, wl_ref, wd_ref):
    del yinit_hbm
    i = pl.program_id(0)
    n_used = nused_ref[0]
    last = pl.num_programs(0) - 1
    slot = lax.rem(i, 2)
    bm, half = xbuf.shape[1], xbuf.shape[2]

    def chunk(ref, c):
        first = c * CHUNK_ROWS if isinstance(c, int) else pl.multiple_of(c * CHUNK_ROWS, CHUNK_ROWS)
        return ref.at[pl.ds(first, CHUNK_ROWS), :]

    def start_gather(step, sl):
        for j in range(BLOCK_CHUNKS):
            pltpu.make_async_copy(chunk(xloc_hbm, src_ref[step * BLOCK_CHUNKS + j]),
                                  chunk(xbuf.at[sl], j), gsem.at[sl]).start()

    def start_writeback(step, sl):
        for j in range(BLOCK_CHUNKS):
            dst = dst_ref[(step + PRIME_BLOCKS) * BLOCK_CHUNKS + j]
            pltpu.make_async_copy(chunk(ybuf.at[sl], j), chunk(yloc_hbm, dst), ssem.at[sl]).start()

    def wait_gather(sl):
        pltpu.make_async_copy(xloc_hbm.at[pl.ds(0, bm), :], xbuf.at[sl], gsem.at[sl]).wait()

    def wait_writeback(sl):
        pltpu.make_async_copy(ybuf.at[sl], yloc_hbm.at[pl.ds(0, bm), :], ssem.at[sl]).wait()

    @pl.when(i == 0)
    def _():
        ybuf[...] = jnp.zeros_like(ybuf)
        start_gather(0, 0)
        start_writeback(-2, 0)

    new_expert = jnp.logical_or(i == 0, be_ref[i] != be_ref[jnp.maximum(i - 1, 0)])

    @pl.when(jnp.logical_and(i < n_used, new_expert))
    def _():
        _convert_expert_weights(wu_ref, wdn_ref, wg_ref, wl_ref, wd_ref)

    @pl.when(i < n_used)
    def _():
        wait_gather(slot)
        start_gather(jnp.minimum(i + 1, last), 1 - slot)
        lo, hi = _unpack_bf16_pairs(xbuf[slot])
        start_writeback(i - 1, 1 - slot)

        def up(w_ref, b_ref):
            return _dot(lo, w_ref[:half, :]) + _dot(hi, w_ref[half:, :]) + b_ref[...]

        glu = jnp.minimum(up(wg_ref, bg_ref), SWIGLU_LIMIT)
        lin = jnp.clip(up(wl_ref, bl_ref), -SWIGLU_LIMIT, SWIGLU_LIMIT)
        act = glu * jax.nn.sigmoid(SWIGLU_ALPHA * glu) * (lin + 1.0)
        y = (_dot(act.astype(BF16), wd_ref[...]) + bd_ref[...]).astype(BF16).astype(F32)

        wait_writeback(slot)
        ybuf[slot] = _pack_bf16_pairs(y[:, :half], y[:, half:])

        @pl.when(i == n_used - 1)
        def _():
            start_writeback(i, slot)
            wait_writeback(1 - slot)
            wait_writeback(slot)
            wait_gather(1 - slot)


def _moe(block_e, n_used, chunk_src, chunk_dst, xloc, w_up, bg, bl, w_down, bd, yinit):
    half = xloc.shape[1]
    e, d, f2 = w_up.shape
    f = f2 // 2
    bm = MOE_BLOCK
    n_blocks = chunk_src.shape[0] // BLOCK_CHUNKS
    assert chunk_dst.shape[0] == (n_blocks + PRIME_BLOCKS) * BLOCK_CHUNKS
    expert3 = lambda i, be, nu, cs, cd: (be[i], 0, 0)
    operands = (block_e, n_used, chunk_src, chunk_dst, xloc, w_up, bg, bl, w_down, bd, yinit)
    grid_spec = pltpu.PrefetchScalarGridSpec(
        num_scalar_prefetch=4,
        grid=(n_blocks,),
        in_specs=[
            pl.BlockSpec(memory_space=pl.ANY),
            pl.BlockSpec((None, d, f2), expert3),
            pl.BlockSpec((None, 1, f), expert3),
            pl.BlockSpec((None, 1, f), expert3),
            pl.BlockSpec((None, f, d), expert3),
            pl.BlockSpec((None, 1, d), expert3),
            pl.BlockSpec(memory_space=pl.ANY),
        ],
        out_specs=pl.BlockSpec(memory_space=pl.ANY),
        scratch_shapes=[
            pltpu.VMEM((2, bm, half), jnp.uint32),
            pltpu.VMEM((2, bm, half), jnp.uint32),
            pltpu.SemaphoreType.DMA((2,)),
            pltpu.SemaphoreType.DMA((2,)),
            pltpu.VMEM((d, f), BF16),
            pltpu.VMEM((d, f), BF16),
            pltpu.VMEM((f, d), BF16),
        ],
    )
    return pl.pallas_call(
        _moe_kernel,
        grid_spec=grid_spec,
        out_shape=jax.ShapeDtypeStruct(yinit.shape, jnp.uint32),
        input_output_aliases={len(operands) - 1: 0},
        compiler_params=_params(("arbitrary",)),
        name="moe",
    )(*operands)


def _combine_ple_kernel(yloc_ref, slot_ref, gate_ref, h1_ref, p_ref, wpg_ref, wpp_ref, gple_ref,
                        gfin_ref, o_ref):
    local = lax.broadcasted_iota(jnp.int32, (LOCAL_ROWS, 1), 0)
    for s in range(h1_ref.shape[0] // ROUTE_TILE):
        tok = slice(s * ROUTE_TILE, (s + 1) * ROUTE_TILE)
        loc = slice(s * LOCAL_ROWS, (s + 1) * LOCAL_ROWS)
        rows = slot_ref[:, tok]
        gates = gate_ref[:, tok]
        weights = jnp.zeros((LOCAL_ROWS, ROUTE_TILE), F32)
        for k in range(TOP_K):
            weights = jnp.where(local == rows[k:k + 1, :], gates[k:k + 1, :], weights)
        weights = weights.astype(BF16)
        y_lo, y_hi = _unpack_bf16_pairs(yloc_ref[loc, :])
        moe = jnp.concatenate([_dot_tn(weights, y_lo), _dot_tn(weights, y_hi)], axis=1)
        h2 = h1_ref[tok, :] + moe
        gate = jax.nn.sigmoid(_dot(h2.astype(BF16), wpg_ref[...]))
        proj = _dot(p_ref[tok, :].astype(BF16), wpp_ref[...])
        h3 = h2 + _rms(gate * proj, gple_ref[...])
        o_ref[tok, :] = _rms(h3, gfin_ref[...])


def _combine_ple(yloc, slot_rows, gates, h1, p2, wpg, wpp, gple, gfin):
    n, d = h1.shape
    pd = p2.shape[1]
    per = ROUTE_STEP_TILES
    tm = per * ROUTE_TILE
    row = lambda i: (i, 0)
    col = lambda i: (0, i)
    fixed = lambda i: (0, 0)
    return pl.pallas_call(
        _combine_ple_kernel,
        grid=(n // tm,),
        in_specs=[
            pl.BlockSpec((per * LOCAL_ROWS, d // 2), row),
            pl.BlockSpec((SUBLANES, tm), col),
            pl.BlockSpec((SUBLANES, tm), col),
            pl.BlockSpec((tm, d), row),
            pl.BlockSpec((tm, pd), row),
            pl.BlockSpec((d, d), fixed),
            pl.BlockSpec((pd, d), fixed),
            pl.BlockSpec((1, d), fixed),
            pl.BlockSpec((1, d), fixed),
        ],
        out_specs=pl.BlockSpec((tm, d), row),
        out_shape=jax.ShapeDtypeStruct((n, d), F32),
        compiler_params=_params(("parallel",)),
        name="combine_ple",
    )(yloc, slot_rows, gates, h1, p2, wpg, wpp, gple, gfin)


def _routing_tables(counts):
    tiles = counts.shape[0]
    local_chunks = LOCAL_ROWS // CHUNK_ROWS
    assert 2 * BLOCK_CHUNKS <= tiles * SPARE_CHUNKS
    run = (counts + CHUNK_ROWS - 1) // CHUNK_ROWS
    local_start = jnp.cumsum(run, axis=1) - run
    run_end = jnp.cumsum(run, axis=0)
    run_start = run_end - run
    total = run_end[-1]
    padded = (total + BLOCK_CHUNKS - 1) // BLOCK_CHUNKS * BLOCK_CHUNKS
    pend = jnp.cumsum(padded)
    poff = pend - padded

    slots = tiles * ROUTE_TILE * TOP_K
    max_chunks = (slots + tiles * N_EXPERTS * (CHUNK_ROWS - 1)) // CHUNK_ROWS + N_EXPERTS * (BLOCK_CHUNKS - 1)
    n_chunks = -(-max_chunks // BLOCK_CHUNKS) * BLOCK_CHUNKS
    lookup = lambda onehot, table: jnp.dot(onehot, table.astype(F32), precision=lax.Precision.HIGHEST)
    q = jnp.arange(n_chunks, dtype=jnp.int32)
    e = jnp.minimum(jnp.sum(q[:, None] >= pend[None, :], axis=1), N_EXPERTS - 1).astype(jnp.int32)
    e_hot = (e[:, None] == jnp.arange(N_EXPERTS)[None, :]).astype(F32)
    per_expert = lookup(e_hot, jnp.stack([poff, total], axis=1)).astype(jnp.int32)
    j = q - per_expert[:, 0]
    valid = j < per_expert[:, 1]
    ends = lookup(e_hot, run_end.T).astype(jnp.int32)
    tile = jnp.minimum(jnp.sum(ends <= j[:, None], axis=1), tiles - 1).astype(jnp.int32)
    tile_hot = (tile[:, None] == jnp.arange(tiles)[None, :]).astype(F32)
    shift = jnp.sum(lookup(tile_hot, local_start - run_start) * e_hot, axis=1).astype(jnp.int32)
    src = jnp.where(valid, tile * local_chunks + shift + j, 0)
    spare = (q // BLOCK_CHUNKS % 2) * BLOCK_CHUNKS + q % BLOCK_CHUNKS
    spare_chunk = (spare // SPARE_CHUNKS) * local_chunks + (local_chunks - SPARE_CHUNKS) + spare % SPARE_CHUNKS
    dst = jnp.where(valid, src, spare_chunk)
    dst = jnp.concatenate([spare_chunk[:PRIME_BLOCKS * BLOCK_CHUNKS], dst])
    n_used = (pend[-1:] // BLOCK_CHUNKS).astype(jnp.int32)
    return e[::BLOCK_CHUNKS], n_used, src.astype(jnp.int32), dst.astype(jnp.int32)


def kernel(x, p, positions, rel_bias_table, g_mix_norm, w_in, b_in, ret_norm_g, att_sinks, w_out,
           g_moe_norm, w_router, b_router, w_up, b_up, w_down, b_down, w_ple_gate, w_ple_proj,
           g_ple_norm, g_final):
    b, s, d = x.shape
    depth = w_in.shape[0]
    assert depth == 1, "single-layer stack only"
    n = b * s
    x2 = x.reshape(n, d)

    half = RET_DK // 2
    inv = ROPE_BASE ** (-jnp.arange(half, dtype=F32) / half)
    ang = (positions.astype(F32)[..., None] * inv).reshape(n, half)

    qr, kr, vr, gr, qa, kva = _in_proj(
        x2, g_mix_norm[0][None], w_in[0].astype(BF16), b_in[0][None], jnp.cos(ang), jnp.sin(ang))

    shape3 = lambda t: t.reshape(b, s, t.shape[-1])
    ret = _retention(shape3(qr), shape3(kr), shape3(vr), shape3(gr), ret_norm_g[0])
    att = _attention(shape3(qa), shape3(kva), att_sinks[0], rel_bias_table)

    h1, xloc, yinit, slot_rows, gates, counts = _out_router(
        ret.reshape(n, RET_WIDTH), att.reshape(n, ATT_WIDTH), x2, w_out[0].astype(BF16),
        g_moe_norm[0][None], w_router[0].T.astype(BF16), b_router[0][:, None])

    block_e, n_used, chunk_src, chunk_dst = _routing_tables(counts[:, 0, :].astype(jnp.int32))

    yloc = _moe(block_e, n_used, chunk_src, chunk_dst, xloc, w_up[0],
                b_up[0][:, None, 0::2], b_up[0][:, None, 1::2],
                w_down[0], b_down[0].reshape(N_EXPERTS, 1, d), yinit)

    out = _combine_ple(yloc, slot_rows, gates, h1, p[0].reshape(n, -1),
                       w_ple_gate[0].astype(BF16), w_ple_proj[0].astype(BF16),
                       g_ple_norm[0][None], g_final[None])
    return out.reshape(b, s, d)
```

```python
import math

import jax
import jax.numpy as jnp
from jax import lax
from jax.experimental import pallas as pl
from jax.experimental.pallas import tpu as pltpu

RET_HEADS = 4
RET_DK = 128
RET_DV = 128
RET_CHUNK = 128
ROPE_BASE = 10000.0
ATT_Q_HEADS = 8
ATT_KV_HEADS = 2
ATT_GROUP = ATT_Q_HEADS // ATT_KV_HEADS
ATT_HEAD_DIM = 64
WINDOW = 128
ATT_BLOCK = 128
NUM_BUCKETS = 32
MAX_DISTANCE = 128
N_EXPERTS = 32
TOP_K = 4
SWIGLU_ALPHA = 1.702
SWIGLU_LIMIT = 7.0
EPS = 1e-5

RET_WIDTH = RET_HEADS * RET_DV
ATT_WIDTH = ATT_Q_HEADS * ATT_HEAD_DIM
KV_WIDTH = ATT_KV_HEADS * ATT_HEAD_DIM

VMEM_LIMIT_BYTES = 48 * 1024 * 1024

TOKEN_TILE = 1024
SUBLANES = 8
ROUTE_TILE = 256
ROUTE_STEP_TILES = 2
CHUNK_ROWS = SUBLANES
LOCAL_ROWS = ROUTE_TILE * TOP_K + N_EXPERTS * CHUNK_ROWS
SPARE_CHUNKS = (LOCAL_ROWS - ROUTE_TILE * TOP_K - N_EXPERTS * (CHUNK_ROWS - 1)) // CHUNK_ROWS
MOE_SUB_BLOCK = 256
MOE_BLOCK = 2 * MOE_SUB_BLOCK
BLOCK_CHUNKS = MOE_BLOCK // CHUNK_ROWS
PRIME_BLOCKS = 2
ATT_STEP_BLOCKS = 8
PERM_CHUNK = 256

F32 = jnp.float32
BF16 = jnp.bfloat16
NEG = float(jnp.finfo(jnp.float32).min)


def _rms(x, g):
    return x * lax.rsqrt(jnp.mean(x * x, axis=-1, keepdims=True) + EPS) * g


def _dot(a, b):
    return jnp.dot(a, b, preferred_element_type=F32)


def _dot_nt(a, b):
    return lax.dot_general(a, b, (((1,), (1,)), ((), ())), preferred_element_type=F32)


def _dot_tn(a, b):
    return lax.dot_general(a, b, (((0,), (0,)), ((), ())), preferred_element_type=F32)


def _params(semantics):
    return pltpu.CompilerParams(dimension_semantics=semantics, vmem_limit_bytes=VMEM_LIMIT_BYTES)


def _in_proj_kernel(x_ref, g_ref, w_ref, b_ref, cos_ref, sin_ref,
                    qr_ref, kr_ref, vr_ref, gr_ref, qa_ref, kva_ref):
    hn = _rms(x_ref[...], g_ref[...]).astype(BF16)
    cos = jnp.concatenate([cos_ref[...], cos_ref[...]], axis=1)
    sin = jnp.concatenate([-sin_ref[...], sin_ref[...]], axis=1)

    def proj(lo, hi):
        return _dot(hn, w_ref[:, lo:hi]) + b_ref[:, lo:hi]

    def rope(t):
        heads = []
        for h in range(RET_HEADS):
            th = t[:, h * RET_DK:(h + 1) * RET_DK]
            heads.append(th * cos + pltpu.roll(th, RET_DK // 2, 1) * sin)
        return jnp.concatenate(heads, axis=1)

    o = 0
    qr_ref[...] = rope(proj(o, o + RET_WIDTH)).astype(BF16)
    o += RET_WIDTH
    kr_ref[...] = (rope(proj(o, o + RET_WIDTH)) * (RET_DK ** -0.5)).astype(BF16)
    o += RET_WIDTH
    vr_ref[...] = proj(o, o + RET_WIDTH).astype(BF16)
    o += RET_WIDTH
    gr_ref[...] = proj(o, o + RET_WIDTH).astype(BF16)
    o += RET_WIDTH
    qa_ref[...] = (proj(o, o + ATT_WIDTH) * (ATT_HEAD_DIM ** -0.5)).astype(BF16)
    o += ATT_WIDTH
    low = lax.broadcasted_iota(jnp.int32, (1, KV_WIDTH), 1) < ATT_HEAD_DIM
    slabs = []
    for part in range(2):
        t = proj(o + part * KV_WIDTH, o + (part + 1) * KV_WIDTH)
        swapped = pltpu.roll(t, ATT_HEAD_DIM, 1)
        slabs += [jnp.where(low, t, swapped), jnp.where(low, swapped, t)]
    kva_ref[...] = jnp.concatenate(slabs, axis=1).astype(BF16)


def _in_proj(x2, g, w_in, b_in, cosf, sinf):
    n, d = x2.shape
    width = w_in.shape[1]
    tm = TOKEN_TILE
    row = lambda i: (i, 0)
    fixed = lambda i: (0, 0)
    outs = [RET_WIDTH, RET_WIDTH, RET_WIDTH, RET_WIDTH, ATT_WIDTH, 4 * KV_WIDTH]
    return pl.pallas_call(
        _in_proj_kernel,
        grid=(n // tm,),
        in_specs=[
            pl.BlockSpec((tm, d), row),
            pl.BlockSpec((1, d), fixed),
            pl.BlockSpec((d, width), fixed),
            pl.BlockSpec((1, width), fixed),
            pl.BlockSpec((tm, RET_DK // 2), row),
            pl.BlockSpec((tm, RET_DK // 2), row),
        ],
        out_specs=[pl.BlockSpec((tm, w), row) for w in outs],
        out_shape=[jax.ShapeDtypeStruct((n, w), BF16) for w in outs],
        compiler_params=_params(("parallel",)),
        name="in_proj",
    )(x2, g, w_in, b_in, cosf, sinf)


def _retention_kernel(q_ref, k_ref, v_ref, g_ref, dmask_ref, qdec_ref, kdec_ref, cdec_ref,
                      gn_ref, o_ref, state_ref):
    batch = q_ref.shape[0]

    @pl.when(pl.program_id(0) == 0)
    def _():
        state_ref[...] = jnp.zeros_like(state_ref)

    def per_batch(b, carry):
        for h in range(RET_HEADS):
            sl = slice(h * RET_DK, (h + 1) * RET_DK)
            q = q_ref[b, :, sl]
            k = k_ref[b, :, sl]
            v = v_ref[b, :, sl]
            scores = _dot_nt(q, k) * dmask_ref[h]
            inner = _dot(scores.astype(BF16), v)
            state = state_ref[b * RET_HEADS + h]
            cross = _dot(q, state.astype(BF16)) * qdec_ref[h]
            kd = (k.astype(F32) * kdec_ref[h]).astype(BF16)
            state_ref[b * RET_HEADS + h] = state * cdec_ref[h] + _dot_tn(kd, v)
            o = _rms(inner + cross, gn_ref[h])
            gate = g_ref[b, :, sl].astype(F32)
            o_ref[b, :, sl] = (gate * jax.nn.sigmoid(gate) * o).astype(BF16)
        return carry

    lax.fori_loop(0, batch, per_batch, 0, unroll=True)


def _retention(q, k, v, g, gn):
    b, s, w = q.shape
    c = RET_CHUNK
    log_g = jnp.log(1.0 - 2.0 ** (-5.0 - jnp.arange(RET_HEADS, dtype=F32)))
    ci = jnp.arange(c)
    diff = (ci[:, None] - ci[None, :]).astype(F32)
    dmask = jnp.where(diff >= 0, jnp.exp(jnp.maximum(diff, 0.0)[None] * log_g[:, None, None]), 0.0)
    cf = ci.astype(F32)
    kdec = jnp.exp((c - 1 - cf)[None, :] * log_g[:, None])
    qdec = jnp.exp((cf + 1)[None, :] * log_g[:, None])
    cdec = jnp.exp(c * log_g)
    kdec = jnp.broadcast_to(kdec[:, :, None], (RET_HEADS, c, RET_DK))
    qdec = jnp.broadcast_to(qdec[:, :, None], (RET_HEADS, c, RET_DV))
    cdec = jnp.broadcast_to(cdec[:, None, None], (RET_HEADS, RET_DK, RET_DV))
    gn3 = gn.reshape(RET_HEADS, 1, RET_DV)

    seq = lambda i: (0, i, 0)
    fixed = lambda i: (0, 0, 0)
    table = lambda shape: pl.BlockSpec(shape, fixed)
    return pl.pallas_call(
        _retention_kernel,
        grid=(s // c,),
        in_specs=[pl.BlockSpec((b, c, w), seq)] * 4 + [
            table((RET_HEADS, c, c)),
            table((RET_HEADS, c, RET_DV)),
            table((RET_HEADS, c, RET_DK)),
            table((RET_HEADS, RET_DK, RET_DV)),
            table((RET_HEADS, 1, RET_DV)),
        ],
        out_specs=pl.BlockSpec((b, c, w), seq),
        out_shape=jax.ShapeDtypeStruct((b, s, w), BF16),
        scratch_shapes=[pltpu.VMEM((b * RET_HEADS, RET_DK, RET_DV), F32)],
        compiler_params=_params(("arbitrary",)),
        name="retention",
    )(q, k, v, g, dmask, qdec, kdec, cdec, gn3)


def _attention_kernel(sinks_ref, q_ref, kvp_ref, kvc_ref, bias_ref, o_ref):
    blk = ATT_BLOCK
    slab = 2 * ATT_HEAD_DIM
    r = lax.broadcasted_iota(jnp.int32, (blk, blk), 0)
    c = lax.broadcasted_iota(jnp.int32, (blk, blk), 1)
    own = c <= r
    first_ok = jnp.logical_or(own, pl.program_id(1) > 0)
    low = lax.broadcasted_iota(jnp.int32, (1, slab), 1) < ATT_HEAD_DIM
    zero = jnp.zeros((), BF16)
    for i in range(q_ref.shape[1] // blk):
        rows = slice(i * blk, (i + 1) * blk)
        prev = (lambda cs: kvp_ref[0, :, cs]) if i == 0 else (
            lambda cs, rs=slice((i - 1) * blk, i * blk): kvc_ref[0, rs, cs])
        slabs = []
        for a in range(ATT_Q_HEADS // 2):
            g = (2 * a) // ATT_GROUP
            ks = slice(g * slab, (g + 1) * slab)
            vs = slice((ATT_KV_HEADS + g) * slab, (ATT_KV_HEADS + g + 1) * slab)
            k = jnp.concatenate([prev(ks), kvc_ref[0, rows, ks]], axis=0)
            v_prev, v_own = prev(vs), kvc_ref[0, rows, vs]
            q = q_ref[0, rows, a * slab:(a + 1) * slab]
            halves = []
            for part in range(2):
                h = 2 * a + part
                qz = jnp.where(low if part == 0 else jnp.logical_not(low), q, zero)
                both = _dot_nt(qz, k)
                s = jnp.where(own, both[:, blk:], both[:, :blk]) + bias_ref[h]
                if i == 0:
                    s = jnp.where(first_ok, s, NEG)
                sink = sinks_ref[h]
                m = jnp.maximum(jnp.max(s, axis=1, keepdims=True), sink)
                p = jnp.exp(s - m)
                denom = jnp.sum(p, axis=1, keepdims=True) + jnp.exp(sink - m)
                p_own = jnp.where(own, p, 0.0).astype(BF16)
                p_prev = jnp.where(own, 0.0, p).astype(BF16)
                halves.append((_dot(p_own, v_own) + _dot(p_prev, v_prev)) / denom)
            slabs.append(jnp.where(low, halves[0], halves[1]))
        o_ref[0, rows, :] = jnp.concatenate(slabs, axis=1).astype(BF16)


def _t5_bucket(n):
    max_exact = NUM_BUCKETS // 2
    nf = jnp.maximum(n, 1).astype(F32)
    large = max_exact + (jnp.log(nf / max_exact) / math.log(MAX_DISTANCE / max_exact)
                         * (NUM_BUCKETS - max_exact)).astype(jnp.int32)
    large = jnp.minimum(large, NUM_BUCKETS - 1)
    return jnp.where(n < max_exact, n, large)


def _attention(q, kv, sinks, bias_table):
    b, s, _ = q.shape
    blk = ATT_BLOCK
    assert WINDOW == blk
    qi = jnp.arange(blk)[:, None]
    ki = jnp.arange(blk)[None, :]
    dist = jnp.where(ki <= qi, qi - ki, qi + blk - ki)
    onehot = jax.nn.one_hot(_t5_bucket(dist), NUM_BUCKETS, dtype=F32)
    bias = jnp.einsum("qkn,nh->hqk", onehot, bias_table.astype(F32), precision=lax.Precision.HIGHEST)

    sub = ATT_STEP_BLOCKS
    kvw = kv.shape[-1]
    return pl.pallas_call(
        _attention_kernel,
        grid=(b, s // (sub * blk)),
        in_specs=[
            pl.BlockSpec(memory_space=pltpu.SMEM),
            pl.BlockSpec((1, sub * blk, ATT_WIDTH), lambda i, j: (i, j, 0)),
            pl.BlockSpec((1, blk, kvw), lambda i, j: (i, jnp.maximum(sub * j - 1, 0), 0)),
            pl.BlockSpec((1, sub * blk, kvw), lambda i, j: (i, j, 0)),
            pl.BlockSpec((ATT_Q_HEADS, blk, blk), lambda i, j: (0, 0, 0)),
        ],
        out_specs=pl.BlockSpec((1, sub * blk, ATT_WIDTH), lambda i, j: (i, j, 0)),
        out_shape=jax.ShapeDtypeStruct((b, s, ATT_WIDTH), BF16),
        compiler_params=_params(("parallel", "arbitrary")),
        name="attention",
    )(sinks, q, kv, kv, bias)


def _pack_bf16_pairs(lo, hi):
    return pltpu.bitcast(hi, jnp.uint32) | (pltpu.bitcast(lo, jnp.uint32) >> 16)


def _unpack_bf16_pairs(packed):
    lo = pltpu.bitcast(packed << 16, F32).astype(BF16)
    hi = pltpu.bitcast(packed & jnp.uint32(0xFFFF0000), F32).astype(BF16)
    return lo, hi


def _stack_rows(rows, dtype):
    t = rows[0].shape[1]
    sub = lax.broadcasted_iota(jnp.int32, (SUBLANES, t), 0)
    out = jnp.zeros((SUBLANES, t), dtype)
    for i, r in enumerate(rows):
        out = jnp.where(sub == i, r.astype(dtype), out)
    return out


def _route_tile(xb, wrt_ref, brt_ref):
    t = xb.shape[0]
    logits = _dot_nt(wrt_ref[...], xb) + brt_ref[...]
    expert = lax.broadcasted_iota(jnp.int32, logits.shape, 0)
    vals = logits
    top_v, sels = [], []
    for _ in range(TOP_K):
        m = jnp.max(vals, axis=0, keepdims=True)
        idx = jnp.min(jnp.where(vals == m, expert, N_EXPERTS), axis=0, keepdims=True)
        sel = expert == idx
        vals = jnp.where(sel, -jnp.inf, vals)
        top_v.append(m)
        sels.append(sel)
    exps = [jnp.exp(v - top_v[0]) for v in top_v]
    denom = exps[0] + exps[1] + exps[2] + exps[3]
    gates = [e / denom for e in exps]

    chosen = jnp.zeros(logits.shape, F32)
    for sel in sels:
        chosen = chosen + sel.astype(F32)
    chosen_b = chosen.astype(BF16)
    r = lax.broadcasted_iota(jnp.int32, (t, t), 0)
    c = lax.broadcasted_iota(jnp.int32, (t, t), 1)
    earlier = jnp.where(r < c, 1.0, 0.0).astype(BF16)
    before = _dot(chosen_b, earlier)
    counts = jnp.sum(chosen, axis=1, keepdims=True)
    chunks = jnp.floor((counts + (CHUNK_ROWS - 1)) * (1.0 / CHUNK_ROWS))
    er = lax.broadcasted_iota(jnp.int32, (N_EXPERTS, N_EXPERTS), 0)
    ec = lax.broadcasted_iota(jnp.int32, (N_EXPERTS, N_EXPERTS), 1)
    lower = jnp.where(ec < er, 1.0, 0.0).astype(BF16)
    start = _dot(lower, jnp.broadcast_to(chunks, logits.shape).astype(BF16)) * float(CHUNK_ROWS)
    rows = [jnp.sum(jnp.where(sel, before + start, 0.0), axis=0, keepdims=True).astype(jnp.int32)
            for sel in sels]
    counts_rows = _dot_nt(jnp.ones((SUBLANES, t), BF16), chosen_b)

    local = lax.broadcasted_iota(jnp.int32, (LOCAL_ROWS, 1), 0)
    select = jnp.zeros((LOCAL_ROWS, t), F32)
    for row in rows:
        select = jnp.where(local == row, 1.0, select)
    return rows, gates, counts_rows, select.astype(BF16)


def _out_router_kernel(ret_ref, att_ref, x_ref, wo_ref, g_ref, wrt_ref, brt_ref,
                       h1_ref, xloc_ref, yzero_ref, slot_ref, gate_ref, cnt_ref):
    half = x_ref.shape[1] // 2
    for s in range(x_ref.shape[0] // ROUTE_TILE):
        tok = slice(s * ROUTE_TILE, (s + 1) * ROUTE_TILE)
        loc = slice(s * LOCAL_ROWS, (s + 1) * LOCAL_ROWS)
        h1 = (x_ref[tok, :] + _dot(ret_ref[tok, :], wo_ref[:RET_WIDTH, :])
              + _dot(att_ref[tok, :], wo_ref[RET_WIDTH:, :]))
        h1_ref[tok, :] = h1
        xb = _rms(h1, g_ref[...]).astype(BF16)
        rows, gates, counts_rows, select = _route_tile(xb, wrt_ref, brt_ref)
        xloc = _dot(select, xb)
        xloc_ref[loc, :] = _pack_bf16_pairs(xloc[:, :half], xloc[:, half:])
        slot_ref[:, tok] = _stack_rows(rows, jnp.int32)
        gate_ref[:, tok] = _stack_rows(gates, F32)
        cnt_ref[s] = counts_rows
    yzero_ref[...] = jnp.zeros_like(yzero_ref)


def _out_router(ret, att, x2, w_out, g, w_router_t, b_router_t):
    n, d = x2.shape
    per = ROUTE_STEP_TILES
    tm = per * ROUTE_TILE
    tiles = n // ROUTE_TILE
    row = lambda i: (i, 0)
    col = lambda i: (0, i)
    fixed = lambda i: (0, 0)
    return pl.pallas_call(
        _out_router_kernel,
        grid=(n // tm,),
        in_specs=[
            pl.BlockSpec((tm, RET_WIDTH), row),
            pl.BlockSpec((tm, ATT_WIDTH), row),
            pl.BlockSpec((tm, d), row),
            pl.BlockSpec((RET_WIDTH + ATT_WIDTH, d), fixed),
            pl.BlockSpec((1, d), fixed),
            pl.BlockSpec((N_EXPERTS, d), fixed),
            pl.BlockSpec((N_EXPERTS, 1), fixed),
        ],
        out_specs=[
            pl.BlockSpec((tm, d), row),
            pl.BlockSpec((per * LOCAL_ROWS, d // 2), row),
            pl.BlockSpec((per * LOCAL_ROWS, d // 2), row),
            pl.BlockSpec((SUBLANES, tm), col),
            pl.BlockSpec((SUBLANES, tm), col),
            pl.BlockSpec((per, SUBLANES, N_EXPERTS), lambda i: (i, 0, 0)),
        ],
        out_shape=[
            jax.ShapeDtypeStruct((n, d), F32),
            jax.ShapeDtypeStruct((tiles * LOCAL_ROWS, d // 2), jnp.uint32),
            jax.ShapeDtypeStruct((tiles * LOCAL_ROWS, d // 2), jnp.uint32),
            jax.ShapeDtypeStruct((SUBLANES, n), jnp.int32),
            jax.ShapeDtypeStruct((SUBLANES, n), F32),
            jax.ShapeDtypeStruct((tiles, SUBLANES, N_EXPERTS), F32),
        ],
        compiler_params=_params(("parallel",)),
        name="out_router",
    )(ret, att, x2, w_out, g, w_router_t, b_router_t)


def _convert_expert_weights(wu_ref, wdn_ref, wg_ref, wl_ref, wd_ref):
    ch = PERM_CHUNK
    hc = ch // 2
    r = lax.broadcasted_iota(jnp.int32, (ch, ch), 0)
    c = lax.broadcasted_iota(jnp.int32, (ch, ch), 1)
    perm = jnp.where(r == jnp.where(c < hc, 2 * c, 2 * (c - hc) + 1), 1.0, 0.0).astype(BF16)
    for j in range(wu_ref.shape[1] // ch):
        both = _dot(wu_ref[:, j * ch:(j + 1) * ch].astype(BF16), perm)
        wg_ref[:, j * hc:(j + 1) * hc] = both[:, :hc].astype(BF16)
        wl_ref[:, j * hc:(j + 1) * hc] = both[:, hc:].astype(BF16)
    wd_ref[...] = wdn_ref[...].astype(BF16)


def _moe_kernel(be_ref, nused_ref, full_ref, src_ref, dst_ref,
                xloc_hbm, wu_ref, bg_ref, bl_ref, wdn_ref, bd_ref, yinit_hbm,
                yloc_hbm, xbuf, ybuf, gsem, ssem, wg_ref, wl_ref, wd_ref):
    del yinit_hbm
    i = pl.program_id(0)
    n_used = nused_ref[0]
    last = pl.num_programs(0) - 1
    slot = lax.rem(i, 2)
    bm, half = xbuf.shape[1], xbuf.shape[2]

    def chunk(ref, c):
        first = c * CHUNK_ROWS if isinstance(c, int) else pl.multiple_of(c * CHUNK_ROWS, CHUNK_ROWS)
        return ref.at[pl.ds(first, CHUNK_ROWS), :]

    def start_gather(step, sl):
        for j in range(BLOCK_CHUNKS):
            pltpu.make_async_copy(chunk(xloc_hbm, src_ref[step * BLOCK_CHUNKS + j]),
                                  chunk(xbuf.at[sl], j), gsem.at[sl]).start()

    def start_writeback(step, sl):
        for j in range(BLOCK_CHUNKS):
            dst = dst_ref[(step + PRIME_BLOCKS) * BLOCK_CHUNKS + j]
            pltpu.make_async_copy(chunk(ybuf.at[sl], j), chunk(yloc_hbm, dst), ssem.at[sl]).start()

    def wait_gather(sl):
        pltpu.make_async_copy(xloc_hbm.at[pl.ds(0, bm), :], xbuf.at[sl], gsem.at[sl]).wait()

    def wait_writeback(sl):
        pltpu.make_async_copy(ybuf.at[sl], yloc_hbm.at[pl.ds(0, bm), :], ssem.at[sl]).wait()

    @pl.when(i == 0)
    def _():
        ybuf[...] = jnp.zeros_like(ybuf)
        start_gather(0, 0)
        start_writeback(-2, 0)

    new_expert = jnp.logical_or(i == 0, be_ref[i] != be_ref[jnp.maximum(i - 1, 0)])

    @pl.when(jnp.logical_and(i < n_used, new_expert))
    def _():
        _convert_expert_weights(wu_ref, wdn_ref, wg_ref, wl_ref, wd_ref)

    def expert_mlp(rows):
        lo, hi = _unpack_bf16_pairs(xbuf[slot, rows, :])

        def up(w_ref, b_ref):
            return _dot(lo, w_ref[:half, :]) + _dot(hi, w_ref[half:, :]) + b_ref[...]

        glu = jnp.minimum(up(wg_ref, bg_ref), SWIGLU_LIMIT)
        lin = jnp.clip(up(wl_ref, bl_ref), -SWIGLU_LIMIT, SWIGLU_LIMIT)
        act = glu * jax.nn.sigmoid(SWIGLU_ALPHA * glu) * (lin + 1.0)
        y = (_dot(act.astype(BF16), wd_ref[...]) + bd_ref[...]).astype(BF16).astype(F32)
        ybuf[slot, rows, :] = _pack_bf16_pairs(y[:, :half], y[:, half:])

    def start_copies():
        start_gather(jnp.minimum(i + 1, last), 1 - slot)
        start_writeback(i - 1, 1 - slot)

    used = i < n_used
    full = full_ref[i] > 0
    first_half, second_half = slice(0, MOE_SUB_BLOCK), slice(MOE_SUB_BLOCK, bm)

    @pl.when(used)
    def _():
        wait_gather(slot)
        wait_writeback(slot)

    @pl.when(jnp.logical_and(used, full))
    def _():
        expert_mlp(first_half)
        start_copies()
        expert_mlp(second_half)

    @pl.when(jnp.logical_and(used, jnp.logical_not(full)))
    def _():
        start_copies()
        expert_mlp(first_half)

    @pl.when(i == n_used - 1)
    def _():
        start_writeback(i, slot)
        wait_writeback(1 - slot)
        wait_writeback(slot)
        wait_gather(1 - slot)


def _moe(block_e, n_used, block_full, chunk_src, chunk_dst, xloc, w_up, bg, bl, w_down, bd, yinit):
    half = xloc.shape[1]
    e, d, f2 = w_up.shape
    f = f2 // 2
    bm = MOE_BLOCK
    n_blocks = chunk_src.shape[0] // BLOCK_CHUNKS
    assert chunk_dst.shape[0] == (n_blocks + PRIME_BLOCKS) * BLOCK_CHUNKS
    expert3 = lambda i, be, nu, bf, cs, cd: (be[i], 0, 0)
    operands = (block_e, n_used, block_full, chunk_src, chunk_dst, xloc, w_up, bg, bl, w_down, bd, yinit)
    grid_spec = pltpu.PrefetchScalarGridSpec(
        num_scalar_prefetch=5,
        grid=(n_blocks,),
        in_specs=[
            pl.BlockSpec(memory_space=pl.ANY),
            pl.BlockSpec((None, d, f2), expert3),
            pl.BlockSpec((None, 1, f), expert3),
            pl.BlockSpec((None, 1, f), expert3),
            pl.BlockSpec((None, f, d), expert3),
            pl.BlockSpec((None, 1, d), expert3),
            pl.BlockSpec(memory_space=pl.ANY),
        ],
        out_specs=pl.BlockSpec(memory_space=pl.ANY),
        scratch_shapes=[
            pltpu.VMEM((2, bm, half), jnp.uint32),
            pltpu.VMEM((2, bm, half), jnp.uint32),
            pltpu.SemaphoreType.DMA((2,)),
            pltpu.SemaphoreType.DMA((2,)),
            pltpu.VMEM((d, f), BF16),
            pltpu.VMEM((d, f), BF16),
            pltpu.VMEM((f, d), BF16),
        ],
    )
    return pl.pallas_call(
        _moe_kernel,
        grid_spec=grid_spec,
        out_shape=jax.ShapeDtypeStruct(yinit.shape, jnp.uint32),
        input_output_aliases={len(operands) - 1: 0},
        compiler_params=_params(("arbitrary",)),
        name="moe",
    )(*operands)


def _combine_ple_kernel(yloc_ref, slot_ref, gate_ref, h1_ref, p_ref, wpg_ref, wpp_ref, gple_ref,
                        gfin_ref, o_ref):
    local = lax.broadcasted_iota(jnp.int32, (LOCAL_ROWS, 1), 0)
    for s in range(h1_ref.shape[0] // ROUTE_TILE):
        tok = slice(s * ROUTE_TILE, (s + 1) * ROUTE_TILE)
        loc = slice(s * LOCAL_ROWS, (s + 1) * LOCAL_ROWS)
        rows = slot_ref[:, tok]
        gates = gate_ref[:, tok]
        weights = jnp.zeros((LOCAL_ROWS, ROUTE_TILE), F32)
        for k in range(TOP_K):
            weights = jnp.where(local == rows[k:k + 1, :], gates[k:k + 1, :], weights)
        weights = weights.astype(BF16)
        y_lo, y_hi = _unpack_bf16_pairs(yloc_ref[loc, :])
        moe = jnp.concatenate([_dot_tn(weights, y_lo), _dot_tn(weights, y_hi)], axis=1)
        h2 = h1_ref[tok, :] + moe
        gate = jax.nn.sigmoid(_dot(h2.astype(BF16), wpg_ref[...]))
        proj = _dot(p_ref[tok, :].astype(BF16), wpp_ref[...])
        h3 = h2 + _rms(gate * proj, gple_ref[...])
        o_ref[tok, :] = _rms(h3, gfin_ref[...])


def _combine_ple(yloc, slot_rows, gates, h1, p2, wpg, wpp, gple, gfin):
    n, d = h1.shape
    pd = p2.shape[1]
    per = ROUTE_STEP_TILES
    tm = per * ROUTE_TILE
    row = lambda i: (i, 0)
    col = lambda i: (0, i)
    fixed = lambda i: (0, 0)
    return pl.pallas_call(
        _combine_ple_kernel,
        grid=(n // tm,),
        in_specs=[
            pl.BlockSpec((per * LOCAL_ROWS, d // 2), row),
            pl.BlockSpec((SUBLANES, tm), col),
            pl.BlockSpec((SUBLANES, tm), col),
            pl.BlockSpec((tm, d), row),
            pl.BlockSpec((tm, pd), row),
            pl.BlockSpec((d, d), fixed),
            pl.BlockSpec((pd, d), fixed),
            pl.BlockSpec((1, d), fixed),
            pl.BlockSpec((1, d), fixed),
        ],
        out_specs=pl.BlockSpec((tm, d), row),
        out_shape=jax.ShapeDtypeStruct((n, d), F32),
        compiler_params=_params(("parallel",)),
        name="combine_ple",
    )(yloc, slot_rows, gates, h1, p2, wpg, wpp, gple, gfin)


def _routing_tables(counts):
    tiles = counts.shape[0]
    local_chunks = LOCAL_ROWS // CHUNK_ROWS
    assert 2 * BLOCK_CHUNKS <= tiles * SPARE_CHUNKS
    run = (counts + CHUNK_ROWS - 1) // CHUNK_ROWS
    local_start = jnp.cumsum(run, axis=1) - run
    run_end = jnp.cumsum(run, axis=0)
    run_start = run_end - run
    total = run_end[-1]
    padded = (total + BLOCK_CHUNKS - 1) // BLOCK_CHUNKS * BLOCK_CHUNKS
    pend = jnp.cumsum(padded)
    poff = pend - padded

    slots = tiles * ROUTE_TILE * TOP_K
    max_chunks = (slots + tiles * N_EXPERTS * (CHUNK_ROWS - 1)) // CHUNK_ROWS + N_EXPERTS * (BLOCK_CHUNKS - 1)
    n_chunks = -(-max_chunks // BLOCK_CHUNKS) * BLOCK_CHUNKS
    lookup = lambda onehot, table: jnp.dot(onehot, table.astype(F32), precision=lax.Precision.HIGHEST)
    q = jnp.arange(n_chunks, dtype=jnp.int32)
    e = jnp.minimum(jnp.sum(q[:, None] >= pend[None, :], axis=1), N_EXPERTS - 1).astype(jnp.int32)
    e_hot = (e[:, None] == jnp.arange(N_EXPERTS)[None, :]).astype(F32)
    per_expert = lookup(e_hot, jnp.stack([poff, total], axis=1)).astype(jnp.int32)
    j = q - per_expert[:, 0]
    valid = j < per_expert[:, 1]
    ends = lookup(e_hot, run_end.T).astype(jnp.int32)
    tile = jnp.minimum(jnp.sum(ends <= j[:, None], axis=1), tiles - 1).astype(jnp.int32)
    tile_hot = (tile[:, None] == jnp.arange(tiles)[None, :]).astype(F32)
    shift = jnp.sum(lookup(tile_hot, local_start - run_start) * e_hot, axis=1).astype(jnp.int32)
    src = jnp.where(valid, tile * local_chunks + shift + j, 0)
    spare = (q // BLOCK_CHUNKS % 2) * BLOCK_CHUNKS + q % BLOCK_CHUNKS
    spare_chunk = (spare // SPARE_CHUNKS) * local_chunks + (local_chunks - SPARE_CHUNKS) + spare % SPARE_CHUNKS
    dst = jnp.where(valid, src, spare_chunk)
    dst = jnp.concatenate([spare_chunk[:PRIME_BLOCKS * BLOCK_CHUNKS], dst])
    n_used = (pend[-1:] // BLOCK_CHUNKS).astype(jnp.int32)
    block_full = valid[BLOCK_CHUNKS // 2::BLOCK_CHUNKS].astype(jnp.int32)
    return e[::BLOCK_CHUNKS], n_used, block_full, src.astype(jnp.int32), dst.astype(jnp.int32)


def kernel(x, p, positions, rel_bias_table, g_mix_norm, w_in, b_in, ret_norm_g, att_sinks, w_out,
           g_moe_norm, w_router, b_router, w_up, b_up, w_down, b_down, w_ple_gate, w_ple_proj,
           g_ple_norm, g_final):
    b, s, d = x.shape
    depth = w_in.shape[0]
    assert depth == 1, "single-layer stack only"
    n = b * s
    x2 = x.reshape(n, d)

    half = RET_DK // 2
    inv = ROPE_BASE ** (-jnp.arange(half, dtype=F32) / half)
    ang = (positions.astype(F32)[..., None] * inv).reshape(n, half)

    qr, kr, vr, gr, qa, kva = _in_proj(
        x2, g_mix_norm[0][None], w_in[0].astype(BF16), b_in[0][None], jnp.cos(ang), jnp.sin(ang))

    shape3 = lambda t: t.reshape(b, s, t.shape[-1])
    ret = _retention(shape3(qr), shape3(kr), shape3(vr), shape3(gr), ret_norm_g[0])
    att = _attention(shape3(qa), shape3(kva), att_sinks[0], rel_bias_table)

    h1, xloc, yinit, slot_rows, gates, counts = _out_router(
        ret.reshape(n, RET_WIDTH), att.reshape(n, ATT_WIDTH), x2, w_out[0].astype(BF16),
        g_moe_norm[0][None], w_router[0].T.astype(BF16), b_router[0][:, None])

    block_e, n_used, block_full, chunk_src, chunk_dst = _routing_tables(counts[:, 0, :].astype(jnp.int32))

    yloc = _moe(block_e, n_used, block_full, chunk_src, chunk_dst, xloc, w_up[0],
                b_up[0][:, None, 0::2], b_up[0][:, None, 1::2],
                w_down[0], b_down[0].reshape(N_EXPERTS, 1, d), yinit)

    out = _combine_ple(yloc, slot_rows, gates, h1, p[0].reshape(n, -1),
                       w_ple_gate[0].astype(BF16), w_ple_proj[0].astype(BF16),
                       g_ple_norm[0][None], g_final[None])
    return out.reshape(b, s, d)
```

```python
import math

import jax
import jax.numpy as jnp
from jax import lax
from jax.experimental import pallas as pl
from jax.experimental.pallas import tpu as pltpu

RET_HEADS = 4
RET_DK = 128
RET_DV = 128
RET_CHUNK = 128
ROPE_BASE = 10000.0
ATT_Q_HEADS = 8
ATT_KV_HEADS = 2
ATT_GROUP = ATT_Q_HEADS // ATT_KV_HEADS
ATT_HEAD_DIM = 64
WINDOW = 128
ATT_BLOCK = 128
NUM_BUCKETS = 32
MAX_DISTANCE = 128
N_EXPERTS = 32
TOP_K = 4
SWIGLU_ALPHA = 1.702
SWIGLU_LIMIT = 7.0
EPS = 1e-5

RET_WIDTH = RET_HEADS * RET_DV
ATT_WIDTH = ATT_Q_HEADS * ATT_HEAD_DIM
KV_WIDTH = ATT_KV_HEADS * ATT_HEAD_DIM

VMEM_LIMIT_BYTES = 48 * 1024 * 1024

TOKEN_TILE = 1024
SUBLANES = 8
ROUTE_TILE = 256
ROUTE_STEP_TILES = 2
CHUNK_ROWS = SUBLANES
LOCAL_ROWS = ROUTE_TILE * TOP_K + N_EXPERTS * CHUNK_ROWS
SPARE_CHUNKS = (LOCAL_ROWS - ROUTE_TILE * TOP_K - N_EXPERTS * (CHUNK_ROWS - 1)) // CHUNK_ROWS
MOE_BLOCK = 256
BLOCK_CHUNKS = MOE_BLOCK // CHUNK_ROWS
PRIME_BLOCKS = 2
ATT_STEP_BLOCKS = 8
PERM_CHUNK = 256

F32 = jnp.float32
BF16 = jnp.bfloat16
NEG = float(jnp.finfo(jnp.float32).min)


def _rms(x, g):
    return x * lax.rsqrt(jnp.mean(x * x, axis=-1, keepdims=True) + EPS) * g


def _dot(a, b):
    return jnp.dot(a, b, preferred_element_type=F32)


def _dot_nt(a, b):
    return lax.dot_general(a, b, (((1,), (1,)), ((), ())), preferred_element_type=F32)


def _dot_tn(a, b):
    return lax.dot_general(a, b, (((0,), (0,)), ((), ())), preferred_element_type=F32)


def _params(semantics):
    return pltpu.CompilerParams(dimension_semantics=semantics, vmem_limit_bytes=VMEM_LIMIT_BYTES)


def _in_proj_kernel(x_ref, g_ref, w_ref, b_ref, cos_ref, sin_ref,
                    qr_ref, kr_ref, vr_ref, gr_ref, qa_ref, kva_ref):
    hn = _rms(x_ref[...], g_ref[...]).astype(BF16)
    cos = jnp.concatenate([cos_ref[...], cos_ref[...]], axis=1)
    sin = jnp.concatenate([-sin_ref[...], sin_ref[...]], axis=1)

    def proj(lo, hi):
        return _dot(hn, w_ref[:, lo:hi]) + b_ref[:, lo:hi]

    def rope(t):
        heads = []
        for h in range(RET_HEADS):
            th = t[:, h * RET_DK:(h + 1) * RET_DK]
            heads.append(th * cos + pltpu.roll(th, RET_DK // 2, 1) * sin)
        return jnp.concatenate(heads, axis=1)

    o = 0
    qr_ref[...] = rope(proj(o, o + RET_WIDTH)).astype(BF16)
    o += RET_WIDTH
    kr_ref[...] = (rope(proj(o, o + RET_WIDTH)) * (RET_DK ** -0.5)).astype(BF16)
    o += RET_WIDTH
    vr_ref[...] = proj(o, o + RET_WIDTH).astype(BF16)
    o += RET_WIDTH
    gr_ref[...] = proj(o, o + RET_WIDTH).astype(BF16)
    o += RET_WIDTH
    qa_ref[...] = (proj(o, o + ATT_WIDTH) * (ATT_HEAD_DIM ** -0.5)).astype(BF16)
    o += ATT_WIDTH
    low = lax.broadcasted_iota(jnp.int32, (1, KV_WIDTH), 1) < ATT_HEAD_DIM
    slabs = []
    for part in range(2):
        t = proj(o + part * KV_WIDTH, o + (part + 1) * KV_WIDTH)
        swapped = pltpu.roll(t, ATT_HEAD_DIM, 1)
        slabs += [jnp.where(low, t, swapped), jnp.where(low, swapped, t)]
    kva_ref[...] = jnp.concatenate(slabs, axis=1).astype(BF16)


def _in_proj(x2, g, w_in, b_in, cosf, sinf):
    n, d = x2.shape
    width = w_in.shape[1]
    tm = TOKEN_TILE
    row = lambda i: (i, 0)
    fixed = lambda i: (0, 0)
    outs = [RET_WIDTH, RET_WIDTH, RET_WIDTH, RET_WIDTH, ATT_WIDTH, 4 * KV_WIDTH]
    return pl.pallas_call(
        _in_proj_kernel,
        grid=(n // tm,),
        in_specs=[
            pl.BlockSpec((tm, d), row),
            pl.BlockSpec((1, d), fixed),
            pl.BlockSpec((d, width), fixed),
            pl.BlockSpec((1, width), fixed),
            pl.BlockSpec((tm, RET_DK // 2), row),
            pl.BlockSpec((tm, RET_DK // 2), row),
        ],
        out_specs=[pl.BlockSpec((tm, w), row) for w in outs],
        out_shape=[jax.ShapeDtypeStruct((n, w), BF16) for w in outs],
        compiler_params=_params(("parallel",)),
        name="in_proj",
    )(x2, g, w_in, b_in, cosf, sinf)


def _retention_kernel(q_ref, k_ref, v_ref, g_ref, dmask_ref, qdec_ref, kdec_ref, cdec_ref,
                      gn_ref, o_ref, state_ref):
    batch = q_ref.shape[0]

    @pl.when(pl.program_id(0) == 0)
    def _():
        state_ref[...] = jnp.zeros_like(state_ref)

    def per_batch(b, carry):
        for h in range(RET_HEADS):
            sl = slice(h * RET_DK, (h + 1) * RET_DK)
            q = q_ref[b, :, sl]
            k = k_ref[b, :, sl]
            v = v_ref[b, :, sl]
            scores = _dot_nt(q, k) * dmask_ref[h]
            inner = _dot(scores.astype(BF16), v)
            state = state_ref[b * RET_HEADS + h]
            cross = _dot(q, state.astype(BF16)) * qdec_ref[h]
            kd = (k.astype(F32) * kdec_ref[h]).astype(BF16)
            state_ref[b * RET_HEADS + h] = state * cdec_ref[h] + _dot_tn(kd, v)
            o = _rms(inner + cross, gn_ref[h])
            gate = g_ref[b, :, sl].astype(F32)
            o_ref[b, :, sl] = (gate * jax.nn.sigmoid(gate) * o).astype(BF16)
        return carry

    lax.fori_loop(0, batch, per_batch, 0, unroll=True)


def _retention(q, k, v, g, gn):
    b, s, w = q.shape
    c = RET_CHUNK
    log_g = jnp.log(1.0 - 2.0 ** (-5.0 - jnp.arange(RET_HEADS, dtype=F32)))
    ci = jnp.arange(c)
    diff = (ci[:, None] - ci[None, :]).astype(F32)
    dmask = jnp.where(diff >= 0, jnp.exp(jnp.maximum(diff, 0.0)[None] * log_g[:, None, None]), 0.0)
    cf = ci.astype(F32)
    kdec = jnp.exp((c - 1 - cf)[None, :] * log_g[:, None])
    qdec = jnp.exp((cf + 1)[None, :] * log_g[:, None])
    cdec = jnp.exp(c * log_g)
    kdec = jnp.broadcast_to(kdec[:, :, None], (RET_HEADS, c, RET_DK))
    qdec = jnp.broadcast_to(qdec[:, :, None], (RET_HEADS, c, RET_DV))
    cdec = jnp.broadcast_to(cdec[:, None, None], (RET_HEADS, RET_DK, RET_DV))
    gn3 = gn.reshape(RET_HEADS, 1, RET_DV)

    seq = lambda i: (0, i, 0)
    fixed = lambda i: (0, 0, 0)
    table = lambda shape: pl.BlockSpec(shape, fixed)
    return pl.pallas_call(
        _retention_kernel,
        grid=(s // c,),
        in_specs=[pl.BlockSpec((b, c, w), seq)] * 4 + [
            table((RET_HEADS, c, c)),
            table((RET_HEADS, c, RET_DV)),
            table((RET_HEADS, c, RET_DK)),
            table((RET_HEADS, RET_DK, RET_DV)),
            table((RET_HEADS, 1, RET_DV)),
        ],
        out_specs=pl.BlockSpec((b, c, w), seq),
        out_shape=jax.ShapeDtypeStruct((b, s, w), BF16),
        scratch_shapes=[pltpu.VMEM((b * RET_HEADS, RET_DK, RET_DV), F32)],
        compiler_params=_params(("arbitrary",)),
        name="retention",
    )(q, k, v, g, dmask, qdec, kdec, cdec, gn3)


def _attention_kernel(sinks_ref, q_ref, kvp_ref, kvc_ref, bias_ref, o_ref):
    blk = ATT_BLOCK
    slab = 2 * ATT_HEAD_DIM
    r = lax.broadcasted_iota(jnp.int32, (blk, blk), 0)
    c = lax.broadcasted_iota(jnp.int32, (blk, blk), 1)
    own = c <= r
    first_ok = jnp.logical_or(own, pl.program_id(1) > 0)
    low = lax.broadcasted_iota(jnp.int32, (1, slab), 1) < ATT_HEAD_DIM
    zero = jnp.zeros((), BF16)
    for i in range(q_ref.shape[1] // blk):
        rows = slice(i * blk, (i + 1) * blk)
        prev = (lambda cs: kvp_ref[0, :, cs]) if i == 0 else (
            lambda cs, rs=slice((i - 1) * blk, i * blk): kvc_ref[0, rs, cs])
        slabs = []
        for a in range(ATT_Q_HEADS // 2):
            g = (2 * a) // ATT_GROUP
            ks = slice(g * slab, (g + 1) * slab)
            vs = slice((ATT_KV_HEADS + g) * slab, (ATT_KV_HEADS + g + 1) * slab)
            k = jnp.concatenate([prev(ks), kvc_ref[0, rows, ks]], axis=0)
            v_prev, v_own = prev(vs), kvc_ref[0, rows, vs]
            q = q_ref[0, rows, a * slab:(a + 1) * slab]
            halves = []
            for part in range(2):
                h = 2 * a + part
                qz = jnp.where(low if part == 0 else jnp.logical_not(low), q, zero)
                both = _dot_nt(qz, k)
                s = jnp.where(own, both[:, blk:], both[:, :blk]) + bias_ref[h]
                if i == 0:
                    s = jnp.where(first_ok, s, NEG)
                sink = sinks_ref[h]
                m = jnp.maximum(jnp.max(s, axis=1, keepdims=True), sink)
                p = jnp.exp(s - m)
                denom = jnp.sum(p, axis=1, keepdims=True) + jnp.exp(sink - m)
                p_own = jnp.where(own, p, 0.0).astype(BF16)
                p_prev = jnp.where(own, 0.0, p).astype(BF16)
                halves.append((_dot(p_own, v_own) + _dot(p_prev, v_prev)) / denom)
            slabs.append(jnp.where(low, halves[0], halves[1]))
        o_ref[0, rows, :] = jnp.concatenate(slabs, axis=1).astype(BF16)


def _t5_bucket(n):
    max_exact = NUM_BUCKETS // 2
    nf = jnp.maximum(n, 1).astype(F32)
    large = max_exact + (jnp.log(nf / max_exact) / math.log(MAX_DISTANCE / max_exact)
                         * (NUM_BUCKETS - max_exact)).astype(jnp.int32)
    large = jnp.minimum(large, NUM_BUCKETS - 1)
    return jnp.where(n < max_exact, n, large)


def _attention(q, kv, sinks, bias_table):
    b, s, _ = q.shape
    blk = ATT_BLOCK
    assert WINDOW == blk
    qi = jnp.arange(blk)[:, None]
    ki = jnp.arange(blk)[None, :]
    dist = jnp.where(ki <= qi, qi - ki, qi + blk - ki)
    onehot = jax.nn.one_hot(_t5_bucket(dist), NUM_BUCKETS, dtype=F32)
    bias = jnp.einsum("qkn,nh->hqk", onehot, bias_table.astype(F32), precision=lax.Precision.HIGHEST)

    sub = ATT_STEP_BLOCKS
    kvw = kv.shape[-1]
    return pl.pallas_call(
        _attention_kernel,
        grid=(b, s // (sub * blk)),
        in_specs=[
            pl.BlockSpec(memory_space=pltpu.SMEM),
            pl.BlockSpec((1, sub * blk, ATT_WIDTH), lambda i, j: (i, j, 0)),
            pl.BlockSpec((1, blk, kvw), lambda i, j: (i, jnp.maximum(sub * j - 1, 0), 0)),
            pl.BlockSpec((1, sub * blk, kvw), lambda i, j: (i, j, 0)),
            pl.BlockSpec((ATT_Q_HEADS, blk, blk), lambda i, j: (0, 0, 0)),
        ],
        out_specs=pl.BlockSpec((1, sub * blk, ATT_WIDTH), lambda i, j: (i, j, 0)),
        out_shape=jax.ShapeDtypeStruct((b, s, ATT_WIDTH), BF16),
        compiler_params=_params(("parallel", "arbitrary")),
        name="attention",
    )(sinks, q, kv, kv, bias)


def _pack_bf16_pairs(lo, hi):
    return pltpu.bitcast(pltpu.pack_elementwise([lo, hi], packed_dtype=BF16), jnp.uint32)


def _unpack_bf16_pairs(packed):
    lo = pltpu.bitcast(packed << 16, F32).astype(BF16)
    hi = pltpu.bitcast(packed & jnp.uint32(0xFFFF0000), F32).astype(BF16)
    return lo, hi


def _stack_rows(rows, dtype):
    t = rows[0].shape[1]
    sub = lax.broadcasted_iota(jnp.int32, (SUBLANES, t), 0)
    out = jnp.zeros((SUBLANES, t), dtype)
    for i, r in enumerate(rows):
        out = jnp.where(sub == i, r.astype(dtype), out)
    return out


def _route_tile(xb, wrt_ref, brt_ref):
    t = xb.shape[0]
    logits = _dot_nt(wrt_ref[...], xb) + brt_ref[...]
    expert = lax.broadcasted_iota(jnp.int32, logits.shape, 0)
    vals = logits
    top_v, sels = [], []
    for _ in range(TOP_K):
        m = jnp.max(vals, axis=0, keepdims=True)
        idx = jnp.min(jnp.where(vals == m, expert, N_EXPERTS), axis=0, keepdims=True)
        sel = expert == idx
        vals = jnp.where(sel, -jnp.inf, vals)
        top_v.append(m)
        sels.append(sel)
    exps = [jnp.exp(v - top_v[0]) for v in top_v]
    denom = exps[0] + exps[1] + exps[2] + exps[3]
    gates = [e / denom for e in exps]

    chosen = jnp.zeros(logits.shape, F32)
    for sel in sels:
        chosen = chosen + sel.astype(F32)
    chosen_b = chosen.astype(BF16)
    r = lax.broadcasted_iota(jnp.int32, (t, t), 0)
    c = lax.broadcasted_iota(jnp.int32, (t, t), 1)
    earlier = jnp.where(r < c, 1.0, 0.0).astype(BF16)
    before = _dot(chosen_b, earlier)
    counts = jnp.sum(chosen, axis=1, keepdims=True)
    chunks = jnp.floor((counts + (CHUNK_ROWS - 1)) * (1.0 / CHUNK_ROWS))
    er = lax.broadcasted_iota(jnp.int32, (N_EXPERTS, N_EXPERTS), 0)
    ec = lax.broadcasted_iota(jnp.int32, (N_EXPERTS, N_EXPERTS), 1)
    lower = jnp.where(ec < er, 1.0, 0.0).astype(BF16)
    start = _dot(lower, jnp.broadcast_to(chunks, logits.shape).astype(BF16)) * float(CHUNK_ROWS)
    rows = [jnp.sum(jnp.where(sel, before + start, 0.0), axis=0, keepdims=True).astype(jnp.int32)
            for sel in sels]
    counts_rows = _dot_nt(jnp.ones((SUBLANES, t), BF16), chosen_b)

    local = lax.broadcasted_iota(jnp.int32, (LOCAL_ROWS, 1), 0)
    select = jnp.zeros((LOCAL_ROWS, t), F32)
    for row in rows:
        select = jnp.where(local == row, 1.0, select)
    return rows, gates, counts_rows, select.astype(BF16)


def _out_router_kernel(ret_ref, att_ref, x_ref, wo_ref, g_ref, wrt_ref, brt_ref,
                       h1_ref, xloc_ref, slot_ref, gate_ref, cnt_ref):
    half = x_ref.shape[1] // 2
    for s in range(x_ref.shape[0] // ROUTE_TILE):
        tok = slice(s * ROUTE_TILE, (s + 1) * ROUTE_TILE)
        loc = slice(s * LOCAL_ROWS, (s + 1) * LOCAL_ROWS)
        h1 = (x_ref[tok, :] + _dot(ret_ref[tok, :], wo_ref[:RET_WIDTH, :])
              + _dot(att_ref[tok, :], wo_ref[RET_WIDTH:, :]))
        h1_ref[tok, :] = h1
        xb = _rms(h1, g_ref[...]).astype(BF16)
        rows, gates, counts_rows, select = _route_tile(xb, wrt_ref, brt_ref)
        xloc = _dot(select, xb)
        xloc_ref[loc, :] = _pack_bf16_pairs(xloc[:, :half], xloc[:, half:])
        slot_ref[:, tok] = _stack_rows(rows, jnp.int32)
        gate_ref[:, tok] = _stack_rows(gates, F32)
        cnt_ref[s] = counts_rows


def _out_router(ret, att, x2, w_out, g, w_router_t, b_router_t):
    n, d = x2.shape
    per = ROUTE_STEP_TILES
    tm = per * ROUTE_TILE
    tiles = n // ROUTE_TILE
    row = lambda i: (i, 0)
    col = lambda i: (0, i)
    fixed = lambda i: (0, 0)
    return pl.pallas_call(
        _out_router_kernel,
        grid=(n // tm,),
        in_specs=[
            pl.BlockSpec((tm, RET_WIDTH), row),
            pl.BlockSpec((tm, ATT_WIDTH), row),
            pl.BlockSpec((tm, d), row),
            pl.BlockSpec((RET_WIDTH + ATT_WIDTH, d), fixed),
            pl.BlockSpec((1, d), fixed),
            pl.BlockSpec((N_EXPERTS, d), fixed),
            pl.BlockSpec((N_EXPERTS, 1), fixed),
        ],
        out_specs=[
            pl.BlockSpec((tm, d), row),
            pl.BlockSpec((per * LOCAL_ROWS, d // 2), row),
            pl.BlockSpec((SUBLANES, tm), col),
            pl.BlockSpec((SUBLANES, tm), col),
            pl.BlockSpec((per, SUBLANES, N_EXPERTS), lambda i: (i, 0, 0)),
        ],
        out_shape=[
            jax.ShapeDtypeStruct((n, d), F32),
            jax.ShapeDtypeStruct((tiles * LOCAL_ROWS, d // 2), jnp.uint32),
            jax.ShapeDtypeStruct((SUBLANES, n), jnp.int32),
            jax.ShapeDtypeStruct((SUBLANES, n), F32),
            jax.ShapeDtypeStruct((tiles, SUBLANES, N_EXPERTS), F32),
        ],
        compiler_params=_params(("parallel",)),
        name="out_router",
    )(ret, att, x2, w_out, g, w_router_t, b_router_t)


def _convert_expert_weights(wu_ref, wdn_ref, wg_ref, wl_ref, wd_ref):
    ch = PERM_CHUNK
    hc = ch // 2
    r = lax.broadcasted_iota(jnp.int32, (ch, ch), 0)
    c = lax.broadcasted_iota(jnp.int32, (ch, ch), 1)
    perm = jnp.where(r == jnp.where(c < hc, 2 * c, 2 * (c - hc) + 1), 1.0, 0.0).astype(BF16)
    for j in range(wu_ref.shape[1] // ch):
        both = _dot(wu_ref[:, j * ch:(j + 1) * ch].astype(BF16), perm)
        wg_ref[:, j * hc:(j + 1) * hc] = both[:, :hc].astype(BF16)
        wl_ref[:, j * hc:(j + 1) * hc] = both[:, hc:].astype(BF16)
    wd_ref[...] = wdn_ref[...].astype(BF16)


def _moe_kernel(be_ref, nused_ref, src_ref, dst_ref,
                xin_hbm, wu_ref, bg_ref, bl_ref, wdn_ref, bd_ref,
                loc_hbm, xbuf, ybuf, gsem, ssem, wg_ref, wl_ref, wd_ref):
    del xin_hbm
    i = pl.program_id(0)
    n_used = nused_ref[0]
    last = pl.num_programs(0) - 1
    slot = lax.rem(i, 2)
    bm, half = xbuf.shape[1], xbuf.shape[2]

    def chunk(ref, c):
        first = c * CHUNK_ROWS if isinstance(c, int) else pl.multiple_of(c * CHUNK_ROWS, CHUNK_ROWS)
        return ref.at[pl.ds(first, CHUNK_ROWS), :]

    def start_gather(step, sl):
        for j in range(BLOCK_CHUNKS):
            pltpu.make_async_copy(chunk(loc_hbm, src_ref[step * BLOCK_CHUNKS + j]),
                                  chunk(xbuf.at[sl], j), gsem.at[sl]).start()

    def start_writeback(step, sl):
        for j in range(BLOCK_CHUNKS):
            dst = dst_ref[(step + PRIME_BLOCKS) * BLOCK_CHUNKS + j]
            pltpu.make_async_copy(chunk(ybuf.at[sl], j), chunk(loc_hbm, dst), ssem.at[sl]).start()

    def wait_gather(sl):
        pltpu.make_async_copy(loc_hbm.at[pl.ds(0, bm), :], xbuf.at[sl], gsem.at[sl]).wait()

    def wait_writeback(sl):
        pltpu.make_async_copy(ybuf.at[sl], loc_hbm.at[pl.ds(0, bm), :], ssem.at[sl]).wait()

    @pl.when(i == 0)
    def _():
        ybuf[...] = jnp.zeros_like(ybuf)
        start_gather(0, 0)
        start_writeback(-2, 0)

    new_expert = jnp.logical_or(i == 0, be_ref[i] != be_ref[jnp.maximum(i - 1, 0)])

    @pl.when(jnp.logical_and(i < n_used, new_expert))
    def _():
        _convert_expert_weights(wu_ref, wdn_ref, wg_ref, wl_ref, wd_ref)

    @pl.when(i < n_used)
    def _():
        wait_gather(slot)
        start_gather(jnp.minimum(i + 1, last), 1 - slot)
        lo, hi = _unpack_bf16_pairs(xbuf[slot])
        start_writeback(i - 1, 1 - slot)

        def up(w_ref, b_ref):
            return _dot(lo, w_ref[:half, :]) + _dot(hi, w_ref[half:, :]) + b_ref[...]

        glu = jnp.minimum(up(wg_ref, bg_ref), SWIGLU_LIMIT)
        lin = jnp.clip(up(wl_ref, bl_ref), -SWIGLU_LIMIT, SWIGLU_LIMIT)
        act = glu * jax.nn.sigmoid(SWIGLU_ALPHA * glu) * (lin + 1.0)
        y = _dot(act.astype(BF16), wd_ref[...]) + bd_ref[...]

        wait_writeback(slot)
        ybuf[slot] = _pack_bf16_pairs(y[:, :half], y[:, half:])

        @pl.when(i == n_used - 1)
        def _():
            start_writeback(i, slot)
            wait_writeback(1 - slot)
            wait_writeback(slot)
            wait_gather(1 - slot)


def _moe(block_e, n_used, chunk_src, chunk_dst, xloc, w_up, bg, bl, w_down, bd):
    half = xloc.shape[1]
    e, d, f2 = w_up.shape
    f = f2 // 2
    bm = MOE_BLOCK
    n_blocks = chunk_src.shape[0] // BLOCK_CHUNKS
    assert chunk_dst.shape[0] == (n_blocks + PRIME_BLOCKS) * BLOCK_CHUNKS
    expert3 = lambda i, be, nu, cs, cd: (be[i], 0, 0)
    prefetch = (block_e, n_used, chunk_src, chunk_dst)
    grid_spec = pltpu.PrefetchScalarGridSpec(
        num_scalar_prefetch=len(prefetch),
        grid=(n_blocks,),
        in_specs=[
            pl.BlockSpec(memory_space=pl.ANY),
            pl.BlockSpec((None, d, f2), expert3),
            pl.BlockSpec((None, 1, f), expert3),
            pl.BlockSpec((None, 1, f), expert3),
            pl.BlockSpec((None, f, d), expert3),
            pl.BlockSpec((None, 1, d), expert3),
        ],
        out_specs=pl.BlockSpec(memory_space=pl.ANY),
        scratch_shapes=[
            pltpu.VMEM((2, bm, half), jnp.uint32),
            pltpu.VMEM((2, bm, half), jnp.uint32),
            pltpu.SemaphoreType.DMA((2,)),
            pltpu.SemaphoreType.DMA((2,)),
            pltpu.VMEM((d, f), BF16),
            pltpu.VMEM((d, f), BF16),
            pltpu.VMEM((f, d), BF16),
        ],
    )
    return pl.pallas_call(
        _moe_kernel,
        grid_spec=grid_spec,
        out_shape=jax.ShapeDtypeStruct(xloc.shape, jnp.uint32),
        input_output_aliases={len(prefetch): 0},
        compiler_params=_params(("arbitrary",)),
        name="moe",
    )(*prefetch, xloc, w_up, bg, bl, w_down, bd)


def _combine_ple_kernel(yloc_ref, slot_ref, gate_ref, h1_ref, p_ref, wpg_ref, wpp_ref, gple_ref,
                        gfin_ref, o_ref):
    local = lax.broadcasted_iota(jnp.int32, (LOCAL_ROWS, 1), 0)
    for s in range(h1_ref.shape[0] // ROUTE_TILE):
        tok = slice(s * ROUTE_TILE, (s + 1) * ROUTE_TILE)
        loc = slice(s * LOCAL_ROWS, (s + 1) * LOCAL_ROWS)
        rows = slot_ref[:, tok]
        gates = gate_ref[:, tok]
        weights = jnp.zeros((LOCAL_ROWS, ROUTE_TILE), F32)
        for k in range(TOP_K):
            weights = jnp.where(local == rows[k:k + 1, :], gates[k:k + 1, :], weights)
        weights = weights.astype(BF16)
        y_lo, y_hi = _unpack_bf16_pairs(yloc_ref[loc, :])
        moe = jnp.concatenate([_dot_tn(weights, y_lo), _dot_tn(weights, y_hi)], axis=1)
        h2 = h1_ref[tok, :] + moe
        gate = jax.nn.sigmoid(_dot(h2.astype(BF16), wpg_ref[...]))
        proj = _dot(p_ref[tok, :].astype(BF16), wpp_ref[...])
        h3 = h2 + _rms(gate * proj, gple_ref[...])
        o_ref[tok, :] = _rms(h3, gfin_ref[...])


def _combine_ple(yloc, slot_rows, gates, h1, p2, wpg, wpp, gple, gfin):
    n, d = h1.shape
    pd = p2.shape[1]
    per = ROUTE_STEP_TILES
    tm = per * ROUTE_TILE
    row = lambda i: (i, 0)
    col = lambda i: (0, i)
    fixed = lambda i: (0, 0)
    return pl.pallas_call(
        _combine_ple_kernel,
        grid=(n // tm,),
        in_specs=[
            pl.BlockSpec((per * LOCAL_ROWS, d // 2), row),
            pl.BlockSpec((SUBLANES, tm), col),
            pl.BlockSpec((SUBLANES, tm), col),
            pl.BlockSpec((tm, d), row),
            pl.BlockSpec((tm, pd), row),
            pl.BlockSpec((d, d), fixed),
            pl.BlockSpec((pd, d), fixed),
            pl.BlockSpec((1, d), fixed),
            pl.BlockSpec((1, d), fixed),
        ],
        out_specs=pl.BlockSpec((tm, d), row),
        out_shape=jax.ShapeDtypeStruct((n, d), F32),
        compiler_params=_params(("parallel",)),
        name="combine_ple",
    )(yloc, slot_rows, gates, h1, p2, wpg, wpp, gple, gfin)


def _routing_tables(counts):
    tiles = counts.shape[0]
    local_chunks = LOCAL_ROWS // CHUNK_ROWS
    assert 2 * BLOCK_CHUNKS < tiles * SPARE_CHUNKS
    run = (counts + CHUNK_ROWS - 1) // CHUNK_ROWS
    local_start = jnp.cumsum(run, axis=1) - run
    run_end = jnp.cumsum(run, axis=0)
    run_start = run_end - run
    total = run_end[-1]
    padded = (total + BLOCK_CHUNKS - 1) // BLOCK_CHUNKS * BLOCK_CHUNKS
    pend = jnp.cumsum(padded)
    poff = pend - padded

    slots = tiles * ROUTE_TILE * TOP_K
    max_chunks = (slots + tiles * N_EXPERTS * (CHUNK_ROWS - 1)) // CHUNK_ROWS + N_EXPERTS * (BLOCK_CHUNKS - 1)
    n_chunks = -(-max_chunks // BLOCK_CHUNKS) * BLOCK_CHUNKS
    lookup = lambda onehot, table: jnp.dot(onehot, table.astype(F32), precision=lax.Precision.HIGHEST)
    q = jnp.arange(n_chunks, dtype=jnp.int32)
    e = jnp.minimum(jnp.sum(q[:, None] >= pend[None, :], axis=1), N_EXPERTS - 1).astype(jnp.int32)
    e_hot = (e[:, None] == jnp.arange(N_EXPERTS)[None, :]).astype(F32)
    per_expert = lookup(e_hot, jnp.stack([poff, total], axis=1)).astype(jnp.int32)
    j = q - per_expert[:, 0]
    valid = j < per_expert[:, 1]
    ends = lookup(e_hot, run_end.T).astype(jnp.int32)
    tile = jnp.minimum(jnp.sum(ends <= j[:, None], axis=1), tiles - 1).astype(jnp.int32)
    tile_hot = (tile[:, None] == jnp.arange(tiles)[None, :]).astype(F32)
    shift = jnp.sum(lookup(tile_hot, local_start - run_start) * e_hot, axis=1).astype(jnp.int32)
    src = jnp.where(valid, tile * local_chunks + shift + j, tiles * local_chunks - 1)
    spare = (q // BLOCK_CHUNKS % 2) * BLOCK_CHUNKS + q % BLOCK_CHUNKS
    spare_chunk = (spare // SPARE_CHUNKS) * local_chunks + (local_chunks - SPARE_CHUNKS) + spare % SPARE_CHUNKS
    dst = jnp.where(valid, tile * local_chunks + shift + j, spare_chunk)
    dst = jnp.concatenate([spare_chunk[:PRIME_BLOCKS * BLOCK_CHUNKS], dst])
    n_used = (pend[-1:] // BLOCK_CHUNKS).astype(jnp.int32)
    return e[::BLOCK_CHUNKS], n_used, src.astype(jnp.int32), dst.astype(jnp.int32)


def kernel(x, p, positions, rel_bias_table, g_mix_norm, w_in, b_in, ret_norm_g, att_sinks, w_out,
           g_moe_norm, w_router, b_router, w_up, b_up, w_down, b_down, w_ple_gate, w_ple_proj,
           g_ple_norm, g_final):
    b, s, d = x.shape
    depth = w_in.shape[0]
    assert depth == 1, "single-layer stack only"
    n = b * s
    x2 = x.reshape(n, d)

    half = RET_DK // 2
    inv = ROPE_BASE ** (-jnp.arange(half, dtype=F32) / half)
    ang = (positions.astype(F32)[..., None] * inv).reshape(n, half)

    qr, kr, vr, gr, qa, kva = _in_proj(
        x2, g_mix_norm[0][None], w_in[0].astype(BF16), b_in[0][None], jnp.cos(ang), jnp.sin(ang))

    shape3 = lambda t: t.reshape(b, s, t.shape[-1])
    ret = _retention(shape3(qr), shape3(kr), shape3(vr), shape3(gr), ret_norm_g[0])
    att = _attention(shape3(qa), shape3(kva), att_sinks[0], rel_bias_table)

    h1, xloc, slot_rows, gates, counts = _out_router(
        ret.reshape(n, RET_WIDTH), att.reshape(n, ATT_WIDTH), x2, w_out[0].astype(BF16),
        g_moe_norm[0][None], w_router[0].T.astype(BF16), b_router[0][:, None])

    block_e, n_used, chunk_src, chunk_dst = _routing_tables(counts[:, 0, :].astype(jnp.int32))

    yloc = _moe(block_e, n_used, chunk_src, chunk_dst, xloc, w_up[0],
                b_up[0][:, None, 0::2], b_up[0][:, None, 1::2],
                w_down[0], b_down[0].reshape(N_EXPERTS, 1, d))

    out = _combine_ple(yloc, slot_rows, gates, h1, p[0].reshape(n, -1),
                       w_ple_gate[0].astype(BF16), w_ple_proj[0].astype(BF16),
                       g_ple_norm[0][None], g_final[None])
    return out.reshape(b, s, d)
```

```python
import math

import jax
import jax.numpy as jnp
from jax import lax
from jax.experimental import pallas as pl
from jax.experimental.pallas import tpu as pltpu

RET_HEADS = 4
RET_DK = 128
RET_DV = 128
RET_CHUNK = 128
ROPE_BASE = 10000.0
ATT_Q_HEADS = 8
ATT_KV_HEADS = 2
ATT_GROUP = ATT_Q_HEADS // ATT_KV_HEADS
ATT_HEAD_DIM = 64
WINDOW = 128
ATT_BLOCK = 128
NUM_BUCKETS = 32
MAX_DISTANCE = 128
N_EXPERTS = 32
TOP_K = 4
SWIGLU_ALPHA = 1.702
SWIGLU_LIMIT = 7.0
EPS = 1e-5

RET_WIDTH = RET_HEADS * RET_DV
ATT_WIDTH = ATT_Q_HEADS * ATT_HEAD_DIM
KV_WIDTH = ATT_KV_HEADS * ATT_HEAD_DIM

VMEM_LIMIT_BYTES = 48 * 1024 * 1024

TOKEN_TILE = 1024
SUBLANES = 8
ROUTE_TILE = 256
ROUTE_STEP_TILES = 2
CHUNK_ROWS = SUBLANES
LOCAL_ROWS = ROUTE_TILE * TOP_K + N_EXPERTS * CHUNK_ROWS
SPARE_CHUNKS = (LOCAL_ROWS - ROUTE_TILE * TOP_K - N_EXPERTS * (CHUNK_ROWS - 1)) // CHUNK_ROWS
MOE_BLOCK = 256
BLOCK_CHUNKS = MOE_BLOCK // CHUNK_ROWS
PRIME_BLOCKS = 2
ATT_STEP_BLOCKS = 8
PERM_CHUNK = 256

F32 = jnp.float32
BF16 = jnp.bfloat16
NEG = float(jnp.finfo(jnp.float32).min)


def _rms(x, g):
    return x * lax.rsqrt(jnp.mean(x * x, axis=-1, keepdims=True) + EPS) * g


def _dot(a, b):
    return jnp.dot(a, b, preferred_element_type=F32)


def _dot_nt(a, b):
    return lax.dot_general(a, b, (((1,), (1,)), ((), ())), preferred_element_type=F32)


def _dot_tn(a, b):
    return lax.dot_general(a, b, (((0,), (0,)), ((), ())), preferred_element_type=F32)


def _params(semantics):
    return pltpu.CompilerParams(dimension_semantics=semantics, vmem_limit_bytes=VMEM_LIMIT_BYTES)


RESIDENT = pl.BlockSpec(memory_space=pltpu.VMEM)


def _in_proj_kernel(x_ref, g_ref, w_ref, b_ref, cos_ref, sin_ref,
                    qr_ref, kr_ref, vr_ref, gr_ref, qa_ref, kva_ref):
    hn = _rms(x_ref[...], g_ref[...]).astype(BF16)
    cos = jnp.concatenate([cos_ref[...], cos_ref[...]], axis=1)
    sin = jnp.concatenate([-sin_ref[...], sin_ref[...]], axis=1)

    def proj(lo, hi):
        return _dot(hn, w_ref[:, lo:hi]) + b_ref[:, lo:hi]

    def rope(t):
        heads = []
        for h in range(RET_HEADS):
            th = t[:, h * RET_DK:(h + 1) * RET_DK]
            heads.append(th * cos + pltpu.roll(th, RET_DK // 2, 1) * sin)
        return jnp.concatenate(heads, axis=1)

    o = 0
    qr_ref[...] = rope(proj(o, o + RET_WIDTH)).astype(BF16)
    o += RET_WIDTH
    kr_ref[...] = (rope(proj(o, o + RET_WIDTH)) * (RET_DK ** -0.5)).astype(BF16)
    o += RET_WIDTH
    vr_ref[...] = proj(o, o + RET_WIDTH).astype(BF16)
    o += RET_WIDTH
    gr_ref[...] = proj(o, o + RET_WIDTH).astype(BF16)
    o += RET_WIDTH
    qa_ref[...] = (proj(o, o + ATT_WIDTH) * (ATT_HEAD_DIM ** -0.5)).astype(BF16)
    o += ATT_WIDTH
    low = lax.broadcasted_iota(jnp.int32, (1, KV_WIDTH), 1) < ATT_HEAD_DIM
    slabs = []
    for part in range(2):
        t = proj(o + part * KV_WIDTH, o + (part + 1) * KV_WIDTH)
        swapped = pltpu.roll(t, ATT_HEAD_DIM, 1)
        slabs += [jnp.where(low, t, swapped), jnp.where(low, swapped, t)]
    kva_ref[...] = jnp.concatenate(slabs, axis=1).astype(BF16)


def _in_proj(x2, g, w_in, b_in, cosf, sinf):
    n, d = x2.shape
    width = w_in.shape[1]
    tm = TOKEN_TILE
    row = lambda i: (i, 0)
    outs = [RET_WIDTH, RET_WIDTH, RET_WIDTH, RET_WIDTH, ATT_WIDTH, 4 * KV_WIDTH]
    return pl.pallas_call(
        _in_proj_kernel,
        grid=(n // tm,),
        in_specs=[
            pl.BlockSpec((tm, d), row),
            RESIDENT,
            RESIDENT,
            RESIDENT,
            pl.BlockSpec((tm, RET_DK // 2), row),
            pl.BlockSpec((tm, RET_DK // 2), row),
        ],
        out_specs=[pl.BlockSpec((tm, w), row) for w in outs],
        out_shape=[jax.ShapeDtypeStruct((n, w), BF16) for w in outs],
        compiler_params=_params(("parallel",)),
        name="in_proj",
    )(x2, g, w_in, b_in, cosf, sinf)


def _retention_kernel(q_ref, k_ref, v_ref, g_ref, dmask_ref, qdec_ref, kdec_ref, cdec_ref,
                      gn_ref, o_ref, state_ref):
    batch = q_ref.shape[0]

    @pl.when(pl.program_id(0) == 0)
    def _():
        state_ref[...] = jnp.zeros_like(state_ref)

    def per_batch(b, carry):
        for h in range(RET_HEADS):
            sl = slice(h * RET_DK, (h + 1) * RET_DK)
            q = q_ref[b, :, sl]
            k = k_ref[b, :, sl]
            v = v_ref[b, :, sl]
            scores = _dot_nt(q, k) * dmask_ref[h]
            inner = _dot(scores.astype(BF16), v)
            state = state_ref[b * RET_HEADS + h]
            cross = _dot(q, state.astype(BF16)) * qdec_ref[h]
            kd = (k.astype(F32) * kdec_ref[h]).astype(BF16)
            state_ref[b * RET_HEADS + h] = state * cdec_ref[h] + _dot_tn(kd, v)
            o = _rms(inner + cross, gn_ref[h])
            gate = g_ref[b, :, sl].astype(F32)
            o_ref[b, :, sl] = (gate * jax.nn.sigmoid(gate) * o).astype(BF16)
        return carry

    lax.fori_loop(0, batch, per_batch, 0, unroll=True)


def _retention(q, k, v, g, gn):
    b, s, w = q.shape
    c = RET_CHUNK
    log_g = jnp.log(1.0 - 2.0 ** (-5.0 - jnp.arange(RET_HEADS, dtype=F32)))
    ci = jnp.arange(c)
    diff = (ci[:, None] - ci[None, :]).astype(F32)
    dmask = jnp.where(diff >= 0, jnp.exp(jnp.maximum(diff, 0.0)[None] * log_g[:, None, None]), 0.0)
    cf = ci.astype(F32)
    kdec = jnp.exp((c - 1 - cf)[None, :] * log_g[:, None])
    qdec = jnp.exp((cf + 1)[None, :] * log_g[:, None])
    cdec = jnp.exp(c * log_g)
    kdec = jnp.broadcast_to(kdec[:, :, None], (RET_HEADS, c, RET_DK))
    qdec = jnp.broadcast_to(qdec[:, :, None], (RET_HEADS, c, RET_DV))
    cdec = jnp.broadcast_to(cdec[:, None, None], (RET_HEADS, RET_DK, RET_DV))
    gn3 = gn.reshape(RET_HEADS, 1, RET_DV)

    seq = lambda i: (0, i, 0)
    return pl.pallas_call(
        _retention_kernel,
        grid=(s // c,),
        in_specs=[pl.BlockSpec((b, c, w), seq)] * 4 + [RESIDENT] * 5,
        out_specs=pl.BlockSpec((b, c, w), seq),
        out_shape=jax.ShapeDtypeStruct((b, s, w), BF16),
        scratch_shapes=[pltpu.VMEM((b * RET_HEADS, RET_DK, RET_DV), F32)],
        compiler_params=_params(("arbitrary",)),
        name="retention",
    )(q, k, v, g, dmask, qdec, kdec, cdec, gn3)


def _attention_kernel(sinks_ref, q_ref, kvp_ref, kvc_ref, bias_ref, o_ref):
    blk = ATT_BLOCK
    slab = 2 * ATT_HEAD_DIM
    r = lax.broadcasted_iota(jnp.int32, (blk, blk), 0)
    c = lax.broadcasted_iota(jnp.int32, (blk, blk), 1)
    own = c <= r
    first_ok = jnp.logical_or(own, pl.program_id(1) > 0)
    low = lax.broadcasted_iota(jnp.int32, (1, slab), 1) < ATT_HEAD_DIM
    zero = jnp.zeros((), BF16)
    for i in range(q_ref.shape[1] // blk):
        rows = slice(i * blk, (i + 1) * blk)
        prev = (lambda cs: kvp_ref[0, :, cs]) if i == 0 else (
            lambda cs, rs=slice((i - 1) * blk, i * blk): kvc_ref[0, rs, cs])
        slabs = []
        for a in range(ATT_Q_HEADS // 2):
            g = (2 * a) // ATT_GROUP
            ks = slice(g * slab, (g + 1) * slab)
            vs = slice((ATT_KV_HEADS + g) * slab, (ATT_KV_HEADS + g + 1) * slab)
            k = jnp.concatenate([prev(ks), kvc_ref[0, rows, ks]], axis=0)
            v_prev, v_own = prev(vs), kvc_ref[0, rows, vs]
            q = q_ref[0, rows, a * slab:(a + 1) * slab]
            halves = []
            for part in range(2):
                h = 2 * a + part
                qz = jnp.where(low if part == 0 else jnp.logical_not(low), q, zero)
                both = _dot_nt(qz, k)
                s = jnp.where(own, both[:, blk:], both[:, :blk]) + bias_ref[h]
                if i == 0:
                    s = jnp.where(first_ok, s, NEG)
                sink = sinks_ref[h]
                m = jnp.maximum(jnp.max(s, axis=1, keepdims=True), sink)
                p = jnp.exp(s - m)
                denom = jnp.sum(p, axis=1, keepdims=True) + jnp.exp(sink - m)
                p_own = jnp.where(own, p, 0.0).astype(BF16)
                p_prev = jnp.where(own, 0.0, p).astype(BF16)
                halves.append((_dot(p_own, v_own) + _dot(p_prev, v_prev)) / denom)
            slabs.append(jnp.where(low, halves[0], halves[1]))
        o_ref[0, rows, :] = jnp.concatenate(slabs, axis=1).astype(BF16)


def _t5_bucket(n):
    max_exact = NUM_BUCKETS // 2
    nf = jnp.maximum(n, 1).astype(F32)
    large = max_exact + (jnp.log(nf / max_exact) / math.log(MAX_DISTANCE / max_exact)
                         * (NUM_BUCKETS - max_exact)).astype(jnp.int32)
    large = jnp.minimum(large, NUM_BUCKETS - 1)
    return jnp.where(n < max_exact, n, large)


def _attention(q, kv, sinks, bias_table):
    b, s, _ = q.shape
    blk = ATT_BLOCK
    assert WINDOW == blk
    qi = jnp.arange(blk)[:, None]
    ki = jnp.arange(blk)[None, :]
    dist = jnp.where(ki <= qi, qi - ki, qi + blk - ki)
    onehot = jax.nn.one_hot(_t5_bucket(dist), NUM_BUCKETS, dtype=F32)
    bias = jnp.einsum("qkn,nh->hqk", onehot, bias_table.astype(F32), precision=lax.Precision.HIGHEST)

    sub = ATT_STEP_BLOCKS
    kvw = kv.shape[-1]
    return pl.pallas_call(
        _attention_kernel,
        grid=(b, s // (sub * blk)),
        in_specs=[
            pl.BlockSpec(memory_space=pltpu.SMEM),
            pl.BlockSpec((1, sub * blk, ATT_WIDTH), lambda i, j: (i, j, 0)),
            pl.BlockSpec((1, blk, kvw), lambda i, j: (i, jnp.maximum(sub * j - 1, 0), 0)),
            pl.BlockSpec((1, sub * blk, kvw), lambda i, j: (i, j, 0)),
            RESIDENT,
        ],
        out_specs=pl.BlockSpec((1, sub * blk, ATT_WIDTH), lambda i, j: (i, j, 0)),
        out_shape=jax.ShapeDtypeStruct((b, s, ATT_WIDTH), BF16),
        compiler_params=_params(("parallel", "arbitrary")),
        name="attention",
    )(sinks, q, kv, kv, bias)


def _pack_bf16_pairs(lo, hi):
    return pltpu.bitcast(pltpu.pack_elementwise([lo, hi], packed_dtype=BF16), jnp.uint32)


def _unpack_bf16_pairs(packed):
    lo = pltpu.bitcast(packed << 16, F32).astype(BF16)
    hi = pltpu.bitcast(packed & jnp.uint32(0xFFFF0000), F32).astype(BF16)
    return lo, hi


def _stack_rows(rows, dtype):
    t = rows[0].shape[1]
    sub = lax.broadcasted_iota(jnp.int32, (SUBLANES, t), 0)
    out = jnp.zeros((SUBLANES, t), dtype)
    for i, r in enumerate(rows):
        out = jnp.where(sub == i, r.astype(dtype), out)
    return out


def _route_tile(xb, wrt_ref, brt_ref):
    t = xb.shape[0]
    logits = _dot_nt(wrt_ref[...], xb) + brt_ref[...]
    expert = lax.broadcasted_iota(jnp.int32, logits.shape, 0)
    vals = logits
    top_v, sels = [], []
    for _ in range(TOP_K):
        m = jnp.max(vals, axis=0, keepdims=True)
        idx = jnp.min(jnp.where(vals == m, expert, N_EXPERTS), axis=0, keepdims=True)
        sel = expert == idx
        vals = jnp.where(sel, -jnp.inf, vals)
        top_v.append(m)
        sels.append(sel)
    exps = [jnp.exp(v - top_v[0]) for v in top_v]
    denom = exps[0] + exps[1] + exps[2] + exps[3]
    gates = [e / denom for e in exps]

    chosen = jnp.zeros(logits.shape, F32)
    for sel in sels:
        chosen = chosen + sel.astype(F32)
    chosen_b = chosen.astype(BF16)
    r = lax.broadcasted_iota(jnp.int32, (t, t), 0)
    c = lax.broadcasted_iota(jnp.int32, (t, t), 1)
    earlier = jnp.where(r < c, 1.0, 0.0).astype(BF16)
    before = _dot(chosen_b, earlier)
    counts = jnp.sum(chosen, axis=1, keepdims=True)
    chunks = jnp.floor((counts + (CHUNK_ROWS - 1)) * (1.0 / CHUNK_ROWS))
    er = lax.broadcasted_iota(jnp.int32, (N_EXPERTS, N_EXPERTS), 0)
    ec = lax.broadcasted_iota(jnp.int32, (N_EXPERTS, N_EXPERTS), 1)
    lower = jnp.where(ec < er, 1.0, 0.0).astype(BF16)
    start = _dot(lower, jnp.broadcast_to(chunks, logits.shape).astype(BF16)) * float(CHUNK_ROWS)
    rows = [jnp.sum(jnp.where(sel, before + start, 0.0), axis=0, keepdims=True).astype(jnp.int32)
            for sel in sels]
    counts_rows = _dot_nt(jnp.ones((SUBLANES, t), BF16), chosen_b)

    local = lax.broadcasted_iota(jnp.int32, (LOCAL_ROWS, 1), 0)
    select = jnp.zeros((LOCAL_ROWS, t), F32)
    for row in rows:
        select = jnp.where(local == row, 1.0, select)
    return rows, gates, counts_rows, select.astype(BF16)


def _out_router_kernel(ret_ref, att_ref, x_ref, wo_ref, g_ref, wrt_ref, brt_ref,
                       h1_ref, xloc_ref, slot_ref, gate_ref, cnt_ref):
    half = x_ref.shape[1] // 2
    for s in range(x_ref.shape[0] // ROUTE_TILE):
        tok = slice(s * ROUTE_TILE, (s + 1) * ROUTE_TILE)
        loc = slice(s * LOCAL_ROWS, (s + 1) * LOCAL_ROWS)
        h1 = (x_ref[tok, :] + _dot(ret_ref[tok, :], wo_ref[:RET_WIDTH, :])
              + _dot(att_ref[tok, :], wo_ref[RET_WIDTH:, :]))
        h1_ref[tok, :] = h1
        xb = _rms(h1, g_ref[...]).astype(BF16)
        rows, gates, counts_rows, select = _route_tile(xb, wrt_ref, brt_ref)
        xloc = _dot(select, xb)
        xloc_ref[loc, :] = _pack_bf16_pairs(xloc[:, :half], xloc[:, half:])
        slot_ref[:, tok] = _stack_rows(rows, jnp.int32)
        gate_ref[:, tok] = _stack_rows(gates, F32)
        cnt_ref[s] = counts_rows


def _out_router(ret, att, x2, w_out, g, w_router_t, b_router_t):
    n, d = x2.shape
    per = ROUTE_STEP_TILES
    tm = per * ROUTE_TILE
    tiles = n // ROUTE_TILE
    row = lambda i: (i, 0)
    col = lambda i: (0, i)
    fixed = lambda i: (0, 0)
    return pl.pallas_call(
        _out_router_kernel,
        grid=(n // tm,),
        in_specs=[
            pl.BlockSpec((tm, RET_WIDTH), row),
            pl.BlockSpec((tm, ATT_WIDTH), row),
            pl.BlockSpec((tm, d), row),
            RESIDENT,
            RESIDENT,
            RESIDENT,
            RESIDENT,
        ],
        out_specs=[
            pl.BlockSpec((tm, d), row),
            pl.BlockSpec((per * LOCAL_ROWS, d // 2), row),
            pl.BlockSpec((SUBLANES, tm), col),
            pl.BlockSpec((SUBLANES, tm), col),
            pl.BlockSpec((per, SUBLANES, N_EXPERTS), lambda i: (i, 0, 0)),
        ],
        out_shape=[
            jax.ShapeDtypeStruct((n, d), F32),
            jax.ShapeDtypeStruct((tiles * LOCAL_ROWS, d // 2), jnp.uint32),
            jax.ShapeDtypeStruct((SUBLANES, n), jnp.int32),
            jax.ShapeDtypeStruct((SUBLANES, n), F32),
            jax.ShapeDtypeStruct((tiles, SUBLANES, N_EXPERTS), F32),
        ],
        compiler_params=_params(("parallel",)),
        name="out_router",
    )(ret, att, x2, w_out, g, w_router_t, b_router_t)


def _convert_expert_weights(wu_ref, wdn_ref, wg_ref, wl_ref, wd_ref):
    ch = PERM_CHUNK
    hc = ch // 2
    r = lax.broadcasted_iota(jnp.int32, (ch, ch), 0)
    c = lax.broadcasted_iota(jnp.int32, (ch, ch), 1)
    perm = jnp.where(r == jnp.where(c < hc, 2 * c, 2 * (c - hc) + 1), 1.0, 0.0).astype(BF16)
    for j in range(wu_ref.shape[1] // ch):
        both = _dot(wu_ref[:, j * ch:(j + 1) * ch].astype(BF16), perm)
        wg_ref[:, j * hc:(j + 1) * hc] = both[:, :hc].astype(BF16)
        wl_ref[:, j * hc:(j + 1) * hc] = both[:, hc:].astype(BF16)
    wd_ref[...] = wdn_ref[...].astype(BF16)


def _moe_kernel(be_ref, nused_ref, ord_ref, ue_ref, src_ref, dst_ref,
                xin_hbm, wu_hbm, bg_ref, bl_ref, wdn_hbm, bd_ref,
                loc_hbm, xbuf, ybuf, gsem, ssem, wu_buf, wdn_buf, wsem, wg_ref, wl_ref, wd_ref):
    del xin_hbm
    i = pl.program_id(0)
    n_used = nused_ref[0]
    last = pl.num_programs(0) - 1
    slot = lax.rem(i, 2)
    bm, half = xbuf.shape[1], xbuf.shape[2]

    def chunk(ref, c):
        first = c * CHUNK_ROWS if isinstance(c, int) else pl.multiple_of(c * CHUNK_ROWS, CHUNK_ROWS)
        return ref.at[pl.ds(first, CHUNK_ROWS), :]

    def start_gather(step, sl):
        for j in range(BLOCK_CHUNKS):
            pltpu.make_async_copy(chunk(loc_hbm, src_ref[step * BLOCK_CHUNKS + j]),
                                  chunk(xbuf.at[sl], j), gsem.at[sl]).start()

    def start_writeback(step, sl):
        for j in range(BLOCK_CHUNKS):
            dst = dst_ref[(step + PRIME_BLOCKS) * BLOCK_CHUNKS + j]
            pltpu.make_async_copy(chunk(ybuf.at[sl], j), chunk(loc_hbm, dst), ssem.at[sl]).start()

    def wait_gather(sl):
        pltpu.make_async_copy(loc_hbm.at[pl.ds(0, bm), :], xbuf.at[sl], gsem.at[sl]).wait()

    def wait_writeback(sl):
        pltpu.make_async_copy(ybuf.at[sl], loc_hbm.at[pl.ds(0, bm), :], ssem.at[sl]).wait()

    def weight_fetch(expert, buf):
        return (pltpu.make_async_copy(wu_hbm.at[expert], wu_buf.at[buf], wsem.at[buf, 0]),
                pltpu.make_async_copy(wdn_hbm.at[expert], wdn_buf.at[buf], wsem.at[buf, 1]))

    @pl.when(i == 0)
    def _():
        ybuf[...] = jnp.zeros_like(ybuf)
        start_gather(0, 0)
        start_writeback(-2, 0)
        for copy in weight_fetch(ue_ref[0], 0):
            copy.start()

    new_expert = jnp.logical_or(i == 0, be_ref[i] != be_ref[jnp.maximum(i - 1, 0)])

    @pl.when(jnp.logical_and(i < n_used, new_expert))
    def _():
        k = ord_ref[i]
        buf = lax.rem(k, 2)
        for copy in weight_fetch(be_ref[i], buf):
            copy.wait()

        @pl.when(k + 1 < nused_ref[1])
        def _():
            for copy in weight_fetch(ue_ref[k + 1], 1 - buf):
                copy.start()

        _convert_expert_weights(wu_buf.at[buf], wdn_buf.at[buf], wg_ref, wl_ref, wd_ref)

    @pl.when(i < n_used)
    def _():
        wait_gather(slot)
        start_gather(jnp.minimum(i + 1, last), 1 - slot)
        lo, hi = _unpack_bf16_pairs(xbuf[slot])
        start_writeback(i - 1, 1 - slot)

        expert = be_ref[i]

        def up(w_ref, b_ref):
            return _dot(lo, w_ref[:half, :]) + _dot(hi, w_ref[half:, :]) + b_ref[expert]

        glu = jnp.minimum(up(wg_ref, bg_ref), SWIGLU_LIMIT)
        lin = jnp.clip(up(wl_ref, bl_ref), -SWIGLU_LIMIT, SWIGLU_LIMIT)
        act = glu * jax.nn.sigmoid(SWIGLU_ALPHA * glu) * (lin + 1.0)
        y = _dot(act.astype(BF16), wd_ref[...]) + bd_ref[expert]

        wait_writeback(slot)
        ybuf[slot] = _pack_bf16_pairs(y[:, :half], y[:, half:])

        @pl.when(i == n_used - 1)
        def _():
            start_writeback(i, slot)
            wait_writeback(1 - slot)
            wait_writeback(slot)
            wait_gather(1 - slot)


def _moe(block_e, n_used, block_ord, used_experts, chunk_src, chunk_dst, xloc, w_up, bg, bl, w_down, bd):
    half = xloc.shape[1]
    e, d, f2 = w_up.shape
    f = f2 // 2
    bm = MOE_BLOCK
    n_blocks = chunk_src.shape[0] // BLOCK_CHUNKS
    assert chunk_dst.shape[0] == (n_blocks + PRIME_BLOCKS) * BLOCK_CHUNKS
    prefetch = (block_e, n_used, block_ord, used_experts, chunk_src, chunk_dst)
    grid_spec = pltpu.PrefetchScalarGridSpec(
        num_scalar_prefetch=len(prefetch),
        grid=(n_blocks,),
        in_specs=[
            pl.BlockSpec(memory_space=pl.ANY),
            pl.BlockSpec(memory_space=pl.ANY),
            RESIDENT,
            RESIDENT,
            pl.BlockSpec(memory_space=pl.ANY),
            RESIDENT,
        ],
        out_specs=pl.BlockSpec(memory_space=pl.ANY),
        scratch_shapes=[
            pltpu.VMEM((2, bm, half), jnp.uint32),
            pltpu.VMEM((2, bm, half), jnp.uint32),
            pltpu.SemaphoreType.DMA((2,)),
            pltpu.SemaphoreType.DMA((2,)),
            pltpu.VMEM((2, d, f2), F32),
            pltpu.VMEM((2, f, d), F32),
            pltpu.SemaphoreType.DMA((2, 2)),
            pltpu.VMEM((d, f), BF16),
            pltpu.VMEM((d, f), BF16),
            pltpu.VMEM((f, d), BF16),
        ],
    )
    return pl.pallas_call(
        _moe_kernel,
        grid_spec=grid_spec,
        out_shape=jax.ShapeDtypeStruct(xloc.shape, jnp.uint32),
        input_output_aliases={len(prefetch): 0},
        compiler_params=_params(("arbitrary",)),
        name="moe",
    )(*prefetch, xloc, w_up, bg, bl, w_down, bd)


def _combine_ple_kernel(yloc_ref, slot_ref, gate_ref, h1_ref, p_ref, wpg_ref, wpp_ref, gple_ref,
                        gfin_ref, o_ref):
    local = lax.broadcasted_iota(jnp.int32, (LOCAL_ROWS, 1), 0)
    for s in range(h1_ref.shape[0] // ROUTE_TILE):
        tok = slice(s * ROUTE_TILE, (s + 1) * ROUTE_TILE)
        loc = slice(s * LOCAL_ROWS, (s + 1) * LOCAL_ROWS)
        rows = slot_ref[:, tok]
        gates = gate_ref[:, tok]
        weights = jnp.zeros((LOCAL_ROWS, ROUTE_TILE), F32)
        for k in range(TOP_K):
            weights = jnp.where(local == rows[k:k + 1, :], gates[k:k + 1, :], weights)
        weights = weights.astype(BF16)
        y_lo, y_hi = _unpack_bf16_pairs(yloc_ref[loc, :])
        moe = jnp.concatenate([_dot_tn(weights, y_lo), _dot_tn(weights, y_hi)], axis=1)
        h2 = h1_ref[tok, :] + moe
        gate = jax.nn.sigmoid(_dot(h2.astype(BF16), wpg_ref[...]))
        proj = _dot(p_ref[tok, :].astype(BF16), wpp_ref[...])
        h3 = h2 + _rms(gate * proj, gple_ref[...])
        o_ref[tok, :] = _rms(h3, gfin_ref[...])


def _combine_ple(yloc, slot_rows, gates, h1, p2, wpg, wpp, gple, gfin):
    n, d = h1.shape
    pd = p2.shape[1]
    per = ROUTE_STEP_TILES
    tm = per * ROUTE_TILE
    row = lambda i: (i, 0)
    col = lambda i: (0, i)
    fixed = lambda i: (0, 0)
    return pl.pallas_call(
        _combine_ple_kernel,
        grid=(n // tm,),
        in_specs=[
            pl.BlockSpec((per * LOCAL_ROWS, d // 2), row),
            pl.BlockSpec((SUBLANES, tm), col),
            pl.BlockSpec((SUBLANES, tm), col),
            pl.BlockSpec((tm, d), row),
            pl.BlockSpec((tm, pd), row),
            RESIDENT,
            RESIDENT,
            RESIDENT,
            RESIDENT,
        ],
        out_specs=pl.BlockSpec((tm, d), row),
        out_shape=jax.ShapeDtypeStruct((n, d), F32),
        compiler_params=_params(("parallel",)),
        name="combine_ple",
    )(yloc, slot_rows, gates, h1, p2, wpg, wpp, gple, gfin)


def _routing_tables(counts):
    tiles = counts.shape[0]
    local_chunks = LOCAL_ROWS // CHUNK_ROWS
    assert 2 * BLOCK_CHUNKS < tiles * SPARE_CHUNKS
    run = (counts + CHUNK_ROWS - 1) // CHUNK_ROWS
    local_start = jnp.cumsum(run, axis=1) - run
    run_end = jnp.cumsum(run, axis=0)
    run_start = run_end - run
    total = run_end[-1]
    padded = (total + BLOCK_CHUNKS - 1) // BLOCK_CHUNKS * BLOCK_CHUNKS
    pend = jnp.cumsum(padded)
    poff = pend - padded

    slots = tiles * ROUTE_TILE * TOP_K
    max_chunks = (slots + tiles * N_EXPERTS * (CHUNK_ROWS - 1)) // CHUNK_ROWS + N_EXPERTS * (BLOCK_CHUNKS - 1)
    n_chunks = -(-max_chunks // BLOCK_CHUNKS) * BLOCK_CHUNKS
    lookup = lambda onehot, table: jnp.dot(onehot, table.astype(F32), precision=lax.Precision.HIGHEST)
    q = jnp.arange(n_chunks, dtype=jnp.int32)
    e = jnp.minimum(jnp.sum(q[:, None] >= pend[None, :], axis=1), N_EXPERTS - 1).astype(jnp.int32)
    e_hot = (e[:, None] == jnp.arange(N_EXPERTS)[None, :]).astype(F32)
    per_expert = lookup(e_hot, jnp.stack([poff, total], axis=1)).astype(jnp.int32)
    j = q - per_expert[:, 0]
    valid = j < per_expert[:, 1]
    ends = lookup(e_hot, run_end.T).astype(jnp.int32)
    tile = jnp.minimum(jnp.sum(ends <= j[:, None], axis=1), tiles - 1).astype(jnp.int32)
    tile_hot = (tile[:, None] == jnp.arange(tiles)[None, :]).astype(F32)
    shift = jnp.sum(lookup(tile_hot, local_start - run_start) * e_hot, axis=1).astype(jnp.int32)
    src = jnp.where(valid, tile * local_chunks + shift + j, tiles * local_chunks - 1)
    spare = (q // BLOCK_CHUNKS % 2) * BLOCK_CHUNKS + q % BLOCK_CHUNKS
    spare_chunk = (spare // SPARE_CHUNKS) * local_chunks + (local_chunks - SPARE_CHUNKS) + spare % SPARE_CHUNKS
    dst = jnp.where(valid, tile * local_chunks + shift + j, spare_chunk)
    dst = jnp.concatenate([spare_chunk[:PRIME_BLOCKS * BLOCK_CHUNKS], dst])
    has_rows = total > 0
    ordinal = jnp.cumsum(has_rows.astype(jnp.int32)) - 1
    used_experts = jnp.argsort(jnp.logical_not(has_rows), stable=True).astype(jnp.int32)
    block_e = e[::BLOCK_CHUNKS]
    block_ord = jnp.sum(jnp.where(block_e[:, None] == jnp.arange(N_EXPERTS)[None, :], ordinal[None, :], 0), axis=1)
    n_used = jnp.stack([pend[-1] // BLOCK_CHUNKS, jnp.sum(has_rows)]).astype(jnp.int32)
    return (block_e, n_used, block_ord.astype(jnp.int32), used_experts,
            src.astype(jnp.int32), dst.astype(jnp.int32))


def kernel(x, p, positions, rel_bias_table, g_mix_norm, w_in, b_in, ret_norm_g, att_sinks, w_out,
           g_moe_norm, w_router, b_router, w_up, b_up, w_down, b_down, w_ple_gate, w_ple_proj,
           g_ple_norm, g_final):
    b, s, d = x.shape
    depth = w_in.shape[0]
    assert depth == 1, "single-layer stack only"
    n = b * s
    x2 = x.reshape(n, d)

    half = RET_DK // 2
    inv = ROPE_BASE ** (-jnp.arange(half, dtype=F32) / half)
    ang = (positions.astype(F32)[..., None] * inv).reshape(n, half)

    qr, kr, vr, gr, qa, kva = _in_proj(
        x2, g_mix_norm[0][None], w_in[0].astype(BF16), b_in[0][None], jnp.cos(ang), jnp.sin(ang))

    shape3 = lambda t: t.reshape(b, s, t.shape[-1])
    ret = _retention(shape3(qr), shape3(kr), shape3(vr), shape3(gr), ret_norm_g[0])
    att = _attention(shape3(qa), shape3(kva), att_sinks[0], rel_bias_table)

    h1, xloc, slot_rows, gates, counts = _out_router(
        ret.reshape(n, RET_WIDTH), att.reshape(n, ATT_WIDTH), x2, w_out[0].astype(BF16),
        g_moe_norm[0][None], w_router[0].T.astype(BF16), b_router[0][:, None])

    tables = _routing_tables(counts[:, 0, :].astype(jnp.int32))

    yloc = _moe(*tables, xloc, w_up[0],
                b_up[0][:, None, 0::2], b_up[0][:, None, 1::2],
                w_down[0], b_down[0].reshape(N_EXPERTS, 1, d))

    out = _combine_ple(yloc, slot_rows, gates, h1, p[0].reshape(n, -1),
                       w_ple_gate[0].astype(BF16), w_ple_proj[0].astype(BF16),
                       g_ple_norm[0][None], g_final[None])
    return out.reshape(b, s, d)
```

```python
import math

import jax
import jax.numpy as jnp
from jax import lax
from jax.experimental import pallas as pl
from jax.experimental.pallas import tpu as pltpu

RET_HEADS = 4
RET_DK = 128
RET_DV = 128
RET_CHUNK = 128
ROPE_BASE = 10000.0
ATT_Q_HEADS = 8
ATT_KV_HEADS = 2
ATT_GROUP = ATT_Q_HEADS // ATT_KV_HEADS
ATT_HEAD_DIM = 64
WINDOW = 128
ATT_BLOCK = 128
NUM_BUCKETS = 32
MAX_DISTANCE = 128
N_EXPERTS = 32
TOP_K = 4
SWIGLU_ALPHA = 1.702
SWIGLU_LIMIT = 7.0
EPS = 1e-5

RET_WIDTH = RET_HEADS * RET_DV
ATT_WIDTH = ATT_Q_HEADS * ATT_HEAD_DIM
KV_WIDTH = ATT_KV_HEADS * ATT_HEAD_DIM

VMEM_LIMIT_BYTES = 48 * 1024 * 1024

TOKEN_TILE = 1024
SUBLANES = 8
ROUTE_TILE = 256
ROUTE_STEP_TILES = 4
CHUNK_ROWS = SUBLANES
LOCAL_ROWS = ROUTE_TILE * TOP_K + N_EXPERTS * CHUNK_ROWS
SPARE_CHUNKS = (LOCAL_ROWS - ROUTE_TILE * TOP_K - N_EXPERTS * (CHUNK_ROWS - 1)) // CHUNK_ROWS
MOE_BLOCK = 256
BLOCK_CHUNKS = MOE_BLOCK // CHUNK_ROWS
PRIME_BLOCKS = 2
ATT_STEP_BLOCKS = 8
PERM_CHUNK = 256

F32 = jnp.float32
BF16 = jnp.bfloat16
NEG = float(jnp.finfo(jnp.float32).min)


def _rms(x, g):
    return x * lax.rsqrt(jnp.mean(x * x, axis=-1, keepdims=True) + EPS) * g


def _dot(a, b):
    return jnp.dot(a, b, preferred_element_type=F32)


def _dot_nt(a, b):
    return lax.dot_general(a, b, (((1,), (1,)), ((), ())), preferred_element_type=F32)


def _dot_tn(a, b):
    return lax.dot_general(a, b, (((0,), (0,)), ((), ())), preferred_element_type=F32)


def _params(semantics):
    return pltpu.CompilerParams(dimension_semantics=semantics, vmem_limit_bytes=VMEM_LIMIT_BYTES)


RESIDENT = pl.BlockSpec(memory_space=pltpu.VMEM)


def _in_proj_kernel(x_ref, g_ref, w_ref, b_ref, cos_ref, sin_ref,
                    qr_ref, kr_ref, vr_ref, gr_ref, qa_ref, kva_ref):
    hn = _rms(x_ref[...], g_ref[...]).astype(BF16)
    cos = jnp.concatenate([cos_ref[...], cos_ref[...]], axis=1)
    sin = jnp.concatenate([-sin_ref[...], sin_ref[...]], axis=1)

    def proj(lo, hi):
        return _dot(hn, w_ref[:, lo:hi]) + b_ref[:, lo:hi]

    def rope(t):
        heads = []
        for h in range(RET_HEADS):
            th = t[:, h * RET_DK:(h + 1) * RET_DK]
            heads.append(th * cos + pltpu.roll(th, RET_DK // 2, 1) * sin)
        return jnp.concatenate(heads, axis=1)

    o = 0
    qr_ref[...] = rope(proj(o, o + RET_WIDTH)).astype(BF16)
    o += RET_WIDTH
    kr_ref[...] = (rope(proj(o, o + RET_WIDTH)) * (RET_DK ** -0.5)).astype(BF16)
    o += RET_WIDTH
    vr_ref[...] = proj(o, o + RET_WIDTH).astype(BF16)
    o += RET_WIDTH
    gr_ref[...] = proj(o, o + RET_WIDTH).astype(BF16)
    o += RET_WIDTH
    qa_ref[...] = (proj(o, o + ATT_WIDTH) * (ATT_HEAD_DIM ** -0.5)).astype(BF16)
    o += ATT_WIDTH
    low = lax.broadcasted_iota(jnp.int32, (1, KV_WIDTH), 1) < ATT_HEAD_DIM
    slabs = []
    for part in range(2):
        t = proj(o + part * KV_WIDTH, o + (part + 1) * KV_WIDTH)
        swapped = pltpu.roll(t, ATT_HEAD_DIM, 1)
        slabs += [jnp.where(low, t, swapped), jnp.where(low, swapped, t)]
    kva_ref[...] = jnp.concatenate(slabs, axis=1).astype(BF16)


def _in_proj(x2, g, w_in, b_in, cosf, sinf):
    n, d = x2.shape
    width = w_in.shape[1]
    tm = TOKEN_TILE
    row = lambda i: (i, 0)
    outs = [RET_WIDTH, RET_WIDTH, RET_WIDTH, RET_WIDTH, ATT_WIDTH, 4 * KV_WIDTH]
    return pl.pallas_call(
        _in_proj_kernel,
        grid=(n // tm,),
        in_specs=[
            pl.BlockSpec((tm, d), row),
            RESIDENT,
            RESIDENT,
            RESIDENT,
            pl.BlockSpec((tm, RET_DK // 2), row),
            pl.BlockSpec((tm, RET_DK // 2), row),
        ],
        out_specs=[pl.BlockSpec((tm, w), row) for w in outs],
        out_shape=[jax.ShapeDtypeStruct((n, w), BF16) for w in outs],
        compiler_params=_params(("parallel",)),
        name="in_proj",
    )(x2, g, w_in, b_in, cosf, sinf)


def _retention_kernel(q_ref, k_ref, v_ref, g_ref, dmask_ref, qdec_ref, kdec_ref, cdec_ref,
                      gn_ref, o_ref, state_ref):
    batch = q_ref.shape[0]

    @pl.when(pl.program_id(0) == 0)
    def _():
        state_ref[...] = jnp.zeros_like(state_ref)

    def per_batch(b, carry):
        for h in range(RET_HEADS):
            sl = slice(h * RET_DK, (h + 1) * RET_DK)
            q = q_ref[b, :, sl]
            k = k_ref[b, :, sl]
            v = v_ref[b, :, sl]
            scores = _dot_nt(q, k) * dmask_ref[h]
            inner = _dot(scores.astype(BF16), v)
            state = state_ref[b * RET_HEADS + h]
            cross = _dot(q, state.astype(BF16)) * qdec_ref[h]
            kd = (k.astype(F32) * kdec_ref[h]).astype(BF16)
            state_ref[b * RET_HEADS + h] = state * cdec_ref[h] + _dot_tn(kd, v)
            o = _rms(inner + cross, gn_ref[h])
            gate = g_ref[b, :, sl].astype(F32)
            o_ref[b, :, sl] = (gate * jax.nn.sigmoid(gate) * o).astype(BF16)
        return carry

    lax.fori_loop(0, batch, per_batch, 0, unroll=True)


def _retention(q, k, v, g, gn):
    b, s, w = q.shape
    c = RET_CHUNK
    log_g = jnp.log(1.0 - 2.0 ** (-5.0 - jnp.arange(RET_HEADS, dtype=F32)))
    ci = jnp.arange(c)
    diff = (ci[:, None] - ci[None, :]).astype(F32)
    dmask = jnp.where(diff >= 0, jnp.exp(jnp.maximum(diff, 0.0)[None] * log_g[:, None, None]), 0.0)
    cf = ci.astype(F32)
    kdec = jnp.exp((c - 1 - cf)[None, :] * log_g[:, None])
    qdec = jnp.exp((cf + 1)[None, :] * log_g[:, None])
    cdec = jnp.exp(c * log_g)
    kdec = jnp.broadcast_to(kdec[:, :, None], (RET_HEADS, c, RET_DK))
    qdec = jnp.broadcast_to(qdec[:, :, None], (RET_HEADS, c, RET_DV))
    cdec = jnp.broadcast_to(cdec[:, None, None], (RET_HEADS, RET_DK, RET_DV))
    gn3 = gn.reshape(RET_HEADS, 1, RET_DV)

    seq = lambda i: (0, i, 0)
    return pl.pallas_call(
        _retention_kernel,
        grid=(s // c,),
        in_specs=[pl.BlockSpec((b, c, w), seq)] * 4 + [RESIDENT] * 5,
        out_specs=pl.BlockSpec((b, c, w), seq),
        out_shape=jax.ShapeDtypeStruct((b, s, w), BF16),
        scratch_shapes=[pltpu.VMEM((b * RET_HEADS, RET_DK, RET_DV), F32)],
        compiler_params=_params(("arbitrary",)),
        name="retention",
    )(q, k, v, g, dmask, qdec, kdec, cdec, gn3)


def _attention_kernel(sinks_ref, q_ref, kvp_ref, kvc_ref, bias_ref, o_ref):
    blk = ATT_BLOCK
    slab = 2 * ATT_HEAD_DIM
    r = lax.broadcasted_iota(jnp.int32, (blk, blk), 0)
    c = lax.broadcasted_iota(jnp.int32, (blk, blk), 1)
    own = c <= r
    first_ok = jnp.logical_or(own, pl.program_id(1) > 0)
    low = lax.broadcasted_iota(jnp.int32, (1, slab), 1) < ATT_HEAD_DIM
    zero = jnp.zeros((), BF16)
    for i in range(q_ref.shape[1] // blk):
        rows = slice(i * blk, (i + 1) * blk)
        prev = (lambda cs: kvp_ref[0, :, cs]) if i == 0 else (
            lambda cs, rs=slice((i - 1) * blk, i * blk): kvc_ref[0, rs, cs])
        slabs = []
        for a in range(ATT_Q_HEADS // 2):
            g = (2 * a) // ATT_GROUP
            ks = slice(g * slab, (g + 1) * slab)
            vs = slice((ATT_KV_HEADS + g) * slab, (ATT_KV_HEADS + g + 1) * slab)
            k = jnp.concatenate([prev(ks), kvc_ref[0, rows, ks]], axis=0)
            v_prev, v_own = prev(vs), kvc_ref[0, rows, vs]
            q = q_ref[0, rows, a * slab:(a + 1) * slab]
            halves = []
            for part in range(2):
                h = 2 * a + part
                qz = jnp.where(low if part == 0 else jnp.logical_not(low), q, zero)
                both = _dot_nt(qz, k)
                s = jnp.where(own, both[:, blk:], both[:, :blk]) + bias_ref[h]
                if i == 0:
                    s = jnp.where(first_ok, s, NEG)
                sink = sinks_ref[h]
                m = jnp.maximum(jnp.max(s, axis=1, keepdims=True), sink)
                p = jnp.exp(s - m)
                denom = jnp.sum(p, axis=1, keepdims=True) + jnp.exp(sink - m)
                p_own = jnp.where(own, p, 0.0).astype(BF16)
                p_prev = jnp.where(own, 0.0, p).astype(BF16)
                halves.append((_dot(p_own, v_own) + _dot(p_prev, v_prev)) / denom)
            slabs.append(jnp.where(low, halves[0], halves[1]))
        o_ref[0, rows, :] = jnp.concatenate(slabs, axis=1).astype(BF16)


def _t5_bucket(n):
    max_exact = NUM_BUCKETS // 2
    nf = jnp.maximum(n, 1).astype(F32)
    large = max_exact + (jnp.log(nf / max_exact) / math.log(MAX_DISTANCE / max_exact)
                         * (NUM_BUCKETS - max_exact)).astype(jnp.int32)
    large = jnp.minimum(large, NUM_BUCKETS - 1)
    return jnp.where(n < max_exact, n, large)


def _attention(q, kv, sinks, bias_table):
    b, s, _ = q.shape
    blk = ATT_BLOCK
    assert WINDOW == blk
    qi = jnp.arange(blk)[:, None]
    ki = jnp.arange(blk)[None, :]
    dist = jnp.where(ki <= qi, qi - ki, qi + blk - ki)
    onehot = jax.nn.one_hot(_t5_bucket(dist), NUM_BUCKETS, dtype=F32)
    bias = jnp.einsum("qkn,nh->hqk", onehot, bias_table.astype(F32), precision=lax.Precision.HIGHEST)

    sub = ATT_STEP_BLOCKS
    kvw = kv.shape[-1]
    return pl.pallas_call(
        _attention_kernel,
        grid=(b, s // (sub * blk)),
        in_specs=[
            pl.BlockSpec(memory_space=pltpu.SMEM),
            pl.BlockSpec((1, sub * blk, ATT_WIDTH), lambda i, j: (i, j, 0)),
            pl.BlockSpec((1, blk, kvw), lambda i, j: (i, jnp.maximum(sub * j - 1, 0), 0)),
            pl.BlockSpec((1, sub * blk, kvw), lambda i, j: (i, j, 0)),
            RESIDENT,
        ],
        out_specs=pl.BlockSpec((1, sub * blk, ATT_WIDTH), lambda i, j: (i, j, 0)),
        out_shape=jax.ShapeDtypeStruct((b, s, ATT_WIDTH), BF16),
        compiler_params=_params(("parallel", "arbitrary")),
        name="attention",
    )(sinks, q, kv, kv, bias)


def _pack_bf16_pairs(lo, hi):
    return pltpu.bitcast(pltpu.pack_elementwise([lo, hi], packed_dtype=BF16), jnp.uint32)


def _unpack_bf16_pairs(packed):
    lo = pltpu.bitcast(packed << 16, F32).astype(BF16)
    hi = pltpu.bitcast(packed & jnp.uint32(0xFFFF0000), F32).astype(BF16)
    return lo, hi


def _stack_rows(rows, dtype):
    t = rows[0].shape[1]
    sub = lax.broadcasted_iota(jnp.int32, (SUBLANES, t), 0)
    out = jnp.zeros((SUBLANES, t), dtype)
    for i, r in enumerate(rows):
        out = jnp.where(sub == i, r.astype(dtype), out)
    return out


def _route_tile(xb, wrt_ref, brt_ref):
    t = xb.shape[0]
    logits = _dot_nt(wrt_ref[...], xb) + brt_ref[...]
    expert = lax.broadcasted_iota(jnp.int32, logits.shape, 0)
    vals = logits
    top_v, sels = [], []
    for _ in range(TOP_K):
        m = jnp.max(vals, axis=0, keepdims=True)
        idx = jnp.min(jnp.where(vals == m, expert, N_EXPERTS), axis=0, keepdims=True)
        sel = expert == idx
        vals = jnp.where(sel, -jnp.inf, vals)
        top_v.append(m)
        sels.append(sel)
    exps = [jnp.exp(v - top_v[0]) for v in top_v]
    denom = exps[0] + exps[1] + exps[2] + exps[3]
    gates = [e / denom for e in exps]

    chosen = jnp.zeros(logits.shape, F32)
    for sel in sels:
        chosen = chosen + sel.astype(F32)
    chosen_b = chosen.astype(BF16)
    r = lax.broadcasted_iota(jnp.int32, (t, t), 0)
    c = lax.broadcasted_iota(jnp.int32, (t, t), 1)
    earlier = jnp.where(r < c, 1.0, 0.0).astype(BF16)
    before = _dot(chosen_b, earlier)
    counts = jnp.sum(chosen, axis=1, keepdims=True)
    chunks = jnp.floor((counts + (CHUNK_ROWS - 1)) * (1.0 / CHUNK_ROWS))
    er = lax.broadcasted_iota(jnp.int32, (N_EXPERTS, N_EXPERTS), 0)
    ec = lax.broadcasted_iota(jnp.int32, (N_EXPERTS, N_EXPERTS), 1)
    lower = jnp.where(ec < er, 1.0, 0.0).astype(BF16)
    start = _dot(lower, jnp.broadcast_to(chunks, logits.shape).astype(BF16)) * float(CHUNK_ROWS)
    rows = [jnp.sum(jnp.where(sel, before + start, 0.0), axis=0, keepdims=True).astype(jnp.int32)
            for sel in sels]
    counts_rows = _dot_nt(jnp.ones((SUBLANES, t), BF16), chosen_b)

    local = lax.broadcasted_iota(jnp.int32, (LOCAL_ROWS, 1), 0)
    select = jnp.zeros((LOCAL_ROWS, t), F32)
    for row in rows:
        select = jnp.where(local == row, 1.0, select)
    return rows, gates, counts_rows, select.astype(BF16)


def _out_router_kernel(ret_ref, att_ref, x_ref, wo_ref, g_ref, wrt_ref, brt_ref,
                       h1_ref, xloc_ref, slot_ref, gate_ref, cnt_ref):
    half = x_ref.shape[1] // 2
    for s in range(x_ref.shape[0] // ROUTE_TILE):
        tok = slice(s * ROUTE_TILE, (s + 1) * ROUTE_TILE)
        loc = slice(s * LOCAL_ROWS, (s + 1) * LOCAL_ROWS)
        h1 = (x_ref[tok, :] + _dot(ret_ref[tok, :], wo_ref[:RET_WIDTH, :])
              + _dot(att_ref[tok, :], wo_ref[RET_WIDTH:, :]))
        h1_ref[tok, :] = h1
        xb = _rms(h1, g_ref[...]).astype(BF16)
        rows, gates, counts_rows, select = _route_tile(xb, wrt_ref, brt_ref)
        xloc = _dot(select, xb)
        xloc_ref[loc, :] = _pack_bf16_pairs(xloc[:, :half], xloc[:, half:])
        slot_ref[:, tok] = _stack_rows(rows, jnp.int32)
        gate_ref[:, tok] = _stack_rows(gates, F32)
        cnt_ref[s] = counts_rows


def _out_router(ret, att, x2, w_out, g, w_router_t, b_router_t):
    n, d = x2.shape
    per = ROUTE_STEP_TILES
    tm = per * ROUTE_TILE
    tiles = n // ROUTE_TILE
    row = lambda i: (i, 0)
    col = lambda i: (0, i)
    fixed = lambda i: (0, 0)
    return pl.pallas_call(
        _out_router_kernel,
        grid=(n // tm,),
        in_specs=[
            pl.BlockSpec((tm, RET_WIDTH), row),
            pl.BlockSpec((tm, ATT_WIDTH), row),
            pl.BlockSpec((tm, d), row),
            RESIDENT,
            RESIDENT,
            RESIDENT,
            RESIDENT,
        ],
        out_specs=[
            pl.BlockSpec((tm, d), row),
            pl.BlockSpec((per * LOCAL_ROWS, d // 2), row),
            pl.BlockSpec((SUBLANES, tm), col),
            pl.BlockSpec((SUBLANES, tm), col),
            pl.BlockSpec((per, SUBLANES, N_EXPERTS), lambda i: (i, 0, 0)),
        ],
        out_shape=[
            jax.ShapeDtypeStruct((n, d), F32),
            jax.ShapeDtypeStruct((tiles * LOCAL_ROWS, d // 2), jnp.uint32),
            jax.ShapeDtypeStruct((SUBLANES, n), jnp.int32),
            jax.ShapeDtypeStruct((SUBLANES, n), F32),
            jax.ShapeDtypeStruct((tiles, SUBLANES, N_EXPERTS), F32),
        ],
        compiler_params=_params(("parallel",)),
        name="out_router",
    )(ret, att, x2, w_out, g, w_router_t, b_router_t)


def _convert_expert_weights(wu_ref, wdn_ref, wg_ref, wl_ref, wd_ref):
    ch = PERM_CHUNK
    hc = ch // 2
    r = lax.broadcasted_iota(jnp.int32, (ch, ch), 0)
    c = lax.broadcasted_iota(jnp.int32, (ch, ch), 1)
    perm = jnp.where(r == jnp.where(c < hc, 2 * c, 2 * (c - hc) + 1), 1.0, 0.0).astype(BF16)
    for j in range(wu_ref.shape[1] // ch):
        both = _dot(wu_ref[:, j * ch:(j + 1) * ch].astype(BF16), perm)
        wg_ref[:, j * hc:(j + 1) * hc] = both[:, :hc].astype(BF16)
        wl_ref[:, j * hc:(j + 1) * hc] = both[:, hc:].astype(BF16)
    wd_ref[...] = wdn_ref[...].astype(BF16)


def _moe_kernel(be_ref, nused_ref, ord_ref, ue_ref, src_ref, dst_ref,
                xin_hbm, wu_hbm, bg_ref, bl_ref, wdn_hbm, bd_ref,
                loc_hbm, xbuf, ybuf, gsem, ssem, wu_buf, wdn_buf, wsem, wg_ref, wl_ref, wd_ref):
    del xin_hbm
    i = pl.program_id(0)
    n_used = nused_ref[0]
    last = pl.num_programs(0) - 1
    slot = lax.rem(i, 2)
    bm, half = xbuf.shape[1], xbuf.shape[2]

    def chunk(ref, c):
        first = c * CHUNK_ROWS if isinstance(c, int) else pl.multiple_of(c * CHUNK_ROWS, CHUNK_ROWS)
        return ref.at[pl.ds(first, CHUNK_ROWS), :]

    def start_gather(step, sl):
        for j in range(BLOCK_CHUNKS):
            pltpu.make_async_copy(chunk(loc_hbm, src_ref[step * BLOCK_CHUNKS + j]),
                                  chunk(xbuf.at[sl], j), gsem.at[sl]).start()

    def start_writeback(step, sl):
        for j in range(BLOCK_CHUNKS):
            dst = dst_ref[(step + PRIME_BLOCKS) * BLOCK_CHUNKS + j]
            pltpu.make_async_copy(chunk(ybuf.at[sl], j), chunk(loc_hbm, dst), ssem.at[sl]).start()

    def wait_gather(sl):
        pltpu.make_async_copy(loc_hbm.at[pl.ds(0, bm), :], xbuf.at[sl], gsem.at[sl]).wait()

    def wait_writeback(sl):
        pltpu.make_async_copy(ybuf.at[sl], loc_hbm.at[pl.ds(0, bm), :], ssem.at[sl]).wait()

    def weight_fetch(expert, buf):
        return (pltpu.make_async_copy(wu_hbm.at[expert], wu_buf.at[buf], wsem.at[buf, 0]),
                pltpu.make_async_copy(wdn_hbm.at[expert], wdn_buf.at[buf], wsem.at[buf, 1]))

    @pl.when(i == 0)
    def _():
        ybuf[...] = jnp.zeros_like(ybuf)
        start_gather(0, 0)
        start_writeback(-2, 0)
        for copy in weight_fetch(ue_ref[0], 0):
            copy.start()

    new_expert = jnp.logical_or(i == 0, be_ref[i] != be_ref[jnp.maximum(i - 1, 0)])

    @pl.when(jnp.logical_and(i < n_used, new_expert))
    def _():
        k = ord_ref[i]
        buf = lax.rem(k, 2)
        for copy in weight_fetch(be_ref[i], buf):
            copy.wait()

        @pl.when(k + 1 < nused_ref[1])
        def _():
            for copy in weight_fetch(ue_ref[k + 1], 1 - buf):
                copy.start()

        _convert_expert_weights(wu_buf.at[buf], wdn_buf.at[buf], wg_ref, wl_ref, wd_ref)

    @pl.when(i < n_used)
    def _():
        wait_gather(slot)
        start_gather(jnp.minimum(i + 1, last), 1 - slot)
        lo, hi = _unpack_bf16_pairs(xbuf[slot])
        start_writeback(i - 1, 1 - slot)

        expert = be_ref[i]

        def up(w_ref, b_ref):
            return _dot(lo, w_ref[:half, :]) + _dot(hi, w_ref[half:, :]) + b_ref[expert]

        glu = jnp.minimum(up(wg_ref, bg_ref), SWIGLU_LIMIT)
        lin = jnp.clip(up(wl_ref, bl_ref), -SWIGLU_LIMIT, SWIGLU_LIMIT)
        act = glu * jax.nn.sigmoid(SWIGLU_ALPHA * glu) * (lin + 1.0)
        y = _dot(act.astype(BF16), wd_ref[...]) + bd_ref[expert]

        wait_writeback(slot)
        ybuf[slot] = _pack_bf16_pairs(y[:, :half], y[:, half:])

        @pl.when(i == n_used - 1)
        def _():
            start_writeback(i, slot)
            wait_writeback(1 - slot)
            wait_writeback(slot)
            wait_gather(1 - slot)


def _moe(block_e, n_used, block_ord, used_experts, chunk_src, chunk_dst, xloc, w_up, bg, bl, w_down, bd):
    half = xloc.shape[1]
    e, d, f2 = w_up.shape
    f = f2 // 2
    bm = MOE_BLOCK
    n_blocks = chunk_src.shape[0] // BLOCK_CHUNKS
    assert chunk_dst.shape[0] == (n_blocks + PRIME_BLOCKS) * BLOCK_CHUNKS
    prefetch = (block_e, n_used, block_ord, used_experts, chunk_src, chunk_dst)
    grid_spec = pltpu.PrefetchScalarGridSpec(
        num_scalar_prefetch=len(prefetch),
        grid=(n_blocks,),
        in_specs=[
            pl.BlockSpec(memory_space=pl.ANY),
            pl.BlockSpec(memory_space=pl.ANY),
            RESIDENT,
            RESIDENT,
            pl.BlockSpec(memory_space=pl.ANY),
            RESIDENT,
        ],
        out_specs=pl.BlockSpec(memory_space=pl.ANY),
        scratch_shapes=[
            pltpu.VMEM((2, bm, half), jnp.uint32),
            pltpu.VMEM((2, bm, half), jnp.uint32),
            pltpu.SemaphoreType.DMA((2,)),
            pltpu.SemaphoreType.DMA((2,)),
            pltpu.VMEM((2, d, f2), F32),
            pltpu.VMEM((2, f, d), F32),
            pltpu.SemaphoreType.DMA((2, 2)),
            pltpu.VMEM((d, f), BF16),
            pltpu.VMEM((d, f), BF16),
            pltpu.VMEM((f, d), BF16),
        ],
    )
    return pl.pallas_call(
        _moe_kernel,
        grid_spec=grid_spec,
        out_shape=jax.ShapeDtypeStruct(xloc.shape, jnp.uint32),
        input_output_aliases={len(prefetch): 0},
        compiler_params=_params(("arbitrary",)),
        name="moe",
    )(*prefetch, xloc, w_up, bg, bl, w_down, bd)


def _combine_ple_kernel(yloc_ref, slot_ref, gate_ref, h1_ref, p_ref, wpg_ref, wpp_ref, gple_ref,
                        gfin_ref, o_ref):
    local = lax.broadcasted_iota(jnp.int32, (LOCAL_ROWS, 1), 0)
    for s in range(h1_ref.shape[0] // ROUTE_TILE):
        tok = slice(s * ROUTE_TILE, (s + 1) * ROUTE_TILE)
        loc = slice(s * LOCAL_ROWS, (s + 1) * LOCAL_ROWS)
        rows = slot_ref[:, tok]
        gates = gate_ref[:, tok]
        weights = jnp.zeros((LOCAL_ROWS, ROUTE_TILE), F32)
        for k in range(TOP_K):
            weights = jnp.where(local == rows[k:k + 1, :], gates[k:k + 1, :], weights)
        weights = weights.astype(BF16)
        y_lo, y_hi = _unpack_bf16_pairs(yloc_ref[loc, :])
        moe = jnp.concatenate([_dot_tn(weights, y_lo), _dot_tn(weights, y_hi)], axis=1)
        h2 = h1_ref[tok, :] + moe
        gate = jax.nn.sigmoid(_dot(h2.astype(BF16), wpg_ref[...]))
        proj = _dot(p_ref[tok, :].astype(BF16), wpp_ref[...])
        h3 = h2 + _rms(gate * proj, gple_ref[...])
        o_ref[tok, :] = _rms(h3, gfin_ref[...])


def _combine_ple(yloc, slot_rows, gates, h1, p2, wpg, wpp, gple, gfin):
    n, d = h1.shape
    pd = p2.shape[1]
    per = ROUTE_STEP_TILES
    tm = per * ROUTE_TILE
    row = lambda i: (i, 0)
    col = lambda i: (0, i)
    fixed = lambda i: (0, 0)
    return pl.pallas_call(
        _combine_ple_kernel,
        grid=(n // tm,),
        in_specs=[
            pl.BlockSpec((per * LOCAL_ROWS, d // 2), row),
            pl.BlockSpec((SUBLANES, tm), col),
            pl.BlockSpec((SUBLANES, tm), col),
            pl.BlockSpec((tm, d), row),
            pl.BlockSpec((tm, pd), row),
            RESIDENT,
            RESIDENT,
            RESIDENT,
            RESIDENT,
        ],
        out_specs=pl.BlockSpec((tm, d), row),
        out_shape=jax.ShapeDtypeStruct((n, d), F32),
        compiler_params=_params(("parallel",)),
        name="combine_ple",
    )(yloc, slot_rows, gates, h1, p2, wpg, wpp, gple, gfin)


def _routing_tables(counts):
    tiles = counts.shape[0]
    local_chunks = LOCAL_ROWS // CHUNK_ROWS
    assert 2 * BLOCK_CHUNKS < tiles * SPARE_CHUNKS
    run = (counts + CHUNK_ROWS - 1) // CHUNK_ROWS
    local_start = jnp.cumsum(run, axis=1) - run
    run_end = jnp.cumsum(run, axis=0)
    run_start = run_end - run
    total = run_end[-1]
    padded = (total + BLOCK_CHUNKS - 1) // BLOCK_CHUNKS * BLOCK_CHUNKS
    pend = jnp.cumsum(padded)
    poff = pend - padded

    slots = tiles * ROUTE_TILE * TOP_K
    max_chunks = (slots + tiles * N_EXPERTS * (CHUNK_ROWS - 1)) // CHUNK_ROWS + N_EXPERTS * (BLOCK_CHUNKS - 1)
    n_chunks = -(-max_chunks // BLOCK_CHUNKS) * BLOCK_CHUNKS
    lookup = lambda onehot, table: jnp.dot(onehot, table.astype(F32), precision=lax.Precision.HIGHEST)
    q = jnp.arange(n_chunks, dtype=jnp.int32)
    e = jnp.minimum(jnp.sum(q[:, None] >= pend[None, :], axis=1), N_EXPERTS - 1).astype(jnp.int32)
    e_hot = (e[:, None] == jnp.arange(N_EXPERTS)[None, :]).astype(F32)
    per_expert = lookup(e_hot, jnp.stack([poff, total], axis=1)).astype(jnp.int32)
    j = q - per_expert[:, 0]
    valid = j < per_expert[:, 1]
    ends = lookup(e_hot, run_end.T).astype(jnp.int32)
    tile = jnp.minimum(jnp.sum(ends <= j[:, None], axis=1), tiles - 1).astype(jnp.int32)
    tile_hot = (tile[:, None] == jnp.arange(tiles)[None, :]).astype(F32)
    shift = jnp.sum(lookup(tile_hot, local_start - run_start) * e_hot, axis=1).astype(jnp.int32)
    src = jnp.where(valid, tile * local_chunks + shift + j, tiles * local_chunks - 1)
    spare = (q // BLOCK_CHUNKS % 2) * BLOCK_CHUNKS + q % BLOCK_CHUNKS
    spare_chunk = (spare // SPARE_CHUNKS) * local_chunks + (local_chunks - SPARE_CHUNKS) + spare % SPARE_CHUNKS
    dst = jnp.where(valid, tile * local_chunks + shift + j, spare_chunk)
    dst = jnp.concatenate([spare_chunk[:PRIME_BLOCKS * BLOCK_CHUNKS], dst])
    has_rows = total > 0
    ordinal = jnp.cumsum(has_rows.astype(jnp.int32)) - 1
    used_experts = jnp.argsort(jnp.logical_not(has_rows), stable=True).astype(jnp.int32)
    block_e = e[::BLOCK_CHUNKS]
    block_ord = jnp.sum(jnp.where(block_e[:, None] == jnp.arange(N_EXPERTS)[None, :], ordinal[None, :], 0), axis=1)
    n_used = jnp.stack([pend[-1] // BLOCK_CHUNKS, jnp.sum(has_rows)]).astype(jnp.int32)
    return (block_e, n_used, block_ord.astype(jnp.int32), used_experts,
            src.astype(jnp.int32), dst.astype(jnp.int32))


def kernel(x, p, positions, rel_bias_table, g_mix_norm, w_in, b_in, ret_norm_g, att_sinks, w_out,
           g_moe_norm, w_router, b_router, w_up, b_up, w_down, b_down, w_ple_gate, w_ple_proj,
           g_ple_norm, g_final):
    b, s, d = x.shape
    depth = w_in.shape[0]
    assert depth == 1, "single-layer stack only"
    n = b * s
    x2 = x.reshape(n, d)

    half = RET_DK // 2
    inv = ROPE_BASE ** (-jnp.arange(half, dtype=F32) / half)
    ang = (positions.astype(F32)[..., None] * inv).reshape(n, half)

    qr, kr, vr, gr, qa, kva = _in_proj(
        x2, g_mix_norm[0][None], w_in[0].astype(BF16), b_in[0][None], jnp.cos(ang), jnp.sin(ang))

    shape3 = lambda t: t.reshape(b, s, t.shape[-1])
    ret = _retention(shape3(qr), shape3(kr), shape3(vr), shape3(gr), ret_norm_g[0])
    att = _attention(shape3(qa), shape3(kva), att_sinks[0], rel_bias_table)

    h1, xloc, slot_rows, gates, counts = _out_router(
        ret.reshape(n, RET_WIDTH), att.reshape(n, ATT_WIDTH), x2, w_out[0].astype(BF16),
        g_moe_norm[0][None], w_router[0].T.astype(BF16), b_router[0][:, None])

    tables = _routing_tables(counts[:, 0, :].astype(jnp.int32))

    yloc = _moe(*tables, xloc, w_up[0],
                b_up[0][:, None, 0::2], b_up[0][:, None, 1::2],
                w_down[0], b_down[0].reshape(N_EXPERTS, 1, d))

    out = _combine_ple(yloc, slot_rows, gates, h1, p[0].reshape(n, -1),
                       w_ple_gate[0].astype(BF16), w_ple_proj[0].astype(BF16),
                       g_ple_norm[0][None], g_final[None])
    return out.reshape(b, s, d)
```

```python
import math

import jax
import jax.numpy as jnp
from jax import lax
from jax.experimental import pallas as pl
from jax.experimental.pallas import tpu as pltpu

RET_HEADS = 4
RET_DK = 128
RET_DV = 128
RET_CHUNK = 128
ROPE_BASE = 10000.0
ATT_Q_HEADS = 8
ATT_KV_HEADS = 2
ATT_GROUP = ATT_Q_HEADS // ATT_KV_HEADS
ATT_HEAD_DIM = 64
WINDOW = 128
ATT_BLOCK = 128
NUM_BUCKETS = 32
MAX_DISTANCE = 128
N_EXPERTS = 32
TOP_K = 4
SWIGLU_ALPHA = 1.702
SWIGLU_LIMIT = 7.0
EPS = 1e-5

RET_WIDTH = RET_HEADS * RET_DV
ATT_WIDTH = ATT_Q_HEADS * ATT_HEAD_DIM
KV_WIDTH = ATT_KV_HEADS * ATT_HEAD_DIM

VMEM_LIMIT_BYTES = 48 * 1024 * 1024

TOKEN_TILE = 1024
SUBLANES = 8
ROUTE_TILE = 256
ROUTE_STEP_TILES = 4
COMBINE_STEP_TILES = 2
CHUNK_ROWS = SUBLANES
LOCAL_ROWS = ROUTE_TILE * TOP_K + N_EXPERTS * CHUNK_ROWS
SPARE_CHUNKS = (LOCAL_ROWS - ROUTE_TILE * TOP_K - N_EXPERTS * (CHUNK_ROWS - 1)) // CHUNK_ROWS
MOE_BLOCK = 512
BLOCK_CHUNKS = MOE_BLOCK // CHUNK_ROWS
PRIME_BLOCKS = 2
ATT_STEP_BLOCKS = 8
PERM_CHUNK = 256

F32 = jnp.float32
BF16 = jnp.bfloat16
NEG = float(jnp.finfo(jnp.float32).min)


def _rms(x, g):
    return x * lax.rsqrt(jnp.mean(x * x, axis=-1, keepdims=True) + EPS) * g


def _dot(a, b):
    return jnp.dot(a, b, preferred_element_type=F32)


def _dot_nt(a, b):
    return lax.dot_general(a, b, (((1,), (1,)), ((), ())), preferred_element_type=F32)


def _dot_tn(a, b):
    return lax.dot_general(a, b, (((0,), (0,)), ((), ())), preferred_element_type=F32)


def _params(semantics):
    return pltpu.CompilerParams(dimension_semantics=semantics, vmem_limit_bytes=VMEM_LIMIT_BYTES)


RESIDENT = pl.BlockSpec(memory_space=pltpu.VMEM)


def _in_proj_kernel(x_ref, g_ref, w_ref, b_ref, cos_ref, sin_ref,
                    qr_ref, kr_ref, vr_ref, gr_ref, qa_ref, kva_ref):
    hn = _rms(x_ref[...], g_ref[...]).astype(BF16)
    cos = jnp.concatenate([cos_ref[...], cos_ref[...]], axis=1)
    sin = jnp.concatenate([-sin_ref[...], sin_ref[...]], axis=1)

    def proj(lo, hi):
        return _dot(hn, w_ref[:, lo:hi]) + b_ref[:, lo:hi]

    def rope(t):
        heads = []
        for h in range(RET_HEADS):
            th = t[:, h * RET_DK:(h + 1) * RET_DK]
            heads.append(th * cos + pltpu.roll(th, RET_DK // 2, 1) * sin)
        return jnp.concatenate(heads, axis=1)

    o = 0
    qr_ref[...] = rope(proj(o, o + RET_WIDTH)).astype(BF16)
    o += RET_WIDTH
    kr_ref[...] = (rope(proj(o, o + RET_WIDTH)) * (RET_DK ** -0.5)).astype(BF16)
    o += RET_WIDTH
    vr_ref[...] = proj(o, o + RET_WIDTH).astype(BF16)
    o += RET_WIDTH
    gr_ref[...] = proj(o, o + RET_WIDTH).astype(BF16)
    o += RET_WIDTH
    qa_ref[...] = (proj(o, o + ATT_WIDTH) * (ATT_HEAD_DIM ** -0.5)).astype(BF16)
    o += ATT_WIDTH
    low = lax.broadcasted_iota(jnp.int32, (1, KV_WIDTH), 1) < ATT_HEAD_DIM
    slabs = []
    for part in range(2):
        t = proj(o + part * KV_WIDTH, o + (part + 1) * KV_WIDTH)
        swapped = pltpu.roll(t, ATT_HEAD_DIM, 1)
        slabs += [jnp.where(low, t, swapped), jnp.where(low, swapped, t)]
    kva_ref[...] = jnp.concatenate(slabs, axis=1).astype(BF16)


def _in_proj(x2, g, w_in, b_in, cosf, sinf):
    n, d = x2.shape
    width = w_in.shape[1]
    tm = TOKEN_TILE
    row = lambda i: (i, 0)
    outs = [RET_WIDTH, RET_WIDTH, RET_WIDTH, RET_WIDTH, ATT_WIDTH, 4 * KV_WIDTH]
    return pl.pallas_call(
        _in_proj_kernel,
        grid=(n // tm,),
        in_specs=[
            pl.BlockSpec((tm, d), row),
            RESIDENT,
            RESIDENT,
            RESIDENT,
            pl.BlockSpec((tm, RET_DK // 2), row),
            pl.BlockSpec((tm, RET_DK // 2), row),
        ],
        out_specs=[pl.BlockSpec((tm, w), row) for w in outs],
        out_shape=[jax.ShapeDtypeStruct((n, w), BF16) for w in outs],
        compiler_params=_params(("parallel",)),
        name="in_proj",
    )(x2, g, w_in, b_in, cosf, sinf)


def _retention_kernel(q_ref, k_ref, v_ref, g_ref, dmask_ref, qdec_ref, kdec_ref, cdec_ref,
                      gn_ref, o_ref, state_ref):
    batch = q_ref.shape[0]

    @pl.when(pl.program_id(0) == 0)
    def _():
        state_ref[...] = jnp.zeros_like(state_ref)

    def per_batch(b, carry):
        for h in range(RET_HEADS):
            sl = slice(h * RET_DK, (h + 1) * RET_DK)
            q = q_ref[b, :, sl]
            k = k_ref[b, :, sl]
            v = v_ref[b, :, sl]
            scores = _dot_nt(q, k) * dmask_ref[h]
            inner = _dot(scores.astype(BF16), v)
            state = state_ref[b * RET_HEADS + h]
            cross = _dot(q, state.astype(BF16)) * qdec_ref[h]
            kd = (k.astype(F32) * kdec_ref[h]).astype(BF16)
            state_ref[b * RET_HEADS + h] = state * cdec_ref[h] + _dot_tn(kd, v)
            o = _rms(inner + cross, gn_ref[h])
            gate = g_ref[b, :, sl].astype(F32)
            o_ref[b, :, sl] = (gate * jax.nn.sigmoid(gate) * o).astype(BF16)
        return carry

    lax.fori_loop(0, batch, per_batch, 0, unroll=True)


def _retention(q, k, v, g, gn):
    b, s, w = q.shape
    c = RET_CHUNK
    log_g = jnp.log(1.0 - 2.0 ** (-5.0 - jnp.arange(RET_HEADS, dtype=F32)))
    ci = jnp.arange(c)
    diff = (ci[:, None] - ci[None, :]).astype(F32)
    dmask = jnp.where(diff >= 0, jnp.exp(jnp.maximum(diff, 0.0)[None] * log_g[:, None, None]), 0.0)
    cf = ci.astype(F32)
    kdec = jnp.exp((c - 1 - cf)[None, :] * log_g[:, None])
    qdec = jnp.exp((cf + 1)[None, :] * log_g[:, None])
    cdec = jnp.exp(c * log_g)
    kdec = jnp.broadcast_to(kdec[:, :, None], (RET_HEADS, c, RET_DK))
    qdec = jnp.broadcast_to(qdec[:, :, None], (RET_HEADS, c, RET_DV))
    cdec = jnp.broadcast_to(cdec[:, None, None], (RET_HEADS, RET_DK, RET_DV))
    gn3 = gn.reshape(RET_HEADS, 1, RET_DV)

    seq = lambda i: (0, i, 0)
    return pl.pallas_call(
        _retention_kernel,
        grid=(s // c,),
        in_specs=[pl.BlockSpec((b, c, w), seq)] * 4 + [RESIDENT] * 5,
        out_specs=pl.BlockSpec((b, c, w), seq),
        out_shape=jax.ShapeDtypeStruct((b, s, w), BF16),
        scratch_shapes=[pltpu.VMEM((b * RET_HEADS, RET_DK, RET_DV), F32)],
        compiler_params=_params(("arbitrary",)),
        name="retention",
    )(q, k, v, g, dmask, qdec, kdec, cdec, gn3)


def _attention_kernel(sinks_ref, q_ref, kvp_ref, kvc_ref, bias_ref, o_ref):
    blk = ATT_BLOCK
    slab = 2 * ATT_HEAD_DIM
    r = lax.broadcasted_iota(jnp.int32, (blk, blk), 0)
    c = lax.broadcasted_iota(jnp.int32, (blk, blk), 1)
    own = c <= r
    first_ok = jnp.logical_or(own, pl.program_id(1) > 0)
    low = lax.broadcasted_iota(jnp.int32, (1, slab), 1) < ATT_HEAD_DIM
    zero = jnp.zeros((), BF16)
    for i in range(q_ref.shape[1] // blk):
        rows = slice(i * blk, (i + 1) * blk)
        prev = (lambda cs: kvp_ref[0, :, cs]) if i == 0 else (
            lambda cs, rs=slice((i - 1) * blk, i * blk): kvc_ref[0, rs, cs])
        slabs = []
        for a in range(ATT_Q_HEADS // 2):
            g = (2 * a) // ATT_GROUP
            ks = slice(g * slab, (g + 1) * slab)
            vs = slice((ATT_KV_HEADS + g) * slab, (ATT_KV_HEADS + g + 1) * slab)
            k = jnp.concatenate([prev(ks), kvc_ref[0, rows, ks]], axis=0)
            v_prev, v_own = prev(vs), kvc_ref[0, rows, vs]
            q = q_ref[0, rows, a * slab:(a + 1) * slab]
            halves = []
            for part in range(2):
                h = 2 * a + part
                qz = jnp.where(low if part == 0 else jnp.logical_not(low), q, zero)
                both = _dot_nt(qz, k)
                s = jnp.where(own, both[:, blk:], both[:, :blk]) + bias_ref[h]
                if i == 0:
                    s = jnp.where(first_ok, s, NEG)
                sink = sinks_ref[h]
                m = jnp.maximum(jnp.max(s, axis=1, keepdims=True), sink)
                p = jnp.exp(s - m)
                denom = jnp.sum(p, axis=1, keepdims=True) + jnp.exp(sink - m)
                p_own = jnp.where(own, p, 0.0).astype(BF16)
                p_prev = jnp.where(own, 0.0, p).astype(BF16)
                halves.append((_dot(p_own, v_own) + _dot(p_prev, v_prev)) / denom)
            slabs.append(jnp.where(low, halves[0], halves[1]))
        o_ref[0, rows, :] = jnp.concatenate(slabs, axis=1).astype(BF16)


def _t5_bucket(n):
    max_exact = NUM_BUCKETS // 2
    nf = jnp.maximum(n, 1).astype(F32)
    large = max_exact + (jnp.log(nf / max_exact) / math.log(MAX_DISTANCE / max_exact)
                         * (NUM_BUCKETS - max_exact)).astype(jnp.int32)
    large = jnp.minimum(large, NUM_BUCKETS - 1)
    return jnp.where(n < max_exact, n, large)


def _attention(q, kv, sinks, bias_table):
    b, s, _ = q.shape
    blk = ATT_BLOCK
    assert WINDOW == blk
    qi = jnp.arange(blk)[:, None]
    ki = jnp.arange(blk)[None, :]
    dist = jnp.where(ki <= qi, qi - ki, qi + blk - ki)
    onehot = jax.nn.one_hot(_t5_bucket(dist), NUM_BUCKETS, dtype=F32)
    bias = jnp.einsum("qkn,nh->hqk", onehot, bias_table.astype(F32), precision=lax.Precision.HIGHEST)

    sub = ATT_STEP_BLOCKS
    kvw = kv.shape[-1]
    return pl.pallas_call(
        _attention_kernel,
        grid=(b, s // (sub * blk)),
        in_specs=[
            pl.BlockSpec(memory_space=pltpu.SMEM),
            pl.BlockSpec((1, sub * blk, ATT_WIDTH), lambda i, j: (i, j, 0)),
            pl.BlockSpec((1, blk, kvw), lambda i, j: (i, jnp.maximum(sub * j - 1, 0), 0)),
            pl.BlockSpec((1, sub * blk, kvw), lambda i, j: (i, j, 0)),
            RESIDENT,
        ],
        out_specs=pl.BlockSpec((1, sub * blk, ATT_WIDTH), lambda i, j: (i, j, 0)),
        out_shape=jax.ShapeDtypeStruct((b, s, ATT_WIDTH), BF16),
        compiler_params=_params(("parallel", "arbitrary")),
        name="attention",
    )(sinks, q, kv, kv, bias)


def _pack_bf16_pairs(lo, hi):
    return pltpu.bitcast(pltpu.pack_elementwise([lo, hi], packed_dtype=BF16), jnp.uint32)


def _unpack_bf16_pairs(packed):
    lo = pltpu.bitcast(packed << 16, F32).astype(BF16)
    hi = pltpu.bitcast(packed & jnp.uint32(0xFFFF0000), F32).astype(BF16)
    return lo, hi


def _stack_rows(rows, dtype):
    t = rows[0].shape[1]
    sub = lax.broadcasted_iota(jnp.int32, (SUBLANES, t), 0)
    out = jnp.zeros((SUBLANES, t), dtype)
    for i, r in enumerate(rows):
        out = jnp.where(sub == i, r.astype(dtype), out)
    return out


def _route_tile(xb, wrt_ref, brt_ref):
    t = xb.shape[0]
    logits = _dot_nt(wrt_ref[...], xb) + brt_ref[...]
    expert = lax.broadcasted_iota(jnp.int32, logits.shape, 0)
    vals = logits
    top_v, sels = [], []
    for _ in range(TOP_K):
        m = jnp.max(vals, axis=0, keepdims=True)
        idx = jnp.min(jnp.where(vals == m, expert, N_EXPERTS), axis=0, keepdims=True)
        sel = expert == idx
        vals = jnp.where(sel, -jnp.inf, vals)
        top_v.append(m)
        sels.append(sel)
    exps = [jnp.exp(v - top_v[0]) for v in top_v]
    denom = exps[0] + exps[1] + exps[2] + exps[3]
    gates = [e / denom for e in exps]

    chosen = jnp.zeros(logits.shape, F32)
    for sel in sels:
        chosen = chosen + sel.astype(F32)
    chosen_b = chosen.astype(BF16)
    r = lax.broadcasted_iota(jnp.int32, (t, t), 0)
    c = lax.broadcasted_iota(jnp.int32, (t, t), 1)
    earlier = jnp.where(r < c, 1.0, 0.0).astype(BF16)
    before = _dot(chosen_b, earlier)
    counts = jnp.sum(chosen, axis=1, keepdims=True)
    chunks = jnp.floor((counts + (CHUNK_ROWS - 1)) * (1.0 / CHUNK_ROWS))
    er = lax.broadcasted_iota(jnp.int32, (N_EXPERTS, N_EXPERTS), 0)
    ec = lax.broadcasted_iota(jnp.int32, (N_EXPERTS, N_EXPERTS), 1)
    lower = jnp.where(ec < er, 1.0, 0.0).astype(BF16)
    start = _dot(lower, jnp.broadcast_to(chunks, logits.shape).astype(BF16)) * float(CHUNK_ROWS)
    rows = [jnp.sum(jnp.where(sel, before + start, 0.0), axis=0, keepdims=True).astype(jnp.int32)
            for sel in sels]
    counts_rows = _dot_nt(jnp.ones((SUBLANES, t), BF16), chosen_b)

    local = lax.broadcasted_iota(jnp.int32, (LOCAL_ROWS, 1), 0)
    select = jnp.zeros((LOCAL_ROWS, t), F32)
    for row in rows:
        select = jnp.where(local == row, 1.0, select)
    return rows, gates, counts_rows, select.astype(BF16)


def _out_router_kernel(ret_ref, att_ref, x_ref, wo_ref, g_ref, wrt_ref, brt_ref,
                       h1_ref, xloc_ref, slot_ref, gate_ref, cnt_ref):
    half = x_ref.shape[1] // 2
    for s in range(x_ref.shape[0] // ROUTE_TILE):
        tok = slice(s * ROUTE_TILE, (s + 1) * ROUTE_TILE)
        loc = slice(s * LOCAL_ROWS, (s + 1) * LOCAL_ROWS)
        h1 = (x_ref[tok, :] + _dot(ret_ref[tok, :], wo_ref[:RET_WIDTH, :])
              + _dot(att_ref[tok, :], wo_ref[RET_WIDTH:, :]))
        h1_ref[tok, :] = h1
        xb = _rms(h1, g_ref[...]).astype(BF16)
        rows, gates, counts_rows, select = _route_tile(xb, wrt_ref, brt_ref)
        xloc = _dot(select, xb)
        xloc_ref[loc, :] = _pack_bf16_pairs(xloc[:, :half], xloc[:, half:])
        slot_ref[:, tok] = _stack_rows(rows, jnp.int32)
        gate_ref[:, tok] = _stack_rows(gates, F32)
        cnt_ref[s] = counts_rows


def _out_router(ret, att, x2, w_out, g, w_router_t, b_router_t):
    n, d = x2.shape
    per = ROUTE_STEP_TILES
    tm = per * ROUTE_TILE
    tiles = n // ROUTE_TILE
    row = lambda i: (i, 0)
    col = lambda i: (0, i)
    fixed = lambda i: (0, 0)
    return pl.pallas_call(
        _out_router_kernel,
        grid=(n // tm,),
        in_specs=[
            pl.BlockSpec((tm, RET_WIDTH), row),
            pl.BlockSpec((tm, ATT_WIDTH), row),
            pl.BlockSpec((tm, d), row),
            RESIDENT,
            RESIDENT,
            RESIDENT,
            RESIDENT,
        ],
        out_specs=[
            pl.BlockSpec((tm, d), row),
            pl.BlockSpec((per * LOCAL_ROWS, d // 2), row),
            pl.BlockSpec((SUBLANES, tm), col),
            pl.BlockSpec((SUBLANES, tm), col),
            pl.BlockSpec((per, SUBLANES, N_EXPERTS), lambda i: (i, 0, 0)),
        ],
        out_shape=[
            jax.ShapeDtypeStruct((n, d), F32),
            jax.ShapeDtypeStruct((tiles * LOCAL_ROWS, d // 2), jnp.uint32),
            jax.ShapeDtypeStruct((SUBLANES, n), jnp.int32),
            jax.ShapeDtypeStruct((SUBLANES, n), F32),
            jax.ShapeDtypeStruct((tiles, SUBLANES, N_EXPERTS), F32),
        ],
        compiler_params=_params(("parallel",)),
        name="out_router",
    )(ret, att, x2, w_out, g, w_router_t, b_router_t)


def _convert_expert_weights(wu_ref, wdn_ref, wg_ref, wl_ref, wd_ref):
    ch = PERM_CHUNK
    hc = ch // 2
    r = lax.broadcasted_iota(jnp.int32, (ch, ch), 0)
    c = lax.broadcasted_iota(jnp.int32, (ch, ch), 1)
    perm = jnp.where(r == jnp.where(c < hc, 2 * c, 2 * (c - hc) + 1), 1.0, 0.0).astype(BF16)
    for j in range(wu_ref.shape[1] // ch):
        both = _dot(wu_ref[:, j * ch:(j + 1) * ch].astype(BF16), perm)
        wg_ref[:, j * hc:(j + 1) * hc] = both[:, :hc].astype(BF16)
        wl_ref[:, j * hc:(j + 1) * hc] = both[:, hc:].astype(BF16)
    wd_ref[...] = wdn_ref[...].astype(BF16)


def _moe_kernel(be_ref, nused_ref, ord_ref, ue_ref, src_ref, dst_ref,
                xin_hbm, wu_hbm, bg_ref, bl_ref, wdn_hbm, bd_ref,
                loc_hbm, xbuf, ybuf, gsem, ssem, wu_buf, wdn_buf, wsem, wg_ref, wl_ref, wd_ref):
    del xin_hbm
    i = pl.program_id(0)
    n_used = nused_ref[0]
    last = pl.num_programs(0) - 1
    slot = lax.rem(i, 2)
    bm, half = xbuf.shape[1], xbuf.shape[2]

    def chunk(ref, c):
        first = c * CHUNK_ROWS if isinstance(c, int) else pl.multiple_of(c * CHUNK_ROWS, CHUNK_ROWS)
        return ref.at[pl.ds(first, CHUNK_ROWS), :]

    def start_gather(step, sl):
        for j in range(BLOCK_CHUNKS):
            pltpu.make_async_copy(chunk(loc_hbm, src_ref[step * BLOCK_CHUNKS + j]),
                                  chunk(xbuf.at[sl], j), gsem.at[sl]).start()

    def start_writeback(step, sl):
        for j in range(BLOCK_CHUNKS):
            dst = dst_ref[(step + PRIME_BLOCKS) * BLOCK_CHUNKS + j]
            pltpu.make_async_copy(chunk(ybuf.at[sl], j), chunk(loc_hbm, dst), ssem.at[sl]).start()

    def wait_gather(sl):
        pltpu.make_async_copy(loc_hbm.at[pl.ds(0, bm), :], xbuf.at[sl], gsem.at[sl]).wait()

    def wait_writeback(sl):
        pltpu.make_async_copy(ybuf.at[sl], loc_hbm.at[pl.ds(0, bm), :], ssem.at[sl]).wait()

    def weight_fetch(expert, buf):
        return (pltpu.make_async_copy(wu_hbm.at[expert], wu_buf.at[buf], wsem.at[buf, 0]),
                pltpu.make_async_copy(wdn_hbm.at[expert], wdn_buf.at[buf], wsem.at[buf, 1]))

    @pl.when(i == 0)
    def _():
        ybuf[...] = jnp.zeros_like(ybuf)
        start_gather(0, 0)
        start_writeback(-2, 0)
        for copy in weight_fetch(ue_ref[0], 0):
            copy.start()

    new_expert = jnp.logical_or(i == 0, be_ref[i] != be_ref[jnp.maximum(i - 1, 0)])

    @pl.when(jnp.logical_and(i < n_used, new_expert))
    def _():
        k = ord_ref[i]
        buf = lax.rem(k, 2)
        for copy in weight_fetch(be_ref[i], buf):
            copy.wait()

        @pl.when(k + 1 < nused_ref[1])
        def _():
            for copy in weight_fetch(ue_ref[k + 1], 1 - buf):
                copy.start()

        _convert_expert_weights(wu_buf.at[buf], wdn_buf.at[buf], wg_ref, wl_ref, wd_ref)

    @pl.when(i < n_used)
    def _():
        wait_gather(slot)
        start_gather(jnp.minimum(i + 1, last), 1 - slot)
        lo, hi = _unpack_bf16_pairs(xbuf[slot])
        start_writeback(i - 1, 1 - slot)

        expert = be_ref[i]

        def up(w_ref, b_ref):
            return _dot(lo, w_ref[:half, :]) + _dot(hi, w_ref[half:, :]) + b_ref[expert]

        glu = jnp.minimum(up(wg_ref, bg_ref), SWIGLU_LIMIT)
        lin = jnp.clip(up(wl_ref, bl_ref), -SWIGLU_LIMIT, SWIGLU_LIMIT)
        act = glu * jax.nn.sigmoid(SWIGLU_ALPHA * glu) * (lin + 1.0)
        y = _dot(act.astype(BF16), wd_ref[...]) + bd_ref[expert]

        wait_writeback(slot)
        ybuf[slot] = _pack_bf16_pairs(y[:, :half], y[:, half:])

        @pl.when(i == n_used - 1)
        def _():
            start_writeback(i, slot)
            wait_writeback(1 - slot)
            wait_writeback(slot)
            wait_gather(1 - slot)


def _moe(block_e, n_used, block_ord, used_experts, chunk_src, chunk_dst, xloc, w_up, bg, bl, w_down, bd):
    half = xloc.shape[1]
    e, d, f2 = w_up.shape
    f = f2 // 2
    bm = MOE_BLOCK
    n_blocks = chunk_src.shape[0] // BLOCK_CHUNKS
    assert chunk_dst.shape[0] == (n_blocks + PRIME_BLOCKS) * BLOCK_CHUNKS
    prefetch = (block_e, n_used, block_ord, used_experts, chunk_src, chunk_dst)
    grid_spec = pltpu.PrefetchScalarGridSpec(
        num_scalar_prefetch=len(prefetch),
        grid=(n_blocks,),
        in_specs=[
            pl.BlockSpec(memory_space=pl.ANY),
            pl.BlockSpec(memory_space=pl.ANY),
            RESIDENT,
            RESIDENT,
            pl.BlockSpec(memory_space=pl.ANY),
            RESIDENT,
        ],
        out_specs=pl.BlockSpec(memory_space=pl.ANY),
        scratch_shapes=[
            pltpu.VMEM((2, bm, half), jnp.uint32),
            pltpu.VMEM((2, bm, half), jnp.uint32),
            pltpu.SemaphoreType.DMA((2,)),
            pltpu.SemaphoreType.DMA((2,)),
            pltpu.VMEM((2, d, f2), F32),
            pltpu.VMEM((2, f, d), F32),
            pltpu.SemaphoreType.DMA((2, 2)),
            pltpu.VMEM((d, f), BF16),
            pltpu.VMEM((d, f), BF16),
            pltpu.VMEM((f, d), BF16),
        ],
    )
    return pl.pallas_call(
        _moe_kernel,
        grid_spec=grid_spec,
        out_shape=jax.ShapeDtypeStruct(xloc.shape, jnp.uint32),
        input_output_aliases={len(prefetch): 0},
        compiler_params=_params(("arbitrary",)),
        name="moe",
    )(*prefetch, xloc, w_up, bg, bl, w_down, bd)


def _combine_ple_kernel(yloc_ref, slot_ref, gate_ref, h1_ref, p_ref, wpg_ref, wpp_ref, gple_ref,
                        gfin_ref, o_ref):
    local = lax.broadcasted_iota(jnp.int32, (LOCAL_ROWS, 1), 0)
    for s in range(h1_ref.shape[0] // ROUTE_TILE):
        tok = slice(s * ROUTE_TILE, (s + 1) * ROUTE_TILE)
        loc = slice(s * LOCAL_ROWS, (s + 1) * LOCAL_ROWS)
        rows = slot_ref[:, tok]
        gates = gate_ref[:, tok]
        weights = jnp.zeros((LOCAL_ROWS, ROUTE_TILE), F32)
        for k in range(TOP_K):
            weights = jnp.where(local == rows[k:k + 1, :], gates[k:k + 1, :], weights)
        weights = weights.astype(BF16)
        y_lo, y_hi = _unpack_bf16_pairs(yloc_ref[loc, :])
        moe = jnp.concatenate([_dot_tn(weights, y_lo), _dot_tn(weights, y_hi)], axis=1)
        h2 = h1_ref[tok, :] + moe
        gate = jax.nn.sigmoid(_dot(h2.astype(BF16), wpg_ref[...]))
        proj = _dot(p_ref[tok, :].astype(BF16), wpp_ref[...])
        h3 = h2 + _rms(gate * proj, gple_ref[...])
        o_ref[tok, :] = _rms(h3, gfin_ref[...])


def _combine_ple(yloc, slot_rows, gates, h1, p2, wpg, wpp, gple, gfin):
    n, d = h1.shape
    pd = p2.shape[1]
    per = COMBINE_STEP_TILES
    tm = per * ROUTE_TILE
    row = lambda i: (i, 0)
    col = lambda i: (0, i)
    fixed = lambda i: (0, 0)
    return pl.pallas_call(
        _combine_ple_kernel,
        grid=(n // tm,),
        in_specs=[
            pl.BlockSpec((per * LOCAL_ROWS, d // 2), row),
            pl.BlockSpec((SUBLANES, tm), col),
            pl.BlockSpec((SUBLANES, tm), col),
            pl.BlockSpec((tm, d), row),
            pl.BlockSpec((tm, pd), row),
            RESIDENT,
            RESIDENT,
            RESIDENT,
            RESIDENT,
        ],
        out_specs=pl.BlockSpec((tm, d), row),
        out_shape=jax.ShapeDtypeStruct((n, d), F32),
        compiler_params=_params(("parallel",)),
        name="combine_ple",
    )(yloc, slot_rows, gates, h1, p2, wpg, wpp, gple, gfin)


def _routing_tables(counts):
    tiles = counts.shape[0]
    local_chunks = LOCAL_ROWS // CHUNK_ROWS
    assert 2 * BLOCK_CHUNKS < tiles * SPARE_CHUNKS
    run = (counts + CHUNK_ROWS - 1) // CHUNK_ROWS
    local_start = jnp.cumsum(run, axis=1) - run
    run_end = jnp.cumsum(run, axis=0)
    run_start = run_end - run
    total = run_end[-1]
    padded = (total + BLOCK_CHUNKS - 1) // BLOCK_CHUNKS * BLOCK_CHUNKS
    pend = jnp.cumsum(padded)
    poff = pend - padded

    slots = tiles * ROUTE_TILE * TOP_K
    max_chunks = (slots + tiles * N_EXPERTS * (CHUNK_ROWS - 1)) // CHUNK_ROWS + N_EXPERTS * (BLOCK_CHUNKS - 1)
    n_chunks = -(-max_chunks // BLOCK_CHUNKS) * BLOCK_CHUNKS
    lookup = lambda onehot, table: jnp.dot(onehot, table.astype(F32), precision=lax.Precision.HIGHEST)
    q = jnp.arange(n_chunks, dtype=jnp.int32)
    e = jnp.minimum(jnp.sum(q[:, None] >= pend[None, :], axis=1), N_EXPERTS - 1).astype(jnp.int32)
    e_hot = (e[:, None] == jnp.arange(N_EXPERTS)[None, :]).astype(F32)
    per_expert = lookup(e_hot, jnp.stack([poff, total], axis=1)).astype(jnp.int32)
    j = q - per_expert[:, 0]
    valid = j < per_expert[:, 1]
    ends = lookup(e_hot, run_end.T).astype(jnp.int32)
    tile = jnp.minimum(jnp.sum(ends <= j[:, None], axis=1), tiles - 1).astype(jnp.int32)
    tile_hot = (tile[:, None] == jnp.arange(tiles)[None, :]).astype(F32)
    shift = jnp.sum(lookup(tile_hot, local_start - run_start) * e_hot, axis=1).astype(jnp.int32)
    src = jnp.where(valid, tile * local_chunks + shift + j, tiles * local_chunks - 1)
    spare = (q // BLOCK_CHUNKS % 2) * BLOCK_CHUNKS + q % BLOCK_CHUNKS
    spare_chunk = (spare // SPARE_CHUNKS) * local_chunks + (local_chunks - SPARE_CHUNKS) + spare % SPARE_CHUNKS
    dst = jnp.where(valid, tile * local_chunks + shift + j, spare_chunk)
    dst = jnp.concatenate([spare_chunk[:PRIME_BLOCKS * BLOCK_CHUNKS], dst])
    has_rows = total > 0
    ordinal = jnp.cumsum(has_rows.astype(jnp.int32)) - 1
    used_experts = jnp.argsort(jnp.logical_not(has_rows), stable=True).astype(jnp.int32)
    block_e = e[::BLOCK_CHUNKS]
    block_ord = jnp.sum(jnp.where(block_e[:, None] == jnp.arange(N_EXPERTS)[None, :], ordinal[None, :], 0), axis=1)
    n_used = jnp.stack([pend[-1] // BLOCK_CHUNKS, jnp.sum(has_rows)]).astype(jnp.int32)
    return (block_e, n_used, block_ord.astype(jnp.int32), used_experts,
            src.astype(jnp.int32), dst.astype(jnp.int32))


def kernel(x, p, positions, rel_bias_table, g_mix_norm, w_in, b_in, ret_norm_g, att_sinks, w_out,
           g_moe_norm, w_router, b_router, w_up, b_up, w_down, b_down, w_ple_gate, w_ple_proj,
           g_ple_norm, g_final):
    b, s, d = x.shape
    depth = w_in.shape[0]
    assert depth == 1, "single-layer stack only"
    n = b * s
    x2 = x.reshape(n, d)

    half = RET_DK // 2
    inv = ROPE_BASE ** (-jnp.arange(half, dtype=F32) / half)
    ang = (positions.astype(F32)[..., None] * inv).reshape(n, half)

    qr, kr, vr, gr, qa, kva = _in_proj(
        x2, g_mix_norm[0][None], w_in[0].astype(BF16), b_in[0][None], jnp.cos(ang), jnp.sin(ang))

    shape3 = lambda t: t.reshape(b, s, t.shape[-1])
    ret = _retention(shape3(qr), shape3(kr), shape3(vr), shape3(gr), ret_norm_g[0])
    att = _attention(shape3(qa), shape3(kva), att_sinks[0], rel_bias_table)

    h1, xloc, slot_rows, gates, counts = _out_router(
        ret.reshape(n, RET_WIDTH), att.reshape(n, ATT_WIDTH), x2, w_out[0].astype(BF16),
        g_moe_norm[0][None], w_router[0].T.astype(BF16), b_router[0][:, None])

    tables = _routing_tables(counts[:, 0, :].astype(jnp.int32))

    yloc = _moe(*tables, xloc, w_up[0],
                b_up[0][:, None, 0::2], b_up[0][:, None, 1::2],
                w_down[0], b_down[0].reshape(N_EXPERTS, 1, d))

    out = _combine_ple(yloc, slot_rows, gates, h1, p[0].reshape(n, -1),
                       w_ple_gate[0].astype(BF16), w_ple_proj[0].astype(BF16),
                       g_ple_norm[0][None], g_final[None])
    return out.reshape(b, s, d)
```

```python
import math

import jax
import jax.numpy as jnp
from jax import lax
from jax.experimental import pallas as pl
from jax.experimental.pallas import tpu as pltpu

RET_HEADS = 4
RET_DK = 128
RET_DV = 128
RET_CHUNK = 128
ROPE_BASE = 10000.0
ATT_Q_HEADS = 8
ATT_KV_HEADS = 2
ATT_GROUP = ATT_Q_HEADS // ATT_KV_HEADS
ATT_HEAD_DIM = 64
WINDOW = 128
ATT_BLOCK = 128
NUM_BUCKETS = 32
MAX_DISTANCE = 128
N_EXPERTS = 32
TOP_K = 4
SWIGLU_ALPHA = 1.702
SWIGLU_LIMIT = 7.0
EPS = 1e-5

RET_WIDTH = RET_HEADS * RET_DV
ATT_WIDTH = ATT_Q_HEADS * ATT_HEAD_DIM
KV_WIDTH = ATT_KV_HEADS * ATT_HEAD_DIM

VMEM_LIMIT_BYTES = 48 * 1024 * 1024

TOKEN_TILE = 1024
SUBLANES = 8
ROUTE_TILE = 256
ROUTE_STEP_TILES = 4
COMBINE_STEP_TILES = 4
CHUNK_ROWS = SUBLANES
LOCAL_ROWS = ROUTE_TILE * TOP_K + N_EXPERTS * CHUNK_ROWS
SPARE_CHUNKS = (LOCAL_ROWS - ROUTE_TILE * TOP_K - N_EXPERTS * (CHUNK_ROWS - 1)) // CHUNK_ROWS
MOE_BLOCK = 512
BLOCK_CHUNKS = MOE_BLOCK // CHUNK_ROWS
PRIME_BLOCKS = 2
ATT_STEP_BLOCKS = 8
PERM_CHUNK = 256

F32 = jnp.float32
BF16 = jnp.bfloat16
NEG = float(jnp.finfo(jnp.float32).min)


def _rms(x, g):
    return x * lax.rsqrt(jnp.mean(x * x, axis=-1, keepdims=True) + EPS) * g


def _dot(a, b):
    return jnp.dot(a, b, preferred_element_type=F32)


def _dot_nt(a, b):
    return lax.dot_general(a, b, (((1,), (1,)), ((), ())), preferred_element_type=F32)


def _dot_tn(a, b):
    return lax.dot_general(a, b, (((0,), (0,)), ((), ())), preferred_element_type=F32)


def _params(semantics):
    return pltpu.CompilerParams(dimension_semantics=semantics, vmem_limit_bytes=VMEM_LIMIT_BYTES)


RESIDENT = pl.BlockSpec(memory_space=pltpu.VMEM)


def _in_proj_kernel(x_ref, g_ref, w_ref, b_ref, cos_ref, sin_ref,
                    qr_ref, kr_ref, vr_ref, gr_ref, qa_ref, kva_ref):
    hn = _rms(x_ref[...], g_ref[...]).astype(BF16)
    cos = jnp.concatenate([cos_ref[...], cos_ref[...]], axis=1)
    sin = jnp.concatenate([-sin_ref[...], sin_ref[...]], axis=1)

    def proj(lo, hi):
        return _dot(hn, w_ref[:, lo:hi]) + b_ref[:, lo:hi]

    def rope(t):
        heads = []
        for h in range(RET_HEADS):
            th = t[:, h * RET_DK:(h + 1) * RET_DK]
            heads.append(th * cos + pltpu.roll(th, RET_DK // 2, 1) * sin)
        return jnp.concatenate(heads, axis=1)

    o = 0
    qr_ref[...] = rope(proj(o, o + RET_WIDTH)).astype(BF16)
    o += RET_WIDTH
    kr_ref[...] = (rope(proj(o, o + RET_WIDTH)) * (RET_DK ** -0.5)).astype(BF16)
    o += RET_WIDTH
    vr_ref[...] = proj(o, o + RET_WIDTH).astype(BF16)
    o += RET_WIDTH
    gr_ref[...] = proj(o, o + RET_WIDTH).astype(BF16)
    o += RET_WIDTH
    qa_ref[...] = (proj(o, o + ATT_WIDTH) * (ATT_HEAD_DIM ** -0.5)).astype(BF16)
    o += ATT_WIDTH
    low = lax.broadcasted_iota(jnp.int32, (1, KV_WIDTH), 1) < ATT_HEAD_DIM
    slabs = []
    for part in range(2):
        t = proj(o + part * KV_WIDTH, o + (part + 1) * KV_WIDTH)
        swapped = pltpu.roll(t, ATT_HEAD_DIM, 1)
        slabs += [jnp.where(low, t, swapped), jnp.where(low, swapped, t)]
    kva_ref[...] = jnp.concatenate(slabs, axis=1).astype(BF16)


def _in_proj(x2, g, w_in, b_in, cosf, sinf):
    n, d = x2.shape
    width = w_in.shape[1]
    tm = TOKEN_TILE
    row = lambda i: (i, 0)
    outs = [RET_WIDTH, RET_WIDTH, RET_WIDTH, RET_WIDTH, ATT_WIDTH, 4 * KV_WIDTH]
    return pl.pallas_call(
        _in_proj_kernel,
        grid=(n // tm,),
        in_specs=[
            pl.BlockSpec((tm, d), row),
            RESIDENT,
            RESIDENT,
            RESIDENT,
            pl.BlockSpec((tm, RET_DK // 2), row),
            pl.BlockSpec((tm, RET_DK // 2), row),
        ],
        out_specs=[pl.BlockSpec((tm, w), row) for w in outs],
        out_shape=[jax.ShapeDtypeStruct((n, w), BF16) for w in outs],
        compiler_params=_params(("parallel",)),
        name="in_proj",
    )(x2, g, w_in, b_in, cosf, sinf)


def _retention_kernel(q_ref, k_ref, v_ref, g_ref, dmask_ref, qdec_ref, kdec_ref, cdec_ref,
                      gn_ref, o_ref, state_ref):
    batch = q_ref.shape[0]

    @pl.when(pl.program_id(0) == 0)
    def _():
        state_ref[...] = jnp.zeros_like(state_ref)

    def per_batch(b, carry):
        for h in range(RET_HEADS):
            sl = slice(h * RET_DK, (h + 1) * RET_DK)
            q = q_ref[b, :, sl]
            k = k_ref[b, :, sl]
            v = v_ref[b, :, sl]
            scores = _dot_nt(q, k) * dmask_ref[h]
            inner = _dot(scores.astype(BF16), v)
            state = state_ref[b * RET_HEADS + h]
            cross = _dot(q, state.astype(BF16)) * qdec_ref[h]
            kd = (k.astype(F32) * kdec_ref[h]).astype(BF16)
            state_ref[b * RET_HEADS + h] = state * cdec_ref[h] + _dot_tn(kd, v)
            o = _rms(inner + cross, gn_ref[h])
            gate = g_ref[b, :, sl].astype(F32)
            o_ref[b, :, sl] = (gate * jax.nn.sigmoid(gate) * o).astype(BF16)
        return carry

    lax.fori_loop(0, batch, per_batch, 0, unroll=True)


def _retention(q, k, v, g, gn):
    b, s, w = q.shape
    c = RET_CHUNK
    log_g = jnp.log(1.0 - 2.0 ** (-5.0 - jnp.arange(RET_HEADS, dtype=F32)))
    ci = jnp.arange(c)
    diff = (ci[:, None] - ci[None, :]).astype(F32)
    dmask = jnp.where(diff >= 0, jnp.exp(jnp.maximum(diff, 0.0)[None] * log_g[:, None, None]), 0.0)
    cf = ci.astype(F32)
    kdec = jnp.exp((c - 1 - cf)[None, :] * log_g[:, None])
    qdec = jnp.exp((cf + 1)[None, :] * log_g[:, None])
    cdec = jnp.exp(c * log_g)
    kdec = jnp.broadcast_to(kdec[:, :, None], (RET_HEADS, c, RET_DK))
    qdec = jnp.broadcast_to(qdec[:, :, None], (RET_HEADS, c, RET_DV))
    cdec = jnp.broadcast_to(cdec[:, None, None], (RET_HEADS, RET_DK, RET_DV))
    gn3 = gn.reshape(RET_HEADS, 1, RET_DV)

    seq = lambda i: (0, i, 0)
    return pl.pallas_call(
        _retention_kernel,
        grid=(s // c,),
        in_specs=[pl.BlockSpec((b, c, w), seq)] * 4 + [RESIDENT] * 5,
        out_specs=pl.BlockSpec((b, c, w), seq),
        out_shape=jax.ShapeDtypeStruct((b, s, w), BF16),
        scratch_shapes=[pltpu.VMEM((b * RET_HEADS, RET_DK, RET_DV), F32)],
        compiler_params=_params(("arbitrary",)),
        name="retention",
    )(q, k, v, g, dmask, qdec, kdec, cdec, gn3)


def _attention_kernel(sinks_ref, q_ref, kvp_ref, kvc_ref, bias_ref, o_ref):
    blk = ATT_BLOCK
    slab = 2 * ATT_HEAD_DIM
    r = lax.broadcasted_iota(jnp.int32, (blk, blk), 0)
    c = lax.broadcasted_iota(jnp.int32, (blk, blk), 1)
    own = c <= r
    first_ok = jnp.logical_or(own, pl.program_id(1) > 0)
    low = lax.broadcasted_iota(jnp.int32, (1, slab), 1) < ATT_HEAD_DIM
    zero = jnp.zeros((), BF16)
    for i in range(q_ref.shape[1] // blk):
        rows = slice(i * blk, (i + 1) * blk)
        prev = (lambda cs: kvp_ref[0, :, cs]) if i == 0 else (
            lambda cs, rs=slice((i - 1) * blk, i * blk): kvc_ref[0, rs, cs])
        slabs = []
        for a in range(ATT_Q_HEADS // 2):
            g = (2 * a) // ATT_GROUP
            ks = slice(g * slab, (g + 1) * slab)
            vs = slice((ATT_KV_HEADS + g) * slab, (ATT_KV_HEADS + g + 1) * slab)
            k = jnp.concatenate([prev(ks), kvc_ref[0, rows, ks]], axis=0)
            v_prev, v_own = prev(vs), kvc_ref[0, rows, vs]
            q = q_ref[0, rows, a * slab:(a + 1) * slab]
            halves = []
            for part in range(2):
                h = 2 * a + part
                qz = jnp.where(low if part == 0 else jnp.logical_not(low), q, zero)
                both = _dot_nt(qz, k)
                s = jnp.where(own, both[:, blk:], both[:, :blk]) + bias_ref[h]
                if i == 0:
                    s = jnp.where(first_ok, s, NEG)
                sink = sinks_ref[h]
                m = jnp.maximum(jnp.max(s, axis=1, keepdims=True), sink)
                p = jnp.exp(s - m)
                denom = jnp.sum(p, axis=1, keepdims=True) + jnp.exp(sink - m)
                p_own = jnp.where(own, p, 0.0).astype(BF16)
                p_prev = jnp.where(own, 0.0, p).astype(BF16)
                halves.append((_dot(p_own, v_own) + _dot(p_prev, v_prev)) / denom)
            slabs.append(jnp.where(low, halves[0], halves[1]))
        o_ref[0, rows, :] = jnp.concatenate(slabs, axis=1).astype(BF16)


def _t5_bucket(n):
    max_exact = NUM_BUCKETS // 2
    nf = jnp.maximum(n, 1).astype(F32)
    large = max_exact + (jnp.log(nf / max_exact) / math.log(MAX_DISTANCE / max_exact)
                         * (NUM_BUCKETS - max_exact)).astype(jnp.int32)
    large = jnp.minimum(large, NUM_BUCKETS - 1)
    return jnp.where(n < max_exact, n, large)


def _attention(q, kv, sinks, bias_table):
    b, s, _ = q.shape
    blk = ATT_BLOCK
    assert WINDOW == blk
    qi = jnp.arange(blk)[:, None]
    ki = jnp.arange(blk)[None, :]
    dist = jnp.where(ki <= qi, qi - ki, qi + blk - ki)
    onehot = jax.nn.one_hot(_t5_bucket(dist), NUM_BUCKETS, dtype=F32)
    bias = jnp.einsum("qkn,nh->hqk", onehot, bias_table.astype(F32), precision=lax.Precision.HIGHEST)

    sub = ATT_STEP_BLOCKS
    kvw = kv.shape[-1]
    return pl.pallas_call(
        _attention_kernel,
        grid=(b, s // (sub * blk)),
        in_specs=[
            pl.BlockSpec(memory_space=pltpu.SMEM),
            pl.BlockSpec((1, sub * blk, ATT_WIDTH), lambda i, j: (i, j, 0)),
            pl.BlockSpec((1, blk, kvw), lambda i, j: (i, jnp.maximum(sub * j - 1, 0), 0)),
            pl.BlockSpec((1, sub * blk, kvw), lambda i, j: (i, j, 0)),
            RESIDENT,
        ],
        out_specs=pl.BlockSpec((1, sub * blk, ATT_WIDTH), lambda i, j: (i, j, 0)),
        out_shape=jax.ShapeDtypeStruct((b, s, ATT_WIDTH), BF16),
        compiler_params=_params(("parallel", "arbitrary")),
        name="attention",
    )(sinks, q, kv, kv, bias)


def _pack_bf16_pairs(lo, hi):
    return pltpu.bitcast(pltpu.pack_elementwise([lo, hi], packed_dtype=BF16), jnp.uint32)


def _unpack_bf16_pairs(packed):
    lo = pltpu.bitcast(packed << 16, F32).astype(BF16)
    hi = pltpu.bitcast(packed & jnp.uint32(0xFFFF0000), F32).astype(BF16)
    return lo, hi


def _stack_rows(rows, dtype):
    t = rows[0].shape[1]
    sub = lax.broadcasted_iota(jnp.int32, (SUBLANES, t), 0)
    out = jnp.zeros((SUBLANES, t), dtype)
    for i, r in enumerate(rows):
        out = jnp.where(sub == i, r.astype(dtype), out)
    return out


def _route_tile(logits):
    t = logits.shape[1]
    expert = lax.broadcasted_iota(jnp.int32, logits.shape, 0)
    vals = logits
    top_v, sels = [], []
    for _ in range(TOP_K):
        m = jnp.max(vals, axis=0, keepdims=True)
        idx = jnp.min(jnp.where(vals == m, expert, N_EXPERTS), axis=0, keepdims=True)
        sel = expert == idx
        vals = jnp.where(sel, -jnp.inf, vals)
        top_v.append(m)
        sels.append(sel)
    exps = [jnp.exp(v - top_v[0]) for v in top_v]
    denom = exps[0] + exps[1] + exps[2] + exps[3]
    gates = [e / denom for e in exps]

    chosen = jnp.zeros(logits.shape, F32)
    for sel in sels:
        chosen = chosen + sel.astype(F32)
    chosen_b = chosen.astype(BF16)
    r = lax.broadcasted_iota(jnp.int32, (t, t), 0)
    c = lax.broadcasted_iota(jnp.int32, (t, t), 1)
    earlier = jnp.where(r < c, 1.0, 0.0).astype(BF16)
    before = _dot(chosen_b, earlier)
    counts = jnp.sum(chosen, axis=1, keepdims=True)
    chunks = jnp.floor((counts + (CHUNK_ROWS - 1)) * (1.0 / CHUNK_ROWS))
    er = lax.broadcasted_iota(jnp.int32, (N_EXPERTS, N_EXPERTS), 0)
    ec = lax.broadcasted_iota(jnp.int32, (N_EXPERTS, N_EXPERTS), 1)
    lower = jnp.where(ec < er, 1.0, 0.0).astype(BF16)
    start = _dot(lower, jnp.broadcast_to(chunks, logits.shape).astype(BF16)) * float(CHUNK_ROWS)
    rows = [jnp.sum(jnp.where(sel, before + start, 0.0), axis=0, keepdims=True).astype(jnp.int32)
            for sel in sels]
    counts_rows = _dot_nt(jnp.ones((SUBLANES, t), BF16), chosen_b)

    local = lax.broadcasted_iota(jnp.int32, (LOCAL_ROWS, 1), 0)
    select = jnp.zeros((LOCAL_ROWS, t), F32)
    for row in rows:
        select = jnp.where(local == row, 1.0, select)
    return rows, gates, counts_rows, select.astype(BF16)


def _out_router_kernel(ret_ref, att_ref, x_ref, wo_ref, g_ref, wrt_ref, brt_ref,
                       h1_ref, xloc_ref, slot_ref, gate_ref, cnt_ref):
    half = x_ref.shape[1] // 2
    h1 = (x_ref[...] + _dot(ret_ref[...], wo_ref[:RET_WIDTH, :])
          + _dot(att_ref[...], wo_ref[RET_WIDTH:, :]))
    h1_ref[...] = h1
    xb_all = _rms(h1, g_ref[...]).astype(BF16)
    logits_all = _dot_nt(wrt_ref[...], xb_all) + brt_ref[...]
    for s in range(x_ref.shape[0] // ROUTE_TILE):
        tok = slice(s * ROUTE_TILE, (s + 1) * ROUTE_TILE)
        loc = slice(s * LOCAL_ROWS, (s + 1) * LOCAL_ROWS)
        xb = xb_all[tok, :]
        rows, gates, counts_rows, select = _route_tile(logits_all[:, tok])
        xloc = _dot(select, xb)
        xloc_ref[loc, :] = _pack_bf16_pairs(xloc[:, :half], xloc[:, half:])
        slot_ref[:, tok] = _stack_rows(rows, jnp.int32)
        gate_ref[:, tok] = _stack_rows(gates, F32)
        cnt_ref[s] = counts_rows


def _out_router(ret, att, x2, w_out, g, w_router_t, b_router_t):
    n, d = x2.shape
    per = ROUTE_STEP_TILES
    tm = per * ROUTE_TILE
    tiles = n // ROUTE_TILE
    row = lambda i: (i, 0)
    col = lambda i: (0, i)
    fixed = lambda i: (0, 0)
    return pl.pallas_call(
        _out_router_kernel,
        grid=(n // tm,),
        in_specs=[
            pl.BlockSpec((tm, RET_WIDTH), row),
            pl.BlockSpec((tm, ATT_WIDTH), row),
            pl.BlockSpec((tm, d), row),
            RESIDENT,
            RESIDENT,
            RESIDENT,
            RESIDENT,
        ],
        out_specs=[
            pl.BlockSpec((tm, d), row),
            pl.BlockSpec((per * LOCAL_ROWS, d // 2), row),
            pl.BlockSpec((SUBLANES, tm), col),
            pl.BlockSpec((SUBLANES, tm), col),
            pl.BlockSpec((per, SUBLANES, N_EXPERTS), lambda i: (i, 0, 0)),
        ],
        out_shape=[
            jax.ShapeDtypeStruct((n, d), F32),
            jax.ShapeDtypeStruct((tiles * LOCAL_ROWS, d // 2), jnp.uint32),
            jax.ShapeDtypeStruct((SUBLANES, n), jnp.int32),
            jax.ShapeDtypeStruct((SUBLANES, n), F32),
            jax.ShapeDtypeStruct((tiles, SUBLANES, N_EXPERTS), F32),
        ],
        compiler_params=_params(("parallel",)),
        name="out_router",
    )(ret, att, x2, w_out, g, w_router_t, b_router_t)


def _convert_expert_weights(wu_ref, wdn_ref, wg_ref, wl_ref, wd_ref):
    ch = PERM_CHUNK
    hc = ch // 2
    r = lax.broadcasted_iota(jnp.int32, (ch, ch), 0)
    c = lax.broadcasted_iota(jnp.int32, (ch, ch), 1)
    perm = jnp.where(r == jnp.where(c < hc, 2 * c, 2 * (c - hc) + 1), 1.0, 0.0).astype(BF16)
    for j in range(wu_ref.shape[1] // ch):
        both = _dot(wu_ref[:, j * ch:(j + 1) * ch].astype(BF16), perm)
        wg_ref[:, j * hc:(j + 1) * hc] = both[:, :hc].astype(BF16)
        wl_ref[:, j * hc:(j + 1) * hc] = both[:, hc:].astype(BF16)
    wd_ref[...] = wdn_ref[...].astype(BF16)


def _moe_kernel(be_ref, nused_ref, ord_ref, ue_ref, full_ref, src_ref, dst_ref,
                xin_hbm, wu_hbm, bg_ref, bl_ref, wdn_hbm, bd_ref,
                loc_hbm, xbuf, ybuf, gsem, ssem, wu_buf, wdn_buf, wsem, wg_ref, wl_ref, wd_ref):
    del xin_hbm
    i = pl.program_id(0)
    n_used = nused_ref[0]
    last = pl.num_programs(0) - 1
    slot = lax.rem(i, 2)
    bm, half = xbuf.shape[1], xbuf.shape[2]

    def chunk(ref, c):
        first = c * CHUNK_ROWS if isinstance(c, int) else pl.multiple_of(c * CHUNK_ROWS, CHUNK_ROWS)
        return ref.at[pl.ds(first, CHUNK_ROWS), :]

    def start_gather(step, sl):
        for j in range(BLOCK_CHUNKS):
            pltpu.make_async_copy(chunk(loc_hbm, src_ref[step * BLOCK_CHUNKS + j]),
                                  chunk(xbuf.at[sl], j), gsem.at[sl]).start()

    def start_writeback(step, sl):
        for j in range(BLOCK_CHUNKS):
            dst = dst_ref[(step + PRIME_BLOCKS) * BLOCK_CHUNKS + j]
            pltpu.make_async_copy(chunk(ybuf.at[sl], j), chunk(loc_hbm, dst), ssem.at[sl]).start()

    def wait_gather(sl):
        pltpu.make_async_copy(loc_hbm.at[pl.ds(0, bm), :], xbuf.at[sl], gsem.at[sl]).wait()

    def wait_writeback(sl):
        pltpu.make_async_copy(ybuf.at[sl], loc_hbm.at[pl.ds(0, bm), :], ssem.at[sl]).wait()

    def weight_fetch(expert, buf):
        return (pltpu.make_async_copy(wu_hbm.at[expert], wu_buf.at[buf], wsem.at[buf, 0]),
                pltpu.make_async_copy(wdn_hbm.at[expert], wdn_buf.at[buf], wsem.at[buf, 1]))

    @pl.when(i == 0)
    def _():
        ybuf[...] = jnp.zeros_like(ybuf)
        start_gather(0, 0)
        start_writeback(-2, 0)
        for copy in weight_fetch(ue_ref[0], 0):
            copy.start()

    new_expert = jnp.logical_or(i == 0, be_ref[i] != be_ref[jnp.maximum(i - 1, 0)])

    @pl.when(jnp.logical_and(i < n_used, new_expert))
    def _():
        k = ord_ref[i]
        buf = lax.rem(k, 2)
        for copy in weight_fetch(be_ref[i], buf):
            copy.wait()

        @pl.when(k + 1 < nused_ref[1])
        def _():
            for copy in weight_fetch(ue_ref[k + 1], 1 - buf):
                copy.start()

        _convert_expert_weights(wu_buf.at[buf], wdn_buf.at[buf], wg_ref, wl_ref, wd_ref)

    def block_step(rows):
        wait_gather(slot)
        start_gather(jnp.minimum(i + 1, last), 1 - slot)
        lo, hi = _unpack_bf16_pairs(xbuf[slot, :rows, :])
        start_writeback(i - 1, 1 - slot)

        expert = be_ref[i]

        def up(w_ref, b_ref):
            return _dot(lo, w_ref[:half, :]) + _dot(hi, w_ref[half:, :]) + b_ref[expert]

        glu = jnp.minimum(up(wg_ref, bg_ref), SWIGLU_LIMIT)
        lin = jnp.clip(up(wl_ref, bl_ref), -SWIGLU_LIMIT, SWIGLU_LIMIT)
        act = glu * jax.nn.sigmoid(SWIGLU_ALPHA * glu) * (lin + 1.0)
        y = _dot(act.astype(BF16), wd_ref[...]) + bd_ref[expert]

        wait_writeback(slot)
        ybuf[slot, :rows, :] = _pack_bf16_pairs(y[:, :half], y[:, half:])

    used = i < n_used
    full = full_ref[i] > 0

    @pl.when(jnp.logical_and(used, full))
    def _():
        block_step(bm)

    @pl.when(jnp.logical_and(used, jnp.logical_not(full)))
    def _():
        block_step(bm // 2)

    @pl.when(i == n_used - 1)
    def _():
        start_writeback(i, slot)
        wait_writeback(1 - slot)
        wait_writeback(slot)
        wait_gather(1 - slot)


def _moe(block_e, n_used, block_ord, used_experts, block_full, chunk_src, chunk_dst,
         xloc, w_up, bg, bl, w_down, bd):
    half = xloc.shape[1]
    e, d, f2 = w_up.shape
    f = f2 // 2
    bm = MOE_BLOCK
    n_blocks = chunk_src.shape[0] // BLOCK_CHUNKS
    assert chunk_dst.shape[0] == (n_blocks + PRIME_BLOCKS) * BLOCK_CHUNKS
    prefetch = (block_e, n_used, block_ord, used_experts, block_full, chunk_src, chunk_dst)
    grid_spec = pltpu.PrefetchScalarGridSpec(
        num_scalar_prefetch=len(prefetch),
        grid=(n_blocks,),
        in_specs=[
            pl.BlockSpec(memory_space=pl.ANY),
            pl.BlockSpec(memory_space=pl.ANY),
            RESIDENT,
            RESIDENT,
            pl.BlockSpec(memory_space=pl.ANY),
            RESIDENT,
        ],
        out_specs=pl.BlockSpec(memory_space=pl.ANY),
        scratch_shapes=[
            pltpu.VMEM((2, bm, half), jnp.uint32),
            pltpu.VMEM((2, bm, half), jnp.uint32),
            pltpu.SemaphoreType.DMA((2,)),
            pltpu.SemaphoreType.DMA((2,)),
            pltpu.VMEM((2, d, f2), F32),
            pltpu.VMEM((2, f, d), F32),
            pltpu.SemaphoreType.DMA((2, 2)),
            pltpu.VMEM((d, f), BF16),
            pltpu.VMEM((d, f), BF16),
            pltpu.VMEM((f, d), BF16),
        ],
    )
    return pl.pallas_call(
        _moe_kernel,
        grid_spec=grid_spec,
        out_shape=jax.ShapeDtypeStruct(xloc.shape, jnp.uint32),
        input_output_aliases={len(prefetch): 0},
        compiler_params=_params(("arbitrary",)),
        name="moe",
    )(*prefetch, xloc, w_up, bg, bl, w_down, bd)


def _combine_ple_kernel(yloc_ref, slot_ref, gate_ref, h1_ref, p_ref, wpg_ref, wpp_ref, gple_ref,
                        gfin_ref, o_ref):
    local = lax.broadcasted_iota(jnp.int32, (LOCAL_ROWS, 1), 0)
    moe = []
    for s in range(h1_ref.shape[0] // ROUTE_TILE):
        tok = slice(s * ROUTE_TILE, (s + 1) * ROUTE_TILE)
        loc = slice(s * LOCAL_ROWS, (s + 1) * LOCAL_ROWS)
        rows = slot_ref[:, tok]
        gates = gate_ref[:, tok]
        weights = jnp.zeros((LOCAL_ROWS, ROUTE_TILE), F32)
        for k in range(TOP_K):
            weights = jnp.where(local == rows[k:k + 1, :], gates[k:k + 1, :], weights)
        weights = weights.astype(BF16)
        y_lo, y_hi = _unpack_bf16_pairs(yloc_ref[loc, :])
        moe.append(jnp.concatenate([_dot_tn(weights, y_lo), _dot_tn(weights, y_hi)], axis=1))
    h2 = h1_ref[...] + jnp.concatenate(moe, axis=0)
    gate = jax.nn.sigmoid(_dot(h2.astype(BF16), wpg_ref[...]))
    proj = _dot(p_ref[...].astype(BF16), wpp_ref[...])
    h3 = h2 + _rms(gate * proj, gple_ref[...])
    o_ref[...] = _rms(h3, gfin_ref[...])


def _combine_ple(yloc, slot_rows, gates, h1, p2, wpg, wpp, gple, gfin):
    n, d = h1.shape
    pd = p2.shape[1]
    per = COMBINE_STEP_TILES
    tm = per * ROUTE_TILE
    row = lambda i: (i, 0)
    col = lambda i: (0, i)
    fixed = lambda i: (0, 0)
    return pl.pallas_call(
        _combine_ple_kernel,
        grid=(n // tm,),
        in_specs=[
            pl.BlockSpec((per * LOCAL_ROWS, d // 2), row),
            pl.BlockSpec((SUBLANES, tm), col),
            pl.BlockSpec((SUBLANES, tm), col),
            pl.BlockSpec((tm, d), row),
            pl.BlockSpec((tm, pd), row),
            RESIDENT,
            RESIDENT,
            RESIDENT,
            RESIDENT,
        ],
        out_specs=pl.BlockSpec((tm, d), row),
        out_shape=jax.ShapeDtypeStruct((n, d), F32),
        compiler_params=_params(("parallel",)),
        name="combine_ple",
    )(yloc, slot_rows, gates, h1, p2, wpg, wpp, gple, gfin)


def _routing_tables(counts):
    tiles = counts.shape[0]
    local_chunks = LOCAL_ROWS // CHUNK_ROWS
    assert 2 * BLOCK_CHUNKS < tiles * SPARE_CHUNKS
    run = (counts + CHUNK_ROWS - 1) // CHUNK_ROWS
    local_start = jnp.cumsum(run, axis=1) - run
    run_end = jnp.cumsum(run, axis=0)
    run_start = run_end - run
    total = run_end[-1]
    padded = (total + BLOCK_CHUNKS - 1) // BLOCK_CHUNKS * BLOCK_CHUNKS
    pend = jnp.cumsum(padded)
    poff = pend - padded

    slots = tiles * ROUTE_TILE * TOP_K
    max_chunks = (slots + tiles * N_EXPERTS * (CHUNK_ROWS - 1)) // CHUNK_ROWS + N_EXPERTS * (BLOCK_CHUNKS - 1)
    n_chunks = -(-max_chunks // BLOCK_CHUNKS) * BLOCK_CHUNKS
    lookup = lambda onehot, table: jnp.dot(onehot, table.astype(F32), precision=lax.Precision.HIGHEST)
    q = jnp.arange(n_chunks, dtype=jnp.int32)
    e = jnp.minimum(jnp.sum(q[:, None] >= pend[None, :], axis=1), N_EXPERTS - 1).astype(jnp.int32)
    e_hot = (e[:, None] == jnp.arange(N_EXPERTS)[None, :]).astype(F32)
    per_expert = lookup(e_hot, jnp.stack([poff, total], axis=1)).astype(jnp.int32)
    j = q - per_expert[:, 0]
    valid = j < per_expert[:, 1]
    ends = lookup(e_hot, run_end.T).astype(jnp.int32)
    tile = jnp.minimum(jnp.sum(ends <= j[:, None], axis=1), tiles - 1).astype(jnp.int32)
    tile_hot = (tile[:, None] == jnp.arange(tiles)[None, :]).astype(F32)
    shift = jnp.sum(lookup(tile_hot, local_start - run_start) * e_hot, axis=1).astype(jnp.int32)
    src = jnp.where(valid, tile * local_chunks + shift + j, tiles * local_chunks - 1)
    spare = (q // BLOCK_CHUNKS % 2) * BLOCK_CHUNKS + q % BLOCK_CHUNKS
    spare_chunk = (spare // SPARE_CHUNKS) * local_chunks + (local_chunks - SPARE_CHUNKS) + spare % SPARE_CHUNKS
    dst = jnp.where(valid, tile * local_chunks + shift + j, spare_chunk)
    dst = jnp.concatenate([spare_chunk[:PRIME_BLOCKS * BLOCK_CHUNKS], dst])
    has_rows = total > 0
    ordinal = jnp.cumsum(has_rows.astype(jnp.int32)) - 1
    used_experts = jnp.argsort(jnp.logical_not(has_rows), stable=True).astype(jnp.int32)
    block_e = e[::BLOCK_CHUNKS]
    block_ord = jnp.sum(jnp.where(block_e[:, None] == jnp.arange(N_EXPERTS)[None, :], ordinal[None, :], 0), axis=1)
    n_used = jnp.stack([pend[-1] // BLOCK_CHUNKS, jnp.sum(has_rows)]).astype(jnp.int32)
    block_full = valid[BLOCK_CHUNKS // 2::BLOCK_CHUNKS].astype(jnp.int32)
    return (block_e, n_used, block_ord.astype(jnp.int32), used_experts, block_full,
            src.astype(jnp.int32), dst.astype(jnp.int32))


def kernel(x, p, positions, rel_bias_table, g_mix_norm, w_in, b_in, ret_norm_g, att_sinks, w_out,
           g_moe_norm, w_router, b_router, w_up, b_up, w_down, b_down, w_ple_gate, w_ple_proj,
           g_ple_norm, g_final):
    b, s, d = x.shape
    depth = w_in.shape[0]
    assert depth == 1, "single-layer stack only"
    n = b * s
    x2 = x.reshape(n, d)

    half = RET_DK // 2
    inv = ROPE_BASE ** (-jnp.arange(half, dtype=F32) / half)
    ang = (positions.astype(F32)[..., None] * inv).reshape(n, half)

    qr, kr, vr, gr, qa, kva = _in_proj(
        x2, g_mix_norm[0][None], w_in[0].astype(BF16), b_in[0][None], jnp.cos(ang), jnp.sin(ang))

    shape3 = lambda t: t.reshape(b, s, t.shape[-1])
    ret = _retention(shape3(qr), shape3(kr), shape3(vr), shape3(gr), ret_norm_g[0])
    att = _attention(shape3(qa), shape3(kva), att_sinks[0], rel_bias_table)

    h1, xloc, slot_rows, gates, counts = _out_router(
        ret.reshape(n, RET_WIDTH), att.reshape(n, ATT_WIDTH), x2, w_out[0].astype(BF16),
        g_moe_norm[0][None], w_router[0].T.astype(BF16), b_router[0][:, None])

    tables = _routing_tables(counts[:, 0, :].astype(jnp.int32))

    yloc = _moe(*tables, xloc, w_up[0],
                b_up[0][:, None, 0::2], b_up[0][:, None, 1::2],
                w_down[0], b_down[0].reshape(N_EXPERTS, 1, d))

    out = _combine_ple(yloc, slot_rows, gates, h1, p[0].reshape(n, -1),
                       w_ple_gate[0].astype(BF16), w_ple_proj[0].astype(BF16),
                       g_ple_norm[0][None], g_final[None])
    return out.reshape(b, s, d)
```

```python
import math

import jax
import jax.numpy as jnp
from jax import lax
from jax.experimental import pallas as pl
from jax.experimental.pallas import tpu as pltpu

RET_HEADS = 4
RET_DK = 128
RET_DV = 128
RET_CHUNK = 128
ROPE_BASE = 10000.0
ATT_Q_HEADS = 8
ATT_KV_HEADS = 2
ATT_GROUP = ATT_Q_HEADS // ATT_KV_HEADS
ATT_HEAD_DIM = 64
WINDOW = 128
ATT_BLOCK = 128
NUM_BUCKETS = 32
MAX_DISTANCE = 128
N_EXPERTS = 32
TOP_K = 4
SWIGLU_ALPHA = 1.702
SWIGLU_LIMIT = 7.0
EPS = 1e-5

RET_WIDTH = RET_HEADS * RET_DV
ATT_WIDTH = ATT_Q_HEADS * ATT_HEAD_DIM
KV_WIDTH = ATT_KV_HEADS * ATT_HEAD_DIM

VMEM_LIMIT_BYTES = 48 * 1024 * 1024

TOKEN_TILE = 1024
SUBLANES = 8
ROUTE_TILE = 256
ROUTE_STEP_TILES = 4
COMBINE_STEP_TILES = 4
CHUNK_ROWS = SUBLANES
LOCAL_ROWS = ROUTE_TILE * TOP_K + N_EXPERTS * CHUNK_ROWS
SPARE_CHUNKS = (LOCAL_ROWS - ROUTE_TILE * TOP_K - N_EXPERTS * (CHUNK_ROWS - 1)) // CHUNK_ROWS
MOE_BLOCK = 512
BLOCK_CHUNKS = MOE_BLOCK // CHUNK_ROWS
PRIME_BLOCKS = 2
ATT_STEP_BLOCKS = 8
PERM_CHUNK = 256

F32 = jnp.float32
BF16 = jnp.bfloat16
NEG = float(jnp.finfo(jnp.float32).min)


def _rms(x, g):
    return x * lax.rsqrt(jnp.mean(x * x, axis=-1, keepdims=True) + EPS) * g


def _dot(a, b):
    return jnp.dot(a, b, preferred_element_type=F32)


def _dot_nt(a, b):
    return lax.dot_general(a, b, (((1,), (1,)), ((), ())), preferred_element_type=F32)


def _dot_tn(a, b):
    return lax.dot_general(a, b, (((0,), (0,)), ((), ())), preferred_element_type=F32)


def _params(semantics):
    return pltpu.CompilerParams(dimension_semantics=semantics, vmem_limit_bytes=VMEM_LIMIT_BYTES)


RESIDENT = pl.BlockSpec(memory_space=pltpu.VMEM)


def _in_proj_kernel(x_ref, g_ref, w_ref, b_ref, cos_ref, sin_ref,
                    qr_ref, kr_ref, vr_ref, gr_ref, qa_ref, kva_ref):
    hn = _rms(x_ref[...], g_ref[...]).astype(BF16)
    cos = jnp.concatenate([cos_ref[...], cos_ref[...]], axis=1)
    sin = jnp.concatenate([-sin_ref[...], sin_ref[...]], axis=1)

    def proj(lo, hi):
        return _dot(hn, w_ref[:, lo:hi]) + b_ref[:, lo:hi]

    def rope(t):
        heads = []
        for h in range(RET_HEADS):
            th = t[:, h * RET_DK:(h + 1) * RET_DK]
            heads.append(th * cos + pltpu.roll(th, RET_DK // 2, 1) * sin)
        return jnp.concatenate(heads, axis=1)

    o = 0
    qr_ref[...] = rope(proj(o, o + RET_WIDTH)).astype(BF16)
    o += RET_WIDTH
    kr_ref[...] = (rope(proj(o, o + RET_WIDTH)) * (RET_DK ** -0.5)).astype(BF16)
    o += RET_WIDTH
    vr_ref[...] = proj(o, o + RET_WIDTH).astype(BF16)
    o += RET_WIDTH
    gr_ref[...] = proj(o, o + RET_WIDTH).astype(BF16)
    o += RET_WIDTH
    qa_ref[...] = (proj(o, o + ATT_WIDTH) * (ATT_HEAD_DIM ** -0.5)).astype(BF16)
    o += ATT_WIDTH
    low = lax.broadcasted_iota(jnp.int32, (1, KV_WIDTH), 1) < ATT_HEAD_DIM
    slabs = []
    for part in range(2):
        t = proj(o + part * KV_WIDTH, o + (part + 1) * KV_WIDTH)
        swapped = pltpu.roll(t, ATT_HEAD_DIM, 1)
        slabs += [jnp.where(low, t, swapped), jnp.where(low, swapped, t)]
    kva_ref[...] = jnp.concatenate(slabs, axis=1).astype(BF16)


def _in_proj(x2, g, w_in, b_in, cosf, sinf):
    n, d = x2.shape
    width = w_in.shape[1]
    tm = TOKEN_TILE
    row = lambda i: (i, 0)
    outs = [RET_WIDTH, RET_WIDTH, RET_WIDTH, RET_WIDTH, ATT_WIDTH, 4 * KV_WIDTH]
    return pl.pallas_call(
        _in_proj_kernel,
        grid=(n // tm,),
        in_specs=[
            pl.BlockSpec((tm, d), row),
            RESIDENT,
            RESIDENT,
            RESIDENT,
            pl.BlockSpec((tm, RET_DK // 2), row),
            pl.BlockSpec((tm, RET_DK // 2), row),
        ],
        out_specs=[pl.BlockSpec((tm, w), row) for w in outs],
        out_shape=[jax.ShapeDtypeStruct((n, w), BF16) for w in outs],
        compiler_params=_params(("parallel",)),
        name="in_proj",
    )(x2, g, w_in, b_in, cosf, sinf)


def _retention_kernel(q_ref, k_ref, v_ref, g_ref, dmask_ref, qdec_ref, kdec_ref, cdec_ref,
                      gn_ref, o_ref, state_ref):
    batch = q_ref.shape[0]

    @pl.when(pl.program_id(0) == 0)
    def _():
        state_ref[...] = jnp.zeros_like(state_ref)

    def per_batch(b, carry):
        for h in range(RET_HEADS):
            sl = slice(h * RET_DK, (h + 1) * RET_DK)
            q = q_ref[b, :, sl]
            k = k_ref[b, :, sl]
            v = v_ref[b, :, sl]
            scores = _dot_nt(q, k) * dmask_ref[h]
            inner = _dot(scores.astype(BF16), v)
            state = state_ref[b * RET_HEADS + h]
            cross = _dot(q, state.astype(BF16)) * qdec_ref[h]
            kd = (k.astype(F32) * kdec_ref[h]).astype(BF16)
            state_ref[b * RET_HEADS + h] = state * cdec_ref[h] + _dot_tn(kd, v)
            o = _rms(inner + cross, gn_ref[h])
            gate = g_ref[b, :, sl].astype(F32)
            o_ref[b, :, sl] = (gate * jax.nn.sigmoid(gate) * o).astype(BF16)
        return carry

    lax.fori_loop(0, batch, per_batch, 0, unroll=True)


def _retention(q, k, v, g, gn):
    b, s, w = q.shape
    c = RET_CHUNK
    log_g = jnp.log(1.0 - 2.0 ** (-5.0 - jnp.arange(RET_HEADS, dtype=F32)))
    ci = jnp.arange(c)
    diff = (ci[:, None] - ci[None, :]).astype(F32)
    dmask = jnp.where(diff >= 0, jnp.exp(jnp.maximum(diff, 0.0)[None] * log_g[:, None, None]), 0.0)
    cf = ci.astype(F32)
    kdec = jnp.exp((c - 1 - cf)[None, :] * log_g[:, None])
    qdec = jnp.exp((cf + 1)[None, :] * log_g[:, None])
    cdec = jnp.exp(c * log_g)
    kdec = jnp.broadcast_to(kdec[:, :, None], (RET_HEADS, c, RET_DK))
    qdec = jnp.broadcast_to(qdec[:, :, None], (RET_HEADS, c, RET_DV))
    cdec = jnp.broadcast_to(cdec[:, None, None], (RET_HEADS, RET_DK, RET_DV))
    gn3 = gn.reshape(RET_HEADS, 1, RET_DV)

    seq = lambda i: (0, i, 0)
    return pl.pallas_call(
        _retention_kernel,
        grid=(s // c,),
        in_specs=[pl.BlockSpec((b, c, w), seq)] * 4 + [RESIDENT] * 5,
        out_specs=pl.BlockSpec((b, c, w), seq),
        out_shape=jax.ShapeDtypeStruct((b, s, w), BF16),
        scratch_shapes=[pltpu.VMEM((b * RET_HEADS, RET_DK, RET_DV), F32)],
        compiler_params=_params(("arbitrary",)),
        name="retention",
    )(q, k, v, g, dmask, qdec, kdec, cdec, gn3)


def _attention_kernel(sinks_ref, q_ref, kvp_ref, kvc_ref, bias_ref, o_ref):
    blk = ATT_BLOCK
    slab = 2 * ATT_HEAD_DIM
    r = lax.broadcasted_iota(jnp.int32, (blk, blk), 0)
    c = lax.broadcasted_iota(jnp.int32, (blk, blk), 1)
    own = c <= r
    first_ok = jnp.logical_or(own, pl.program_id(1) > 0)
    low = lax.broadcasted_iota(jnp.int32, (1, slab), 1) < ATT_HEAD_DIM
    zero = jnp.zeros((), BF16)
    for i in range(q_ref.shape[1] // blk):
        rows = slice(i * blk, (i + 1) * blk)
        prev = (lambda cs: kvp_ref[0, :, cs]) if i == 0 else (
            lambda cs, rs=slice((i - 1) * blk, i * blk): kvc_ref[0, rs, cs])
        slabs = []
        for a in range(ATT_Q_HEADS // 2):
            g = (2 * a) // ATT_GROUP
            ks = slice(g * slab, (g + 1) * slab)
            vs = slice((ATT_KV_HEADS + g) * slab, (ATT_KV_HEADS + g + 1) * slab)
            k = jnp.concatenate([prev(ks), kvc_ref[0, rows, ks]], axis=0)
            v_prev, v_own = prev(vs), kvc_ref[0, rows, vs]
            q = q_ref[0, rows, a * slab:(a + 1) * slab]
            halves = []
            for part in range(2):
                h = 2 * a + part
                qz = jnp.where(low if part == 0 else jnp.logical_not(low), q, zero)
                both = _dot_nt(qz, k)
                s = jnp.where(own, both[:, blk:], both[:, :blk]) + bias_ref[h]
                if i == 0:
                    s = jnp.where(first_ok, s, NEG)
                sink = sinks_ref[h]
                m = jnp.maximum(jnp.max(s, axis=1, keepdims=True), sink)
                p = jnp.exp(s - m)
                denom = jnp.sum(p, axis=1, keepdims=True) + jnp.exp(sink - m)
                p_own = jnp.where(own, p, 0.0).astype(BF16)
                p_prev = jnp.where(own, 0.0, p).astype(BF16)
                halves.append((_dot(p_own, v_own) + _dot(p_prev, v_prev)) / denom)
            slabs.append(jnp.where(low, halves[0], halves[1]))
        o_ref[0, rows, :] = jnp.concatenate(slabs, axis=1).astype(BF16)


def _t5_bucket(n):
    max_exact = NUM_BUCKETS // 2
    nf = jnp.maximum(n, 1).astype(F32)
    large = max_exact + (jnp.log(nf / max_exact) / math.log(MAX_DISTANCE / max_exact)
                         * (NUM_BUCKETS - max_exact)).astype(jnp.int32)
    large = jnp.minimum(large, NUM_BUCKETS - 1)
    return jnp.where(n < max_exact, n, large)


def _attention(q, kv, sinks, bias_table):
    b, s, _ = q.shape
    blk = ATT_BLOCK
    assert WINDOW == blk
    qi = jnp.arange(blk)[:, None]
    ki = jnp.arange(blk)[None, :]
    dist = jnp.where(ki <= qi, qi - ki, qi + blk - ki)
    onehot = jax.nn.one_hot(_t5_bucket(dist), NUM_BUCKETS, dtype=F32)
    bias = jnp.einsum("qkn,nh->hqk", onehot, bias_table.astype(F32), precision=lax.Precision.HIGHEST)

    sub = ATT_STEP_BLOCKS
    kvw = kv.shape[-1]
    return pl.pallas_call(
        _attention_kernel,
        grid=(b, s // (sub * blk)),
        in_specs=[
            pl.BlockSpec(memory_space=pltpu.SMEM),
            pl.BlockSpec((1, sub * blk, ATT_WIDTH), lambda i, j: (i, j, 0)),
            pl.BlockSpec((1, blk, kvw), lambda i, j: (i, jnp.maximum(sub * j - 1, 0), 0)),
            pl.BlockSpec((1, sub * blk, kvw), lambda i, j: (i, j, 0)),
            RESIDENT,
        ],
        out_specs=pl.BlockSpec((1, sub * blk, ATT_WIDTH), lambda i, j: (i, j, 0)),
        out_shape=jax.ShapeDtypeStruct((b, s, ATT_WIDTH), BF16),
        compiler_params=_params(("parallel", "arbitrary")),
        name="attention",
    )(sinks, q, kv, kv, bias)


def _pack_bf16_pairs(lo, hi):
    return pltpu.bitcast(pltpu.pack_elementwise([lo, hi], packed_dtype=BF16), jnp.uint32)


def _unpack_bf16_pairs(packed):
    lo = pltpu.bitcast(packed << 16, F32).astype(BF16)
    hi = pltpu.bitcast(packed & jnp.uint32(0xFFFF0000), F32).astype(BF16)
    return lo, hi


def _stack_rows(rows, dtype):
    t = rows[0].shape[1]
    sub = lax.broadcasted_iota(jnp.int32, (SUBLANES, t), 0)
    out = jnp.zeros((SUBLANES, t), dtype)
    for i, r in enumerate(rows):
        out = jnp.where(sub == i, r.astype(dtype), out)
    return out


def _route_tile(logits):
    t = logits.shape[1]
    expert = lax.broadcasted_iota(jnp.int32, logits.shape, 0)
    vals = logits
    top_v, sels = [], []
    for _ in range(TOP_K):
        m = jnp.max(vals, axis=0, keepdims=True)
        idx = jnp.min(jnp.where(vals == m, expert, N_EXPERTS), axis=0, keepdims=True)
        sel = expert == idx
        vals = jnp.where(sel, -jnp.inf, vals)
        top_v.append(m)
        sels.append(sel)
    exps = [jnp.exp(v - top_v[0]) for v in top_v]
    denom = exps[0] + exps[1] + exps[2] + exps[3]
    gates = [e / denom for e in exps]

    chosen = jnp.zeros(logits.shape, F32)
    for sel in sels:
        chosen = chosen + sel.astype(F32)
    chosen_b = chosen.astype(BF16)
    r = lax.broadcasted_iota(jnp.int32, (t, t), 0)
    c = lax.broadcasted_iota(jnp.int32, (t, t), 1)
    earlier = jnp.where(r < c, 1.0, 0.0).astype(BF16)
    before = _dot(chosen_b, earlier)
    counts = jnp.sum(chosen, axis=1, keepdims=True)
    chunks = jnp.floor((counts + (CHUNK_ROWS - 1)) * (1.0 / CHUNK_ROWS))
    er = lax.broadcasted_iota(jnp.int32, (N_EXPERTS, N_EXPERTS), 0)
    ec = lax.broadcasted_iota(jnp.int32, (N_EXPERTS, N_EXPERTS), 1)
    lower = jnp.where(ec < er, 1.0, 0.0).astype(BF16)
    start = _dot(lower, jnp.broadcast_to(chunks, logits.shape).astype(BF16)) * float(CHUNK_ROWS)
    rows = [jnp.sum(jnp.where(sel, before + start, 0.0), axis=0, keepdims=True).astype(jnp.int32)
            for sel in sels]
    counts_rows = _dot_nt(jnp.ones((SUBLANES, t), BF16), chosen_b)

    local = lax.broadcasted_iota(jnp.int32, (LOCAL_ROWS, 1), 0)
    select = jnp.zeros((LOCAL_ROWS, t), F32)
    for row in rows:
        select = jnp.where(local == row, 1.0, select)
    return rows, gates, counts_rows, select.astype(BF16)


def _out_router_kernel(ret_ref, att_ref, x_ref, wo_ref, g_ref, wrt_ref, brt_ref,
                       h1_ref, xloc_ref, slot_ref, gate_ref, cnt_ref):
    half = x_ref.shape[1] // 2
    h1 = (x_ref[...] + _dot(ret_ref[...], wo_ref[:RET_WIDTH, :])
          + _dot(att_ref[...], wo_ref[RET_WIDTH:, :]))
    h1_ref[...] = h1
    xb_all = _rms(h1, g_ref[...]).astype(BF16)
    logits_all = _dot_nt(wrt_ref[...], xb_all) + brt_ref[...]
    for s in range(x_ref.shape[0] // ROUTE_TILE):
        tok = slice(s * ROUTE_TILE, (s + 1) * ROUTE_TILE)
        loc = slice(s * LOCAL_ROWS, (s + 1) * LOCAL_ROWS)
        xb = xb_all[tok, :]
        rows, gates, counts_rows, select = _route_tile(logits_all[:, tok])
        xloc = _dot(select, xb)
        xloc_ref[loc, :] = _pack_bf16_pairs(xloc[:, :half], xloc[:, half:])
        slot_ref[:, tok] = _stack_rows(rows, jnp.int32)
        gate_ref[:, tok] = _stack_rows(gates, F32)
        cnt_ref[s] = counts_rows


def _out_router(ret, att, x2, w_out, g, w_router_t, b_router_t):
    n, d = x2.shape
    per = ROUTE_STEP_TILES
    tm = per * ROUTE_TILE
    tiles = n // ROUTE_TILE
    row = lambda i: (i, 0)
    col = lambda i: (0, i)
    fixed = lambda i: (0, 0)
    return pl.pallas_call(
        _out_router_kernel,
        grid=(n // tm,),
        in_specs=[
            pl.BlockSpec((tm, RET_WIDTH), row),
            pl.BlockSpec((tm, ATT_WIDTH), row),
            pl.BlockSpec((tm, d), row),
            RESIDENT,
            RESIDENT,
            RESIDENT,
            RESIDENT,
        ],
        out_specs=[
            pl.BlockSpec((tm, d), row),
            pl.BlockSpec((per * LOCAL_ROWS, d // 2), row),
            pl.BlockSpec((SUBLANES, tm), col),
            pl.BlockSpec((SUBLANES, tm), col),
            pl.BlockSpec((per, SUBLANES, N_EXPERTS), lambda i: (i, 0, 0)),
        ],
        out_shape=[
            jax.ShapeDtypeStruct((n, d), F32),
            jax.ShapeDtypeStruct((tiles * LOCAL_ROWS, d // 2), jnp.uint32),
            jax.ShapeDtypeStruct((SUBLANES, n), jnp.int32),
            jax.ShapeDtypeStruct((SUBLANES, n), F32),
            jax.ShapeDtypeStruct((tiles, SUBLANES, N_EXPERTS), F32),
        ],
        compiler_params=_params(("parallel",)),
        name="out_router",
    )(ret, att, x2, w_out, g, w_router_t, b_router_t)


def _convert_expert_weights(wu_ref, wdn_ref, wg_ref, wl_ref, wd_ref):
    ch = PERM_CHUNK
    hc = ch // 2
    r = lax.broadcasted_iota(jnp.int32, (ch, ch), 0)
    c = lax.broadcasted_iota(jnp.int32, (ch, ch), 1)
    perm = jnp.where(r == jnp.where(c < hc, 2 * c, 2 * (c - hc) + 1), 1.0, 0.0).astype(BF16)
    for j in range(wu_ref.shape[1] // ch):
        both = _dot(wu_ref[:, j * ch:(j + 1) * ch].astype(BF16), perm)
        wg_ref[:, j * hc:(j + 1) * hc] = both[:, :hc].astype(BF16)
        wl_ref[:, j * hc:(j + 1) * hc] = both[:, hc:].astype(BF16)
    wd_ref[...] = wdn_ref[...].astype(BF16)


def _moe_kernel(be_ref, nused_ref, ord_ref, ue_ref, full_ref, src_ref, dst_ref,
                xin_hbm, wu_hbm, bg_ref, bl_ref, wdn_hbm, bd_ref,
                loc_hbm, xbuf, ybuf, gsem, ssem, wu_buf, wdn_buf, wsem, wg_ref, wl_ref, wd_ref):
    del xin_hbm
    i = pl.program_id(0)
    n_used = nused_ref[0]
    last = pl.num_programs(0) - 1
    slot = lax.rem(i, 2)
    bm, half = xbuf.shape[1], xbuf.shape[2]

    def chunk(ref, c):
        first = c * CHUNK_ROWS if isinstance(c, int) else pl.multiple_of(c * CHUNK_ROWS, CHUNK_ROWS)
        return ref.at[pl.ds(first, CHUNK_ROWS), :]

    def start_gather(step, sl):
        for j in range(BLOCK_CHUNKS):
            pltpu.make_async_copy(chunk(loc_hbm, src_ref[step * BLOCK_CHUNKS + j]),
                                  chunk(xbuf.at[sl], j), gsem.at[sl]).start()

    def start_writeback(step, sl):
        for j in range(BLOCK_CHUNKS):
            dst = dst_ref[(step + PRIME_BLOCKS) * BLOCK_CHUNKS + j]
            pltpu.make_async_copy(chunk(ybuf.at[sl], j), chunk(loc_hbm, dst), ssem.at[sl]).start()

    def wait_gather(sl):
        pltpu.make_async_copy(loc_hbm.at[pl.ds(0, bm), :], xbuf.at[sl], gsem.at[sl]).wait()

    def wait_writeback(sl):
        pltpu.make_async_copy(ybuf.at[sl], loc_hbm.at[pl.ds(0, bm), :], ssem.at[sl]).wait()

    def weight_fetch(expert, buf):
        return (pltpu.make_async_copy(wu_hbm.at[expert], wu_buf.at[buf], wsem.at[buf, 0]),
                pltpu.make_async_copy(wdn_hbm.at[expert], wdn_buf.at[buf], wsem.at[buf, 1]))

    @pl.when(i == 0)
    def _():
        ybuf[...] = jnp.zeros_like(ybuf)
        start_gather(0, 0)
        start_writeback(-2, 0)
        for copy in weight_fetch(ue_ref[0], 0):
            copy.start()

    new_expert = jnp.logical_or(i == 0, be_ref[i] != be_ref[jnp.maximum(i - 1, 0)])

    @pl.when(jnp.logical_and(i < n_used, new_expert))
    def _():
        k = ord_ref[i]
        buf = lax.rem(k, 2)
        for copy in weight_fetch(be_ref[i], buf):
            copy.wait()

        @pl.when(k + 1 < nused_ref[1])
        def _():
            for copy in weight_fetch(ue_ref[k + 1], 1 - buf):
                copy.start()

        _convert_expert_weights(wu_buf.at[buf], wdn_buf.at[buf], wg_ref, wl_ref, wd_ref)

    def block_step(rows):
        wait_gather(slot)
        start_gather(jnp.minimum(i + 1, last), 1 - slot)
        x = jnp.concatenate(_unpack_bf16_pairs(xbuf[slot, :rows, :]), axis=1)
        start_writeback(i - 1, 1 - slot)

        expert = be_ref[i]

        def up(w_ref, b_ref):
            return _dot(x, w_ref[...]) + b_ref[expert]

        glu = jnp.minimum(up(wg_ref, bg_ref), SWIGLU_LIMIT)
        lin = jnp.clip(up(wl_ref, bl_ref), -SWIGLU_LIMIT, SWIGLU_LIMIT)
        act = glu * jax.nn.sigmoid(SWIGLU_ALPHA * glu) * (lin + 1.0)
        y = _dot(act.astype(BF16), wd_ref[...]) + bd_ref[expert]

        wait_writeback(slot)
        ybuf[slot, :rows, :] = _pack_bf16_pairs(y[:, :half], y[:, half:])

    used = i < n_used
    full = full_ref[i] > 0

    @pl.when(jnp.logical_and(used, full))
    def _():
        block_step(bm)

    @pl.when(jnp.logical_and(used, jnp.logical_not(full)))
    def _():
        block_step(bm // 2)

    @pl.when(i == n_used - 1)
    def _():
        start_writeback(i, slot)
        wait_writeback(1 - slot)
        wait_writeback(slot)
        wait_gather(1 - slot)


def _moe(block_e, n_used, block_ord, used_experts, block_full, chunk_src, chunk_dst,
         xloc, w_up, bg, bl, w_down, bd):
    half = xloc.shape[1]
    e, d, f2 = w_up.shape
    f = f2 // 2
    bm = MOE_BLOCK
    n_blocks = chunk_src.shape[0] // BLOCK_CHUNKS
    assert chunk_dst.shape[0] == (n_blocks + PRIME_BLOCKS) * BLOCK_CHUNKS
    prefetch = (block_e, n_used, block_ord, used_experts, block_full, chunk_src, chunk_dst)
    grid_spec = pltpu.PrefetchScalarGridSpec(
        num_scalar_prefetch=len(prefetch),
        grid=(n_blocks,),
        in_specs=[
            pl.BlockSpec(memory_space=pl.ANY),
            pl.BlockSpec(memory_space=pl.ANY),
            RESIDENT,
            RESIDENT,
            pl.BlockSpec(memory_space=pl.ANY),
            RESIDENT,
        ],
        out_specs=pl.BlockSpec(memory_space=pl.ANY),
        scratch_shapes=[
            pltpu.VMEM((2, bm, half), jnp.uint32),
            pltpu.VMEM((2, bm, half), jnp.uint32),
            pltpu.SemaphoreType.DMA((2,)),
            pltpu.SemaphoreType.DMA((2,)),
            pltpu.VMEM((2, d, f2), F32),
            pltpu.VMEM((2, f, d), F32),
            pltpu.SemaphoreType.DMA((2, 2)),
            pltpu.VMEM((d, f), BF16),
            pltpu.VMEM((d, f), BF16),
            pltpu.VMEM((f, d), BF16),
        ],
    )
    return pl.pallas_call(
        _moe_kernel,
        grid_spec=grid_spec,
        out_shape=jax.ShapeDtypeStruct(xloc.shape, jnp.uint32),
        input_output_aliases={len(prefetch): 0},
        compiler_params=_params(("arbitrary",)),
        name="moe",
    )(*prefetch, xloc, w_up, bg, bl, w_down, bd)


def _combine_ple_kernel(yloc_ref, slot_ref, gate_ref, h1_ref, p_ref, wpg_ref, wpp_ref, gple_ref,
                        gfin_ref, o_ref):
    local = lax.broadcasted_iota(jnp.int32, (LOCAL_ROWS, 1), 0)
    moe = []
    for s in range(h1_ref.shape[0] // ROUTE_TILE):
        tok = slice(s * ROUTE_TILE, (s + 1) * ROUTE_TILE)
        loc = slice(s * LOCAL_ROWS, (s + 1) * LOCAL_ROWS)
        rows = slot_ref[:, tok]
        gates = gate_ref[:, tok]
        weights = jnp.zeros((LOCAL_ROWS, ROUTE_TILE), F32)
        for k in range(TOP_K):
            weights = jnp.where(local == rows[k:k + 1, :], gates[k:k + 1, :], weights)
        weights = weights.astype(BF16)
        y_lo, y_hi = _unpack_bf16_pairs(yloc_ref[loc, :])
        moe.append(jnp.concatenate([_dot_tn(y_lo, weights), _dot_tn(y_hi, weights)], axis=0).T)
    h2 = h1_ref[...] + jnp.concatenate(moe, axis=0)
    gate = jax.nn.sigmoid(_dot(h2.astype(BF16), wpg_ref[...]))
    proj = _dot(p_ref[...].astype(BF16), wpp_ref[...])
    h3 = h2 + _rms(gate * proj, gple_ref[...])
    o_ref[...] = _rms(h3, gfin_ref[...])


def _combine_ple(yloc, slot_rows, gates, h1, p2, wpg, wpp, gple, gfin):
    n, d = h1.shape
    pd = p2.shape[1]
    per = COMBINE_STEP_TILES
    tm = per * ROUTE_TILE
    row = lambda i: (i, 0)
    col = lambda i: (0, i)
    fixed = lambda i: (0, 0)
    return pl.pallas_call(
        _combine_ple_kernel,
        grid=(n // tm,),
        in_specs=[
            pl.BlockSpec((per * LOCAL_ROWS, d // 2), row),
            pl.BlockSpec((SUBLANES, tm), col),
            pl.BlockSpec((SUBLANES, tm), col),
            pl.BlockSpec((tm, d), row),
            pl.BlockSpec((tm, pd), row),
            RESIDENT,
            RESIDENT,
            RESIDENT,
            RESIDENT,
        ],
        out_specs=pl.BlockSpec((tm, d), row),
        out_shape=jax.ShapeDtypeStruct((n, d), F32),
        compiler_params=_params(("parallel",)),
        name="combine_ple",
    )(yloc, slot_rows, gates, h1, p2, wpg, wpp, gple, gfin)


def _routing_tables(counts):
    tiles = counts.shape[0]
    local_chunks = LOCAL_ROWS // CHUNK_ROWS
    assert 2 * BLOCK_CHUNKS < tiles * SPARE_CHUNKS
    run = (counts + CHUNK_ROWS - 1) // CHUNK_ROWS
    local_start = jnp.cumsum(run, axis=1) - run
    run_end = jnp.cumsum(run, axis=0)
    run_start = run_end - run
    total = run_end[-1]
    padded = (total + BLOCK_CHUNKS - 1) // BLOCK_CHUNKS * BLOCK_CHUNKS
    pend = jnp.cumsum(padded)
    poff = pend - padded

    slots = tiles * ROUTE_TILE * TOP_K
    max_chunks = (slots + tiles * N_EXPERTS * (CHUNK_ROWS - 1)) // CHUNK_ROWS + N_EXPERTS * (BLOCK_CHUNKS - 1)
    n_chunks = -(-max_chunks // BLOCK_CHUNKS) * BLOCK_CHUNKS
    lookup = lambda onehot, table: jnp.dot(onehot, table.astype(F32), precision=lax.Precision.HIGHEST)
    q = jnp.arange(n_chunks, dtype=jnp.int32)
    e = jnp.minimum(jnp.sum(q[:, None] >= pend[None, :], axis=1), N_EXPERTS - 1).astype(jnp.int32)
    e_hot = (e[:, None] == jnp.arange(N_EXPERTS)[None, :]).astype(F32)
    per_expert = lookup(e_hot, jnp.stack([poff, total], axis=1)).astype(jnp.int32)
    j = q - per_expert[:, 0]
    valid = j < per_expert[:, 1]
    ends = lookup(e_hot, run_end.T).astype(jnp.int32)
    tile = jnp.minimum(jnp.sum(ends <= j[:, None], axis=1), tiles - 1).astype(jnp.int32)
    tile_hot = (tile[:, None] == jnp.arange(tiles)[None, :]).astype(F32)
    shift = jnp.sum(lookup(tile_hot, local_start - run_start) * e_hot, axis=1).astype(jnp.int32)
    src = jnp.where(valid, tile * local_chunks + shift + j, tiles * local_chunks - 1)
    spare = (q // BLOCK_CHUNKS % 2) * BLOCK_CHUNKS + q % BLOCK_CHUNKS
    spare_chunk = (spare // SPARE_CHUNKS) * local_chunks + (local_chunks - SPARE_CHUNKS) + spare % SPARE_CHUNKS
    dst = jnp.where(valid, tile * local_chunks + shift + j, spare_chunk)
    dst = jnp.concatenate([spare_chunk[:PRIME_BLOCKS * BLOCK_CHUNKS], dst])
    has_rows = total > 0
    ordinal = jnp.cumsum(has_rows.astype(jnp.int32)) - 1
    used_experts = jnp.argsort(jnp.logical_not(has_rows), stable=True).astype(jnp.int32)
    block_e = e[::BLOCK_CHUNKS]
    block_ord = jnp.sum(jnp.where(block_e[:, None] == jnp.arange(N_EXPERTS)[None, :], ordinal[None, :], 0), axis=1)
    n_used = jnp.stack([pend[-1] // BLOCK_CHUNKS, jnp.sum(has_rows)]).astype(jnp.int32)
    block_full = valid[BLOCK_CHUNKS // 2::BLOCK_CHUNKS].astype(jnp.int32)
    return (block_e, n_used, block_ord.astype(jnp.int32), used_experts, block_full,
            src.astype(jnp.int32), dst.astype(jnp.int32))


def kernel(x, p, positions, rel_bias_table, g_mix_norm, w_in, b_in, ret_norm_g, att_sinks, w_out,
           g_moe_norm, w_router, b_router, w_up, b_up, w_down, b_down, w_ple_gate, w_ple_proj,
           g_ple_norm, g_final):
    b, s, d = x.shape
    depth = w_in.shape[0]
    assert depth == 1, "single-layer stack only"
    n = b * s
    x2 = x.reshape(n, d)

    half = RET_DK // 2
    inv = ROPE_BASE ** (-jnp.arange(half, dtype=F32) / half)
    ang = (positions.astype(F32)[..., None] * inv).reshape(n, half)

    qr, kr, vr, gr, qa, kva = _in_proj(
        x2, g_mix_norm[0][None], w_in[0].astype(BF16), b_in[0][None], jnp.cos(ang), jnp.sin(ang))

    shape3 = lambda t: t.reshape(b, s, t.shape[-1])
    ret = _retention(shape3(qr), shape3(kr), shape3(vr), shape3(gr), ret_norm_g[0])
    att = _attention(shape3(qa), shape3(kva), att_sinks[0], rel_bias_table)

    h1, xloc, slot_rows, gates, counts = _out_router(
        ret.reshape(n, RET_WIDTH), att.reshape(n, ATT_WIDTH), x2, w_out[0].astype(BF16),
        g_moe_norm[0][None], w_router[0].T.astype(BF16), b_router[0][:, None])

    tables = _routing_tables(counts[:, 0, :].astype(jnp.int32))

    yloc = _moe(*tables, xloc, w_up[0],
                b_up[0][:, None, 0::2], b_up[0][:, None, 1::2],
                w_down[0], b_down[0].reshape(N_EXPERTS, 1, d))

    out = _combine_ple(yloc, slot_rows, gates, h1, p[0].reshape(n, -1),
                       w_ple_gate[0].astype(BF16), w_ple_proj[0].astype(BF16),
                       g_ple_norm[0][None], g_final[None])
    return out.reshape(b, s, d)
```

```python
import math

import jax
import jax.numpy as jnp
from jax import lax
from jax.experimental import pallas as pl
from jax.experimental.pallas import tpu as pltpu

RET_HEADS = 4
RET_DK = 128
RET_DV = 128
RET_CHUNK = 128
ROPE_BASE = 10000.0
ATT_Q_HEADS = 8
ATT_KV_HEADS = 2
ATT_GROUP = ATT_Q_HEADS // ATT_KV_HEADS
ATT_HEAD_DIM = 64
WINDOW = 128
ATT_BLOCK = 128
NUM_BUCKETS = 32
MAX_DISTANCE = 128
N_EXPERTS = 32
TOP_K = 4
SWIGLU_ALPHA = 1.702
SWIGLU_LIMIT = 7.0
EPS = 1e-5

RET_WIDTH = RET_HEADS * RET_DV
ATT_WIDTH = ATT_Q_HEADS * ATT_HEAD_DIM
KV_WIDTH = ATT_KV_HEADS * ATT_HEAD_DIM

VMEM_LIMIT_BYTES = 48 * 1024 * 1024

TOKEN_TILE = 1024
SUBLANES = 8
ROUTE_TILE = 256
ROUTE_STEP_TILES = 4
COMBINE_STEP_TILES = 4
CHUNK_ROWS = SUBLANES
LOCAL_ROWS = ROUTE_TILE * TOP_K + N_EXPERTS * CHUNK_ROWS
SPARE_CHUNKS = (LOCAL_ROWS - ROUTE_TILE * TOP_K - N_EXPERTS * (CHUNK_ROWS - 1)) // CHUNK_ROWS
MOE_BLOCK = 512
BLOCK_CHUNKS = MOE_BLOCK // CHUNK_ROWS
PRIME_BLOCKS = 2
ATT_STEP_BLOCKS = 16
PERM_CHUNK = 256

F32 = jnp.float32
BF16 = jnp.bfloat16
NEG = float(jnp.finfo(jnp.float32).min)


def _rms(x, g):
    return x * lax.rsqrt(jnp.mean(x * x, axis=-1, keepdims=True) + EPS) * g


def _dot(a, b):
    return jnp.dot(a, b, preferred_element_type=F32)


def _dot_nt(a, b):
    return lax.dot_general(a, b, (((1,), (1,)), ((), ())), preferred_element_type=F32)


def _dot_tn(a, b):
    return lax.dot_general(a, b, (((0,), (0,)), ((), ())), preferred_element_type=F32)


def _params(semantics):
    return pltpu.CompilerParams(dimension_semantics=semantics, vmem_limit_bytes=VMEM_LIMIT_BYTES)


RESIDENT = pl.BlockSpec(memory_space=pltpu.VMEM)


def _in_proj_kernel(x_ref, g_ref, w_ref, b_ref, cos_ref, sin_ref,
                    qr_ref, kr_ref, vr_ref, gr_ref, qa_ref, kva_ref):
    hn = _rms(x_ref[...], g_ref[...]).astype(BF16)
    cos = jnp.concatenate([cos_ref[...], cos_ref[...]], axis=1)
    sin = jnp.concatenate([-sin_ref[...], sin_ref[...]], axis=1)

    def proj(lo, hi):
        return _dot(hn, w_ref[:, lo:hi]) + b_ref[:, lo:hi]

    def rope(t):
        heads = []
        for h in range(RET_HEADS):
            th = t[:, h * RET_DK:(h + 1) * RET_DK]
            heads.append(th * cos + pltpu.roll(th, RET_DK // 2, 1) * sin)
        return jnp.concatenate(heads, axis=1)

    o = 0
    qr_ref[...] = rope(proj(o, o + RET_WIDTH)).astype(BF16)
    o += RET_WIDTH
    kr_ref[...] = (rope(proj(o, o + RET_WIDTH)) * (RET_DK ** -0.5)).astype(BF16)
    o += RET_WIDTH
    vr_ref[...] = proj(o, o + RET_WIDTH).astype(BF16)
    o += RET_WIDTH
    gr_ref[...] = proj(o, o + RET_WIDTH).astype(BF16)
    o += RET_WIDTH
    qa_ref[...] = (proj(o, o + ATT_WIDTH) * (ATT_HEAD_DIM ** -0.5)).astype(BF16)
    o += ATT_WIDTH
    low = lax.broadcasted_iota(jnp.int32, (1, KV_WIDTH), 1) < ATT_HEAD_DIM
    slabs = []
    for part in range(2):
        t = proj(o + part * KV_WIDTH, o + (part + 1) * KV_WIDTH)
        swapped = pltpu.roll(t, ATT_HEAD_DIM, 1)
        slabs += [jnp.where(low, t, swapped), jnp.where(low, swapped, t)]
    kva_ref[...] = jnp.concatenate(slabs, axis=1).astype(BF16)


def _in_proj(x2, g, w_in, b_in, cosf, sinf):
    n, d = x2.shape
    width = w_in.shape[1]
    tm = TOKEN_TILE
    row = lambda i: (i, 0)
    outs = [RET_WIDTH, RET_WIDTH, RET_WIDTH, RET_WIDTH, ATT_WIDTH, 4 * KV_WIDTH]
    return pl.pallas_call(
        _in_proj_kernel,
        grid=(n // tm,),
        in_specs=[
            pl.BlockSpec((tm, d), row),
            RESIDENT,
            RESIDENT,
            RESIDENT,
            pl.BlockSpec((tm, RET_DK // 2), row),
            pl.BlockSpec((tm, RET_DK // 2), row),
        ],
        out_specs=[pl.BlockSpec((tm, w), row) for w in outs],
        out_shape=[jax.ShapeDtypeStruct((n, w), BF16) for w in outs],
        compiler_params=_params(("parallel",)),
        name="in_proj",
    )(x2, g, w_in, b_in, cosf, sinf)


def _retention_kernel(q_ref, k_ref, v_ref, g_ref, dmask_ref, qdec_ref, kdec_ref, cdec_ref,
                      gn_ref, o_ref, state_ref):
    batch = q_ref.shape[0]

    @pl.when(pl.program_id(0) == 0)
    def _():
        state_ref[...] = jnp.zeros_like(state_ref)

    def per_batch(b, carry):
        for h in range(RET_HEADS):
            sl = slice(h * RET_DK, (h + 1) * RET_DK)
            q = q_ref[b, :, sl]
            k = k_ref[b, :, sl]
            v = v_ref[b, :, sl]
            scores = _dot_nt(q, k) * dmask_ref[h]
            inner = _dot(scores.astype(BF16), v)
            state = state_ref[b * RET_HEADS + h]
            cross = _dot(q, state.astype(BF16)) * qdec_ref[h]
            kd = (k.astype(F32) * kdec_ref[h]).astype(BF16)
            state_ref[b * RET_HEADS + h] = state * cdec_ref[h] + _dot_tn(kd, v)
            o = _rms(inner + cross, gn_ref[h])
            gate = g_ref[b, :, sl].astype(F32)
            o_ref[b, :, sl] = (gate * jax.nn.sigmoid(gate) * o).astype(BF16)
        return carry

    lax.fori_loop(0, batch, per_batch, 0, unroll=True)


def _retention(q, k, v, g, gn):
    b, s, w = q.shape
    c = RET_CHUNK
    log_g = jnp.log(1.0 - 2.0 ** (-5.0 - jnp.arange(RET_HEADS, dtype=F32)))
    ci = jnp.arange(c)
    diff = (ci[:, None] - ci[None, :]).astype(F32)
    dmask = jnp.where(diff >= 0, jnp.exp(jnp.maximum(diff, 0.0)[None] * log_g[:, None, None]), 0.0)
    cf = ci.astype(F32)
    kdec = jnp.exp((c - 1 - cf)[None, :] * log_g[:, None])
    qdec = jnp.exp((cf + 1)[None, :] * log_g[:, None])
    cdec = jnp.exp(c * log_g)
    kdec = jnp.broadcast_to(kdec[:, :, None], (RET_HEADS, c, RET_DK))
    qdec = jnp.broadcast_to(qdec[:, :, None], (RET_HEADS, c, RET_DV))
    cdec = jnp.broadcast_to(cdec[:, None, None], (RET_HEADS, RET_DK, RET_DV))
    gn3 = gn.reshape(RET_HEADS, 1, RET_DV)

    seq = lambda i: (0, i, 0)
    return pl.pallas_call(
        _retention_kernel,
        grid=(s // c,),
        in_specs=[pl.BlockSpec((b, c, w), seq)] * 4 + [RESIDENT] * 5,
        out_specs=pl.BlockSpec((b, c, w), seq),
        out_shape=jax.ShapeDtypeStruct((b, s, w), BF16),
        scratch_shapes=[pltpu.VMEM((b * RET_HEADS, RET_DK, RET_DV), F32)],
        compiler_params=_params(("arbitrary",)),
        name="retention",
    )(q, k, v, g, dmask, qdec, kdec, cdec, gn3)


def _attention_kernel(sinks_ref, q_ref, kvp_ref, kvc_ref, bias_ref, o_ref):
    blk = ATT_BLOCK
    slab = 2 * ATT_HEAD_DIM
    r = lax.broadcasted_iota(jnp.int32, (blk, blk), 0)
    c = lax.broadcasted_iota(jnp.int32, (blk, blk), 1)
    own = c <= r
    first_ok = jnp.logical_or(own, pl.program_id(1) > 0)
    low = lax.broadcasted_iota(jnp.int32, (1, slab), 1) < ATT_HEAD_DIM
    zero = jnp.zeros((), BF16)
    for i in range(q_ref.shape[1] // blk):
        rows = slice(i * blk, (i + 1) * blk)
        prev = (lambda cs: kvp_ref[0, :, cs]) if i == 0 else (
            lambda cs, rs=slice((i - 1) * blk, i * blk): kvc_ref[0, rs, cs])
        slabs = []
        for a in range(ATT_Q_HEADS // 2):
            g = (2 * a) // ATT_GROUP
            ks = slice(g * slab, (g + 1) * slab)
            vs = slice((ATT_KV_HEADS + g) * slab, (ATT_KV_HEADS + g + 1) * slab)
            k = jnp.concatenate([prev(ks), kvc_ref[0, rows, ks]], axis=0)
            v_prev, v_own = prev(vs), kvc_ref[0, rows, vs]
            q = q_ref[0, rows, a * slab:(a + 1) * slab]
            halves = []
            for part in range(2):
                h = 2 * a + part
                qz = jnp.where(low if part == 0 else jnp.logical_not(low), q, zero)
                both = _dot_nt(qz, k)
                s = jnp.where(own, both[:, blk:], both[:, :blk]) + bias_ref[h]
                if i == 0:
                    s = jnp.where(first_ok, s, NEG)
                sink = sinks_ref[h]
                m = jnp.maximum(jnp.max(s, axis=1, keepdims=True), sink)
                p = jnp.exp(s - m)
                denom = jnp.sum(p, axis=1, keepdims=True) + jnp.exp(sink - m)
                p_own = jnp.where(own, p, 0.0).astype(BF16)
                p_prev = jnp.where(own, 0.0, p).astype(BF16)
                halves.append((_dot(p_own, v_own) + _dot(p_prev, v_prev)) / denom)
            slabs.append(jnp.where(low, halves[0], halves[1]))
        o_ref[0, rows, :] = jnp.concatenate(slabs, axis=1).astype(BF16)


def _t5_bucket(n):
    max_exact = NUM_BUCKETS // 2
    nf = jnp.maximum(n, 1).astype(F32)
    large = max_exact + (jnp.log(nf / max_exact) / math.log(MAX_DISTANCE / max_exact)
                         * (NUM_BUCKETS - max_exact)).astype(jnp.int32)
    large = jnp.minimum(large, NUM_BUCKETS - 1)
    return jnp.where(n < max_exact, n, large)


def _attention(q, kv, sinks, bias_table):
    b, s, _ = q.shape
    blk = ATT_BLOCK
    assert WINDOW == blk
    qi = jnp.arange(blk)[:, None]
    ki = jnp.arange(blk)[None, :]
    dist = jnp.where(ki <= qi, qi - ki, qi + blk - ki)
    onehot = jax.nn.one_hot(_t5_bucket(dist), NUM_BUCKETS, dtype=F32)
    bias = jnp.einsum("qkn,nh->hqk", onehot, bias_table.astype(F32), precision=lax.Precision.HIGHEST)

    sub = ATT_STEP_BLOCKS
    kvw = kv.shape[-1]
    return pl.pallas_call(
        _attention_kernel,
        grid=(b, s // (sub * blk)),
        in_specs=[
            pl.BlockSpec(memory_space=pltpu.SMEM),
            pl.BlockSpec((1, sub * blk, ATT_WIDTH), lambda i, j: (i, j, 0)),
            pl.BlockSpec((1, blk, kvw), lambda i, j: (i, jnp.maximum(sub * j - 1, 0), 0)),
            pl.BlockSpec((1, sub * blk, kvw), lambda i, j: (i, j, 0)),
            RESIDENT,
        ],
        out_specs=pl.BlockSpec((1, sub * blk, ATT_WIDTH), lambda i, j: (i, j, 0)),
        out_shape=jax.ShapeDtypeStruct((b, s, ATT_WIDTH), BF16),
        compiler_params=_params(("parallel", "arbitrary")),
        name="attention",
    )(sinks, q, kv, kv, bias)


def _pack_bf16_pairs(lo, hi):
    return pltpu.bitcast(pltpu.pack_elementwise([lo, hi], packed_dtype=BF16), jnp.uint32)


def _unpack_bf16_pairs(packed):
    lo = pltpu.bitcast(packed << 16, F32).astype(BF16)
    hi = pltpu.bitcast(packed & jnp.uint32(0xFFFF0000), F32).astype(BF16)
    return lo, hi


def _stack_rows(rows, dtype):
    t = rows[0].shape[1]
    sub = lax.broadcasted_iota(jnp.int32, (SUBLANES, t), 0)
    out = jnp.zeros((SUBLANES, t), dtype)
    for i, r in enumerate(rows):
        out = jnp.where(sub == i, r.astype(dtype), out)
    return out


def _route_tile(logits):
    t = logits.shape[1]
    expert = lax.broadcasted_iota(jnp.int32, logits.shape, 0)
    vals = logits
    top_v, sels = [], []
    for _ in range(TOP_K):
        m = jnp.max(vals, axis=0, keepdims=True)
        idx = jnp.min(jnp.where(vals == m, expert, N_EXPERTS), axis=0, keepdims=True)
        sel = expert == idx
        vals = jnp.where(sel, -jnp.inf, vals)
        top_v.append(m)
        sels.append(sel)
    exps = [jnp.exp(v - top_v[0]) for v in top_v]
    denom = exps[0] + exps[1] + exps[2] + exps[3]
    gates = [e / denom for e in exps]

    chosen = jnp.zeros(logits.shape, F32)
    for sel in sels:
        chosen = chosen + sel.astype(F32)
    chosen_b = chosen.astype(BF16)
    r = lax.broadcasted_iota(jnp.int32, (t, t), 0)
    c = lax.broadcasted_iota(jnp.int32, (t, t), 1)
    earlier = jnp.where(r < c, 1.0, 0.0).astype(BF16)
    before = _dot(chosen_b, earlier)
    counts = jnp.sum(chosen, axis=1, keepdims=True)
    chunks = jnp.floor((counts + (CHUNK_ROWS - 1)) * (1.0 / CHUNK_ROWS))
    er = lax.broadcasted_iota(jnp.int32, (N_EXPERTS, N_EXPERTS), 0)
    ec = lax.broadcasted_iota(jnp.int32, (N_EXPERTS, N_EXPERTS), 1)
    lower = jnp.where(ec < er, 1.0, 0.0).astype(BF16)
    start = _dot(lower, jnp.broadcast_to(chunks, logits.shape).astype(BF16)) * float(CHUNK_ROWS)
    rows = [jnp.sum(jnp.where(sel, before + start, 0.0), axis=0, keepdims=True).astype(jnp.int32)
            for sel in sels]
    counts_rows = _dot_nt(jnp.ones((SUBLANES, t), BF16), chosen_b)

    local = lax.broadcasted_iota(jnp.int32, (LOCAL_ROWS, 1), 0)
    select = jnp.zeros((LOCAL_ROWS, t), F32)
    for row in rows:
        select = jnp.where(local == row, 1.0, select)
    return rows, gates, counts_rows, select.astype(BF16)


def _out_router_kernel(ret_ref, att_ref, x_ref, wo_ref, g_ref, wrt_ref, brt_ref,
                       h1_ref, xloc_ref, slot_ref, gate_ref, cnt_ref):
    half = x_ref.shape[1] // 2
    h1 = (x_ref[...] + _dot(ret_ref[...], wo_ref[:RET_WIDTH, :])
          + _dot(att_ref[...], wo_ref[RET_WIDTH:, :]))
    h1_ref[...] = h1
    xb_all = _rms(h1, g_ref[...]).astype(BF16)
    logits_all = _dot_nt(wrt_ref[...], xb_all) + brt_ref[...]
    for s in range(x_ref.shape[0] // ROUTE_TILE):
        tok = slice(s * ROUTE_TILE, (s + 1) * ROUTE_TILE)
        loc = slice(s * LOCAL_ROWS, (s + 1) * LOCAL_ROWS)
        xb = xb_all[tok, :]
        rows, gates, counts_rows, select = _route_tile(logits_all[:, tok])
        xloc = _dot(select, xb)
        xloc_ref[loc, :] = _pack_bf16_pairs(xloc[:, :half], xloc[:, half:])
        slot_ref[:, tok] = _stack_rows(rows, jnp.int32)
        gate_ref[:, tok] = _stack_rows(gates, F32)
        cnt_ref[s] = counts_rows


def _out_router(ret, att, x2, w_out, g, w_router_t, b_router_t):
    n, d = x2.shape
    per = ROUTE_STEP_TILES
    tm = per * ROUTE_TILE
    tiles = n // ROUTE_TILE
    row = lambda i: (i, 0)
    col = lambda i: (0, i)
    fixed = lambda i: (0, 0)
    return pl.pallas_call(
        _out_router_kernel,
        grid=(n // tm,),
        in_specs=[
            pl.BlockSpec((tm, RET_WIDTH), row),
            pl.BlockSpec((tm, ATT_WIDTH), row),
            pl.BlockSpec((tm, d), row),
            RESIDENT,
            RESIDENT,
            RESIDENT,
            RESIDENT,
        ],
        out_specs=[
            pl.BlockSpec((tm, d), row),
            pl.BlockSpec((per * LOCAL_ROWS, d // 2), row),
            pl.BlockSpec((SUBLANES, tm), col),
            pl.BlockSpec((SUBLANES, tm), col),
            pl.BlockSpec((per, SUBLANES, N_EXPERTS), lambda i: (i, 0, 0)),
        ],
        out_shape=[
            jax.ShapeDtypeStruct((n, d), F32),
            jax.ShapeDtypeStruct((tiles * LOCAL_ROWS, d // 2), jnp.uint32),
            jax.ShapeDtypeStruct((SUBLANES, n), jnp.int32),
            jax.ShapeDtypeStruct((SUBLANES, n), F32),
            jax.ShapeDtypeStruct((tiles, SUBLANES, N_EXPERTS), F32),
        ],
        compiler_params=_params(("parallel",)),
        name="out_router",
    )(ret, att, x2, w_out, g, w_router_t, b_router_t)


def _convert_expert_weights(wu_ref, wdn_ref, wg_ref, wl_ref, wd_ref):
    ch = PERM_CHUNK
    hc = ch // 2
    r = lax.broadcasted_iota(jnp.int32, (ch, ch), 0)
    c = lax.broadcasted_iota(jnp.int32, (ch, ch), 1)
    perm = jnp.where(r == jnp.where(c < hc, 2 * c, 2 * (c - hc) + 1), 1.0, 0.0).astype(BF16)
    for j in range(wu_ref.shape[1] // ch):
        both = _dot(wu_ref[:, j * ch:(j + 1) * ch].astype(BF16), perm)
        wg_ref[:, j * hc:(j + 1) * hc] = both[:, :hc].astype(BF16)
        wl_ref[:, j * hc:(j + 1) * hc] = both[:, hc:].astype(BF16)
    wd_ref[...] = wdn_ref[...].astype(BF16)


def _moe_kernel(be_ref, nused_ref, ord_ref, ue_ref, full_ref, src_ref, dst_ref,
                xin_hbm, wu_hbm, bg_ref, bl_ref, wdn_hbm, bd_ref,
                loc_hbm, xbuf, ybuf, gsem, ssem, wu_buf, wdn_buf, wsem, wg_ref, wl_ref, wd_ref):
    del xin_hbm
    i = pl.program_id(0)
    n_used = nused_ref[0]
    last = pl.num_programs(0) - 1
    slot = lax.rem(i, 2)
    bm, half = xbuf.shape[1], xbuf.shape[2]

    def chunk(ref, c):
        first = c * CHUNK_ROWS if isinstance(c, int) else pl.multiple_of(c * CHUNK_ROWS, CHUNK_ROWS)
        return ref.at[pl.ds(first, CHUNK_ROWS), :]

    def start_gather(step, sl):
        for j in range(BLOCK_CHUNKS):
            pltpu.make_async_copy(chunk(loc_hbm, src_ref[step * BLOCK_CHUNKS + j]),
                                  chunk(xbuf.at[sl], j), gsem.at[sl]).start()

    def start_writeback(step, sl):
        for j in range(BLOCK_CHUNKS):
            dst = dst_ref[(step + PRIME_BLOCKS) * BLOCK_CHUNKS + j]
            pltpu.make_async_copy(chunk(ybuf.at[sl], j), chunk(loc_hbm, dst), ssem.at[sl]).start()

    def wait_gather(sl):
        pltpu.make_async_copy(loc_hbm.at[pl.ds(0, bm), :], xbuf.at[sl], gsem.at[sl]).wait()

    def wait_writeback(sl):
        pltpu.make_async_copy(ybuf.at[sl], loc_hbm.at[pl.ds(0, bm), :], ssem.at[sl]).wait()

    def weight_fetch(expert, buf):
        return (pltpu.make_async_copy(wu_hbm.at[expert], wu_buf.at[buf], wsem.at[buf, 0]),
                pltpu.make_async_copy(wdn_hbm.at[expert], wdn_buf.at[buf], wsem.at[buf, 1]))

    @pl.when(i == 0)
    def _():
        ybuf[...] = jnp.zeros_like(ybuf)
        start_gather(0, 0)
        start_writeback(-2, 0)
        for copy in weight_fetch(ue_ref[0], 0):
            copy.start()

    new_expert = jnp.logical_or(i == 0, be_ref[i] != be_ref[jnp.maximum(i - 1, 0)])

    @pl.when(jnp.logical_and(i < n_used, new_expert))
    def _():
        k = ord_ref[i]
        buf = lax.rem(k, 2)
        for copy in weight_fetch(be_ref[i], buf):
            copy.wait()

        @pl.when(k + 1 < nused_ref[1])
        def _():
            for copy in weight_fetch(ue_ref[k + 1], 1 - buf):
                copy.start()

        _convert_expert_weights(wu_buf.at[buf], wdn_buf.at[buf], wg_ref, wl_ref, wd_ref)

    def block_step(rows):
        wait_gather(slot)
        start_gather(jnp.minimum(i + 1, last), 1 - slot)
        x = jnp.concatenate(_unpack_bf16_pairs(xbuf[slot, :rows, :]), axis=1)
        start_writeback(i - 1, 1 - slot)

        expert = be_ref[i]

        def up(w_ref, b_ref):
            return _dot(x, w_ref[...]) + b_ref[expert]

        glu = jnp.minimum(up(wg_ref, bg_ref), SWIGLU_LIMIT)
        lin = jnp.clip(up(wl_ref, bl_ref), -SWIGLU_LIMIT, SWIGLU_LIMIT)
        act = glu * jax.nn.sigmoid(SWIGLU_ALPHA * glu) * (lin + 1.0)
        y = _dot(act.astype(BF16), wd_ref[...]) + bd_ref[expert]

        wait_writeback(slot)
        ybuf[slot, :rows, :] = _pack_bf16_pairs(y[:, :half], y[:, half:])

    used = i < n_used
    full = full_ref[i] > 0

    @pl.when(jnp.logical_and(used, full))
    def _():
        block_step(bm)

    @pl.when(jnp.logical_and(used, jnp.logical_not(full)))
    def _():
        block_step(bm // 2)

    @pl.when(i == n_used - 1)
    def _():
        start_writeback(i, slot)
        wait_writeback(1 - slot)
        wait_writeback(slot)
        wait_gather(1 - slot)


def _moe(block_e, n_used, block_ord, used_experts, block_full, chunk_src, chunk_dst,
         xloc, w_up, bg, bl, w_down, bd):
    half = xloc.shape[1]
    e, d, f2 = w_up.shape
    f = f2 // 2
    bm = MOE_BLOCK
    n_blocks = chunk_src.shape[0] // BLOCK_CHUNKS
    assert chunk_dst.shape[0] == (n_blocks + PRIME_BLOCKS) * BLOCK_CHUNKS
    prefetch = (block_e, n_used, block_ord, used_experts, block_full, chunk_src, chunk_dst)
    grid_spec = pltpu.PrefetchScalarGridSpec(
        num_scalar_prefetch=len(prefetch),
        grid=(n_blocks,),
        in_specs=[
            pl.BlockSpec(memory_space=pl.ANY),
            pl.BlockSpec(memory_space=pl.ANY),
            RESIDENT,
            RESIDENT,
            pl.BlockSpec(memory_space=pl.ANY),
            RESIDENT,
        ],
        out_specs=pl.BlockSpec(memory_space=pl.ANY),
        scratch_shapes=[
            pltpu.VMEM((2, bm, half), jnp.uint32),
            pltpu.VMEM((2, bm, half), jnp.uint32),
            pltpu.SemaphoreType.DMA((2,)),
            pltpu.SemaphoreType.DMA((2,)),
            pltpu.VMEM((2, d, f2), F32),
            pltpu.VMEM((2, f, d), F32),
            pltpu.SemaphoreType.DMA((2, 2)),
            pltpu.VMEM((d, f), BF16),
            pltpu.VMEM((d, f), BF16),
            pltpu.VMEM((f, d), BF16),
        ],
    )
    return pl.pallas_call(
        _moe_kernel,
        grid_spec=grid_spec,
        out_shape=jax.ShapeDtypeStruct(xloc.shape, jnp.uint32),
        input_output_aliases={len(prefetch): 0},
        compiler_params=_params(("arbitrary",)),
        name="moe",
    )(*prefetch, xloc, w_up, bg, bl, w_down, bd)


def _combine_ple_kernel(yloc_ref, slot_ref, gate_ref, h1_ref, p_ref, wpg_ref, wpp_ref, gple_ref,
                        gfin_ref, o_ref):
    local = lax.broadcasted_iota(jnp.int32, (LOCAL_ROWS, 1), 0)
    moe = []
    for s in range(h1_ref.shape[0] // ROUTE_TILE):
        tok = slice(s * ROUTE_TILE, (s + 1) * ROUTE_TILE)
        loc = slice(s * LOCAL_ROWS, (s + 1) * LOCAL_ROWS)
        rows = slot_ref[:, tok]
        gates = gate_ref[:, tok]
        weights = jnp.zeros((LOCAL_ROWS, ROUTE_TILE), F32)
        for k in range(TOP_K):
            weights = jnp.where(local == rows[k:k + 1, :], gates[k:k + 1, :], weights)
        weights = weights.astype(BF16)
        y_lo, y_hi = _unpack_bf16_pairs(yloc_ref[loc, :])
        moe.append(jnp.concatenate([_dot_tn(y_lo, weights), _dot_tn(y_hi, weights)], axis=0).T)
    h2 = h1_ref[...] + jnp.concatenate(moe, axis=0)
    gate = jax.nn.sigmoid(_dot(h2.astype(BF16), wpg_ref[...]))
    proj = _dot(p_ref[...].astype(BF16), wpp_ref[...])
    h3 = h2 + _rms(gate * proj, gple_ref[...])
    o_ref[...] = _rms(h3, gfin_ref[...])


def _combine_ple(yloc, slot_rows, gates, h1, p2, wpg, wpp, gple, gfin):
    n, d = h1.shape
    pd = p2.shape[1]
    per = COMBINE_STEP_TILES
    tm = per * ROUTE_TILE
    row = lambda i: (i, 0)
    col = lambda i: (0, i)
    fixed = lambda i: (0, 0)
    return pl.pallas_call(
        _combine_ple_kernel,
        grid=(n // tm,),
        in_specs=[
            pl.BlockSpec((per * LOCAL_ROWS, d // 2), row),
            pl.BlockSpec((SUBLANES, tm), col),
            pl.BlockSpec((SUBLANES, tm), col),
            pl.BlockSpec((tm, d), row),
            pl.BlockSpec((tm, pd), row),
            RESIDENT,
            RESIDENT,
            RESIDENT,
            RESIDENT,
        ],
        out_specs=pl.BlockSpec((tm, d), row),
        out_shape=jax.ShapeDtypeStruct((n, d), F32),
        compiler_params=_params(("parallel",)),
        name="combine_ple",
    )(yloc, slot_rows, gates, h1, p2, wpg, wpp, gple, gfin)


def _routing_tables(counts):
    tiles = counts.shape[0]
    local_chunks = LOCAL_ROWS // CHUNK_ROWS
    assert 2 * BLOCK_CHUNKS < tiles * SPARE_CHUNKS
    run = (counts + CHUNK_ROWS - 1) // CHUNK_ROWS
    local_start = jnp.cumsum(run, axis=1) - run
    run_end = jnp.cumsum(run, axis=0)
    run_start = run_end - run
    total = run_end[-1]
    padded = (total + BLOCK_CHUNKS - 1) // BLOCK_CHUNKS * BLOCK_CHUNKS
    pend = jnp.cumsum(padded)
    poff = pend - padded

    slots = tiles * ROUTE_TILE * TOP_K
    max_chunks = (slots + tiles * N_EXPERTS * (CHUNK_ROWS - 1)) // CHUNK_ROWS + N_EXPERTS * (BLOCK_CHUNKS - 1)
    n_chunks = -(-max_chunks // BLOCK_CHUNKS) * BLOCK_CHUNKS
    lookup = lambda onehot, table: jnp.dot(onehot, table.astype(F32), precision=lax.Precision.HIGHEST)
    q = jnp.arange(n_chunks, dtype=jnp.int32)
    e = jnp.minimum(jnp.sum(q[:, None] >= pend[None, :], axis=1), N_EXPERTS - 1).astype(jnp.int32)
    e_hot = (e[:, None] == jnp.arange(N_EXPERTS)[None, :]).astype(F32)
    per_expert = lookup(e_hot, jnp.stack([poff, total], axis=1)).astype(jnp.int32)
    j = q - per_expert[:, 0]
    valid = j < per_expert[:, 1]
    ends = lookup(e_hot, run_end.T).astype(jnp.int32)
    tile = jnp.minimum(jnp.sum(ends <= j[:, None], axis=1), tiles - 1).astype(jnp.int32)
    tile_hot = (tile[:, None] == jnp.arange(tiles)[None, :]).astype(F32)
    shift = jnp.sum(lookup(tile_hot, local_start - run_start) * e_hot, axis=1).astype(jnp.int32)
    src = jnp.where(valid, tile * local_chunks + shift + j, tiles * local_chunks - 1)
    spare = (q // BLOCK_CHUNKS % 2) * BLOCK_CHUNKS + q % BLOCK_CHUNKS
    spare_chunk = (spare // SPARE_CHUNKS) * local_chunks + (local_chunks - SPARE_CHUNKS) + spare % SPARE_CHUNKS
    dst = jnp.where(valid, tile * local_chunks + shift + j, spare_chunk)
    dst = jnp.concatenate([spare_chunk[:PRIME_BLOCKS * BLOCK_CHUNKS], dst])
    has_rows = total > 0
    ordinal = jnp.cumsum(has_rows.astype(jnp.int32)) - 1
    used_experts = jnp.argsort(jnp.logical_not(has_rows), stable=True).astype(jnp.int32)
    block_e = e[::BLOCK_CHUNKS]
    block_ord = jnp.sum(jnp.where(block_e[:, None] == jnp.arange(N_EXPERTS)[None, :], ordinal[None, :], 0), axis=1)
    n_used = jnp.stack([pend[-1] // BLOCK_CHUNKS, jnp.sum(has_rows)]).astype(jnp.int32)
    block_full = valid[BLOCK_CHUNKS // 2::BLOCK_CHUNKS].astype(jnp.int32)
    return (block_e, n_used, block_ord.astype(jnp.int32), used_experts, block_full,
            src.astype(jnp.int32), dst.astype(jnp.int32))


def kernel(x, p, positions, rel_bias_table, g_mix_norm, w_in, b_in, ret_norm_g, att_sinks, w_out,
           g_moe_norm, w_router, b_router, w_up, b_up, w_down, b_down, w_ple_gate, w_ple_proj,
           g_ple_norm, g_final):
    b, s, d = x.shape
    depth = w_in.shape[0]
    assert depth == 1, "single-layer stack only"
    n = b * s
    x2 = x.reshape(n, d)

    half = RET_DK // 2
    inv = ROPE_BASE ** (-jnp.arange(half, dtype=F32) / half)
    ang = (positions.astype(F32)[..., None] * inv).reshape(n, half)

    qr, kr, vr, gr, qa, kva = _in_proj(
        x2, g_mix_norm[0][None], w_in[0].astype(BF16), b_in[0][None], jnp.cos(ang), jnp.sin(ang))

    shape3 = lambda t: t.reshape(b, s, t.shape[-1])
    ret = _retention(shape3(qr), shape3(kr), shape3(vr), shape3(gr), ret_norm_g[0])
    att = _attention(shape3(qa), shape3(kva), att_sinks[0], rel_bias_table)

    h1, xloc, slot_rows, gates, counts = _out_router(
        ret.reshape(n, RET_WIDTH), att.reshape(n, ATT_WIDTH), x2, w_out[0].astype(BF16),
        g_moe_norm[0][None], w_router[0].T.astype(BF16), b_router[0][:, None])

    tables = _routing_tables(counts[:, 0, :].astype(jnp.int32))

    yloc = _moe(*tables, xloc, w_up[0],
                b_up[0][:, None, 0::2], b_up[0][:, None, 1::2],
                w_down[0], b_down[0].reshape(N_EXPERTS, 1, d))

    out = _combine_ple(yloc, slot_rows, gates, h1, p[0].reshape(n, -1),
                       w_ple_gate[0].astype(BF16), w_ple_proj[0].astype(BF16),
                       g_ple_norm[0][None], g_final[None])
    return out.reshape(b, s, d)
```

```python
import math

import jax
import jax.numpy as jnp
from jax import lax
from jax.experimental import pallas as pl
from jax.experimental.pallas import tpu as pltpu

RET_HEADS = 4
RET_DK = 128
RET_DV = 128
RET_CHUNK = 128
ROPE_BASE = 10000.0
ATT_Q_HEADS = 8
ATT_KV_HEADS = 2
ATT_GROUP = ATT_Q_HEADS // ATT_KV_HEADS
ATT_HEAD_DIM = 64
WINDOW = 128
ATT_BLOCK = 128
NUM_BUCKETS = 32
MAX_DISTANCE = 128
N_EXPERTS = 32
TOP_K = 4
SWIGLU_ALPHA = 1.702
SWIGLU_LIMIT = 7.0
EPS = 1e-5

RET_WIDTH = RET_HEADS * RET_DV
ATT_WIDTH = ATT_Q_HEADS * ATT_HEAD_DIM
KV_WIDTH = ATT_KV_HEADS * ATT_HEAD_DIM

VMEM_LIMIT_BYTES = 48 * 1024 * 1024

TOKEN_TILE = 1024
RET_STEP_CHUNKS = 2
SUBLANES = 8
ROUTE_TILE = 256
ROUTE_STEP_TILES = 4
COMBINE_STEP_TILES = 4
CHUNK_ROWS = SUBLANES
LOCAL_ROWS = ROUTE_TILE * TOP_K + N_EXPERTS * CHUNK_ROWS
SPARE_CHUNKS = (LOCAL_ROWS - ROUTE_TILE * TOP_K - N_EXPERTS * (CHUNK_ROWS - 1)) // CHUNK_ROWS
MOE_BLOCK = 512
BLOCK_CHUNKS = MOE_BLOCK // CHUNK_ROWS
PRIME_BLOCKS = 2
ATT_STEP_BLOCKS = 16
PERM_CHUNK = 256

F32 = jnp.float32
BF16 = jnp.bfloat16
NEG = float(jnp.finfo(jnp.float32).min)


def _rms(x, g):
    return x * lax.rsqrt(jnp.mean(x * x, axis=-1, keepdims=True) + EPS) * g


def _dot(a, b):
    return jnp.dot(a, b, preferred_element_type=F32)


def _dot_nt(a, b):
    return lax.dot_general(a, b, (((1,), (1,)), ((), ())), preferred_element_type=F32)


def _dot_tn(a, b):
    return lax.dot_general(a, b, (((0,), (0,)), ((), ())), preferred_element_type=F32)


def _params(semantics):
    return pltpu.CompilerParams(dimension_semantics=semantics, vmem_limit_bytes=VMEM_LIMIT_BYTES)


RESIDENT = pl.BlockSpec(memory_space=pltpu.VMEM)


def _in_proj_kernel(x_ref, g_ref, w_ref, b_ref, cos_ref, sin_ref,
                    qr_ref, kr_ref, vr_ref, gr_ref, qa_ref, kva_ref):
    hn = _rms(x_ref[...], g_ref[...]).astype(BF16)
    cos = jnp.concatenate([cos_ref[...], cos_ref[...]], axis=1)
    sin = jnp.concatenate([-sin_ref[...], sin_ref[...]], axis=1)

    def proj(lo, hi):
        return _dot(hn, w_ref[:, lo:hi]) + b_ref[:, lo:hi]

    def rope(t):
        heads = []
        for h in range(RET_HEADS):
            th = t[:, h * RET_DK:(h + 1) * RET_DK]
            heads.append(th * cos + pltpu.roll(th, RET_DK // 2, 1) * sin)
        return jnp.concatenate(heads, axis=1)

    o = 0
    qr_ref[...] = rope(proj(o, o + RET_WIDTH)).astype(BF16)
    o += RET_WIDTH
    kr_ref[...] = (rope(proj(o, o + RET_WIDTH)) * (RET_DK ** -0.5)).astype(BF16)
    o += RET_WIDTH
    vr_ref[...] = proj(o, o + RET_WIDTH).astype(BF16)
    o += RET_WIDTH
    gr_ref[...] = proj(o, o + RET_WIDTH).astype(BF16)
    o += RET_WIDTH
    qa_ref[...] = (proj(o, o + ATT_WIDTH) * (ATT_HEAD_DIM ** -0.5)).astype(BF16)
    o += ATT_WIDTH
    low = lax.broadcasted_iota(jnp.int32, (1, KV_WIDTH), 1) < ATT_HEAD_DIM
    slabs = []
    for part in range(2):
        t = proj(o + part * KV_WIDTH, o + (part + 1) * KV_WIDTH)
        swapped = pltpu.roll(t, ATT_HEAD_DIM, 1)
        slabs += [jnp.where(low, t, swapped), jnp.where(low, swapped, t)]
    kva_ref[...] = jnp.concatenate(slabs, axis=1).astype(BF16)


def _in_proj(x2, g, w_in, b_in, cosf, sinf):
    n, d = x2.shape
    tm = TOKEN_TILE
    row = lambda i: (i, 0)
    outs = [RET_WIDTH, RET_WIDTH, RET_WIDTH, RET_WIDTH, ATT_WIDTH, 4 * KV_WIDTH]
    return pl.pallas_call(
        _in_proj_kernel,
        grid=(n // tm,),
        in_specs=[
            pl.BlockSpec((tm, d), row),
            RESIDENT,
            RESIDENT,
            RESIDENT,
            pl.BlockSpec((tm, RET_DK // 2), row),
            pl.BlockSpec((tm, RET_DK // 2), row),
        ],
        out_specs=[pl.BlockSpec((tm, w), row) for w in outs],
        out_shape=[jax.ShapeDtypeStruct((n, w), BF16) for w in outs],
        compiler_params=_params(("parallel",)),
        name="in_proj",
    )(x2, g, w_in, b_in, cosf, sinf)


def _retention_kernel(q_ref, k_ref, v_ref, g_ref, dmask_ref, qdec_ref, kdec_ref, cdec_ref,
                      gn_ref, o_ref, state_ref):
    batch = q_ref.shape[0]

    @pl.when(pl.program_id(0) == 0)
    def _():
        state_ref[...] = jnp.zeros_like(state_ref)

    def per_batch(b, carry):
        for h in range(RET_HEADS):
            sl = slice(h * RET_DK, (h + 1) * RET_DK)
            for c in range(q_ref.shape[1] // RET_CHUNK):
                rows = slice(c * RET_CHUNK, (c + 1) * RET_CHUNK)
                q = q_ref[b, rows, sl]
                k = k_ref[b, rows, sl]
                v = v_ref[b, rows, sl]
                scores = _dot_nt(q, k) * dmask_ref[h]
                inner = _dot(scores.astype(BF16), v)
                state = state_ref[b * RET_HEADS + h]
                cross = _dot(q, state.astype(BF16)) * qdec_ref[h]
                kd = (k.astype(F32) * kdec_ref[h]).astype(BF16)
                state_ref[b * RET_HEADS + h] = state * cdec_ref[h] + _dot_tn(kd, v)
                o = _rms(inner + cross, gn_ref[h])
                gate = g_ref[b, rows, sl].astype(F32)
                o_ref[b, rows, sl] = (gate * jax.nn.sigmoid(gate) * o).astype(BF16)
        return carry

    lax.fori_loop(0, batch, per_batch, 0, unroll=True)


def _retention(q, k, v, g, gn):
    b, s, w = q.shape
    c = RET_CHUNK
    log_g = jnp.log(1.0 - 2.0 ** (-5.0 - jnp.arange(RET_HEADS, dtype=F32)))
    ci = jnp.arange(c)
    diff = (ci[:, None] - ci[None, :]).astype(F32)
    dmask = jnp.where(diff >= 0, jnp.exp(jnp.maximum(diff, 0.0)[None] * log_g[:, None, None]), 0.0)
    cf = ci.astype(F32)
    kdec = jnp.exp((c - 1 - cf)[None, :] * log_g[:, None])
    qdec = jnp.exp((cf + 1)[None, :] * log_g[:, None])
    cdec = jnp.exp(c * log_g)
    kdec = jnp.broadcast_to(kdec[:, :, None], (RET_HEADS, c, RET_DK))
    qdec = jnp.broadcast_to(qdec[:, :, None], (RET_HEADS, c, RET_DV))
    cdec = jnp.broadcast_to(cdec[:, None, None], (RET_HEADS, RET_DK, RET_DV))
    gn3 = gn.reshape(RET_HEADS, 1, RET_DV)

    seq = lambda i: (0, i, 0)
    rows = RET_STEP_CHUNKS * c
    return pl.pallas_call(
        _retention_kernel,
        grid=(s // rows,),
        in_specs=[pl.BlockSpec((b, rows, w), seq)] * 4 + [RESIDENT] * 5,
        out_specs=pl.BlockSpec((b, rows, w), seq),
        out_shape=jax.ShapeDtypeStruct((b, s, w), BF16),
        scratch_shapes=[pltpu.VMEM((b * RET_HEADS, RET_DK, RET_DV), F32)],
        compiler_params=_params(("arbitrary",)),
        name="retention",
    )(q, k, v, g, dmask, qdec, kdec, cdec, gn3)


def _attention_kernel(sinks_ref, q_ref, kvp_ref, kvc_ref, bias_ref, o_ref):
    blk = ATT_BLOCK
    slab = 2 * ATT_HEAD_DIM
    r = lax.broadcasted_iota(jnp.int32, (blk, blk), 0)
    c = lax.broadcasted_iota(jnp.int32, (blk, blk), 1)
    own = c <= r
    first_ok = jnp.logical_or(own, pl.program_id(1) > 0)
    low = lax.broadcasted_iota(jnp.int32, (1, slab), 1) < ATT_HEAD_DIM
    zero = jnp.zeros((), BF16)
    for i in range(q_ref.shape[1] // blk):
        rows = slice(i * blk, (i + 1) * blk)
        prev = (lambda cs: kvp_ref[0, :, cs]) if i == 0 else (
            lambda cs, rs=slice((i - 1) * blk, i * blk): kvc_ref[0, rs, cs])
        slabs = []
        for a in range(ATT_Q_HEADS // 2):
            g = (2 * a) // ATT_GROUP
            ks = slice(g * slab, (g + 1) * slab)
            vs = slice((ATT_KV_HEADS + g) * slab, (ATT_KV_HEADS + g + 1) * slab)
            k = jnp.concatenate([prev(ks), kvc_ref[0, rows, ks]], axis=0)
            v_prev, v_own = prev(vs), kvc_ref[0, rows, vs]
            q = q_ref[0, rows, a * slab:(a + 1) * slab]
            halves = []
            for part in range(2):
                h = 2 * a + part
                qz = jnp.where(low if part == 0 else jnp.logical_not(low), q, zero)
                both = _dot_nt(qz, k)
                s = jnp.where(own, both[:, blk:], both[:, :blk]) + bias_ref[h]
                if i == 0:
                    s = jnp.where(first_ok, s, NEG)
                sink = sinks_ref[h]
                m = jnp.maximum(jnp.max(s, axis=1, keepdims=True), sink)
                p = jnp.exp(s - m)
                denom = jnp.sum(p, axis=1, keepdims=True) + jnp.exp(sink - m)
                p_own = jnp.where(own, p, 0.0).astype(BF16)
                p_prev = jnp.where(own, 0.0, p).astype(BF16)
                halves.append((_dot(p_own, v_own) + _dot(p_prev, v_prev)) / denom)
            slabs.append(jnp.where(low, halves[0], halves[1]))
        o_ref[0, rows, :] = jnp.concatenate(slabs, axis=1).astype(BF16)


def _t5_bucket(n):
    max_exact = NUM_BUCKETS // 2
    nf = jnp.maximum(n, 1).astype(F32)
    large = max_exact + (jnp.log(nf / max_exact) / math.log(MAX_DISTANCE / max_exact)
                         * (NUM_BUCKETS - max_exact)).astype(jnp.int32)
    large = jnp.minimum(large, NUM_BUCKETS - 1)
    return jnp.where(n < max_exact, n, large)


def _attention(q, kv, sinks, bias_table):
    b, s, _ = q.shape
    blk = ATT_BLOCK
    assert WINDOW == blk
    qi = jnp.arange(blk)[:, None]
    ki = jnp.arange(blk)[None, :]
    dist = jnp.where(ki <= qi, qi - ki, qi + blk - ki)
    onehot = jax.nn.one_hot(_t5_bucket(dist), NUM_BUCKETS, dtype=F32)
    bias = jnp.einsum("qkn,nh->hqk", onehot, bias_table.astype(F32), precision=lax.Precision.HIGHEST)

    sub = ATT_STEP_BLOCKS
    kvw = kv.shape[-1]
    return pl.pallas_call(
        _attention_kernel,
        grid=(b, s // (sub * blk)),
        in_specs=[
            pl.BlockSpec(memory_space=pltpu.SMEM),
            pl.BlockSpec((1, sub * blk, ATT_WIDTH), lambda i, j: (i, j, 0)),
            pl.BlockSpec((1, blk, kvw), lambda i, j: (i, jnp.maximum(sub * j - 1, 0), 0)),
            pl.BlockSpec((1, sub * blk, kvw), lambda i, j: (i, j, 0)),
            RESIDENT,
        ],
        out_specs=pl.BlockSpec((1, sub * blk, ATT_WIDTH), lambda i, j: (i, j, 0)),
        out_shape=jax.ShapeDtypeStruct((b, s, ATT_WIDTH), BF16),
        compiler_params=_params(("parallel", "arbitrary")),
        name="attention",
    )(sinks, q, kv, kv, bias)


def _pack_bf16_pairs(lo, hi):
    return pltpu.bitcast(pltpu.pack_elementwise([lo, hi], packed_dtype=BF16), jnp.uint32)


def _unpack_bf16_pairs(packed):
    lo = pltpu.bitcast(packed << 16, F32).astype(BF16)
    hi = pltpu.bitcast(packed & jnp.uint32(0xFFFF0000), F32).astype(BF16)
    return lo, hi


def _stack_rows(rows, dtype):
    t = rows[0].shape[1]
    sub = lax.broadcasted_iota(jnp.int32, (SUBLANES, t), 0)
    out = jnp.zeros((SUBLANES, t), dtype)
    for i, r in enumerate(rows):
        out = jnp.where(sub == i, r.astype(dtype), out)
    return out


def _route_tile(logits):
    t = logits.shape[1]
    expert = lax.broadcasted_iota(jnp.int32, logits.shape, 0)
    vals = logits
    top_v, sels = [], []
    for _ in range(TOP_K):
        m = jnp.max(vals, axis=0, keepdims=True)
        idx = jnp.min(jnp.where(vals == m, expert, N_EXPERTS), axis=0, keepdims=True)
        sel = expert == idx
        vals = jnp.where(sel, -jnp.inf, vals)
        top_v.append(m)
        sels.append(sel)
    exps = [jnp.exp(v - top_v[0]) for v in top_v]
    denom = exps[0] + exps[1] + exps[2] + exps[3]
    gates = [e / denom for e in exps]

    chosen = jnp.zeros(logits.shape, F32)
    for sel in sels:
        chosen = chosen + sel.astype(F32)
    chosen_b = chosen.astype(BF16)
    r = lax.broadcasted_iota(jnp.int32, (t, t), 0)
    c = lax.broadcasted_iota(jnp.int32, (t, t), 1)
    earlier = jnp.where(r < c, 1.0, 0.0).astype(BF16)
    before = _dot(chosen_b, earlier)
    counts = jnp.sum(chosen, axis=1, keepdims=True)
    chunks = jnp.floor((counts + (CHUNK_ROWS - 1)) * (1.0 / CHUNK_ROWS))
    er = lax.broadcasted_iota(jnp.int32, (N_EXPERTS, N_EXPERTS), 0)
    ec = lax.broadcasted_iota(jnp.int32, (N_EXPERTS, N_EXPERTS), 1)
    lower = jnp.where(ec < er, 1.0, 0.0).astype(BF16)
    start = _dot(lower, jnp.broadcast_to(chunks, logits.shape).astype(BF16)) * float(CHUNK_ROWS)
    rows = [jnp.sum(jnp.where(sel, before + start, 0.0), axis=0, keepdims=True).astype(jnp.int32)
            for sel in sels]
    counts_rows = _dot_nt(jnp.ones((SUBLANES, t), BF16), chosen_b)

    local = lax.broadcasted_iota(jnp.int32, (LOCAL_ROWS, 1), 0)
    select = jnp.zeros((LOCAL_ROWS, t), F32)
    for row in rows:
        select = jnp.where(local == row, 1.0, select)
    return rows, gates, counts_rows, select.astype(BF16)


def _out_router_kernel(ret_ref, att_ref, x_ref, wo_ref, g_ref, wrt_ref, brt_ref,
                       h1_ref, xloc_ref, slot_ref, gate_ref, cnt_ref):
    half = x_ref.shape[1] // 2
    h1 = (x_ref[...] + _dot(ret_ref[...], wo_ref[:RET_WIDTH, :])
          + _dot(att_ref[...], wo_ref[RET_WIDTH:, :]))
    h1_ref[...] = h1
    xb_all = _rms(h1, g_ref[...]).astype(BF16)
    logits_all = _dot_nt(wrt_ref[...], xb_all) + brt_ref[...]
    for s in range(x_ref.shape[0] // ROUTE_TILE):
        tok = slice(s * ROUTE_TILE, (s + 1) * ROUTE_TILE)
        loc = slice(s * LOCAL_ROWS, (s + 1) * LOCAL_ROWS)
        xb = xb_all[tok, :]
        rows, gates, counts_rows, select = _route_tile(logits_all[:, tok])
        xloc = _dot(select, xb)
        xloc_ref[loc, :] = _pack_bf16_pairs(xloc[:, :half], xloc[:, half:])
        slot_ref[:, tok] = _stack_rows(rows, jnp.int32)
        gate_ref[:, tok] = _stack_rows(gates, F32)
        cnt_ref[s] = counts_rows


def _out_router(ret, att, x2, w_out, g, w_router_t, b_router_t):
    n, d = x2.shape
    per = ROUTE_STEP_TILES
    tm = per * ROUTE_TILE
    tiles = n // ROUTE_TILE
    row = lambda i: (i, 0)
    col = lambda i: (0, i)
    return pl.pallas_call(
        _out_router_kernel,
        grid=(n // tm,),
        in_specs=[
            pl.BlockSpec((tm, RET_WIDTH), row),
            pl.BlockSpec((tm, ATT_WIDTH), row),
            pl.BlockSpec((tm, d), row),
            RESIDENT,
            RESIDENT,
            RESIDENT,
            RESIDENT,
        ],
        out_specs=[
            pl.BlockSpec((tm, d), row),
            pl.BlockSpec((per * LOCAL_ROWS, d // 2), row),
            pl.BlockSpec((SUBLANES, tm), col),
            pl.BlockSpec((SUBLANES, tm), col),
            pl.BlockSpec((per, SUBLANES, N_EXPERTS), lambda i: (i, 0, 0)),
        ],
        out_shape=[
            jax.ShapeDtypeStruct((n, d), F32),
            jax.ShapeDtypeStruct((tiles * LOCAL_ROWS, d // 2), jnp.uint32),
            jax.ShapeDtypeStruct((SUBLANES, n), jnp.int32),
            jax.ShapeDtypeStruct((SUBLANES, n), F32),
            jax.ShapeDtypeStruct((tiles, SUBLANES, N_EXPERTS), F32),
        ],
        compiler_params=_params(("parallel",)),
        name="out_router",
    )(ret, att, x2, w_out, g, w_router_t, b_router_t)


def _convert_expert_weights(wu_ref, wdn_ref, wg_ref, wl_ref, wd_ref):
    ch = PERM_CHUNK
    hc = ch // 2
    r = lax.broadcasted_iota(jnp.int32, (ch, ch), 0)
    c = lax.broadcasted_iota(jnp.int32, (ch, ch), 1)
    perm = jnp.where(r == jnp.where(c < hc, 2 * c, 2 * (c - hc) + 1), 1.0, 0.0).astype(BF16)
    for j in range(wu_ref.shape[1] // ch):
        both = _dot(wu_ref[:, j * ch:(j + 1) * ch].astype(BF16), perm)
        wg_ref[:, j * hc:(j + 1) * hc] = both[:, :hc].astype(BF16)
        wl_ref[:, j * hc:(j + 1) * hc] = both[:, hc:].astype(BF16)
    wd_ref[...] = wdn_ref[...].astype(BF16)


def _moe_kernel(be_ref, nused_ref, ord_ref, ue_ref, full_ref, src_ref, dst_ref,
                xin_hbm, wu_hbm, bg_ref, bl_ref, wdn_hbm, bd_ref,
                loc_hbm, xbuf, ybuf, gsem, ssem, wu_buf, wdn_buf, wsem, wg_ref, wl_ref, wd_ref):
    del xin_hbm
    i = pl.program_id(0)
    n_used = nused_ref[0]
    last = pl.num_programs(0) - 1
    slot = lax.rem(i, 2)
    bm, half = xbuf.shape[1], xbuf.shape[2]

    def chunk(ref, c):
        first = c * CHUNK_ROWS if isinstance(c, int) else pl.multiple_of(c * CHUNK_ROWS, CHUNK_ROWS)
        return ref.at[pl.ds(first, CHUNK_ROWS), :]

    def start_gather(step, sl):
        for j in range(BLOCK_CHUNKS):
            pltpu.make_async_copy(chunk(loc_hbm, src_ref[step * BLOCK_CHUNKS + j]),
                                  chunk(xbuf.at[sl], j), gsem.at[sl]).start()

    def start_writeback(step, sl):
        for j in range(BLOCK_CHUNKS):
            dst = dst_ref[(step + PRIME_BLOCKS) * BLOCK_CHUNKS + j]
            pltpu.make_async_copy(chunk(ybuf.at[sl], j), chunk(loc_hbm, dst), ssem.at[sl]).start()

    def wait_gather(sl):
        pltpu.make_async_copy(loc_hbm.at[pl.ds(0, bm), :], xbuf.at[sl], gsem.at[sl]).wait()

    def wait_writeback(sl):
        pltpu.make_async_copy(ybuf.at[sl], loc_hbm.at[pl.ds(0, bm), :], ssem.at[sl]).wait()

    def weight_fetch(expert, buf):
        return (pltpu.make_async_copy(wu_hbm.at[expert], wu_buf.at[buf], wsem.at[buf, 0]),
                pltpu.make_async_copy(wdn_hbm.at[expert], wdn_buf.at[buf], wsem.at[buf, 1]))

    @pl.when(i == 0)
    def _():
        ybuf[...] = jnp.zeros_like(ybuf)
        start_gather(0, 0)
        start_writeback(-2, 0)
        for copy in weight_fetch(ue_ref[0], 0):
            copy.start()

    new_expert = jnp.logical_or(i == 0, be_ref[i] != be_ref[jnp.maximum(i - 1, 0)])

    @pl.when(jnp.logical_and(i < n_used, new_expert))
    def _():
        k = ord_ref[i]
        buf = lax.rem(k, 2)
        for copy in weight_fetch(be_ref[i], buf):
            copy.wait()

        @pl.when(k + 1 < nused_ref[1])
        def _():
            for copy in weight_fetch(ue_ref[k + 1], 1 - buf):
                copy.start()

        _convert_expert_weights(wu_buf.at[buf], wdn_buf.at[buf], wg_ref, wl_ref, wd_ref)

    def block_step(rows):
        wait_gather(slot)
        start_gather(jnp.minimum(i + 1, last), 1 - slot)
        x = jnp.concatenate(_unpack_bf16_pairs(xbuf[slot, :rows, :]), axis=1)
        start_writeback(i - 1, 1 - slot)

        expert = be_ref[i]

        def up(w_ref, b_ref):
            return _dot(x, w_ref[...]) + b_ref[expert]

        glu = jnp.minimum(up(wg_ref, bg_ref), SWIGLU_LIMIT)
        lin = jnp.clip(up(wl_ref, bl_ref), -SWIGLU_LIMIT, SWIGLU_LIMIT)
        act = glu * jax.nn.sigmoid(SWIGLU_ALPHA * glu) * (lin + 1.0)
        y = _dot(act.astype(BF16), wd_ref[...]) + bd_ref[expert]

        wait_writeback(slot)
        ybuf[slot, :rows, :] = _pack_bf16_pairs(y[:, :half], y[:, half:])

    used = i < n_used
    full = full_ref[i] > 0

    @pl.when(jnp.logical_and(used, full))
    def _():
        block_step(bm)

    @pl.when(jnp.logical_and(used, jnp.logical_not(full)))
    def _():
        block_step(bm // 2)

    @pl.when(i == n_used - 1)
    def _():
        start_writeback(i, slot)
        wait_writeback(1 - slot)
        wait_writeback(slot)
        wait_gather(1 - slot)


def _moe(block_e, n_used, block_ord, used_experts, block_full, chunk_src, chunk_dst,
         xloc, w_up, bg, bl, w_down, bd):
    half = xloc.shape[1]
    e, d, f2 = w_up.shape
    f = f2 // 2
    bm = MOE_BLOCK
    n_blocks = chunk_src.shape[0] // BLOCK_CHUNKS
    assert chunk_dst.shape[0] == (n_blocks + PRIME_BLOCKS) * BLOCK_CHUNKS
    prefetch = (block_e, n_used, block_ord, used_experts, block_full, chunk_src, chunk_dst)
    grid_spec = pltpu.PrefetchScalarGridSpec(
        num_scalar_prefetch=len(prefetch),
        grid=(n_blocks,),
        in_specs=[
            pl.BlockSpec(memory_space=pl.ANY),
            pl.BlockSpec(memory_space=pl.ANY),
            RESIDENT,
            RESIDENT,
            pl.BlockSpec(memory_space=pl.ANY),
            RESIDENT,
        ],
        out_specs=pl.BlockSpec(memory_space=pl.ANY),
        scratch_shapes=[
            pltpu.VMEM((2, bm, half), jnp.uint32),
            pltpu.VMEM((2, bm, half), jnp.uint32),
            pltpu.SemaphoreType.DMA((2,)),
            pltpu.SemaphoreType.DMA((2,)),
            pltpu.VMEM((2, d, f2), F32),
            pltpu.VMEM((2, f, d), F32),
            pltpu.SemaphoreType.DMA((2, 2)),
            pltpu.VMEM((d, f), BF16),
            pltpu.VMEM((d, f), BF16),
            pltpu.VMEM((f, d), BF16),
        ],
    )
    return pl.pallas_call(
        _moe_kernel,
        grid_spec=grid_spec,
        out_shape=jax.ShapeDtypeStruct(xloc.shape, jnp.uint32),
        input_output_aliases={len(prefetch): 0},
        compiler_params=_params(("arbitrary",)),
        name="moe",
    )(*prefetch, xloc, w_up, bg, bl, w_down, bd)


def _combine_ple_kernel(yloc_ref, slot_ref, gate_ref, h1_ref, p_ref, wpg_ref, wpp_ref, gple_ref,
                        gfin_ref, o_ref):
    local = lax.broadcasted_iota(jnp.int32, (LOCAL_ROWS, 1), 0)
    moe = []
    for s in range(h1_ref.shape[0] // ROUTE_TILE):
        tok = slice(s * ROUTE_TILE, (s + 1) * ROUTE_TILE)
        loc = slice(s * LOCAL_ROWS, (s + 1) * LOCAL_ROWS)
        rows = slot_ref[:, tok]
        gates = gate_ref[:, tok]
        weights = jnp.zeros((LOCAL_ROWS, ROUTE_TILE), F32)
        for k in range(TOP_K):
            weights = jnp.where(local == rows[k:k + 1, :], gates[k:k + 1, :], weights)
        weights = weights.astype(BF16)
        y_lo, y_hi = _unpack_bf16_pairs(yloc_ref[loc, :])
        moe.append(jnp.concatenate([_dot_tn(y_lo, weights), _dot_tn(y_hi, weights)], axis=0).T)
    h2 = h1_ref[...] + jnp.concatenate(moe, axis=0)
    gate = jax.nn.sigmoid(_dot(h2.astype(BF16), wpg_ref[...]))
    proj = _dot(p_ref[...].astype(BF16), wpp_ref[...])
    h3 = h2 + _rms(gate * proj, gple_ref[...])
    o_ref[...] = _rms(h3, gfin_ref[...])


def _combine_ple(yloc, slot_rows, gates, h1, p2, wpg, wpp, gple, gfin):
    n, d = h1.shape
    pd = p2.shape[1]
    per = COMBINE_STEP_TILES
    tm = per * ROUTE_TILE
    row = lambda i: (i, 0)
    col = lambda i: (0, i)
    return pl.pallas_call(
        _combine_ple_kernel,
        grid=(n // tm,),
        in_specs=[
            pl.BlockSpec((per * LOCAL_ROWS, d // 2), row),
            pl.BlockSpec((SUBLANES, tm), col),
            pl.BlockSpec((SUBLANES, tm), col),
            pl.BlockSpec((tm, d), row),
            pl.BlockSpec((tm, pd), row),
            RESIDENT,
            RESIDENT,
            RESIDENT,
            RESIDENT,
        ],
        out_specs=pl.BlockSpec((tm, d), row),
        out_shape=jax.ShapeDtypeStruct((n, d), F32),
        compiler_params=_params(("parallel",)),
        name="combine_ple",
    )(yloc, slot_rows, gates, h1, p2, wpg, wpp, gple, gfin)


def _routing_tables(counts):
    tiles = counts.shape[0]
    local_chunks = LOCAL_ROWS // CHUNK_ROWS
    assert 2 * BLOCK_CHUNKS < tiles * SPARE_CHUNKS
    run = (counts + CHUNK_ROWS - 1) // CHUNK_ROWS
    local_start = jnp.cumsum(run, axis=1) - run
    run_end = jnp.cumsum(run, axis=0)
    run_start = run_end - run
    total = run_end[-1]
    padded = (total + BLOCK_CHUNKS - 1) // BLOCK_CHUNKS * BLOCK_CHUNKS
    pend = jnp.cumsum(padded)
    poff = pend - padded

    slots = tiles * ROUTE_TILE * TOP_K
    max_chunks = (slots + tiles * N_EXPERTS * (CHUNK_ROWS - 1)) // CHUNK_ROWS + N_EXPERTS * (BLOCK_CHUNKS - 1)
    n_chunks = -(-max_chunks // BLOCK_CHUNKS) * BLOCK_CHUNKS
    lookup = lambda onehot, table: jnp.dot(onehot, table.astype(F32), precision=lax.Precision.HIGHEST)
    q = jnp.arange(n_chunks, dtype=jnp.int32)
    e = jnp.minimum(jnp.sum(q[:, None] >= pend[None, :], axis=1), N_EXPERTS - 1).astype(jnp.int32)
    e_hot = (e[:, None] == jnp.arange(N_EXPERTS)[None, :]).astype(F32)
    per_expert = lookup(e_hot, jnp.stack([poff, total], axis=1)).astype(jnp.int32)
    j = q - per_expert[:, 0]
    valid = j < per_expert[:, 1]
    ends = lookup(e_hot, run_end.T).astype(jnp.int32)
    tile = jnp.minimum(jnp.sum(ends <= j[:, None], axis=1), tiles - 1).astype(jnp.int32)
    tile_hot = (tile[:, None] == jnp.arange(tiles)[None, :]).astype(F32)
    shift = jnp.sum(lookup(tile_hot, local_start - run_start) * e_hot, axis=1).astype(jnp.int32)
    src = jnp.where(valid, tile * local_chunks + shift + j, tiles * local_chunks - 1)
    spare = (q // BLOCK_CHUNKS % 2) * BLOCK_CHUNKS + q % BLOCK_CHUNKS
    spare_chunk = (spare // SPARE_CHUNKS) * local_chunks + (local_chunks - SPARE_CHUNKS) + spare % SPARE_CHUNKS
    dst = jnp.where(valid, tile * local_chunks + shift + j, spare_chunk)
    dst = jnp.concatenate([spare_chunk[:PRIME_BLOCKS * BLOCK_CHUNKS], dst])
    has_rows = total > 0
    ordinal = jnp.cumsum(has_rows.astype(jnp.int32)) - 1
    used_experts = jnp.argsort(jnp.logical_not(has_rows), stable=True).astype(jnp.int32)
    block_e = e[::BLOCK_CHUNKS]
    block_ord = jnp.sum(jnp.where(block_e[:, None] == jnp.arange(N_EXPERTS)[None, :], ordinal[None, :], 0), axis=1)
    n_used = jnp.stack([pend[-1] // BLOCK_CHUNKS, jnp.sum(has_rows)]).astype(jnp.int32)
    block_full = valid[BLOCK_CHUNKS // 2::BLOCK_CHUNKS].astype(jnp.int32)
    return (block_e, n_used, block_ord.astype(jnp.int32), used_experts, block_full,
            src.astype(jnp.int32), dst.astype(jnp.int32))


def kernel(x, p, positions, rel_bias_table, g_mix_norm, w_in, b_in, ret_norm_g, att_sinks, w_out,
           g_moe_norm, w_router, b_router, w_up, b_up, w_down, b_down, w_ple_gate, w_ple_proj,
           g_ple_norm, g_final):
    b, s, d = x.shape
    depth = w_in.shape[0]
    assert depth == 1, "single-layer stack only"
    n = b * s
    x2 = x.reshape(n, d)

    half = RET_DK // 2
    inv = ROPE_BASE ** (-jnp.arange(half, dtype=F32) / half)
    ang = (positions.astype(F32)[..., None] * inv).reshape(n, half)

    qr, kr, vr, gr, qa, kva = _in_proj(
        x2, g_mix_norm[0][None], w_in[0].astype(BF16), b_in[0][None], jnp.cos(ang), jnp.sin(ang))

    shape3 = lambda t: t.reshape(b, s, t.shape[-1])
    ret = _retention(shape3(qr), shape3(kr), shape3(vr), shape3(gr), ret_norm_g[0])
    att = _attention(shape3(qa), shape3(kva), att_sinks[0], rel_bias_table)

    h1, xloc, slot_rows, gates, counts = _out_router(
        ret.reshape(n, RET_WIDTH), att.reshape(n, ATT_WIDTH), x2, w_out[0].astype(BF16),
        g_moe_norm[0][None], w_router[0].T.astype(BF16), b_router[0][:, None])

    tables = _routing_tables(counts[:, 0, :].astype(jnp.int32))

    yloc = _moe(*tables, xloc, w_up[0],
                b_up[0][:, None, 0::2], b_up[0][:, None, 1::2],
                w_down[0], b_down[0].reshape(N_EXPERTS, 1, d))

    out = _combine_ple(yloc, slot_rows, gates, h1, p[0].reshape(n, -1),
                       w_ple_gate[0].astype(BF16), w_ple_proj[0].astype(BF16),
                       g_ple_norm[0][None], g_final[None])
    return out.reshape(b, s, d)
```

```python
import math

import jax
import jax.numpy as jnp
from jax import lax
from jax.experimental import pallas as pl
from jax.experimental.pallas import tpu as pltpu

RET_HEADS = 4
RET_DK = 128
RET_DV = 128
RET_CHUNK = 128
ROPE_BASE = 10000.0
ATT_Q_HEADS = 8
ATT_KV_HEADS = 2
ATT_GROUP = ATT_Q_HEADS // ATT_KV_HEADS
ATT_HEAD_DIM = 64
WINDOW = 128
ATT_BLOCK = 128
NUM_BUCKETS = 32
MAX_DISTANCE = 128
N_EXPERTS = 32
TOP_K = 4
SWIGLU_ALPHA = 1.702
SWIGLU_LIMIT = 7.0
EPS = 1e-5

RET_WIDTH = RET_HEADS * RET_DV
ATT_WIDTH = ATT_Q_HEADS * ATT_HEAD_DIM
KV_WIDTH = ATT_KV_HEADS * ATT_HEAD_DIM

VMEM_LIMIT_BYTES = 48 * 1024 * 1024

TOKEN_TILE = 1024
RET_STEP_CHUNKS = 2
SUBLANES = 8
ROUTE_TILE = 256
ROUTE_STEP_TILES = 4
COMBINE_STEP_TILES = 4
CHUNK_ROWS = SUBLANES
LOCAL_ROWS = ROUTE_TILE * TOP_K + N_EXPERTS * CHUNK_ROWS
SPARE_CHUNKS = (LOCAL_ROWS - ROUTE_TILE * TOP_K - N_EXPERTS * (CHUNK_ROWS - 1)) // CHUNK_ROWS
MOE_BLOCK = 512
BLOCK_CHUNKS = MOE_BLOCK // CHUNK_ROWS
PRIME_BLOCKS = 2
BULK_DMA_PRIORITY = 1
ATT_STEP_BLOCKS = 16
PERM_CHUNK = 256

F32 = jnp.float32
BF16 = jnp.bfloat16
NEG = float(jnp.finfo(jnp.float32).min)


def _rms(x, g):
    return x * lax.rsqrt(jnp.mean(x * x, axis=-1, keepdims=True) + EPS) * g


def _dot(a, b):
    return jnp.dot(a, b, preferred_element_type=F32)


def _dot_nt(a, b):
    return lax.dot_general(a, b, (((1,), (1,)), ((), ())), preferred_element_type=F32)


def _dot_tn(a, b):
    return lax.dot_general(a, b, (((0,), (0,)), ((), ())), preferred_element_type=F32)


def _params(semantics):
    return pltpu.CompilerParams(dimension_semantics=semantics, vmem_limit_bytes=VMEM_LIMIT_BYTES)


RESIDENT = pl.BlockSpec(memory_space=pltpu.VMEM)


def _in_proj_kernel(x_ref, g_ref, w_ref, b_ref, cos_ref, sin_ref,
                    qr_ref, kr_ref, vr_ref, gr_ref, qa_ref, kva_ref):
    hn = _rms(x_ref[...], g_ref[...]).astype(BF16)
    cos = jnp.concatenate([cos_ref[...], cos_ref[...]], axis=1)
    sin = jnp.concatenate([-sin_ref[...], sin_ref[...]], axis=1)

    def proj(lo, hi):
        return _dot(hn, w_ref[:, lo:hi]) + b_ref[:, lo:hi]

    def rope(t):
        heads = []
        for h in range(RET_HEADS):
            th = t[:, h * RET_DK:(h + 1) * RET_DK]
            heads.append(th * cos + pltpu.roll(th, RET_DK // 2, 1) * sin)
        return jnp.concatenate(heads, axis=1)

    o = 0
    qr_ref[...] = rope(proj(o, o + RET_WIDTH)).astype(BF16)
    o += RET_WIDTH
    kr_ref[...] = (rope(proj(o, o + RET_WIDTH)) * (RET_DK ** -0.5)).astype(BF16)
    o += RET_WIDTH
    vr_ref[...] = proj(o, o + RET_WIDTH).astype(BF16)
    o += RET_WIDTH
    gr_ref[...] = proj(o, o + RET_WIDTH).astype(BF16)
    o += RET_WIDTH
    qa_ref[...] = (proj(o, o + ATT_WIDTH) * (ATT_HEAD_DIM ** -0.5)).astype(BF16)
    o += ATT_WIDTH
    low = lax.broadcasted_iota(jnp.int32, (1, KV_WIDTH), 1) < ATT_HEAD_DIM
    slabs = []
    for part in range(2):
        t = proj(o + part * KV_WIDTH, o + (part + 1) * KV_WIDTH)
        swapped = pltpu.roll(t, ATT_HEAD_DIM, 1)
        slabs += [jnp.where(low, t, swapped), jnp.where(low, swapped, t)]
    kva_ref[...] = jnp.concatenate(slabs, axis=1).astype(BF16)


def _in_proj(x2, g, w_in, b_in, cosf, sinf):
    n, d = x2.shape
    tm = TOKEN_TILE
    row = lambda i: (i, 0)
    outs = [RET_WIDTH, RET_WIDTH, RET_WIDTH, RET_WIDTH, ATT_WIDTH, 4 * KV_WIDTH]
    return pl.pallas_call(
        _in_proj_kernel,
        grid=(n // tm,),
        in_specs=[
            pl.BlockSpec((tm, d), row),
            RESIDENT,
            RESIDENT,
            RESIDENT,
            pl.BlockSpec((tm, RET_DK // 2), row),
            pl.BlockSpec((tm, RET_DK // 2), row),
        ],
        out_specs=[pl.BlockSpec((tm, w), row) for w in outs],
        out_shape=[jax.ShapeDtypeStruct((n, w), BF16) for w in outs],
        compiler_params=_params(("parallel",)),
        name="in_proj",
    )(x2, g, w_in, b_in, cosf, sinf)


def _retention_kernel(q_ref, k_ref, v_ref, g_ref, dmask_ref, qdec_ref, kdec_ref, cdec_ref,
                      gn_ref, o_ref, state_ref):
    batch = q_ref.shape[0]

    @pl.when(pl.program_id(0) == 0)
    def _():
        state_ref[...] = jnp.zeros_like(state_ref)

    def per_batch(b, carry):
        for h in range(RET_HEADS):
            sl = slice(h * RET_DK, (h + 1) * RET_DK)
            for c in range(q_ref.shape[1] // RET_CHUNK):
                rows = slice(c * RET_CHUNK, (c + 1) * RET_CHUNK)
                q = q_ref[b, rows, sl]
                k = k_ref[b, rows, sl]
                v = v_ref[b, rows, sl]
                scores = _dot_nt(q, k) * dmask_ref[h]
                inner = _dot(scores.astype(BF16), v)
                state = state_ref[b * RET_HEADS + h]
                cross = _dot(q, state.astype(BF16)) * qdec_ref[h]
                kd = (k.astype(F32) * kdec_ref[h]).astype(BF16)
                state_ref[b * RET_HEADS + h] = state * cdec_ref[h] + _dot_tn(kd, v)
                o = _rms(inner + cross, gn_ref[h])
                gate = g_ref[b, rows, sl].astype(F32)
                o_ref[b, rows, sl] = (gate * jax.nn.sigmoid(gate) * o).astype(BF16)
        return carry

    lax.fori_loop(0, batch, per_batch, 0, unroll=True)


def _retention(q, k, v, g, gn):
    b, s, w = q.shape
    c = RET_CHUNK
    log_g = jnp.log(1.0 - 2.0 ** (-5.0 - jnp.arange(RET_HEADS, dtype=F32)))
    ci = jnp.arange(c)
    diff = (ci[:, None] - ci[None, :]).astype(F32)
    dmask = jnp.where(diff >= 0, jnp.exp(jnp.maximum(diff, 0.0)[None] * log_g[:, None, None]), 0.0)
    cf = ci.astype(F32)
    kdec = jnp.exp((c - 1 - cf)[None, :] * log_g[:, None])
    qdec = jnp.exp((cf + 1)[None, :] * log_g[:, None])
    cdec = jnp.exp(c * log_g)
    kdec = jnp.broadcast_to(kdec[:, :, None], (RET_HEADS, c, RET_DK))
    qdec = jnp.broadcast_to(qdec[:, :, None], (RET_HEADS, c, RET_DV))
    cdec = jnp.broadcast_to(cdec[:, None, None], (RET_HEADS, RET_DK, RET_DV))
    gn3 = gn.reshape(RET_HEADS, 1, RET_DV)

    seq = lambda i: (0, i, 0)
    rows = RET_STEP_CHUNKS * c
    return pl.pallas_call(
        _retention_kernel,
        grid=(s // rows,),
        in_specs=[pl.BlockSpec((b, rows, w), seq)] * 4 + [RESIDENT] * 5,
        out_specs=pl.BlockSpec((b, rows, w), seq),
        out_shape=jax.ShapeDtypeStruct((b, s, w), BF16),
        scratch_shapes=[pltpu.VMEM((b * RET_HEADS, RET_DK, RET_DV), F32)],
        compiler_params=_params(("arbitrary",)),
        name="retention",
    )(q, k, v, g, dmask, qdec, kdec, cdec, gn3)


def _attention_kernel(sinks_ref, q_ref, kvp_ref, kvc_ref, bias_ref, o_ref):
    blk = ATT_BLOCK
    slab = 2 * ATT_HEAD_DIM
    r = lax.broadcasted_iota(jnp.int32, (blk, blk), 0)
    c = lax.broadcasted_iota(jnp.int32, (blk, blk), 1)
    own = c <= r
    first_ok = jnp.logical_or(own, pl.program_id(1) > 0)
    low = lax.broadcasted_iota(jnp.int32, (1, slab), 1) < ATT_HEAD_DIM
    zero = jnp.zeros((), BF16)
    for i in range(q_ref.shape[1] // blk):
        rows = slice(i * blk, (i + 1) * blk)
        prev = (lambda cs: kvp_ref[0, :, cs]) if i == 0 else (
            lambda cs, rs=slice((i - 1) * blk, i * blk): kvc_ref[0, rs, cs])
        slabs = []
        for a in range(ATT_Q_HEADS // 2):
            g = (2 * a) // ATT_GROUP
            ks = slice(g * slab, (g + 1) * slab)
            vs = slice((ATT_KV_HEADS + g) * slab, (ATT_KV_HEADS + g + 1) * slab)
            k = jnp.concatenate([prev(ks), kvc_ref[0, rows, ks]], axis=0)
            v_prev, v_own = prev(vs), kvc_ref[0, rows, vs]
            q = q_ref[0, rows, a * slab:(a + 1) * slab]
            halves = []
            for part in range(2):
                h = 2 * a + part
                qz = jnp.where(low if part == 0 else jnp.logical_not(low), q, zero)
                both = _dot_nt(qz, k)
                s = jnp.where(own, both[:, blk:], both[:, :blk]) + bias_ref[h]
                if i == 0:
                    s = jnp.where(first_ok, s, NEG)
                sink = sinks_ref[h]
                m = jnp.maximum(jnp.max(s, axis=1, keepdims=True), sink)
                p = jnp.exp(s - m)
                denom = jnp.sum(p, axis=1, keepdims=True) + jnp.exp(sink - m)
                p_own = jnp.where(own, p, 0.0).astype(BF16)
                p_prev = jnp.where(own, 0.0, p).astype(BF16)
                halves.append((_dot(p_own, v_own) + _dot(p_prev, v_prev)) / denom)
            slabs.append(jnp.where(low, halves[0], halves[1]))
        o_ref[0, rows, :] = jnp.concatenate(slabs, axis=1).astype(BF16)


def _t5_bucket(n):
    max_exact = NUM_BUCKETS // 2
    nf = jnp.maximum(n, 1).astype(F32)
    large = max_exact + (jnp.log(nf / max_exact) / math.log(MAX_DISTANCE / max_exact)
                         * (NUM_BUCKETS - max_exact)).astype(jnp.int32)
    large = jnp.minimum(large, NUM_BUCKETS - 1)
    return jnp.where(n < max_exact, n, large)


def _attention(q, kv, sinks, bias_table):
    b, s, _ = q.shape
    blk = ATT_BLOCK
    assert WINDOW == blk
    qi = jnp.arange(blk)[:, None]
    ki = jnp.arange(blk)[None, :]
    dist = jnp.where(ki <= qi, qi - ki, qi + blk - ki)
    onehot = jax.nn.one_hot(_t5_bucket(dist), NUM_BUCKETS, dtype=F32)
    bias = jnp.einsum("qkn,nh->hqk", onehot, bias_table.astype(F32), precision=lax.Precision.HIGHEST)

    sub = ATT_STEP_BLOCKS
    kvw = kv.shape[-1]
    return pl.pallas_call(
        _attention_kernel,
        grid=(b, s // (sub * blk)),
        in_specs=[
            pl.BlockSpec(memory_space=pltpu.SMEM),
            pl.BlockSpec((1, sub * blk, ATT_WIDTH), lambda i, j: (i, j, 0)),
            pl.BlockSpec((1, blk, kvw), lambda i, j: (i, jnp.maximum(sub * j - 1, 0), 0)),
            pl.BlockSpec((1, sub * blk, kvw), lambda i, j: (i, j, 0)),
            RESIDENT,
        ],
        out_specs=pl.BlockSpec((1, sub * blk, ATT_WIDTH), lambda i, j: (i, j, 0)),
        out_shape=jax.ShapeDtypeStruct((b, s, ATT_WIDTH), BF16),
        compiler_params=_params(("parallel", "arbitrary")),
        name="attention",
    )(sinks, q, kv, kv, bias)


def _pack_bf16_pairs(lo, hi):
    return pltpu.bitcast(pltpu.pack_elementwise([lo, hi], packed_dtype=BF16), jnp.uint32)


def _unpack_bf16_pairs(packed):
    lo = pltpu.bitcast(packed << 16, F32).astype(BF16)
    hi = pltpu.bitcast(packed & jnp.uint32(0xFFFF0000), F32).astype(BF16)
    return lo, hi


def _stack_rows(rows, dtype):
    t = rows[0].shape[1]
    sub = lax.broadcasted_iota(jnp.int32, (SUBLANES, t), 0)
    out = jnp.zeros((SUBLANES, t), dtype)
    for i, r in enumerate(rows):
        out = jnp.where(sub == i, r.astype(dtype), out)
    return out


def _route_tile(logits):
    t = logits.shape[1]
    expert = lax.broadcasted_iota(jnp.int32, logits.shape, 0)
    vals = logits
    top_v, sels = [], []
    for _ in range(TOP_K):
        m = jnp.max(vals, axis=0, keepdims=True)
        idx = jnp.min(jnp.where(vals == m, expert, N_EXPERTS), axis=0, keepdims=True)
        sel = expert == idx
        vals = jnp.where(sel, -jnp.inf, vals)
        top_v.append(m)
        sels.append(sel)
    exps = [jnp.exp(v - top_v[0]) for v in top_v]
    denom = exps[0] + exps[1] + exps[2] + exps[3]
    gates = [e / denom for e in exps]

    chosen = jnp.zeros(logits.shape, F32)
    for sel in sels:
        chosen = chosen + sel.astype(F32)
    chosen_b = chosen.astype(BF16)
    r = lax.broadcasted_iota(jnp.int32, (t, t), 0)
    c = lax.broadcasted_iota(jnp.int32, (t, t), 1)
    earlier = jnp.where(r < c, 1.0, 0.0).astype(BF16)
    before = _dot(chosen_b, earlier)
    counts = jnp.sum(chosen, axis=1, keepdims=True)
    chunks = jnp.floor((counts + (CHUNK_ROWS - 1)) * (1.0 / CHUNK_ROWS))
    er = lax.broadcasted_iota(jnp.int32, (N_EXPERTS, N_EXPERTS), 0)
    ec = lax.broadcasted_iota(jnp.int32, (N_EXPERTS, N_EXPERTS), 1)
    lower = jnp.where(ec < er, 1.0, 0.0).astype(BF16)
    start = _dot(lower, jnp.broadcast_to(chunks, logits.shape).astype(BF16)) * float(CHUNK_ROWS)
    rows = [jnp.sum(jnp.where(sel, before + start, 0.0), axis=0, keepdims=True).astype(jnp.int32)
            for sel in sels]
    counts_rows = _dot_nt(jnp.ones((SUBLANES, t), BF16), chosen_b)

    local = lax.broadcasted_iota(jnp.int32, (LOCAL_ROWS, 1), 0)
    select = jnp.zeros((LOCAL_ROWS, t), F32)
    for row in rows:
        select = jnp.where(local == row, 1.0, select)
    return rows, gates, counts_rows, select.astype(BF16)


def _out_router_kernel(ret_ref, att_ref, x_ref, wo_ref, g_ref, wrt_ref, brt_ref,
                       h1_ref, xloc_ref, slot_ref, gate_ref, cnt_ref):
    half = x_ref.shape[1] // 2
    h1 = (x_ref[...] + _dot(ret_ref[...], wo_ref[:RET_WIDTH, :])
          + _dot(att_ref[...], wo_ref[RET_WIDTH:, :]))
    h1_ref[...] = h1
    xb_all = _rms(h1, g_ref[...]).astype(BF16)
    logits_all = _dot_nt(wrt_ref[...], xb_all) + brt_ref[...]
    for s in range(x_ref.shape[0] // ROUTE_TILE):
        tok = slice(s * ROUTE_TILE, (s + 1) * ROUTE_TILE)
        loc = slice(s * LOCAL_ROWS, (s + 1) * LOCAL_ROWS)
        xb = xb_all[tok, :]
        rows, gates, counts_rows, select = _route_tile(logits_all[:, tok])
        xloc = _dot(select, xb)
        xloc_ref[loc, :] = _pack_bf16_pairs(xloc[:, :half], xloc[:, half:])
        slot_ref[:, tok] = _stack_rows(rows, jnp.int32)
        gate_ref[:, tok] = _stack_rows(gates, F32)
        cnt_ref[s] = counts_rows


def _out_router(ret, att, x2, w_out, g, w_router_t, b_router_t):
    n, d = x2.shape
    per = ROUTE_STEP_TILES
    tm = per * ROUTE_TILE
    tiles = n // ROUTE_TILE
    row = lambda i: (i, 0)
    col = lambda i: (0, i)
    return pl.pallas_call(
        _out_router_kernel,
        grid=(n // tm,),
        in_specs=[
            pl.BlockSpec((tm, RET_WIDTH), row),
            pl.BlockSpec((tm, ATT_WIDTH), row),
            pl.BlockSpec((tm, d), row),
            RESIDENT,
            RESIDENT,
            RESIDENT,
            RESIDENT,
        ],
        out_specs=[
            pl.BlockSpec((tm, d), row),
            pl.BlockSpec((per * LOCAL_ROWS, d // 2), row),
            pl.BlockSpec((SUBLANES, tm), col),
            pl.BlockSpec((SUBLANES, tm), col),
            pl.BlockSpec((per, SUBLANES, N_EXPERTS), lambda i: (i, 0, 0)),
        ],
        out_shape=[
            jax.ShapeDtypeStruct((n, d), F32),
            jax.ShapeDtypeStruct((tiles * LOCAL_ROWS, d // 2), jnp.uint32),
            jax.ShapeDtypeStruct((SUBLANES, n), jnp.int32),
            jax.ShapeDtypeStruct((SUBLANES, n), F32),
            jax.ShapeDtypeStruct((tiles, SUBLANES, N_EXPERTS), F32),
        ],
        compiler_params=_params(("parallel",)),
        name="out_router",
    )(ret, att, x2, w_out, g, w_router_t, b_router_t)


def _convert_expert_weights(wu_ref, wdn_ref, wg_ref, wl_ref, wd_ref):
    ch = PERM_CHUNK
    hc = ch // 2
    r = lax.broadcasted_iota(jnp.int32, (ch, ch), 0)
    c = lax.broadcasted_iota(jnp.int32, (ch, ch), 1)
    perm = jnp.where(r == jnp.where(c < hc, 2 * c, 2 * (c - hc) + 1), 1.0, 0.0).astype(BF16)
    for j in range(wu_ref.shape[1] // ch):
        both = _dot(wu_ref[:, j * ch:(j + 1) * ch].astype(BF16), perm)
        wg_ref[:, j * hc:(j + 1) * hc] = both[:, :hc].astype(BF16)
        wl_ref[:, j * hc:(j + 1) * hc] = both[:, hc:].astype(BF16)
    wd_ref[...] = wdn_ref[...].astype(BF16)


def _moe_kernel(be_ref, nused_ref, ord_ref, ue_ref, full_ref, src_ref, dst_ref,
                xin_hbm, wu_hbm, bg_ref, bl_ref, wdn_hbm, bd_ref,
                loc_hbm, xbuf, ybuf, gsem, ssem, wu_buf, wdn_buf, wsem, wg_ref, wl_ref, wd_ref):
    del xin_hbm
    i = pl.program_id(0)
    n_used = nused_ref[0]
    last = pl.num_programs(0) - 1
    slot = lax.rem(i, 2)
    bm, half = xbuf.shape[1], xbuf.shape[2]

    def chunk(ref, c):
        first = c * CHUNK_ROWS if isinstance(c, int) else pl.multiple_of(c * CHUNK_ROWS, CHUNK_ROWS)
        return ref.at[pl.ds(first, CHUNK_ROWS), :]

    def start_gather(step, sl):
        for j in range(BLOCK_CHUNKS):
            pltpu.make_async_copy(chunk(loc_hbm, src_ref[step * BLOCK_CHUNKS + j]),
                                  chunk(xbuf.at[sl], j), gsem.at[sl]).start()

    def start_writeback(step, sl):
        for j in range(BLOCK_CHUNKS):
            dst = dst_ref[(step + PRIME_BLOCKS) * BLOCK_CHUNKS + j]
            pltpu.make_async_copy(chunk(ybuf.at[sl], j), chunk(loc_hbm, dst),
                                  ssem.at[sl]).start(priority=BULK_DMA_PRIORITY)

    def wait_gather(sl):
        pltpu.make_async_copy(loc_hbm.at[pl.ds(0, bm), :], xbuf.at[sl], gsem.at[sl]).wait()

    def wait_writeback(sl):
        pltpu.make_async_copy(ybuf.at[sl], loc_hbm.at[pl.ds(0, bm), :], ssem.at[sl]).wait()

    def weight_fetch(expert, buf):
        return (pltpu.make_async_copy(wu_hbm.at[expert], wu_buf.at[buf], wsem.at[buf, 0]),
                pltpu.make_async_copy(wdn_hbm.at[expert], wdn_buf.at[buf], wsem.at[buf, 1]))

    @pl.when(i == 0)
    def _():
        ybuf[...] = jnp.zeros_like(ybuf)
        start_gather(0, 0)
        start_writeback(-2, 0)
        for copy in weight_fetch(ue_ref[0], 0):
            copy.start(priority=BULK_DMA_PRIORITY)

    new_expert = jnp.logical_or(i == 0, be_ref[i] != be_ref[jnp.maximum(i - 1, 0)])

    @pl.when(jnp.logical_and(i < n_used, new_expert))
    def _():
        k = ord_ref[i]
        buf = lax.rem(k, 2)
        for copy in weight_fetch(be_ref[i], buf):
            copy.wait()

        @pl.when(k + 1 < nused_ref[1])
        def _():
            for copy in weight_fetch(ue_ref[k + 1], 1 - buf):
                copy.start(priority=BULK_DMA_PRIORITY)

        _convert_expert_weights(wu_buf.at[buf], wdn_buf.at[buf], wg_ref, wl_ref, wd_ref)

    def block_step(rows):
        wait_gather(slot)
        start_gather(jnp.minimum(i + 1, last), 1 - slot)
        x = jnp.concatenate(_unpack_bf16_pairs(xbuf[slot, :rows, :]), axis=1)
        start_writeback(i - 1, 1 - slot)

        expert = be_ref[i]

        def up(w_ref, b_ref):
            return _dot(x, w_ref[...]) + b_ref[expert]

        glu = jnp.minimum(up(wg_ref, bg_ref), SWIGLU_LIMIT)
        lin = jnp.clip(up(wl_ref, bl_ref), -SWIGLU_LIMIT, SWIGLU_LIMIT)
        act = glu * jax.nn.sigmoid(SWIGLU_ALPHA * glu) * (lin + 1.0)
        y = _dot(act.astype(BF16), wd_ref[...]) + bd_ref[expert]

        wait_writeback(slot)
        ybuf[slot, :rows, :] = _pack_bf16_pairs(y[:, :half], y[:, half:])

    used = i < n_used
    full = full_ref[i] > 0

    @pl.when(jnp.logical_and(used, full))
    def _():
        block_step(bm)

    @pl.when(jnp.logical_and(used, jnp.logical_not(full)))
    def _():
        block_step(bm // 2)

    @pl.when(i == n_used - 1)
    def _():
        start_writeback(i, slot)
        wait_writeback(1 - slot)
        wait_writeback(slot)
        wait_gather(1 - slot)


def _moe(block_e, n_used, block_ord, used_experts, block_full, chunk_src, chunk_dst,
         xloc, w_up, bg, bl, w_down, bd):
    half = xloc.shape[1]
    e, d, f2 = w_up.shape
    f = f2 // 2
    bm = MOE_BLOCK
    n_blocks = chunk_src.shape[0] // BLOCK_CHUNKS
    assert chunk_dst.shape[0] == (n_blocks + PRIME_BLOCKS) * BLOCK_CHUNKS
    prefetch = (block_e, n_used, block_ord, used_experts, block_full, chunk_src, chunk_dst)
    grid_spec = pltpu.PrefetchScalarGridSpec(
        num_scalar_prefetch=len(prefetch),
        grid=(n_blocks,),
        in_specs=[
            pl.BlockSpec(memory_space=pl.ANY),
            pl.BlockSpec(memory_space=pl.ANY),
            RESIDENT,
            RESIDENT,
            pl.BlockSpec(memory_space=pl.ANY),
            RESIDENT,
        ],
        out_specs=pl.BlockSpec(memory_space=pl.ANY),
        scratch_shapes=[
            pltpu.VMEM((2, bm, half), jnp.uint32),
            pltpu.VMEM((2, bm, half), jnp.uint32),
            pltpu.SemaphoreType.DMA((2,)),
            pltpu.SemaphoreType.DMA((2,)),
            pltpu.VMEM((2, d, f2), F32),
            pltpu.VMEM((2, f, d), F32),
            pltpu.SemaphoreType.DMA((2, 2)),
            pltpu.VMEM((d, f), BF16),
            pltpu.VMEM((d, f), BF16),
            pltpu.VMEM((f, d), BF16),
        ],
    )
    return pl.pallas_call(
        _moe_kernel,
        grid_spec=grid_spec,
        out_shape=jax.ShapeDtypeStruct(xloc.shape, jnp.uint32),
        input_output_aliases={len(prefetch): 0},
        compiler_params=_params(("arbitrary",)),
        name="moe",
    )(*prefetch, xloc, w_up, bg, bl, w_down, bd)


def _combine_ple_kernel(yloc_ref, slot_ref, gate_ref, h1_ref, p_ref, wpg_ref, wpp_ref, gple_ref,
                        gfin_ref, o_ref):
    local = lax.broadcasted_iota(jnp.int32, (LOCAL_ROWS, 1), 0)
    moe = []
    for s in range(h1_ref.shape[0] // ROUTE_TILE):
        tok = slice(s * ROUTE_TILE, (s + 1) * ROUTE_TILE)
        loc = slice(s * LOCAL_ROWS, (s + 1) * LOCAL_ROWS)
        rows = slot_ref[:, tok]
        gates = gate_ref[:, tok]
        weights = jnp.zeros((LOCAL_ROWS, ROUTE_TILE), F32)
        for k in range(TOP_K):
            weights = jnp.where(local == rows[k:k + 1, :], gates[k:k + 1, :], weights)
        weights = weights.astype(BF16)
        y_lo, y_hi = _unpack_bf16_pairs(yloc_ref[loc, :])
        moe.append(jnp.concatenate([_dot_tn(y_lo, weights), _dot_tn(y_hi, weights)], axis=0).T)
    h2 = h1_ref[...] + jnp.concatenate(moe, axis=0)
    gate = jax.nn.sigmoid(_dot(h2.astype(BF16), wpg_ref[...]))
    proj = _dot(p_ref[...].astype(BF16), wpp_ref[...])
    h3 = h2 + _rms(gate * proj, gple_ref[...])
    o_ref[...] = _rms(h3, gfin_ref[...])


def _combine_ple(yloc, slot_rows, gates, h1, p2, wpg, wpp, gple, gfin):
    n, d = h1.shape
    pd = p2.shape[1]
    per = COMBINE_STEP_TILES
    tm = per * ROUTE_TILE
    row = lambda i: (i, 0)
    col = lambda i: (0, i)
    return pl.pallas_call(
        _combine_ple_kernel,
        grid=(n // tm,),
        in_specs=[
            pl.BlockSpec((per * LOCAL_ROWS, d // 2), row),
            pl.BlockSpec((SUBLANES, tm), col),
            pl.BlockSpec((SUBLANES, tm), col),
            pl.BlockSpec((tm, d), row),
            pl.BlockSpec((tm, pd), row),
            RESIDENT,
            RESIDENT,
            RESIDENT,
            RESIDENT,
        ],
        out_specs=pl.BlockSpec((tm, d), row),
        out_shape=jax.ShapeDtypeStruct((n, d), F32),
        compiler_params=_params(("parallel",)),
        name="combine_ple",
    )(yloc, slot_rows, gates, h1, p2, wpg, wpp, gple, gfin)


def _routing_tables(counts):
    tiles = counts.shape[0]
    local_chunks = LOCAL_ROWS // CHUNK_ROWS
    assert 2 * BLOCK_CHUNKS < tiles * SPARE_CHUNKS
    run = (counts + CHUNK_ROWS - 1) // CHUNK_ROWS
    local_start = jnp.cumsum(run, axis=1) - run
    run_end = jnp.cumsum(run, axis=0)
    run_start = run_end - run
    total = run_end[-1]
    padded = (total + BLOCK_CHUNKS - 1) // BLOCK_CHUNKS * BLOCK_CHUNKS
    pend = jnp.cumsum(padded)
    poff = pend - padded

    slots = tiles * ROUTE_TILE * TOP_K
    max_chunks = (slots + tiles * N_EXPERTS * (CHUNK_ROWS - 1)) // CHUNK_ROWS + N_EXPERTS * (BLOCK_CHUNKS - 1)
    n_chunks = -(-max_chunks // BLOCK_CHUNKS) * BLOCK_CHUNKS
    lookup = lambda onehot, table: jnp.dot(onehot, table.astype(F32), precision=lax.Precision.HIGHEST)
    q = jnp.arange(n_chunks, dtype=jnp.int32)
    e = jnp.minimum(jnp.sum(q[:, None] >= pend[None, :], axis=1), N_EXPERTS - 1).astype(jnp.int32)
    e_hot = (e[:, None] == jnp.arange(N_EXPERTS)[None, :]).astype(F32)
    per_expert = lookup(e_hot, jnp.stack([poff, total], axis=1)).astype(jnp.int32)
    j = q - per_expert[:, 0]
    valid = j < per_expert[:, 1]
    ends = lookup(e_hot, run_end.T).astype(jnp.int32)
    tile = jnp.minimum(jnp.sum(ends <= j[:, None], axis=1), tiles - 1).astype(jnp.int32)
    tile_hot = (tile[:, None] == jnp.arange(tiles)[None, :]).astype(F32)
    shift = jnp.sum(lookup(tile_hot, local_start - run_start) * e_hot, axis=1).astype(jnp.int32)
    src = jnp.where(valid, tile * local_chunks + shift + j, tiles * local_chunks - 1)
    spare = (q // BLOCK_CHUNKS % 2) * BLOCK_CHUNKS + q % BLOCK_CHUNKS
    spare_chunk = (spare // SPARE_CHUNKS) * local_chunks + (local_chunks - SPARE_CHUNKS) + spare % SPARE_CHUNKS
    dst = jnp.where(valid, tile * local_chunks + shift + j, spare_chunk)
    dst = jnp.concatenate([spare_chunk[:PRIME_BLOCKS * BLOCK_CHUNKS], dst])
    has_rows = total > 0
    ordinal = jnp.cumsum(has_rows.astype(jnp.int32)) - 1
    used_experts = jnp.argsort(jnp.logical_not(has_rows), stable=True).astype(jnp.int32)
    block_e = e[::BLOCK_CHUNKS]
    block_ord = jnp.sum(jnp.where(block_e[:, None] == jnp.arange(N_EXPERTS)[None, :], ordinal[None, :], 0), axis=1)
    n_used = jnp.stack([pend[-1] // BLOCK_CHUNKS, jnp.sum(has_rows)]).astype(jnp.int32)
    block_full = valid[BLOCK_CHUNKS // 2::BLOCK_CHUNKS].astype(jnp.int32)
    return (block_e, n_used, block_ord.astype(jnp.int32), used_experts, block_full,
            src.astype(jnp.int32), dst.astype(jnp.int32))


def kernel(x, p, positions, rel_bias_table, g_mix_norm, w_in, b_in, ret_norm_g, att_sinks, w_out,
           g_moe_norm, w_router, b_router, w_up, b_up, w_down, b_down, w_ple_gate, w_ple_proj,
           g_ple_norm, g_final):
    b, s, d = x.shape
    depth = w_in.shape[0]
    assert depth == 1, "single-layer stack only"
    n = b * s
    x2 = x.reshape(n, d)

    half = RET_DK // 2
    inv = ROPE_BASE ** (-jnp.arange(half, dtype=F32) / half)
    ang = (positions.astype(F32)[..., None] * inv).reshape(n, half)

    qr, kr, vr, gr, qa, kva = _in_proj(
        x2, g_mix_norm[0][None], w_in[0].astype(BF16), b_in[0][None], jnp.cos(ang), jnp.sin(ang))

    shape3 = lambda t: t.reshape(b, s, t.shape[-1])
    ret = _retention(shape3(qr), shape3(kr), shape3(vr), shape3(gr), ret_norm_g[0])
    att = _attention(shape3(qa), shape3(kva), att_sinks[0], rel_bias_table)

    h1, xloc, slot_rows, gates, counts = _out_router(
        ret.reshape(n, RET_WIDTH), att.reshape(n, ATT_WIDTH), x2, w_out[0].astype(BF16),
        g_moe_norm[0][None], w_router[0].T.astype(BF16), b_router[0][:, None])

    tables = _routing_tables(counts[:, 0, :].astype(jnp.int32))

    yloc = _moe(*tables, xloc, w_up[0],
                b_up[0][:, None, 0::2], b_up[0][:, None, 1::2],
                w_down[0], b_down[0].reshape(N_EXPERTS, 1, d))

    out = _combine_ple(yloc, slot_rows, gates, h1, p[0].reshape(n, -1),
                       w_ple_gate[0].astype(BF16), w_ple_proj[0].astype(BF16),
                       g_ple_norm[0][None], g_final[None])
    return out.reshape(b, s, d)
```

```python
import math

import jax
import jax.numpy as jnp
from jax import lax
from jax.experimental import pallas as pl
from jax.experimental.pallas import tpu as pltpu

RET_HEADS = 4
RET_DK = 128
RET_DV = 128
RET_CHUNK = 128
ROPE_BASE = 10000.0
ATT_Q_HEADS = 8
ATT_KV_HEADS = 2
ATT_GROUP = ATT_Q_HEADS // ATT_KV_HEADS
ATT_HEAD_DIM = 64
WINDOW = 128
ATT_BLOCK = 128
NUM_BUCKETS = 32
MAX_DISTANCE = 128
N_EXPERTS = 32
TOP_K = 4
SWIGLU_ALPHA = 1.702
SWIGLU_LIMIT = 7.0
EPS = 1e-5

RET_WIDTH = RET_HEADS * RET_DV
ATT_WIDTH = ATT_Q_HEADS * ATT_HEAD_DIM
KV_WIDTH = ATT_KV_HEADS * ATT_HEAD_DIM

VMEM_LIMIT_BYTES = 48 * 1024 * 1024

TOKEN_TILE = 1024
RET_STEP_CHUNKS = 2
SUBLANES = 8
ROUTE_TILE = 256
ROUTE_STEP_TILES = 4
COMBINE_STEP_TILES = 4
CHUNK_ROWS = SUBLANES
LOCAL_ROWS = ROUTE_TILE * TOP_K + N_EXPERTS * CHUNK_ROWS
SPARE_CHUNKS = (LOCAL_ROWS - ROUTE_TILE * TOP_K - N_EXPERTS * (CHUNK_ROWS - 1)) // CHUNK_ROWS
MOE_BLOCK = 512
BLOCK_CHUNKS = MOE_BLOCK // CHUNK_ROWS
PRIME_BLOCKS = 2
MLP_HIDDEN_PARTS = 2
ATT_STEP_BLOCKS = 16
PERM_CHUNK = 256

F32 = jnp.float32
BF16 = jnp.bfloat16
NEG = float(jnp.finfo(jnp.float32).min)


def _rms(x, g):
    return x * lax.rsqrt(jnp.mean(x * x, axis=-1, keepdims=True) + EPS) * g


def _dot(a, b):
    return jnp.dot(a, b, preferred_element_type=F32)


def _dot_nt(a, b):
    return lax.dot_general(a, b, (((1,), (1,)), ((), ())), preferred_element_type=F32)


def _dot_tn(a, b):
    return lax.dot_general(a, b, (((0,), (0,)), ((), ())), preferred_element_type=F32)


def _params(semantics):
    return pltpu.CompilerParams(dimension_semantics=semantics, vmem_limit_bytes=VMEM_LIMIT_BYTES)


RESIDENT = pl.BlockSpec(memory_space=pltpu.VMEM)


def _in_proj_kernel(x_ref, g_ref, w_ref, b_ref, cos_ref, sin_ref,
                    qr_ref, kr_ref, vr_ref, gr_ref, qa_ref, kva_ref):
    hn = _rms(x_ref[...], g_ref[...]).astype(BF16)
    cos = jnp.concatenate([cos_ref[...], cos_ref[...]], axis=1)
    sin = jnp.concatenate([-sin_ref[...], sin_ref[...]], axis=1)

    def proj(lo, hi):
        return _dot(hn, w_ref[:, lo:hi]) + b_ref[:, lo:hi]

    def rope(t):
        heads = []
        for h in range(RET_HEADS):
            th = t[:, h * RET_DK:(h + 1) * RET_DK]
            heads.append(th * cos + pltpu.roll(th, RET_DK // 2, 1) * sin)
        return jnp.concatenate(heads, axis=1)

    o = 0
    qr_ref[...] = rope(proj(o, o + RET_WIDTH)).astype(BF16)
    o += RET_WIDTH
    kr_ref[...] = (rope(proj(o, o + RET_WIDTH)) * (RET_DK ** -0.5)).astype(BF16)
    o += RET_WIDTH
    vr_ref[...] = proj(o, o + RET_WIDTH).astype(BF16)
    o += RET_WIDTH
    gr_ref[...] = proj(o, o + RET_WIDTH).astype(BF16)
    o += RET_WIDTH
    qa_ref[...] = (proj(o, o + ATT_WIDTH) * (ATT_HEAD_DIM ** -0.5)).astype(BF16)
    o += ATT_WIDTH
    low = lax.broadcasted_iota(jnp.int32, (1, KV_WIDTH), 1) < ATT_HEAD_DIM
    slabs = []
    for part in range(2):
        t = proj(o + part * KV_WIDTH, o + (part + 1) * KV_WIDTH)
        swapped = pltpu.roll(t, ATT_HEAD_DIM, 1)
        slabs += [jnp.where(low, t, swapped), jnp.where(low, swapped, t)]
    kva_ref[...] = jnp.concatenate(slabs, axis=1).astype(BF16)


def _in_proj(x2, g, w_in, b_in, cosf, sinf):
    n, d = x2.shape
    tm = TOKEN_TILE
    row = lambda i: (i, 0)
    outs = [RET_WIDTH, RET_WIDTH, RET_WIDTH, RET_WIDTH, ATT_WIDTH, 4 * KV_WIDTH]
    return pl.pallas_call(
        _in_proj_kernel,
        grid=(n // tm,),
        in_specs=[
            pl.BlockSpec((tm, d), row),
            RESIDENT,
            RESIDENT,
            RESIDENT,
            pl.BlockSpec((tm, RET_DK // 2), row),
            pl.BlockSpec((tm, RET_DK // 2), row),
        ],
        out_specs=[pl.BlockSpec((tm, w), row) for w in outs],
        out_shape=[jax.ShapeDtypeStruct((n, w), BF16) for w in outs],
        compiler_params=_params(("parallel",)),
        name="in_proj",
    )(x2, g, w_in, b_in, cosf, sinf)


def _retention_kernel(q_ref, k_ref, v_ref, g_ref, dmask_ref, qdec_ref, kdec_ref, cdec_ref,
                      gn_ref, o_ref, state_ref):
    batch = q_ref.shape[0]

    @pl.when(pl.program_id(0) == 0)
    def _():
        state_ref[...] = jnp.zeros_like(state_ref)

    def per_batch(b, carry):
        for h in range(RET_HEADS):
            sl = slice(h * RET_DK, (h + 1) * RET_DK)
            for c in range(q_ref.shape[1] // RET_CHUNK):
                rows = slice(c * RET_CHUNK, (c + 1) * RET_CHUNK)
                q = q_ref[b, rows, sl]
                k = k_ref[b, rows, sl]
                v = v_ref[b, rows, sl]
                scores = _dot_nt(q, k) * dmask_ref[h]
                inner = _dot(scores.astype(BF16), v)
                state = state_ref[b * RET_HEADS + h]
                cross = _dot(q, state.astype(BF16)) * qdec_ref[h]
                kd = (k.astype(F32) * kdec_ref[h]).astype(BF16)
                state_ref[b * RET_HEADS + h] = state * cdec_ref[h] + _dot_tn(kd, v)
                o = _rms(inner + cross, gn_ref[h])
                gate = g_ref[b, rows, sl].astype(F32)
                o_ref[b, rows, sl] = (gate * jax.nn.sigmoid(gate) * o).astype(BF16)
        return carry

    lax.fori_loop(0, batch, per_batch, 0, unroll=True)


def _retention(q, k, v, g, gn):
    b, s, w = q.shape
    c = RET_CHUNK
    log_g = jnp.log(1.0 - 2.0 ** (-5.0 - jnp.arange(RET_HEADS, dtype=F32)))
    ci = jnp.arange(c)
    diff = (ci[:, None] - ci[None, :]).astype(F32)
    dmask = jnp.where(diff >= 0, jnp.exp(jnp.maximum(diff, 0.0)[None] * log_g[:, None, None]), 0.0)
    cf = ci.astype(F32)
    kdec = jnp.exp((c - 1 - cf)[None, :] * log_g[:, None])
    qdec = jnp.exp((cf + 1)[None, :] * log_g[:, None])
    cdec = jnp.exp(c * log_g)
    kdec = jnp.broadcast_to(kdec[:, :, None], (RET_HEADS, c, RET_DK))
    qdec = jnp.broadcast_to(qdec[:, :, None], (RET_HEADS, c, RET_DV))
    cdec = jnp.broadcast_to(cdec[:, None, None], (RET_HEADS, RET_DK, RET_DV))
    gn3 = gn.reshape(RET_HEADS, 1, RET_DV)

    seq = lambda i: (0, i, 0)
    rows = RET_STEP_CHUNKS * c
    return pl.pallas_call(
        _retention_kernel,
        grid=(s // rows,),
        in_specs=[pl.BlockSpec((b, rows, w), seq)] * 4 + [RESIDENT] * 5,
        out_specs=pl.BlockSpec((b, rows, w), seq),
        out_shape=jax.ShapeDtypeStruct((b, s, w), BF16),
        scratch_shapes=[pltpu.VMEM((b * RET_HEADS, RET_DK, RET_DV), F32)],
        compiler_params=_params(("arbitrary",)),
        name="retention",
    )(q, k, v, g, dmask, qdec, kdec, cdec, gn3)


def _attention_kernel(sinks_ref, q_ref, kvp_ref, kvc_ref, bias_ref, o_ref):
    blk = ATT_BLOCK
    slab = 2 * ATT_HEAD_DIM
    r = lax.broadcasted_iota(jnp.int32, (blk, blk), 0)
    c = lax.broadcasted_iota(jnp.int32, (blk, blk), 1)
    own = c <= r
    first_ok = jnp.logical_or(own, pl.program_id(1) > 0)
    low = lax.broadcasted_iota(jnp.int32, (1, slab), 1) < ATT_HEAD_DIM
    zero = jnp.zeros((), BF16)
    for i in range(q_ref.shape[1] // blk):
        rows = slice(i * blk, (i + 1) * blk)
        prev = (lambda cs: kvp_ref[0, :, cs]) if i == 0 else (
            lambda cs, rs=slice((i - 1) * blk, i * blk): kvc_ref[0, rs, cs])
        slabs = []
        for a in range(ATT_Q_HEADS // 2):
            g = (2 * a) // ATT_GROUP
            ks = slice(g * slab, (g + 1) * slab)
            vs = slice((ATT_KV_HEADS + g) * slab, (ATT_KV_HEADS + g + 1) * slab)
            k = jnp.concatenate([prev(ks), kvc_ref[0, rows, ks]], axis=0)
            v_prev, v_own = prev(vs), kvc_ref[0, rows, vs]
            q = q_ref[0, rows, a * slab:(a + 1) * slab]
            halves = []
            for part in range(2):
                h = 2 * a + part
                qz = jnp.where(low if part == 0 else jnp.logical_not(low), q, zero)
                both = _dot_nt(qz, k)
                s = jnp.where(own, both[:, blk:], both[:, :blk]) + bias_ref[h]
                if i == 0:
                    s = jnp.where(first_ok, s, NEG)
                sink = sinks_ref[h]
                m = jnp.maximum(jnp.max(s, axis=1, keepdims=True), sink)
                p = jnp.exp(s - m)
                denom = jnp.sum(p, axis=1, keepdims=True) + jnp.exp(sink - m)
                p_own = jnp.where(own, p, 0.0).astype(BF16)
                p_prev = jnp.where(own, 0.0, p).astype(BF16)
                halves.append((_dot(p_own, v_own) + _dot(p_prev, v_prev)) / denom)
            slabs.append(jnp.where(low, halves[0], halves[1]))
        o_ref[0, rows, :] = jnp.concatenate(slabs, axis=1).astype(BF16)


def _t5_bucket(n):
    max_exact = NUM_BUCKETS // 2
    nf = jnp.maximum(n, 1).astype(F32)
    large = max_exact + (jnp.log(nf / max_exact) / math.log(MAX_DISTANCE / max_exact)
                         * (NUM_BUCKETS - max_exact)).astype(jnp.int32)
    large = jnp.minimum(large, NUM_BUCKETS - 1)
    return jnp.where(n < max_exact, n, large)


def _attention(q, kv, sinks, bias_table):
    b, s, _ = q.shape
    blk = ATT_BLOCK
    assert WINDOW == blk
    qi = jnp.arange(blk)[:, None]
    ki = jnp.arange(blk)[None, :]
    dist = jnp.where(ki <= qi, qi - ki, qi + blk - ki)
    onehot = jax.nn.one_hot(_t5_bucket(dist), NUM_BUCKETS, dtype=F32)
    bias = jnp.einsum("qkn,nh->hqk", onehot, bias_table.astype(F32), precision=lax.Precision.HIGHEST)

    sub = ATT_STEP_BLOCKS
    kvw = kv.shape[-1]
    return pl.pallas_call(
        _attention_kernel,
        grid=(b, s // (sub * blk)),
        in_specs=[
            pl.BlockSpec(memory_space=pltpu.SMEM),
            pl.BlockSpec((1, sub * blk, ATT_WIDTH), lambda i, j: (i, j, 0)),
            pl.BlockSpec((1, blk, kvw), lambda i, j: (i, jnp.maximum(sub * j - 1, 0), 0)),
            pl.BlockSpec((1, sub * blk, kvw), lambda i, j: (i, j, 0)),
            RESIDENT,
        ],
        out_specs=pl.BlockSpec((1, sub * blk, ATT_WIDTH), lambda i, j: (i, j, 0)),
        out_shape=jax.ShapeDtypeStruct((b, s, ATT_WIDTH), BF16),
        compiler_params=_params(("parallel", "arbitrary")),
        name="attention",
    )(sinks, q, kv, kv, bias)


def _pack_bf16_pairs(lo, hi):
    return pltpu.bitcast(pltpu.pack_elementwise([lo, hi], packed_dtype=BF16), jnp.uint32)


def _unpack_bf16_pairs(packed):
    lo = pltpu.bitcast(packed << 16, F32).astype(BF16)
    hi = pltpu.bitcast(packed & jnp.uint32(0xFFFF0000), F32).astype(BF16)
    return lo, hi


def _stack_rows(rows, dtype):
    t = rows[0].shape[1]
    sub = lax.broadcasted_iota(jnp.int32, (SUBLANES, t), 0)
    out = jnp.zeros((SUBLANES, t), dtype)
    for i, r in enumerate(rows):
        out = jnp.where(sub == i, r.astype(dtype), out)
    return out


def _route_tile(logits):
    t = logits.shape[1]
    expert = lax.broadcasted_iota(jnp.int32, logits.shape, 0)
    vals = logits
    top_v, sels = [], []
    for _ in range(TOP_K):
        m = jnp.max(vals, axis=0, keepdims=True)
        idx = jnp.min(jnp.where(vals == m, expert, N_EXPERTS), axis=0, keepdims=True)
        sel = expert == idx
        vals = jnp.where(sel, -jnp.inf, vals)
        top_v.append(m)
        sels.append(sel)
    exps = [jnp.exp(v - top_v[0]) for v in top_v]
    denom = exps[0] + exps[1] + exps[2] + exps[3]
    gates = [e / denom for e in exps]

    chosen = jnp.zeros(logits.shape, F32)
    for sel in sels:
        chosen = chosen + sel.astype(F32)
    chosen_b = chosen.astype(BF16)
    r = lax.broadcasted_iota(jnp.int32, (t, t), 0)
    c = lax.broadcasted_iota(jnp.int32, (t, t), 1)
    earlier = jnp.where(r < c, 1.0, 0.0).astype(BF16)
    before = _dot(chosen_b, earlier)
    counts = jnp.sum(chosen, axis=1, keepdims=True)
    chunks = jnp.floor((counts + (CHUNK_ROWS - 1)) * (1.0 / CHUNK_ROWS))
    er = lax.broadcasted_iota(jnp.int32, (N_EXPERTS, N_EXPERTS), 0)
    ec = lax.broadcasted_iota(jnp.int32, (N_EXPERTS, N_EXPERTS), 1)
    lower = jnp.where(ec < er, 1.0, 0.0).astype(BF16)
    start = _dot(lower, jnp.broadcast_to(chunks, logits.shape).astype(BF16)) * float(CHUNK_ROWS)
    rows = [jnp.sum(jnp.where(sel, before + start, 0.0), axis=0, keepdims=True).astype(jnp.int32)
            for sel in sels]
    counts_rows = _dot_nt(jnp.ones((SUBLANES, t), BF16), chosen_b)

    local = lax.broadcasted_iota(jnp.int32, (LOCAL_ROWS, 1), 0)
    select = jnp.zeros((LOCAL_ROWS, t), F32)
    for row in rows:
        select = jnp.where(local == row, 1.0, select)
    return rows, gates, counts_rows, select.astype(BF16)


def _out_router_kernel(ret_ref, att_ref, x_ref, wo_ref, g_ref, wrt_ref, brt_ref,
                       h1_ref, xloc_ref, slot_ref, gate_ref, cnt_ref):
    half = x_ref.shape[1] // 2
    h1 = (x_ref[...] + _dot(ret_ref[...], wo_ref[:RET_WIDTH, :])
          + _dot(att_ref[...], wo_ref[RET_WIDTH:, :]))
    h1_ref[...] = h1
    xb_all = _rms(h1, g_ref[...]).astype(BF16)
    logits_all = _dot_nt(wrt_ref[...], xb_all) + brt_ref[...]
    for s in range(x_ref.shape[0] // ROUTE_TILE):
        tok = slice(s * ROUTE_TILE, (s + 1) * ROUTE_TILE)
        loc = slice(s * LOCAL_ROWS, (s + 1) * LOCAL_ROWS)
        xb = xb_all[tok, :]
        rows, gates, counts_rows, select = _route_tile(logits_all[:, tok])
        xloc = _dot(select, xb)
        xloc_ref[loc, :] = _pack_bf16_pairs(xloc[:, :half], xloc[:, half:])
        slot_ref[:, tok] = _stack_rows(rows, jnp.int32)
        gate_ref[:, tok] = _stack_rows(gates, F32)
        cnt_ref[s] = counts_rows


def _out_router(ret, att, x2, w_out, g, w_router_t, b_router_t):
    n, d = x2.shape
    per = ROUTE_STEP_TILES
    tm = per * ROUTE_TILE
    tiles = n // ROUTE_TILE
    row = lambda i: (i, 0)
    col = lambda i: (0, i)
    return pl.pallas_call(
        _out_router_kernel,
        grid=(n // tm,),
        in_specs=[
            pl.BlockSpec((tm, RET_WIDTH), row),
            pl.BlockSpec((tm, ATT_WIDTH), row),
            pl.BlockSpec((tm, d), row),
            RESIDENT,
            RESIDENT,
            RESIDENT,
            RESIDENT,
        ],
        out_specs=[
            pl.BlockSpec((tm, d), row),
            pl.BlockSpec((per * LOCAL_ROWS, d // 2), row),
            pl.BlockSpec((SUBLANES, tm), col),
            pl.BlockSpec((SUBLANES, tm), col),
            pl.BlockSpec((per, SUBLANES, N_EXPERTS), lambda i: (i, 0, 0)),
        ],
        out_shape=[
            jax.ShapeDtypeStruct((n, d), F32),
            jax.ShapeDtypeStruct((tiles * LOCAL_ROWS, d // 2), jnp.uint32),
            jax.ShapeDtypeStruct((SUBLANES, n), jnp.int32),
            jax.ShapeDtypeStruct((SUBLANES, n), F32),
            jax.ShapeDtypeStruct((tiles, SUBLANES, N_EXPERTS), F32),
        ],
        compiler_params=_params(("parallel",)),
        name="out_router",
    )(ret, att, x2, w_out, g, w_router_t, b_router_t)


def _convert_expert_weights(wu_ref, wdn_ref, wg_ref, wl_ref, wd_ref):
    ch = PERM_CHUNK
    hc = ch // 2
    r = lax.broadcasted_iota(jnp.int32, (ch, ch), 0)
    c = lax.broadcasted_iota(jnp.int32, (ch, ch), 1)
    perm = jnp.where(r == jnp.where(c < hc, 2 * c, 2 * (c - hc) + 1), 1.0, 0.0).astype(BF16)
    for j in range(wu_ref.shape[1] // ch):
        both = _dot(wu_ref[:, j * ch:(j + 1) * ch].astype(BF16), perm)
        wg_ref[:, j * hc:(j + 1) * hc] = both[:, :hc].astype(BF16)
        wl_ref[:, j * hc:(j + 1) * hc] = both[:, hc:].astype(BF16)
    wd_ref[...] = wdn_ref[...].astype(BF16)


def _moe_kernel(be_ref, nused_ref, ord_ref, ue_ref, full_ref, src_ref, dst_ref,
                xin_hbm, wu_hbm, bg_ref, bl_ref, wdn_hbm, bd_ref,
                loc_hbm, xbuf, ybuf, gsem, ssem, wu_buf, wdn_buf, wsem, wg_ref, wl_ref, wd_ref):
    del xin_hbm
    i = pl.program_id(0)
    n_used = nused_ref[0]
    last = pl.num_programs(0) - 1
    slot = lax.rem(i, 2)
    bm, half = xbuf.shape[1], xbuf.shape[2]

    def chunk(ref, c):
        first = c * CHUNK_ROWS if isinstance(c, int) else pl.multiple_of(c * CHUNK_ROWS, CHUNK_ROWS)
        return ref.at[pl.ds(first, CHUNK_ROWS), :]

    def start_gather(step, sl):
        for j in range(BLOCK_CHUNKS):
            pltpu.make_async_copy(chunk(loc_hbm, src_ref[step * BLOCK_CHUNKS + j]),
                                  chunk(xbuf.at[sl], j), gsem.at[sl]).start()

    def start_writeback(step, sl):
        for j in range(BLOCK_CHUNKS):
            dst = dst_ref[(step + PRIME_BLOCKS) * BLOCK_CHUNKS + j]
            pltpu.make_async_copy(chunk(ybuf.at[sl], j), chunk(loc_hbm, dst), ssem.at[sl]).start()

    def wait_gather(sl):
        pltpu.make_async_copy(loc_hbm.at[pl.ds(0, bm), :], xbuf.at[sl], gsem.at[sl]).wait()

    def wait_writeback(sl):
        pltpu.make_async_copy(ybuf.at[sl], loc_hbm.at[pl.ds(0, bm), :], ssem.at[sl]).wait()

    def weight_fetch(expert, buf):
        return (pltpu.make_async_copy(wu_hbm.at[expert], wu_buf.at[buf], wsem.at[buf, 0]),
                pltpu.make_async_copy(wdn_hbm.at[expert], wdn_buf.at[buf], wsem.at[buf, 1]))

    @pl.when(i == 0)
    def _():
        ybuf[...] = jnp.zeros_like(ybuf)
        start_gather(0, 0)
        start_writeback(-2, 0)
        for copy in weight_fetch(ue_ref[0], 0):
            copy.start()

    new_expert = jnp.logical_or(i == 0, be_ref[i] != be_ref[jnp.maximum(i - 1, 0)])

    @pl.when(jnp.logical_and(i < n_used, new_expert))
    def _():
        k = ord_ref[i]
        buf = lax.rem(k, 2)
        for copy in weight_fetch(be_ref[i], buf):
            copy.wait()

        @pl.when(k + 1 < nused_ref[1])
        def _():
            for copy in weight_fetch(ue_ref[k + 1], 1 - buf):
                copy.start()

        _convert_expert_weights(wu_buf.at[buf], wdn_buf.at[buf], wg_ref, wl_ref, wd_ref)

    def block_step(rows):
        wait_gather(slot)
        start_gather(jnp.minimum(i + 1, last), 1 - slot)
        x = jnp.concatenate(_unpack_bf16_pairs(xbuf[slot, :rows, :]), axis=1)
        start_writeback(i - 1, 1 - slot)

        expert = be_ref[i]

        y = bd_ref[expert]
        hidden = wg_ref.shape[1]
        for part in range(MLP_HIDDEN_PARTS):
            cols = slice(part * hidden // MLP_HIDDEN_PARTS, (part + 1) * hidden // MLP_HIDDEN_PARTS)

            def up(w_ref, b_ref):
                return _dot(x, w_ref[:, cols]) + b_ref[expert][:, cols]

            glu = jnp.minimum(up(wg_ref, bg_ref), SWIGLU_LIMIT)
            lin = jnp.clip(up(wl_ref, bl_ref), -SWIGLU_LIMIT, SWIGLU_LIMIT)
            act = glu * jax.nn.sigmoid(SWIGLU_ALPHA * glu) * (lin + 1.0)
            y = y + _dot(act.astype(BF16), wd_ref[cols, :])

        wait_writeback(slot)
        ybuf[slot, :rows, :] = _pack_bf16_pairs(y[:, :half], y[:, half:])

    used = i < n_used
    full = full_ref[i] > 0

    @pl.when(jnp.logical_and(used, full))
    def _():
        block_step(bm)

    @pl.when(jnp.logical_and(used, jnp.logical_not(full)))
    def _():
        block_step(bm // 2)

    @pl.when(i == n_used - 1)
    def _():
        start_writeback(i, slot)
        wait_writeback(1 - slot)
        wait_writeback(slot)
        wait_gather(1 - slot)


def _moe(block_e, n_used, block_ord, used_experts, block_full, chunk_src, chunk_dst,
         xloc, w_up, bg, bl, w_down, bd):
    half = xloc.shape[1]
    e, d, f2 = w_up.shape
    f = f2 // 2
    bm = MOE_BLOCK
    n_blocks = chunk_src.shape[0] // BLOCK_CHUNKS
    assert chunk_dst.shape[0] == (n_blocks + PRIME_BLOCKS) * BLOCK_CHUNKS
    prefetch = (block_e, n_used, block_ord, used_experts, block_full, chunk_src, chunk_dst)
    grid_spec = pltpu.PrefetchScalarGridSpec(
        num_scalar_prefetch=len(prefetch),
        grid=(n_blocks,),
        in_specs=[
            pl.BlockSpec(memory_space=pl.ANY),
            pl.BlockSpec(memory_space=pl.ANY),
            RESIDENT,
            RESIDENT,
            pl.BlockSpec(memory_space=pl.ANY),
            RESIDENT,
        ],
        out_specs=pl.BlockSpec(memory_space=pl.ANY),
        scratch_shapes=[
            pltpu.VMEM((2, bm, half), jnp.uint32),
            pltpu.VMEM((2, bm, half), jnp.uint32),
            pltpu.SemaphoreType.DMA((2,)),
            pltpu.SemaphoreType.DMA((2,)),
            pltpu.VMEM((2, d, f2), F32),
            pltpu.VMEM((2, f, d), F32),
            pltpu.SemaphoreType.DMA((2, 2)),
            pltpu.VMEM((d, f), BF16),
            pltpu.VMEM((d, f), BF16),
            pltpu.VMEM((f, d), BF16),
        ],
    )
    return pl.pallas_call(
        _moe_kernel,
        grid_spec=grid_spec,
        out_shape=jax.ShapeDtypeStruct(xloc.shape, jnp.uint32),
        input_output_aliases={len(prefetch): 0},
        compiler_params=_params(("arbitrary",)),
        name="moe",
    )(*prefetch, xloc, w_up, bg, bl, w_down, bd)


def _combine_ple_kernel(yloc_ref, slot_ref, gate_ref, h1_ref, p_ref, wpg_ref, wpp_ref, gple_ref,
                        gfin_ref, o_ref):
    local = lax.broadcasted_iota(jnp.int32, (LOCAL_ROWS, 1), 0)
    moe = []
    for s in range(h1_ref.shape[0] // ROUTE_TILE):
        tok = slice(s * ROUTE_TILE, (s + 1) * ROUTE_TILE)
        loc = slice(s * LOCAL_ROWS, (s + 1) * LOCAL_ROWS)
        rows = slot_ref[:, tok]
        gates = gate_ref[:, tok]
        weights = jnp.zeros((LOCAL_ROWS, ROUTE_TILE), F32)
        for k in range(TOP_K):
            weights = jnp.where(local == rows[k:k + 1, :], gates[k:k + 1, :], weights)
        weights = weights.astype(BF16)
        y_lo, y_hi = _unpack_bf16_pairs(yloc_ref[loc, :])
        moe.append(jnp.concatenate([_dot_tn(y_lo, weights), _dot_tn(y_hi, weights)], axis=0).T)
    h2 = h1_ref[...] + jnp.concatenate(moe, axis=0)
    gate = jax.nn.sigmoid(_dot(h2.astype(BF16), wpg_ref[...]))
    proj = _dot(p_ref[...].astype(BF16), wpp_ref[...])
    h3 = h2 + _rms(gate * proj, gple_ref[...])
    o_ref[...] = _rms(h3, gfin_ref[...])


def _combine_ple(yloc, slot_rows, gates, h1, p2, wpg, wpp, gple, gfin):
    n, d = h1.shape
    pd = p2.shape[1]
    per = COMBINE_STEP_TILES
    tm = per * ROUTE_TILE
    row = lambda i: (i, 0)
    col = lambda i: (0, i)
    return pl.pallas_call(
        _combine_ple_kernel,
        grid=(n // tm,),
        in_specs=[
            pl.BlockSpec((per * LOCAL_ROWS, d // 2), row),
            pl.BlockSpec((SUBLANES, tm), col),
            pl.BlockSpec((SUBLANES, tm), col),
            pl.BlockSpec((tm, d), row),
            pl.BlockSpec((tm, pd), row),
            RESIDENT,
            RESIDENT,
            RESIDENT,
            RESIDENT,
        ],
        out_specs=pl.BlockSpec((tm, d), row),
        out_shape=jax.ShapeDtypeStruct((n, d), F32),
        compiler_params=_params(("parallel",)),
        name="combine_ple",
    )(yloc, slot_rows, gates, h1, p2, wpg, wpp, gple, gfin)


def _routing_tables(counts):
    tiles = counts.shape[0]
    local_chunks = LOCAL_ROWS // CHUNK_ROWS
    assert 2 * BLOCK_CHUNKS < tiles * SPARE_CHUNKS
    run = (counts + CHUNK_ROWS - 1) // CHUNK_ROWS
    local_start = jnp.cumsum(run, axis=1) - run
    run_end = jnp.cumsum(run, axis=0)
    run_start = run_end - run
    total = run_end[-1]
    padded = (total + BLOCK_CHUNKS - 1) // BLOCK_CHUNKS * BLOCK_CHUNKS
    pend = jnp.cumsum(padded)
    poff = pend - padded

    slots = tiles * ROUTE_TILE * TOP_K
    max_chunks = (slots + tiles * N_EXPERTS * (CHUNK_ROWS - 1)) // CHUNK_ROWS + N_EXPERTS * (BLOCK_CHUNKS - 1)
    n_chunks = -(-max_chunks // BLOCK_CHUNKS) * BLOCK_CHUNKS
    lookup = lambda onehot, table: jnp.dot(onehot, table.astype(F32), precision=lax.Precision.HIGHEST)
    q = jnp.arange(n_chunks, dtype=jnp.int32)
    e = jnp.minimum(jnp.sum(q[:, None] >= pend[None, :], axis=1), N_EXPERTS - 1).astype(jnp.int32)
    e_hot = (e[:, None] == jnp.arange(N_EXPERTS)[None, :]).astype(F32)
    per_expert = lookup(e_hot, jnp.stack([poff, total], axis=1)).astype(jnp.int32)
    j = q - per_expert[:, 0]
    valid = j < per_expert[:, 1]
    ends = lookup(e_hot, run_end.T).astype(jnp.int32)
    tile = jnp.minimum(jnp.sum(ends <= j[:, None], axis=1), tiles - 1).astype(jnp.int32)
    tile_hot = (tile[:, None] == jnp.arange(tiles)[None, :]).astype(F32)
    shift = jnp.sum(lookup(tile_hot, local_start - run_start) * e_hot, axis=1).astype(jnp.int32)
    src = jnp.where(valid, tile * local_chunks + shift + j, tiles * local_chunks - 1)
    spare = (q // BLOCK_CHUNKS % 2) * BLOCK_CHUNKS + q % BLOCK_CHUNKS
    spare_chunk = (spare // SPARE_CHUNKS) * local_chunks + (local_chunks - SPARE_CHUNKS) + spare % SPARE_CHUNKS
    dst = jnp.where(valid, tile * local_chunks + shift + j, spare_chunk)
    dst = jnp.concatenate([spare_chunk[:PRIME_BLOCKS * BLOCK_CHUNKS], dst])
    has_rows = total > 0
    ordinal = jnp.cumsum(has_rows.astype(jnp.int32)) - 1
    used_experts = jnp.argsort(jnp.logical_not(has_rows), stable=True).astype(jnp.int32)
    block_e = e[::BLOCK_CHUNKS]
    block_ord = jnp.sum(jnp.where(block_e[:, None] == jnp.arange(N_EXPERTS)[None, :], ordinal[None, :], 0), axis=1)
    n_used = jnp.stack([pend[-1] // BLOCK_CHUNKS, jnp.sum(has_rows)]).astype(jnp.int32)
    block_full = valid[BLOCK_CHUNKS // 2::BLOCK_CHUNKS].astype(jnp.int32)
    return (block_e, n_used, block_ord.astype(jnp.int32), used_experts, block_full,
            src.astype(jnp.int32), dst.astype(jnp.int32))


def kernel(x, p, positions, rel_bias_table, g_mix_norm, w_in, b_in, ret_norm_g, att_sinks, w_out,
           g_moe_norm, w_router, b_router, w_up, b_up, w_down, b_down, w_ple_gate, w_ple_proj,
           g_ple_norm, g_final):
    b, s, d = x.shape
    depth = w_in.shape[0]
    assert depth == 1, "single-layer stack only"
    n = b * s
    x2 = x.reshape(n, d)

    half = RET_DK // 2
    inv = ROPE_BASE ** (-jnp.arange(half, dtype=F32) / half)
    ang = (positions.astype(F32)[..., None] * inv).reshape(n, half)

    qr, kr, vr, gr, qa, kva = _in_proj(
        x2, g_mix_norm[0][None], w_in[0].astype(BF16), b_in[0][None], jnp.cos(ang), jnp.sin(ang))

    shape3 = lambda t: t.reshape(b, s, t.shape[-1])
    ret = _retention(shape3(qr), shape3(kr), shape3(vr), shape3(gr), ret_norm_g[0])
    att = _attention(shape3(qa), shape3(kva), att_sinks[0], rel_bias_table)

    h1, xloc, slot_rows, gates, counts = _out_router(
        ret.reshape(n, RET_WIDTH), att.reshape(n, ATT_WIDTH), x2, w_out[0].astype(BF16),
        g_moe_norm[0][None], w_router[0].T.astype(BF16), b_router[0][:, None])

    tables = _routing_tables(counts[:, 0, :].astype(jnp.int32))

    yloc = _moe(*tables, xloc, w_up[0],
                b_up[0][:, None, 0::2], b_up[0][:, None, 1::2],
                w_down[0], b_down[0].reshape(N_EXPERTS, 1, d))

    out = _combine_ple(yloc, slot_rows, gates, h1, p[0].reshape(n, -1),
                       w_ple_gate[0].astype(BF16), w_ple_proj[0].astype(BF16),
                       g_ple_norm[0][None], g_final[None])
    return out.reshape(b, s, d)
```
